```python
import math
import jax
import jax.numpy as jnp
from jax import lax
import numpy as np

D_MODEL = 1024
BATCH = 2
SEQ = 8192
DEPTH = 2

GRID_W = 64
CTX_LEN = 256
NORM_EPS = 1e-6

SSM_WIDTH = D_MODEL // 2
SSM_GROUP = 16
SSM_GROUPS = SSM_WIDTH // SSM_GROUP
SSM_STATE = 64
DT_MIN = 1e-3
DT_MAX = 1e-1

HEAD_DIM = 64
N_Q_HEADS = D_MODEL // 128
N_KV_HEADS = N_Q_HEADS // 4
GQA_GROUP = N_Q_HEADS // N_KV_HEADS
ATTN_WIDTH = N_Q_HEADS * HEAD_DIM
KV_WIDTH = N_KV_HEADS * HEAD_DIM
WINDOW = 128
ATTN_BLOCK = 128
ROPE_BASE = 10000.0
ATTN_SCALE = HEAD_DIM ** -0.5

POOL_WINDOWS = (2, 4, 8, 16)
POOL_WIDTH = D_MODEL // 2
POOL_GROUP = POOL_WIDTH // len(POOL_WINDOWS)
POOL_OUT_GROUP = D_MODEL // len(POOL_WINDOWS)

N_BRANCHES = 3
OFF_SSM = 0
OFF_K = OFF_SSM + SSM_WIDTH
OFF_V = OFF_K + KV_WIDTH
OFF_Q = OFF_V + KV_WIDTH
OFF_POOL = OFF_Q + ATTN_WIDTH
OFF_GATE = OFF_POOL + POOL_WIDTH
IN_WIDTH = OFF_GATE + N_BRANCHES * D_MODEL

N_EXPERTS = 32
TOP_K = 4
D_EXPERT = D_MODEL
SWIGLU_LIMIT = 7.0
SWIGLU_ALPHA = 1.702
MOE_BLOCK = 128

kernel_name = 'hybrid_s5_swa_pool_moe_prefix_dit'


def _rmsnorm(x, g):
    xf = x.astype(jnp.float32)
    y = xf * lax.rsqrt(jnp.mean(xf * xf, axis=-1, keepdims=True) + NORM_EPS)
    return (y * g.astype(jnp.float32)).astype(x.dtype)


def _modulate(h, shift, scale):
    return h * (1 + scale) + shift


def _heads(t, n_heads):
    return t.reshape(t.shape[:-1] + (n_heads, HEAD_DIM))


def _axial_rope(n_tokens):
    n_rows = n_tokens // GRID_W
    row = jnp.repeat(jnp.arange(n_rows), GRID_W).astype(jnp.float32)
    col = (jnp.arange(n_rows * GRID_W) % GRID_W).astype(jnp.float32)
    axis_dim = HEAD_DIM // 2
    inv_freq = 1.0 / (ROPE_BASE ** (jnp.arange(0, axis_dim, 2, dtype=jnp.float32) / axis_dim))
    ang_r = row[:, None] * inv_freq[None]
    ang_c = col[:, None] * inv_freq[None]
    ang = jnp.concatenate([ang_r, ang_r, ang_c, ang_c], axis=-1)
    return jnp.cos(ang), jnp.sin(ang)


def _apply_rope(t, cos, sin):
    tf = t.astype(jnp.float32)
    a, b, c, d = jnp.split(tf, 4, axis=-1)
    rot = jnp.concatenate([-b, a, -d, c], axis=-1)
    out = tf * cos[None, :, None, :] + rot * sin[None, :, None, :]
    return out.astype(t.dtype)


def _s5_discretize(lam_re, lam_im, log_dt, b_re, b_im):
    lr = lam_re.astype(jnp.float32)
    li = lam_im.astype(jnp.float32)
    dt = jnp.exp(log_dt.astype(jnp.float32))[:, None]
    mag = jnp.exp(lr * dt)
    ang = li * dt
    a_re = mag * jnp.cos(ang)
    a_im = mag * jnp.sin(ang)
    den = lr * lr + li * li
    k_re = ((a_re - 1.0) * lr + a_im * li) / den
    k_im = (a_im * lr - (a_re - 1.0) * li) / den
    br = b_re.astype(jnp.float32)
    bi = b_im.astype(jnp.float32)
    bb_re = k_re[..., None] * br - k_im[..., None] * bi
    bb_im = k_re[..., None] * bi + k_im[..., None] * br
    return a_re, a_im, bb_re, bb_im


def _complex_affine_combine(e1, e2):
    a1r, a1i, b1r, b1i = e1
    a2r, a2i, b2r, b2i = e2
    return (a2r * a1r - a2i * a1i,
            a2r * a1i + a2i * a1r,
            a2r * b1r - a2i * b1i + b2r,
            a2r * b1i + a2i * b1r + b2i)


def _s5_scan(u, a_re, a_im, bb_re, bb_im, s0):
    bu_re = jnp.einsum('blgh,gph->blgp', u, bb_re)
    bu_im = jnp.einsum('blgh,gph->blgp', u, bb_im)
    if s0 is not None:
        s_re, s_im = s0
        bu_re = bu_re.at[:, 0].add(a_re * s_re - a_im * s_im)
        bu_im = bu_im.at[:, 0].add(a_re * s_im + a_im * s_re)
    elems = (jnp.broadcast_to(a_re, bu_re.shape), jnp.broadcast_to(a_im, bu_im.shape), bu_re, bu_im)
    _, _, h_re, h_im = lax.associative_scan(_complex_affine_combine, elems, axis=1)
    return h_re, h_im


def _s5_readout(h_re, h_im, c_re, c_im):
    return (jnp.einsum('blgp,ghp->blgh', h_re, c_re.astype(jnp.float32))
            - jnp.einsum('blgp,ghp->blgh', h_im, c_im.astype(jnp.float32)))


def _s5_glu(y, u, d_skip, w_val, w_gate, dtype):
    bsz, length = y.shape[:2]
    z = y + d_skip.astype(jnp.float32).reshape(SSM_GROUPS, SSM_GROUP) * u
    z = jax.nn.gelu(z).reshape(bsz, length, SSM_WIDTH).astype(dtype)
    return (z @ w_val) * jax.nn.sigmoid(z @ w_gate)


def _ssm_branch(u_l, u_c, lam_re, lam_im, log_dt, b_re, b_im, c_re, c_im, d_skip, w_val, w_gate, need_ctx):
    bsz, n_lat = u_l.shape[:2]
    n_ctx = u_c.shape[1]
    ul = u_l.astype(jnp.float32).reshape(bsz, n_lat, SSM_GROUPS, SSM_GROUP)
    uc = u_c.astype(jnp.float32).reshape(bsz, n_ctx, SSM_GROUPS, SSM_GROUP)
    y_l = jnp.zeros_like(ul)
    y_c = jnp.zeros_like(uc)
    for direction in range(2):
        rev = direction == 1
        a_re, a_im, bb_re, bb_im = _s5_discretize(lam_re[direction], lam_im[direction], log_dt[direction],
                                                  b_re[direction], b_im[direction])
        uc_d = uc[:, ::-1] if rev else uc
        ul_d = ul[:, ::-1] if rev else ul
        hc_re, hc_im = _s5_scan(uc_d, a_re, a_im, bb_re, bb_im, None)
        hl_re, hl_im = _s5_scan(ul_d, a_re, a_im, bb_re, bb_im, (hc_re[:, -1], hc_im[:, -1]))
        yl = _s5_readout(hl_re, hl_im, c_re[direction], c_im[direction])
        y_l = y_l + (yl[:, ::-1] if rev else yl)
        if need_ctx:
            yc = _s5_readout(hc_re, hc_im, c_re[direction], c_im[direction])
            y_c = y_c + (yc[:, ::-1] if rev else yc)
    out_l = _s5_glu(y_l, ul, d_skip, w_val, w_gate, u_l.dtype)
    out_c = _s5_glu(y_c, uc, d_skip, w_val, w_gate, u_c.dtype) if need_ctx else None
    return out_l, out_c


def _window_attention(q, k, v, k_c, v_c, sink):
    bsz, n_tok = q.shape[:2]
    nb = n_tok // ATTN_BLOCK
    qb = q.reshape(bsz, nb, ATTN_BLOCK, N_KV_HEADS, GQA_GROUP, HEAD_DIM)
    pad = ((0, 0), (1, 1), (0, 0), (0, 0), (0, 0))
    kp = jnp.pad(k.reshape(bsz, nb, ATTN_BLOCK, N_KV_HEADS, HEAD_DIM), pad)
    vp = jnp.pad(v.reshape(bsz, nb, ATTN_BLOCK, N_KV_HEADS, HEAD_DIM), pad)
    k_band = jnp.concatenate([kp[:, :-2], kp[:, 1:-1], kp[:, 2:]], axis=2)
    v_band = jnp.concatenate([vp[:, :-2], vp[:, 1:-1], vp[:, 2:]], axis=2)
    s_win = jnp.einsum('bnqhgd,bnkhd->bnhgqk', qb, k_band, preferred_element_type=jnp.float32) * ATTN_SCALE
    q_pos = jnp.arange(nb)[:, None] * ATTN_BLOCK + jnp.arange(ATTN_BLOCK)[None]
    k_pos = (jnp.arange(nb)[:, None] - 1) * ATTN_BLOCK + jnp.arange(3 * ATTN_BLOCK)[None]
    kp3 = k_pos[:, None, :]
    valid = (jnp.abs(kp3 - q_pos[:, :, None]) <= WINDOW) & (kp3 >= 0) & (kp3 < n_tok)
    s_win = jnp.where(valid[None, :, None, None], s_win, -jnp.inf)
    s_ctx = jnp.einsum('bnqhgd,bkhd->bnhgqk', qb, k_c, preferred_element_type=jnp.float32) * ATTN_SCALE
    s_sink = jnp.broadcast_to(sink.astype(jnp.float32).reshape(1, 1, N_KV_HEADS, GQA_GROUP, 1, 1),
                              s_win.shape[:-1] + (1,))
    p = jax.nn.softmax(jnp.concatenate([s_sink, s_win, s_ctx], axis=-1), axis=-1)
    p_win = p[..., 1:1 + 3 * ATTN_BLOCK].astype(v.dtype)
    p_ctx = p[..., 1 + 3 * ATTN_BLOCK:].astype(v.dtype)
    o = (jnp.einsum('bnhgqk,bnkhd->bnqhgd', p_win, v_band)
         + jnp.einsum('bnhgqk,bkhd->bnqhgd', p_ctx, v_c))
    return o.reshape(bsz, n_tok, ATTN_WIDTH)


def _context_attention(q_c, k_c, v_c, sink):
    bsz, n_ctx = q_c.shape[:2]
    qg = q_c.reshape(bsz, n_ctx, N_KV_HEADS, GQA_GROUP, HEAD_DIM)
    s = jnp.einsum('bqhgd,bkhd->bhgqk', qg, k_c, preferred_element_type=jnp.float32) * ATTN_SCALE
    s_sink = jnp.broadcast_to(sink.astype(jnp.float32).reshape(1, N_KV_HEADS, GQA_GROUP, 1, 1),
                              s.shape[:-1] + (1,))
    p = jax.nn.softmax(jnp.concatenate([s_sink, s], axis=-1), axis=-1)[..., 1:].astype(v_c.dtype)
    o = jnp.einsum('bhgqk,bkhd->bqhgd', p, v_c)
    return o.reshape(bsz, n_ctx, ATTN_WIDTH)


def _multiscale_pool(u, w_pool, pool_scale):
    length = u.shape[1]
    t = jnp.arange(length)
    outs = []
    for gi, w in enumerate(POOL_WINDOWS):
        ug = u[..., gi * POOL_GROUP:(gi + 1) * POOL_GROUP].astype(jnp.float32)
        lo = w // 2
        hi = w - 1 - lo
        cs = jnp.pad(jnp.cumsum(ug, axis=1), ((0, 0), (1, 0), (0, 0)))
        start = jnp.clip(t - lo, 0, length)
        end = jnp.clip(t + hi + 1, 0, length)
        mean = (cs[:, end] - cs[:, start]) / (end - start).astype(jnp.float32)[None, :, None]
        diff = (mean - ug).astype(u.dtype)
        outs.append(diff @ w_pool[gi])
    return jnp.concatenate(outs, axis=-1) * pool_scale


def _merge(p, y_ssm, y_attn, y_pool, w_out):
    g = jax.nn.sigmoid(p[..., OFF_GATE:].reshape(p.shape[:-1] + (N_BRANCHES, D_MODEL)))
    m = g[..., 0, :] * y_ssm + g[..., 1, :] * y_attn + g[..., 2, :] * y_pool
    return m @ w_out


def _moe(h, router_w, router_b, w1, b1, w2, b2):
    n_tok, d = h.shape
    nk = n_tok * TOP_K
    logits = jnp.dot(h, router_w, preferred_element_type=jnp.float32) + router_b.astype(jnp.float32)
    top_v, top_e = lax.top_k(logits, TOP_K)
    probs = jax.nn.softmax(top_v, axis=-1)
    flat_e = top_e.reshape(-1)
    flat_tok = jnp.repeat(jnp.arange(n_tok, dtype=jnp.int32), TOP_K)
    order = jnp.argsort(flat_e)
    sorted_e = flat_e[order]
    counts = jnp.bincount(flat_e, length=N_EXPERTS)
    padded = (counts + MOE_BLOCK - 1) // MOE_BLOCK * MOE_BLOCK
    start = jnp.cumsum(counts) - counts
    pad_end = jnp.cumsum(padded)
    pad_start = pad_end - padded
    dest = pad_start[sorted_e] + jnp.arange(nk) - start[sorted_e]
    n_blocks = -(-(nk + N_EXPERTS * (MOE_BLOCK - 1)) // MOE_BLOCK)
    n_rows = n_blocks * MOE_BLOCK
    row_tok = jnp.full((n_rows,), n_tok, jnp.int32).at[dest].set(flat_tok[order])
    row_w = jnp.zeros((n_rows,), jnp.float32).at[dest].set(probs.reshape(-1)[order])
    block_e = jnp.minimum(jnp.searchsorted(pad_end, jnp.arange(n_blocks) * MOE_BLOCK, side='right'),
                          N_EXPERTS - 1)
    h_pad = jnp.concatenate([h, jnp.zeros((1, d), h.dtype)], axis=0)
    xb = h_pad[row_tok].reshape(n_blocks, MOE_BLOCK, d)

    def expert_block(args):
        xblk, e = args
        gu = xblk @ w1[e] + b1[e]
        gate = jnp.minimum(gu[..., :D_EXPERT], SWIGLU_LIMIT)
        lin = jnp.clip(gu[..., D_EXPERT:], -SWIGLU_LIMIT, SWIGLU_LIMIT)
        act = gate * jax.nn.sigmoid(SWIGLU_ALPHA * gate) * (lin + 1)
        return act @ w2[e] + b2[e]

    yb = lax.map(expert_block, (xb, block_e)).reshape(n_rows, d)
    y = yb * row_w[:, None].astype(yb.dtype)
    return jax.ops.segment_sum(y, row_tok, num_segments=n_tok + 1)[:n_tok]


def setup_inputs(seed: int = 0) -> dict:
    key = jax.random.key(seed)
    keys = jax.random.split(key, 40)
    kidx = iter(range(40))
    L, D = DEPTH, D_MODEL
    G, P, H = SSM_GROUPS, SSM_STATE, SSM_GROUP

    def nrm(shape, scale):
        return jax.random.normal(keys[next(kidx)], shape, jnp.float32) * scale

    n_idx = jnp.arange(P, dtype=jnp.float32)
    return {
        'x': nrm((BATCH, SEQ, D), 1.0),
        'c': nrm((BATCH, D), 1.0),
        'ctx': nrm((BATCH, CTX_LEN, D), 1.0),
        'c_ctx': nrm((D,), 1.0),
        'w_ada': nrm((L, D, 6 * D), 0.5 * D ** -0.5),
        'b_ada': nrm((L, 6 * D), 0.02),
        'norm1_g': 1.0 + nrm((L, D), 0.02),
        'norm2_g': 1.0 + nrm((L, D), 0.02),
        'w_in': nrm((L, D, IN_WIDTH), D ** -0.5),
        'ssm_lam_re': -0.5 + nrm((L, 2, G, P), 0.01),
        'ssm_lam_im': math.pi * n_idx + nrm((L, 2, G, P), 0.01),
        'ssm_log_dt': jax.random.uniform(keys[next(kidx)], (L, 2, G), jnp.float32,
                                         math.log(DT_MIN), math.log(DT_MAX)),
        'ssm_b_re': nrm((L, 2, G, P, H), (2 * H) ** -0.5),
        'ssm_b_im': nrm((L, 2, G, P, H), (2 * H) ** -0.5),
        'ssm_c_re': nrm((L, 2, G, H, P), (2 * P) ** -0.5),
        'ssm_c_im': nrm((L, 2, G, H, P), (2 * P) ** -0.5),
        'ssm_d': nrm((L, SSM_WIDTH), 0.5),
        'ssm_w_val': nrm((L, SSM_WIDTH, D), SSM_WIDTH ** -0.5),
        'ssm_w_gate': nrm((L, SSM_WIDTH, D), SSM_WIDTH ** -0.5),
        'attn_sink': nrm((L, N_Q_HEADS), 0.5),
        'attn_w_up': nrm((L, ATTN_WIDTH, D), ATTN_WIDTH ** -0.5),
        'pool_w': nrm((L, len(POOL_WINDOWS), POOL_GROUP, POOL_OUT_GROUP), POOL_GROUP ** -0.5),
        'pool_scale': 1.0 + nrm((L, D), 0.02),
        'w_out': nrm((L, D, D), D ** -0.5),
        'router_w': nrm((L, D, N_EXPERTS), D ** -0.5),
        'router_b': nrm((L, N_EXPERTS), 0.01),
        'exp_w1': nrm((L, N_EXPERTS, D, 2 * D_EXPERT), D ** -0.5),
        'exp_b1': nrm((L, N_EXPERTS, 2 * D_EXPERT), 0.02),
        'exp_w2': nrm((L, N_EXPERTS, D_EXPERT, D), D_EXPERT ** -0.5),
        'exp_b2': nrm((L, N_EXPERTS, D), 0.02),
        'final_g': 1.0 + nrm((D,), 0.02),
    }


def reference(x, c, ctx, c_ctx, w_ada, b_ada, norm1_g, norm2_g, w_in,
              ssm_lam_re, ssm_lam_im, ssm_log_dt, ssm_b_re, ssm_b_im, ssm_c_re, ssm_c_im, ssm_d,
              ssm_w_val, ssm_w_gate, attn_sink, attn_w_up, pool_w, pool_scale, w_out,
              router_w, router_b, exp_w1, exp_b1, exp_w2, exp_b2, final_g):
    bsz, n_lat, d = x.shape
    cos, sin = _axial_rope(n_lat)
    xc = ctx
    silu_c = jax.nn.silu(c)
    silu_cc = jax.nn.silu(c_ctx)
    for l in range(DEPTH):
        need_ctx = l < DEPTH - 1
        mod_l = (silu_c @ w_ada[l] + b_ada[l]).reshape(bsz, 1, 6, d)
        mod_c = (silu_cc @ w_ada[l] + b_ada[l]).reshape(6, d)

        h_l = _modulate(_rmsnorm(x, norm1_g[l]), mod_l[:, :, 0], mod_l[:, :, 1])
        h_c = _modulate(_rmsnorm(xc, norm1_g[l]), mod_c[0], mod_c[1])
        p_l = h_l @ w_in[l]
        p_c = h_c @ (w_in[l] if need_ctx else w_in[l][:, :OFF_Q])

        y_ssm_l, y_ssm_c = _ssm_branch(p_l[..., OFF_SSM:OFF_K], p_c[..., OFF_SSM:OFF_K],
                                       ssm_lam_re[l], ssm_lam_im[l], ssm_log_dt[l], ssm_b_re[l], ssm_b_im[l],
                                       ssm_c_re[l], ssm_c_im[l], ssm_d[l], ssm_w_val[l], ssm_w_gate[l], need_ctx)

        q_l = _apply_rope(_heads(p_l[..., OFF_Q:OFF_POOL], N_Q_HEADS), cos, sin)
        k_l = _apply_rope(_heads(p_l[..., OFF_K:OFF_V], N_KV_HEADS), cos, sin)
        v_l = _heads(p_l[..., OFF_V:OFF_Q], N_KV_HEADS)
        k_c = _heads(p_c[..., OFF_K:OFF_V], N_KV_HEADS)
        v_c = _heads(p_c[..., OFF_V:OFF_Q], N_KV_HEADS)
        y_attn_l = _window_attention(q_l, k_l, v_l, k_c, v_c, attn_sink[l]) @ attn_w_up[l]

        y_pool_l = _multiscale_pool(p_l[..., OFF_POOL:OFF_GATE], pool_w[l], pool_scale[l])

        x = x + mod_l[:, :, 2] * _merge(p_l, y_ssm_l, y_attn_l, y_pool_l, w_out[l])
        if need_ctx:
            q_c = _heads(p_c[..., OFF_Q:OFF_POOL], N_Q_HEADS)
            y_attn_c = _context_attention(q_c, k_c, v_c, attn_sink[l]) @ attn_w_up[l]
            y_pool_c = _multiscale_pool(p_c[..., OFF_POOL:OFF_GATE], pool_w[l], pool_scale[l])
            xc = xc + mod_c[2] * _merge(p_c, y_ssm_c, y_attn_c, y_pool_c, w_out[l])

        f_l = _modulate(_rmsnorm(x, norm2_g[l]), mod_l[:, :, 3], mod_l[:, :, 4])
        if need_ctx:
            f_c = _modulate(_rmsnorm(xc, norm2_g[l]), mod_c[3], mod_c[4])
            tokens = jnp.concatenate([f_l.reshape(-1, d), f_c.reshape(-1, d)], axis=0)
            y = _moe(tokens, router_w[l], router_b[l], exp_w1[l], exp_b1[l], exp_w2[l], exp_b2[l])
            n_flat = bsz * n_lat
            x = x + mod_l[:, :, 5] * y[:n_flat].reshape(bsz, n_lat, d)
            xc = xc + mod_c[5] * y[n_flat:].reshape(xc.shape)
        else:
            y = _moe(f_l.reshape(-1, d), router_w[l], router_b[l], exp_w1[l], exp_b1[l], exp_w2[l], exp_b2[l])
            x = x + mod_l[:, :, 5] * y.reshape(bsz, n_lat, d)
    return _rmsnorm(x, final_g)
```

```python
import functools
import math

import jax
import jax.numpy as jnp
from jax import lax
from jax.experimental import pallas as pl
from jax.experimental.pallas import tpu as pltpu
from jax.experimental.pallas import tpu_sc as plsc

F32 = jnp.float32
BF16 = jnp.bfloat16
HIGHEST = lax.Precision.HIGHEST

D_MODEL = 1024
BATCH = 2
SEQ = 8192
DEPTH = 2
GRID_W = 64
CTX_LEN = 256
NORM_EPS = 1e-6
SSM_WIDTH = 512
SSM_GROUP = 16
SSM_GROUPS = 32
SSM_STATE = 64
HEAD_DIM = 64
N_Q_HEADS = 8
N_KV_HEADS = 2
GQA_GROUP = 4
ATTN_WIDTH = 512
KV_WIDTH = 128
WINDOW = 128
ROPE_BASE = 10000.0
ATTN_SCALE = HEAD_DIM ** -0.5
POOL_WINDOWS = (2, 4, 8, 16)
POOL_WIDTH = 512
POOL_GROUP = 128
POOL_OUT_GROUP = 256
OFF_K = 512
OFF_V = 640
OFF_Q = 768
OFF_POOL = 1280
OFF_GATE = 1792
N_EXPERTS = 32
TOP_K = 4
D_EXPERT = 1024
SWIGLU_LIMIT = 7.0
SWIGLU_ALPHA = 1.702

S_TOT = CTX_LEN + SEQ
N_TOK = BATCH * S_TOT
TOK_TILE = 256
N_TILES = S_TOT // TOK_TILE
ATTN_BLOCK = 128
POOL_HALO = 8
ROUTE_TILE = 512
ROW_BLOCK = 256
N_ROWS = -(-(N_TOK * TOP_K + N_EXPERTS * (ROW_BLOCK - 1)) // ROW_BLOCK) * ROW_BLOCK
N_BLOCKS = N_ROWS // ROW_BLOCK
N_BLOCKS_PAD = -(-N_BLOCKS // 8) * 8
SC_CORES = 2
SC_WORKERS = SC_CORES * 16
SC_SCATTER_ROWS = 48
SC_GATHER_ROWS = 32
assert N_TOK % (SC_WORKERS * SC_SCATTER_ROWS) == 0 and (N_TOK * TOP_K) % (SC_WORKERS * SC_GATHER_ROWS) == 0
VMEM_LIMIT = 56 * 1024 * 1024
LANES = 128
MASK_VALUE = -1e30


def _rms(x, g):
    return x * lax.rsqrt(jnp.mean(x * x, axis=-1, keepdims=True) + NORM_EPS) * g


def _mod_row(i, b):
    return jnp.where(i == 0, 2, b)


def _ada_kernel(cv_ref, w_ref, b_ref, o_ref):
    cv = cv_ref[...]
    s = cv * jax.nn.sigmoid(cv)
    o_ref[0, 0] = jnp.dot(s, w_ref[0], preferred_element_type=F32, precision=HIGHEST) + b_ref[0, 0]


def _ada(cv, w_ada, b_ada):
    return pl.pallas_call(
        _ada_kernel,
        grid=(DEPTH, 6),
        in_specs=[
            pl.BlockSpec((8, D_MODEL), lambda l, j: (0, 0)),
            pl.BlockSpec((1, D_MODEL, D_MODEL), lambda l, j: (l, 0, j)),
            pl.BlockSpec((1, 1, 1, D_MODEL), lambda l, j: (l, j, 0, 0)),
        ],
        out_specs=pl.BlockSpec((1, 1, 8, D_MODEL), lambda l, j: (l, j, 0, 0)),
        out_shape=jax.ShapeDtypeStruct((DEPTH, 6, 8, D_MODEL), F32),
        name="ada",
    )(cv, w_ada, b_ada.reshape(DEPTH, 6, 1, D_MODEL))


def _rope(t, cos, sin):
    lane = lax.broadcasted_iota(jnp.int32, t.shape, 1)
    rot = jnp.where((lane % 32) < 16, -pltpu.roll(t, LANES - 16, 1), pltpu.roll(t, 16, 1))
    return t * cos + rot * sin


def _inproj_kernel(x_ref, mod_ref, g_ref, w_ref, cos_ref, sin_ref, u_ref, kv_ref, q_ref, pool_ref):
    b = pl.program_id(0)
    i = pl.program_id(1)
    row = _mod_row(i, b)
    shift = mod_ref[0, pl.ds(row, 1), :]
    scale = mod_ref[1, pl.ds(row, 1), :]
    h = _rms(x_ref[0], g_ref[...]) * (1.0 + scale) + shift
    p = jnp.dot(h.astype(BF16), w_ref[...], preferred_element_type=F32)
    cos = cos_ref[...]
    sin = sin_ref[...]
    u_ref[0] = p[:, :OFF_K]
    kv_ref[0, :, :KV_WIDTH] = _rope(p[:, OFF_K:OFF_V], cos, sin).astype(BF16)
    kv_ref[0, :, KV_WIDTH:] = p[:, OFF_V:OFF_Q].astype(BF16)
    for c in range(ATTN_WIDTH // LANES):
        qc = p[:, OFF_Q + c * LANES:OFF_Q + (c + 1) * LANES]
        q_ref[0, :, c * LANES:(c + 1) * LANES] = (_rope(qc, cos, sin) * ATTN_SCALE).astype(BF16)
    pool_ref[0] = p[:, OFF_POOL:OFF_GATE]


def _inproj(xa, mod, g1, w1, cos_t, sin_t):
    tok = lambda b, i: (b, i, 0)
    return pl.pallas_call(
        _inproj_kernel,
        grid=(BATCH, N_TILES),
        in_specs=[
            pl.BlockSpec((1, TOK_TILE, D_MODEL), tok),
            pl.BlockSpec((6, 8, D_MODEL), lambda b, i: (0, 0, 0)),
            pl.BlockSpec((1, D_MODEL), lambda b, i: (0, 0)),
            pl.BlockSpec((D_MODEL, OFF_GATE), lambda b, i: (0, 0)),
            pl.BlockSpec((TOK_TILE, LANES), lambda b, i: (i, 0)),
            pl.BlockSpec((TOK_TILE, LANES), lambda b, i: (i, 0)),
        ],
        out_specs=[
            pl.BlockSpec((1, TOK_TILE, SSM_WIDTH), tok),
            pl.BlockSpec((1, TOK_TILE, 2 * KV_WIDTH), tok),
            pl.BlockSpec((1, TOK_TILE, ATTN_WIDTH), tok),
            pl.BlockSpec((1, TOK_TILE, POOL_WIDTH), tok),
        ],
        out_shape=[
            jax.ShapeDtypeStruct((BATCH, S_TOT, SSM_WIDTH), F32),
            jax.ShapeDtypeStruct((BATCH, S_TOT, 2 * KV_WIDTH), BF16),
            jax.ShapeDtypeStruct((BATCH, S_TOT, ATTN_WIDTH), BF16),
            jax.ShapeDtypeStruct((BATCH, S_TOT, POOL_WIDTH), F32),
        ],
        compiler_params=pltpu.CompilerParams(
            dimension_semantics=("parallel", "parallel"), vmem_limit_bytes=VMEM_LIMIT),
        name="inproj",
    )(xa, mod, g1, w1, cos_t, sin_t)


SSM_HALF_IN = SSM_WIDTH // 2
SSM_HALF_ST = SSM_GROUPS // 2 * SSM_STATE
N_STATE_ROWS = 8
N_LANE_TILES = SSM_HALF_ST // LANES


def _ssm_kernel(uf_ref, ub_ref, wb_ref, wc_ref, ar_ref, ai_ref, yf_ref, yb_ref, buf_f, buf_b, hf_ref, hb_ref):
    i = pl.program_id(0)
    T = TOK_TILE

    @pl.when(i == 0)
    def _():
        hf_ref[...] = jnp.zeros_like(hf_ref)
        hb_ref[...] = jnp.zeros_like(hb_ref)

    for d, (u_ref, buf) in enumerate(((uf_ref, buf_f), (ub_ref, buf_b))):
        for b in range(BATCH):
            for half in range(2):
                ub16 = u_ref[b, :, half * SSM_HALF_IN:(half + 1) * SSM_HALF_IN].astype(BF16)
                for reim in range(2):
                    k = reim * 4 + half * 2 + b
                    bu = jnp.dot(ub16, wb_ref[d, reim, half], preferred_element_type=F32)
                    for l in range(N_LANE_TILES):
                        buf[l, k * T:(k + 1) * T, :] = bu[:, l * LANES:(l + 1) * LANES]

    def step(t, carry):
        hf, hb = carry
        tb = T - 1 - t
        nf, nb = [], []
        for l in range(N_LANE_TILES):
            sl = pl.ds(l * LANES, LANES)
            rows_f = pl.ds(t, N_STATE_ROWS, stride=T)
            h = ar_ref[0, :, sl] * hf[l] + ai_ref[0, :, sl] * pltpu.roll(hf[l], 4, 0) + buf_f[l, rows_f, :]
            buf_f[l, rows_f, :] = h
            nf.append(h)
            rows_b = pl.ds(tb, N_STATE_ROWS, stride=T)
            h = ar_ref[1, :, sl] * hb[l] + ai_ref[1, :, sl] * pltpu.roll(hb[l], 4, 0) + buf_b[l, rows_b, :]
            buf_b[l, rows_b, :] = h
            nb.append(h)
        return tuple(nf), tuple(nb)

    hf0 = tuple(hf_ref[:, l * LANES:(l + 1) * LANES] for l in range(N_LANE_TILES))
    hb0 = tuple(hb_ref[:, l * LANES:(l + 1) * LANES] for l in range(N_LANE_TILES))
    hf1, hb1 = lax.fori_loop(0, T, step, (hf0, hb0))
    for l in range(N_LANE_TILES):
        hf_ref[:, l * LANES:(l + 1) * LANES] = hf1[l]
        hb_ref[:, l * LANES:(l + 1) * LANES] = hb1[l]

    for d, (buf, y_ref) in enumerate(((buf_f, yf_ref), (buf_b, yb_ref))):
        for b in range(BATCH):
            for half in range(2):
                kre = half * 2 + b
                kim = 4 + half * 2 + b
                state = lambda k: jnp.concatenate(
                    [buf[l, k * T:(k + 1) * T, :] for l in range(N_LANE_TILES)], axis=1).astype(BF16)
                y = jnp.dot(state(kre), wc_ref[d, 0, half], preferred_element_type=F32)
                y = y + jnp.dot(state(kim), wc_ref[d, 1, half], preferred_element_type=F32)
                y_ref[b, :, half * SSM_HALF_IN:(half + 1) * SSM_HALF_IN] = y


def _ssm(u, wb, wc, ar, ai):
    fwd = lambda i: (0, i, 0)
    bwd = lambda i: (0, jnp.where(i == 0, 0, N_TILES - i), 0)
    return pl.pallas_call(
        _ssm_kernel,
        grid=(N_TILES,),
        in_specs=[
            pl.BlockSpec((BATCH, TOK_TILE, SSM_WIDTH), fwd),
            pl.BlockSpec((BATCH, TOK_TILE, SSM_WIDTH), bwd),
            pl.BlockSpec((2, 2, 2, SSM_HALF_IN, SSM_HALF_ST), lambda i: (0, 0, 0, 0, 0)),
            pl.BlockSpec((2, 2, 2, SSM_HALF_ST, SSM_HALF_IN), lambda i: (0, 0, 0, 0, 0)),
            pl.BlockSpec((2, N_STATE_ROWS, SSM_HALF_ST), lambda i: (0, 0, 0)),
            pl.BlockSpec((2, N_STATE_ROWS, SSM_HALF_ST), lambda i: (0, 0, 0)),
        ],
        out_specs=[
            pl.BlockSpec((BATCH, TOK_TILE, SSM_WIDTH), fwd),
            pl.BlockSpec((BATCH, TOK_TILE, SSM_WIDTH), bwd),
        ],
        out_shape=[jax.ShapeDtypeStruct((BATCH, S_TOT, SSM_WIDTH), F32)] * 2,
        scratch_shapes=[
            pltpu.VMEM((N_LANE_TILES, N_STATE_ROWS * TOK_TILE, LANES), F32),
            pltpu.VMEM((N_LANE_TILES, N_STATE_ROWS * TOK_TILE, LANES), F32),
            pltpu.VMEM((N_STATE_ROWS, SSM_HALF_ST), F32),
            pltpu.VMEM((N_STATE_ROWS, SSM_HALF_ST), F32),
        ],
        compiler_params=pltpu.CompilerParams(
            dimension_semantics=("arbitrary",), vmem_limit_bytes=VMEM_LIMIT),
        name="ssm",
    )(u, u, wb, wc, ar, ai)


def _ssm_params(lam_re, lam_im, log_dt, b_re, b_im, c_re, c_im):
    lr = lam_re.astype(F32)
    li = lam_im.astype(F32)
    dt = jnp.exp(log_dt.astype(F32))[..., None]
    mag = jnp.exp(lr * dt)
    ang = li * dt
    a_re = mag * jnp.cos(ang)
    a_im = mag * jnp.sin(ang)
    den = lr * lr + li * li
    k_re = ((a_re - 1.0) * lr + a_im * li) / den
    k_im = (a_im * lr - (a_re - 1.0) * li) / den
    bb_re = k_re[..., None] * b_re - k_im[..., None] * b_im
    bb_im = k_re[..., None] * b_im + k_im[..., None] * b_re
    eye = jnp.eye(SSM_GROUPS // 2, dtype=F32)

    def drive(bb):
        blk = jnp.swapaxes(bb, -1, -2).reshape(2, 2, SSM_GROUPS // 2, SSM_GROUP, SSM_STATE)
        return jnp.einsum('dsghp,gk->dsghkp', blk, eye).reshape(2, 2, SSM_HALF_IN, SSM_HALF_ST)

    def readout(c):
        blk = jnp.swapaxes(c, -1, -2).reshape(2, 2, SSM_GROUPS // 2, SSM_STATE, SSM_GROUP)
        return jnp.einsum('dsgph,gk->dsgpkh', blk, eye).reshape(2, 2, SSM_HALF_ST, SSM_HALF_IN)

    wb = jnp.stack([drive(bb_re), drive(bb_im)], axis=1).astype(BF16)
    wc = jnp.stack([readout(c_re.astype(F32)), readout(-c_im.astype(F32))], axis=1).astype(BF16)

    def rows(a, sign_re, sign_im):
        halves = a.reshape(2, 2, SSM_HALF_ST)
        per = []
        for reim in range(2):
            for half in range(2):
                for _ in range(BATCH):
                    per.append(halves[:, half] * (sign_re if reim == 0 else sign_im))
        return jnp.stack(per, axis=1)

    ar = rows(a_re, 1.0, 1.0)
    ai = rows(a_im, -1.0, 1.0)
    return wb, wc, ar, ai


def _attn_kernel(sink_ref, q_ref, kc_ref, kp_ref, kn0_ref, kn_ref, o_ref):
    j = pl.program_id(1)
    n = j - CTX_LEN // ATTN_BLOCK
    q = q_ref[0]
    kvc = kc_ref[0]
    band = jnp.concatenate([kp_ref[0], kn0_ref[0], kn_ref[0]], axis=0)
    nb = band.shape[0]
    q_pos = n * ATTN_BLOCK + lax.broadcasted_iota(jnp.int32, (ATTN_BLOCK, nb), 0)
    k_pos = (n - 1) * ATTN_BLOCK + lax.broadcasted_iota(jnp.int32, (ATTN_BLOCK, nb), 1)
    valid = (jnp.abs(k_pos - q_pos) <= WINDOW) & (k_pos >= 0) & (k_pos < SEQ) & (n >= 0)
    nt = (((1,), (1,)), ((), ()))
    outs = []
    for hk in range(N_KV_HEADS):
        k_b = band[:, hk * HEAD_DIM:(hk + 1) * HEAD_DIM]
        v_b = band[:, KV_WIDTH + hk * HEAD_DIM:KV_WIDTH + (hk + 1) * HEAD_DIM]
        k_c = kvc[:, hk * HEAD_DIM:(hk + 1) * HEAD_DIM]
        v_c = kvc[:, KV_WIDTH + hk * HEAD_DIM:KV_WIDTH + (hk + 1) * HEAD_DIM]
        for g in range(GQA_GROUP):
            hq = hk * GQA_GROUP + g
            qh = q[:, hq * HEAD_DIM:(hq + 1) * HEAD_DIM]
            s_w = lax.dot_general(qh, k_b, nt, preferred_element_type=F32)
            s_w = jnp.where(valid, s_w, MASK_VALUE)
            s_c = lax.dot_general(qh, k_c, nt, preferred_element_type=F32)
            sink = sink_ref[hq]
            m = jnp.maximum(jnp.maximum(jnp.max(s_w, axis=-1, keepdims=True),
                                        jnp.max(s_c, axis=-1, keepdims=True)), sink)
            e_w = jnp.exp(s_w - m)
            e_c = jnp.exp(s_c - m)
            den = jnp.sum(e_w, axis=-1, keepdims=True) + jnp.sum(e_c, axis=-1, keepdims=True) + jnp.exp(sink - m)
            o = jnp.dot(e_w.astype(BF16), v_b, preferred_element_type=F32)
            o = o + jnp.dot(e_c.astype(BF16), v_c, preferred_element_type=F32)
            outs.append(o / den)
    o_ref[0] = jnp.concatenate(outs, axis=1).astype(BF16)


def _attn(sink, q, kv):
    nq = S_TOT // ATTN_BLOCK
    first = CTX_LEN // ATTN_BLOCK
    clamp = lambda j: jnp.clip(j, first, nq - 1)
    return pl.pallas_call(
        _attn_kernel,
        grid=(BATCH, nq),
        in_specs=[
            pl.BlockSpec(memory_space=pltpu.SMEM),
            pl.BlockSpec((1, ATTN_BLOCK, ATTN_WIDTH), lambda b, j: (b, j, 0)),
            pl.BlockSpec((1, CTX_LEN, 2 * KV_WIDTH), lambda b, j: (b, 0, 0)),
            pl.BlockSpec((1, ATTN_BLOCK, 2 * KV_WIDTH), lambda b, j: (b, clamp(j - 1), 0)),
            pl.BlockSpec((1, ATTN_BLOCK, 2 * KV_WIDTH), lambda b, j: (b, clamp(j), 0)),
            pl.BlockSpec((1, ATTN_BLOCK, 2 * KV_WIDTH), lambda b, j: (b, clamp(j + 1), 0)),
        ],
        out_specs=pl.BlockSpec((1, ATTN_BLOCK, ATTN_WIDTH), lambda b, j: (b, j, 0)),
        out_shape=jax.ShapeDtypeStruct((BATCH, S_TOT, ATTN_WIDTH), BF16),
        compiler_params=pltpu.CompilerParams(dimension_semantics=("parallel", "parallel")),
        name="attn",
    )(sink, q, kv, kv, kv, kv)


def _gelu_tanh(x):
    return 0.5 * x * (1.0 + jnp.tanh(math.sqrt(2.0 / math.pi) * (x + 0.044715 * (x * x * x))))


def _pool_means(ext, i):
    T = TOK_TILE
    n_ext = T + 2 * POOL_HALO
    seq_len = jnp.where(i == 0, CTX_LEN, SEQ)
    start = jnp.where(i == 0, 0, (i - 1) * T)
    r = start - POOL_HALO + lax.broadcasted_iota(jnp.int32, (n_ext, 1), 0)
    ext = jnp.where((r >= 0) & (r < seq_len), ext, 0.0)
    tl = start + lax.broadcasted_iota(jnp.int32, (T, 1), 0)
    back = lambda a, s: pltpu.roll(a, s, 0)
    fwd = lambda a, s: pltpu.roll(a, n_ext - s, 0)
    diffs = []
    for gi, w in enumerate(POOL_WINDOWS):
        e = ext[:, gi * POOL_GROUP:(gi + 1) * POOL_GROUP]
        lo = w // 2
        hi = w - 1 - lo
        s = e + back(e, 1)
        span = 2
        while span < w:
            s = s + back(s, span)
            span *= 2
        if w > 2:
            s = fwd(s, hi)
        cnt = (jnp.clip(tl + hi + 1, 0, seq_len) - jnp.clip(tl - lo, 0, seq_len)).astype(F32)
        mean = s[POOL_HALO:POOL_HALO + T] / cnt
        diffs.append((mean - e[POOL_HALO:POOL_HALO + T]).astype(BF16))
    return diffs


def _merge_kernel(x_ref, mod_ref, g1_ref, wg_ref, yf_ref, yb_ref, u_ref, dsk_ref, wval_ref, wgate_ref,
                  o_ref, wup_ref, pc_ref, pp_ref, pn_ref, pw_ref, ps_ref, wout_ref, g2_ref, rw_ref, rb_ref,
                  xo_ref, f_ref, te_ref, tp_ref):
    b = pl.program_id(0)
    i = pl.program_id(1)
    row = _mod_row(i, b)
    mods = [mod_ref[j, pl.ds(row, 1), :] for j in range(6)]
    x = x_ref[0]
    h = (_rms(x, g1_ref[...]) * (1.0 + mods[1]) + mods[0]).astype(BF16)
    gates = jax.nn.sigmoid(jnp.dot(h, wg_ref[...], preferred_element_type=F32))

    z = yf_ref[0] + yb_ref[0] + dsk_ref[...] * u_ref[0]
    z = _gelu_tanh(z).astype(BF16)
    y_ssm = (jnp.dot(z, wval_ref[...], preferred_element_type=F32)
             * jax.nn.sigmoid(jnp.dot(z, wgate_ref[...], preferred_element_type=F32)))
    y_attn = jnp.dot(o_ref[0], wup_ref[...], preferred_element_type=F32)

    ext = jnp.concatenate([pp_ref[0], pc_ref[0], pn_ref[0]], axis=0)
    diffs = _pool_means(ext, i)
    y_pool = jnp.concatenate(
        [jnp.dot(diffs[gi], pw_ref[gi], preferred_element_type=F32) for gi in range(len(POOL_WINDOWS))],
        axis=1) * ps_ref[...]

    m = (gates[:, :D_MODEL] * y_ssm + gates[:, D_MODEL:2 * D_MODEL] * y_attn
         + gates[:, 2 * D_MODEL:] * y_pool)
    xo = x + mods[2] * jnp.dot(m.astype(BF16), wout_ref[...], preferred_element_type=F32)
    xo_ref[0] = xo
    f = _rms(xo, g2_ref[...]) * (1.0 + mods[4]) + mods[3]
    f_ref[0] = f

    logits = jnp.dot(f, rw_ref[...], preferred_element_type=F32, precision=HIGHEST) + rb_ref[...]
    lane = lax.broadcasted_iota(jnp.int32, logits.shape, 1)
    vals, idxs = [], []
    for _ in range(TOP_K):
        mx = jnp.max(logits, axis=-1, keepdims=True)
        ix = jnp.min(jnp.where(logits == mx, lane, N_EXPERTS), axis=-1, keepdims=True)
        vals.append(mx)
        idxs.append(ix)
        logits = jnp.where(lane == ix, -jnp.inf, logits)
    ex = [jnp.exp(v - vals[0]) for v in vals]
    tot = ex[0] + ex[1] + ex[2] + ex[3]
    te_ref[0] = jnp.concatenate(idxs, axis=1)
    tp_ref[0] = jnp.concatenate(ex, axis=1) / tot


def _merge(xa, mod, g1, wg, yf, yb, u, dsk, wval, wgate, o, wup, pool_in, pw, ps, wout, g2, rw, rb):
    tok = lambda b, i: (b, i, 0)
    const2 = lambda b, i: (0, 0)
    const3 = lambda b, i: (0, 0, 0)
    per_tile = TOK_TILE // POOL_HALO
    n_halo = S_TOT // POOL_HALO
    return pl.pallas_call(
        _merge_kernel,
        grid=(BATCH, N_TILES),
        in_specs=[
            pl.BlockSpec((1, TOK_TILE, D_MODEL), tok),
            pl.BlockSpec((6, 8, D_MODEL), const3),
            pl.BlockSpec((1, D_MODEL), const2),
            pl.BlockSpec((D_MODEL, 3 * D_MODEL), const2),
            pl.BlockSpec((1, TOK_TILE, SSM_WIDTH), tok),
            pl.BlockSpec((1, TOK_TILE, SSM_WIDTH), tok),
            pl.BlockSpec((1, TOK_TILE, SSM_WIDTH), tok),
            pl.BlockSpec((1, SSM_WIDTH), const2),
            pl.BlockSpec((SSM_WIDTH, D_MODEL), const2),
            pl.BlockSpec((SSM_WIDTH, D_MODEL), const2),
            pl.BlockSpec((1, TOK_TILE, ATTN_WIDTH), tok),
            pl.BlockSpec((ATTN_WIDTH, D_MODEL), const2),
            pl.BlockSpec((1, TOK_TILE, POOL_WIDTH), tok),
            pl.BlockSpec((1, POOL_HALO, POOL_WIDTH), lambda b, i: (b, jnp.maximum(i * per_tile - 1, 0), 0)),
            pl.BlockSpec((1, POOL_HALO, POOL_WIDTH),
                         lambda b, i: (b, jnp.minimum((i + 1) * per_tile, n_halo - 1), 0)),
            pl.BlockSpec((len(POOL_WINDOWS), POOL_GROUP, POOL_OUT_GROUP), const3),
            pl.BlockSpec((1, D_MODEL), const2),
            pl.BlockSpec((D_MODEL, D_MODEL), const2),
            pl.BlockSpec((1, D_MODEL), const2),
            pl.BlockSpec((D_MODEL, N_EXPERTS), const2),
            pl.BlockSpec((1, N_EXPERTS), const2),
        ],
        out_specs=[
            pl.BlockSpec((1, TOK_TILE, D_MODEL), tok),
            pl.BlockSpec((1, TOK_TILE, D_MODEL), tok),
            pl.BlockSpec((1, TOK_TILE, TOP_K), tok),
            pl.BlockSpec((1, TOK_TILE, TOP_K), tok),
        ],
        out_shape=[
            jax.ShapeDtypeStruct((BATCH, S_TOT, D_MODEL), F32),
            jax.ShapeDtypeStruct((BATCH, S_TOT, D_MODEL), F32),
            jax.ShapeDtypeStruct((BATCH, S_TOT, TOP_K), jnp.int32),
            jax.ShapeDtypeStruct((BATCH, S_TOT, TOP_K), F32),
        ],
        compiler_params=pltpu.CompilerParams(
            dimension_semantics=("parallel", "parallel"), vmem_limit_bytes=VMEM_LIMIT),
        name="merge",
    )(xa, mod, g1, wg, yf, yb, u, dsk, wval, wgate, o, wup, pool_in, pool_in, pool_in, pw, ps, wout, g2, rw, rb)


def _route_kernel(te_ref, dest_ref, be_ref, cnt_ref, carry_ref):
    ph = pl.program_id(0)
    i = pl.program_id(1)
    te = te_ref[...]
    lane = lax.broadcasted_iota(jnp.int32, (ROUTE_TILE, LANES), 1)
    ohs = [(te[:, k:k + 1] == lane).astype(F32) for k in range(TOP_K)]
    oh_all = ohs[0] + ohs[1] + ohs[2] + ohs[3]
    tile_cnt = jnp.sum(oh_all, axis=0, keepdims=True)

    @pl.when((ph == 0) & (i == 0))
    def _():
        cnt_ref[...] = jnp.zeros_like(cnt_ref)

    @pl.when(ph == 0)
    def _():
        cnt_ref[...] += tile_cnt

    @pl.when((ph == 1) & (i == 0))
    def _():
        cnt = jnp.broadcast_to(cnt_ref[...], (8, LANES))
        padded = jnp.floor((cnt + (ROW_BLOCK - 1)) / ROW_BLOCK) * ROW_BLOCK
        l8 = lax.broadcasted_iota(jnp.int32, (8, LANES), 1)
        end = padded
        s = 1
        while s < N_EXPERTS:
            end = end + jnp.where(l8 >= s, pltpu.roll(end, s, 1), 0.0)
            s *= 2
        carry_ref[...] = (end - padded)[0:1]
        blk_row = (lax.broadcasted_iota(jnp.int32, (N_BLOCKS_PAD, LANES), 0) * ROW_BLOCK).astype(F32)
        lb = lax.broadcasted_iota(jnp.int32, (N_BLOCKS_PAD, LANES), 1)
        hit = jnp.where((lb < N_EXPERTS) & (jnp.broadcast_to(end[0:1], blk_row.shape) <= blk_row), 1.0, 0.0)
        be = jnp.sum(hit, axis=-1, keepdims=True)
        be_ref[...] = jnp.broadcast_to(be, (N_BLOCKS_PAD, LANES)).astype(jnp.int32)

    @pl.when(ph == 1)
    def _():
        r = lax.broadcasted_iota(jnp.int32, (ROUTE_TILE, ROUTE_TILE), 0)
        c = lax.broadcasted_iota(jnp.int32, (ROUTE_TILE, ROUTE_TILE), 1)
        below = jnp.where(r > c, 1.0, 0.0).astype(BF16)
        rank = carry_ref[...] + jnp.dot(below, oh_all.astype(BF16), preferred_element_type=F32)
        dests = [jnp.sum(ohs[k] * rank, axis=-1, keepdims=True) for k in range(TOP_K)]
        dest_ref[...] = jnp.concatenate(dests, axis=1).astype(jnp.int32)
        carry_ref[...] += tile_cnt


def _route(top_e):
    n_steps = N_TOK // ROUTE_TILE
    return pl.pallas_call(
        _route_kernel,
        grid=(2, n_steps),
        in_specs=[pl.BlockSpec((ROUTE_TILE, TOP_K), lambda ph, i: (i, 0))],
        out_specs=[
            pl.BlockSpec((ROUTE_TILE, TOP_K), lambda ph, i: (ph * i, 0)),
            pl.BlockSpec((N_BLOCKS_PAD, LANES), lambda ph, i: (0, 0)),
        ],
        out_shape=[
            jax.ShapeDtypeStruct((N_TOK, TOP_K), jnp.int32),
            jax.ShapeDtypeStruct((N_BLOCKS_PAD, LANES), jnp.int32),
        ],
        scratch_shapes=[pltpu.VMEM((1, LANES), F32), pltpu.VMEM((1, LANES), F32)],
        compiler_params=pltpu.CompilerParams(dimension_semantics=("arbitrary", "arbitrary")),
        name="route",
    )(top_e)


def _sc_mesh():
    return plsc.VectorSubcoreMesh(core_axis_name="c", subcore_axis_name="s")


def _sc_worker():
    return lax.axis_index("s") * SC_CORES + lax.axis_index("c")


def _dispatch(f, dest_t):
    per_w = N_TOK // SC_WORKERS
    n_chunks = per_w // SC_SCATTER_ROWS

    @functools.partial(
        pl.kernel,
        out_type=jax.ShapeDtypeStruct((N_ROWS, D_MODEL), F32),
        mesh=_sc_mesh(),
        scratch_types=[
            pltpu.VMEM((SC_SCATTER_ROWS,), jnp.int32),
            pltpu.VMEM((SC_SCATTER_ROWS, D_MODEL), F32),
            pltpu.SemaphoreType.DMA,
        ],
        name="dispatch",
    )
    def k(f_hbm, i_hbm, o_hbm, idx_v, rows_v, sem):
        base = _sc_worker() * per_w

        @pl.loop(0, n_chunks)
        def _(j):
            t0 = pl.multiple_of(base + j * SC_SCATTER_ROWS, 8)
            pltpu.sync_copy(f_hbm.at[pl.ds(t0, SC_SCATTER_ROWS)], rows_v)
            for kk in range(TOP_K):
                pltpu.sync_copy(i_hbm.at[pl.ds(kk * N_TOK + t0, SC_SCATTER_ROWS)], idx_v)
                pltpu.async_copy(rows_v, o_hbm.at[idx_v], sem).wait()

    return k(f, dest_t.reshape(TOP_K * N_TOK))


def _combine_gather(yb, dest_t):
    n = TOP_K * N_TOK
    per_w = n // SC_WORKERS
    n_chunks = per_w // SC_GATHER_ROWS

    @functools.partial(
        pl.kernel,
        out_type=jax.ShapeDtypeStruct((n, D_MODEL), F32),
        mesh=_sc_mesh(),
        scratch_types=[
            pltpu.VMEM((SC_GATHER_ROWS,), jnp.int32),
            pltpu.VMEM((SC_GATHER_ROWS, D_MODEL), F32),
            pltpu.SemaphoreType.DMA,
        ],
        name="combine_gather",
    )
    def k(y_hbm, i_hbm, o_hbm, idx_v, rows_v, sem):
        base = _sc_worker() * per_w

        @pl.loop(0, n_chunks)
        def _(j):
            r0 = pl.multiple_of(base + j * SC_GATHER_ROWS, 8)
            pltpu.sync_copy(i_hbm.at[pl.ds(r0, SC_GATHER_ROWS)], idx_v)
            pltpu.async_copy(y_hbm.at[idx_v], rows_v, sem).wait()
            pltpu.sync_copy(rows_v, o_hbm.at[pl.ds(r0, SC_GATHER_ROWS)])

    return k(yb, dest_t.reshape(n))


def _expert_kernel(be_ref, x_ref, w1_ref, b1_ref, w2_ref, b2_ref, y_ref, w1b, w2b):
    j = pl.program_id(0)
    e = be_ref[j]
    prev = be_ref[jnp.maximum(j - 1, 0)]

    @pl.when((j == 0) | (e != prev))
    def _():
        w1b[...] = w1_ref[0].astype(BF16)
        w2b[...] = w2_ref[0].astype(BF16)

    @pl.when(e < N_EXPERTS)
    def _():
        gu = jnp.dot(x_ref[...].astype(BF16), w1b[...], preferred_element_type=F32) + b1_ref[0]
        gate = jnp.minimum(gu[:, :D_EXPERT], SWIGLU_LIMIT)
        lin = jnp.clip(gu[:, D_EXPERT:], -SWIGLU_LIMIT, SWIGLU_LIMIT)
        act = gate * jax.nn.sigmoid(SWIGLU_ALPHA * gate) * (lin + 1.0)
        y_ref[...] = jnp.dot(act.astype(BF16), w2b[...], preferred_element_type=F32) + b2_ref[0]


def _experts(block_e, rows, w1, b1, w2, b2):
    ex = lambda j, be: jnp.minimum(be[j], N_EXPERTS - 1)
    return pl.pallas_call(
        _expert_kernel,
        grid_spec=pltpu.PrefetchScalarGridSpec(
            num_scalar_prefetch=1,
            grid=(N_BLOCKS,),
            in_specs=[
                pl.BlockSpec((ROW_BLOCK, D_MODEL), lambda j, be: (j, 0)),
                pl.BlockSpec((1, D_MODEL, 2 * D_EXPERT), lambda j, be: (ex(j, be), 0, 0)),
                pl.BlockSpec((1, 1, 2 * D_EXPERT), lambda j, be: (ex(j, be), 0, 0)),
                pl.BlockSpec((1, D_EXPERT, D_MODEL), lambda j, be: (ex(j, be), 0, 0)),
                pl.BlockSpec((1, 1, D_MODEL), lambda j, be: (ex(j, be), 0, 0)),
            ],
            out_specs=pl.BlockSpec((ROW_BLOCK, D_MODEL), lambda j, be: (j, 0)),
            scratch_shapes=[
                pltpu.VMEM((D_MODEL, 2 * D_EXPERT), BF16),
                pltpu.VMEM((D_EXPERT, D_MODEL), BF16),
            ],
        ),
        out_shape=jax.ShapeDtypeStruct((N_ROWS, D_MODEL), F32),
        compiler_params=pltpu.CompilerParams(
            dimension_semantics=("arbitrary",), vmem_limit_bytes=VMEM_LIMIT),
        name="experts",
    )(block_e, rows, w1, b1.reshape(N_EXPERTS, 1, 2 * D_EXPERT), w2, b2.reshape(N_EXPERTS, 1, D_MODEL))


def _combine_kernel(x_ref, mod_ref, g_ref, p_ref, o_ref, *, tile_off):
    b = pl.program_id(0)
    i = pl.program_id(1) + tile_off
    row = _mod_row(i, b)
    p = p_ref[0]
    y = p[:, 0:1] * g_ref[0, 0]
    for k in range(1, TOP_K):
        y = y + p[:, k:k + 1] * g_ref[k, 0]
    o_ref[0] = x_ref[0] + mod_ref[5, pl.ds(row, 1), :] * y


def _combine_final_kernel(x_ref, mod_ref, g_ref, p_ref, fg_ref, o_ref, *, tile_off):
    _combine_kernel(x_ref, mod_ref, g_ref, p_ref, o_ref, tile_off=tile_off)
    o_ref[0] = _rms(o_ref[0], fg_ref[...])


def _combine(xa, mod, gathered, probs, final_g=None):
    last = final_g is not None
    tile_off = CTX_LEN // TOK_TILE if last else 0
    n_t = N_TILES - tile_off
    tok = lambda b, i: (b, i + tile_off, 0)
    in_specs = [
        pl.BlockSpec((1, TOK_TILE, D_MODEL), tok),
        pl.BlockSpec((6, 8, D_MODEL), lambda b, i: (0, 0, 0)),
        pl.BlockSpec((TOP_K, 1, TOK_TILE, D_MODEL), lambda b, i: (0, b, i + tile_off, 0)),
        pl.BlockSpec((1, TOK_TILE, TOP_K), tok),
    ]
    args = [xa, mod, gathered.reshape(TOP_K, BATCH, S_TOT, D_MODEL), probs]
    if last:
        in_specs.append(pl.BlockSpec((1, D_MODEL), lambda b, i: (0, 0)))
        args.append(final_g)
        body = functools.partial(_combine_final_kernel, tile_off=tile_off)
    else:
        body = functools.partial(_combine_kernel, tile_off=tile_off)
    return pl.pallas_call(
        body,
        grid=(BATCH, n_t),
        in_specs=in_specs,
        out_specs=pl.BlockSpec((1, TOK_TILE, D_MODEL), lambda b, i: (b, i, 0)),
        out_shape=jax.ShapeDtypeStruct((BATCH, n_t * TOK_TILE, D_MODEL), F32),
        compiler_params=pltpu.CompilerParams(dimension_semantics=("parallel", "parallel")),
        name="combine",
    )(*args)


def _rope_tables():
    n_rows = SEQ // GRID_W
    row = jnp.repeat(jnp.arange(n_rows), GRID_W).astype(F32)
    col = (jnp.arange(SEQ) % GRID_W).astype(F32)
    axis_dim = HEAD_DIM // 2
    inv_freq = 1.0 / (ROPE_BASE ** (jnp.arange(0, axis_dim, 2, dtype=F32) / axis_dim))
    ang_r = row[:, None] * inv_freq[None]
    ang_c = col[:, None] * inv_freq[None]
    ang = jnp.concatenate([ang_r, ang_r, ang_c, ang_c] * (LANES // HEAD_DIM), axis=-1)
    cos = jnp.concatenate([jnp.ones((CTX_LEN, LANES), F32), jnp.cos(ang)], axis=0)
    sin = jnp.concatenate([jnp.zeros((CTX_LEN, LANES), F32), jnp.sin(ang)], axis=0)
    return cos, sin


def kernel(x, c, ctx, c_ctx, w_ada, b_ada, norm1_g, norm2_g, w_in, ssm_lam_re, ssm_lam_im, ssm_log_dt, ssm_b_re, ssm_b_im, ssm_c_re, ssm_c_im, ssm_d, ssm_w_val, ssm_w_gate, attn_sink, attn_w_up, pool_w, pool_scale, w_out, router_w, router_b, exp_w1, exp_b1, exp_w2, exp_b2, final_g):
    assert x.shape == (BATCH, SEQ, D_MODEL) and ctx.shape == (BATCH, CTX_LEN, D_MODEL)
    cos_t, sin_t = _rope_tables()
    cv = jnp.concatenate([c, c_ctx[None], jnp.zeros((8 - BATCH - 1, D_MODEL), F32)], axis=0)
    mods = _ada(cv, w_ada, b_ada)
    xa = jnp.concatenate([ctx, x], axis=1)
    out = None
    for l in range(DEPTH):
        mod = mods[l]
        w_in_b = w_in[l].astype(BF16)
        u, kv, q, pool_in = _inproj(xa, mod, norm1_g[l][None], w_in_b[:, :OFF_GATE], cos_t, sin_t)
        wb, wc, ar, ai = _ssm_params(ssm_lam_re[l], ssm_lam_im[l], ssm_log_dt[l], ssm_b_re[l], ssm_b_im[l],
                                     ssm_c_re[l], ssm_c_im[l])
        yf, yb = _ssm(u, wb, wc, ar, ai)
        o = _attn(attn_sink[l], q, kv)
        xa, f, top_e, top_p = _merge(
            xa, mod, norm1_g[l][None], w_in_b[:, OFF_GATE:], yf, yb, u, ssm_d[l][None],
            ssm_w_val[l].astype(BF16), ssm_w_gate[l].astype(BF16), o, attn_w_up[l].astype(BF16),
            pool_in, pool_w[l].astype(BF16), pool_scale[l][None], w_out[l].astype(BF16), norm2_g[l][None],
            router_w[l], router_b[l][None])
        dest, block_e = _route(top_e.reshape(N_TOK, TOP_K))
        dest_t = dest.T
        rows = _dispatch(f.reshape(N_TOK, D_MODEL), dest_t)
        yrows = _experts(block_e[:N_BLOCKS, 0], rows, exp_w1[l], exp_b1[l], exp_w2[l], exp_b2[l])
        gathered = _combine_gather(yrows, dest_t)
        if l < DEPTH - 1:
            xa = _combine(xa, mod, gathered, top_p)
        else:
            out = _combine(xa, mod, gathered, top_p, final_g[None])
    return out
```

```python
import functools
import math

import jax
import jax.numpy as jnp
from jax import lax
from jax.experimental import pallas as pl
from jax.experimental.pallas import tpu as pltpu
from jax.experimental.pallas import tpu_sc as plsc

F32 = jnp.float32
BF16 = jnp.bfloat16
HIGHEST = lax.Precision.HIGHEST

D_MODEL = 1024
BATCH = 2
SEQ = 8192
DEPTH = 2
GRID_W = 64
CTX_LEN = 256
NORM_EPS = 1e-6
SSM_WIDTH = 512
SSM_GROUP = 16
SSM_GROUPS = 32
SSM_STATE = 64
HEAD_DIM = 64
N_Q_HEADS = 8
N_KV_HEADS = 2
GQA_GROUP = 4
ATTN_WIDTH = 512
KV_WIDTH = 128
WINDOW = 128
ROPE_BASE = 10000.0
ATTN_SCALE = HEAD_DIM ** -0.5
POOL_WINDOWS = (2, 4, 8, 16)
POOL_WIDTH = 512
POOL_GROUP = 128
POOL_OUT_GROUP = 256
OFF_K = 512
OFF_V = 640
OFF_Q = 768
OFF_POOL = 1280
OFF_GATE = 1792
N_EXPERTS = 32
TOP_K = 4
D_EXPERT = 1024
SWIGLU_LIMIT = 7.0
SWIGLU_ALPHA = 1.702

S_TOT = CTX_LEN + SEQ
N_TOK = BATCH * S_TOT
TOK_TILE = 256
N_TILES = S_TOT // TOK_TILE
ATTN_BLOCK = 128
ATTN_STACK = 2
POOL_HALO = 8
ROUTE_TILE = 512
ROW_BLOCK = 256
N_ROWS = -(-(N_TOK * TOP_K + N_EXPERTS * (ROW_BLOCK - 1)) // ROW_BLOCK) * ROW_BLOCK
N_BLOCKS = N_ROWS // ROW_BLOCK
N_BLOCKS_PAD = -(-N_BLOCKS // 8) * 8
SC_CORES = 2
SC_WORKERS = SC_CORES * 16
SC_SCATTER_ROWS = 48
SC_GATHER_ROWS = 32
assert N_TOK % (SC_WORKERS * SC_SCATTER_ROWS) == 0 and (N_TOK * TOP_K) % (SC_WORKERS * SC_GATHER_ROWS) == 0
VMEM_LIMIT = 56 * 1024 * 1024
LANES = 128
MASK_VALUE = -1e30


def _rms(x, g):
    return x * lax.rsqrt(jnp.mean(x * x, axis=-1, keepdims=True) + NORM_EPS) * g


def _sigmoid(x):
    return 0.5 * jnp.tanh(0.5 * x) + 0.5


def _mod_row(i, b):
    return jnp.where(i == 0, 2, b)


def _ada_kernel(cv_ref, w_ref, b_ref, o_ref):
    cv = cv_ref[...]
    s = cv * jax.nn.sigmoid(cv)
    o_ref[0, 0] = jnp.dot(s, w_ref[0], preferred_element_type=F32, precision=HIGHEST) + b_ref[0, 0]


def _ada(cv, w_ada, b_ada):
    return pl.pallas_call(
        _ada_kernel,
        grid=(DEPTH, 6),
        in_specs=[
            pl.BlockSpec((8, D_MODEL), lambda l, j: (0, 0)),
            pl.BlockSpec((1, D_MODEL, D_MODEL), lambda l, j: (l, 0, j)),
            pl.BlockSpec((1, 1, 1, D_MODEL), lambda l, j: (l, j, 0, 0)),
        ],
        out_specs=pl.BlockSpec((1, 1, 8, D_MODEL), lambda l, j: (l, j, 0, 0)),
        out_shape=jax.ShapeDtypeStruct((DEPTH, 6, 8, D_MODEL), F32),
        name="ada",
    )(cv, w_ada, b_ada.reshape(DEPTH, 6, 1, D_MODEL))


def _rope(t, cos, sin):
    lane = lax.broadcasted_iota(jnp.int32, t.shape, 1)
    rot = jnp.where((lane % 32) < 16, -pltpu.roll(t, LANES - 16, 1), pltpu.roll(t, 16, 1))
    return t * cos + rot * sin


def _inproj_kernel(x_ref, mod_ref, g_ref, w_ref, cos_ref, sin_ref, u_ref, kv_ref, q_ref, pool_ref):
    b = pl.program_id(0)
    i = pl.program_id(1)
    row = _mod_row(i, b)
    shift = mod_ref[0, pl.ds(row, 1), :]
    scale = mod_ref[1, pl.ds(row, 1), :]
    h = _rms(x_ref[0], g_ref[...]) * (1.0 + scale) + shift
    p = jnp.dot(h.astype(BF16), w_ref[...], preferred_element_type=F32)
    cos = cos_ref[...]
    sin = sin_ref[...]
    u_ref[0] = p[:, :OFF_K]
    kv_ref[0, :, :KV_WIDTH] = _rope(p[:, OFF_K:OFF_V], cos, sin).astype(BF16)
    kv_ref[0, :, KV_WIDTH:] = p[:, OFF_V:OFF_Q].astype(BF16)
    for c in range(ATTN_WIDTH // LANES):
        qc = p[:, OFF_Q + c * LANES:OFF_Q + (c + 1) * LANES]
        q_ref[0, :, c * LANES:(c + 1) * LANES] = (_rope(qc, cos, sin) * ATTN_SCALE).astype(BF16)
    pool_ref[0] = p[:, OFF_POOL:OFF_GATE]


def _inproj(xa, mod, g1, w1, cos_t, sin_t):
    tok = lambda b, i: (b, i, 0)
    return pl.pallas_call(
        _inproj_kernel,
        grid=(BATCH, N_TILES),
        in_specs=[
            pl.BlockSpec((1, TOK_TILE, D_MODEL), tok),
            pl.BlockSpec((6, 8, D_MODEL), lambda b, i: (0, 0, 0)),
            pl.BlockSpec((1, D_MODEL), lambda b, i: (0, 0)),
            pl.BlockSpec((D_MODEL, OFF_GATE), lambda b, i: (0, 0)),
            pl.BlockSpec((TOK_TILE, LANES), lambda b, i: (i, 0)),
            pl.BlockSpec((TOK_TILE, LANES), lambda b, i: (i, 0)),
        ],
        out_specs=[
            pl.BlockSpec((1, TOK_TILE, SSM_WIDTH), tok),
            pl.BlockSpec((1, TOK_TILE, 2 * KV_WIDTH), tok),
            pl.BlockSpec((1, TOK_TILE, ATTN_WIDTH), tok),
            pl.BlockSpec((1, TOK_TILE, POOL_WIDTH), tok),
        ],
        out_shape=[
            jax.ShapeDtypeStruct((BATCH, S_TOT, SSM_WIDTH), F32),
            jax.ShapeDtypeStruct((BATCH, S_TOT, 2 * KV_WIDTH), BF16),
            jax.ShapeDtypeStruct((BATCH, S_TOT, ATTN_WIDTH), BF16),
            jax.ShapeDtypeStruct((BATCH, S_TOT, POOL_WIDTH), F32),
        ],
        compiler_params=pltpu.CompilerParams(
            dimension_semantics=("parallel", "parallel"), vmem_limit_bytes=VMEM_LIMIT),
        name="inproj",
    )(xa, mod, g1, w1, cos_t, sin_t)


SSM_HALF_IN = SSM_WIDTH // 2
SSM_HALF_ST = SSM_GROUPS // 2 * SSM_STATE
N_STATE_ROWS = 8
N_LANE_TILES = SSM_HALF_ST // LANES
SCAN_STRIDE = TOK_TILE + 8


def _ssm_kernel(uf_ref, ub_ref, wb_ref, wc_ref, ar_ref, ai_ref, yf_ref, yb_ref, buf_f, buf_b, hf_ref, hb_ref):
    i = pl.program_id(0)
    T = TOK_TILE

    @pl.when(i == 0)
    def _():
        hf_ref[...] = jnp.zeros_like(hf_ref)
        hb_ref[...] = jnp.zeros_like(hb_ref)

    for d, (u_ref, buf) in enumerate(((uf_ref, buf_f), (ub_ref, buf_b))):
        for b in range(BATCH):
            for half in range(2):
                ub16 = u_ref[b, :, half * SSM_HALF_IN:(half + 1) * SSM_HALF_IN].astype(BF16)
                for reim in range(2):
                    k = reim * 4 + half * 2 + b
                    bu = jnp.dot(ub16, wb_ref[d, reim, half], preferred_element_type=F32)
                    for l in range(N_LANE_TILES):
                        buf[l, k * SCAN_STRIDE:k * SCAN_STRIDE + T, :] = bu[:, l * LANES:(l + 1) * LANES]

    def step(t, carry):
        hf, hb = carry
        tb = T - 1 - t
        nf, nb = [], []
        for l in range(N_LANE_TILES):
            sl = pl.ds(l * LANES, LANES)
            rows_f = pl.ds(t, N_STATE_ROWS, stride=SCAN_STRIDE)
            h = ar_ref[0, :, sl] * hf[l] + ai_ref[0, :, sl] * pltpu.roll(hf[l], 4, 0) + buf_f[l, rows_f, :]
            buf_f[l, rows_f, :] = h
            nf.append(h)
            rows_b = pl.ds(tb, N_STATE_ROWS, stride=SCAN_STRIDE)
            h = ar_ref[1, :, sl] * hb[l] + ai_ref[1, :, sl] * pltpu.roll(hb[l], 4, 0) + buf_b[l, rows_b, :]
            buf_b[l, rows_b, :] = h
            nb.append(h)
        return tuple(nf), tuple(nb)

    hf0 = tuple(hf_ref[:, l * LANES:(l + 1) * LANES] for l in range(N_LANE_TILES))
    hb0 = tuple(hb_ref[:, l * LANES:(l + 1) * LANES] for l in range(N_LANE_TILES))
    hf1, hb1 = lax.fori_loop(0, T, step, (hf0, hb0), unroll=2)
    for l in range(N_LANE_TILES):
        hf_ref[:, l * LANES:(l + 1) * LANES] = hf1[l]
        hb_ref[:, l * LANES:(l + 1) * LANES] = hb1[l]

    for d, (buf, y_ref) in enumerate(((buf_f, yf_ref), (buf_b, yb_ref))):
        for b in range(BATCH):
            for half in range(2):
                kre = half * 2 + b
                kim = 4 + half * 2 + b
                state = lambda k: jnp.concatenate(
                    [buf[l, k * SCAN_STRIDE:k * SCAN_STRIDE + T, :] for l in range(N_LANE_TILES)],
                    axis=1).astype(BF16)
                y = jnp.dot(state(kre), wc_ref[d, 0, half], preferred_element_type=F32)
                y = y + jnp.dot(state(kim), wc_ref[d, 1, half], preferred_element_type=F32)
                y_ref[b, :, half * SSM_HALF_IN:(half + 1) * SSM_HALF_IN] = y


def _ssm(u, wb, wc, ar, ai):
    fwd = lambda i: (0, i, 0)
    bwd = lambda i: (0, jnp.where(i == 0, 0, N_TILES - i), 0)
    return pl.pallas_call(
        _ssm_kernel,
        grid=(N_TILES,),
        in_specs=[
            pl.BlockSpec((BATCH, TOK_TILE, SSM_WIDTH), fwd),
            pl.BlockSpec((BATCH, TOK_TILE, SSM_WIDTH), bwd),
            pl.BlockSpec((2, 2, 2, SSM_HALF_IN, SSM_HALF_ST), lambda i: (0, 0, 0, 0, 0)),
            pl.BlockSpec((2, 2, 2, SSM_HALF_ST, SSM_HALF_IN), lambda i: (0, 0, 0, 0, 0)),
            pl.BlockSpec((2, N_STATE_ROWS, SSM_HALF_ST), lambda i: (0, 0, 0)),
            pl.BlockSpec((2, N_STATE_ROWS, SSM_HALF_ST), lambda i: (0, 0, 0)),
        ],
        out_specs=[
            pl.BlockSpec((BATCH, TOK_TILE, SSM_WIDTH), fwd),
            pl.BlockSpec((BATCH, TOK_TILE, SSM_WIDTH), bwd),
        ],
        out_shape=[jax.ShapeDtypeStruct((BATCH, S_TOT, SSM_WIDTH), F32)] * 2,
        scratch_shapes=[
            pltpu.VMEM((N_LANE_TILES, N_STATE_ROWS * SCAN_STRIDE, LANES), F32),
            pltpu.VMEM((N_LANE_TILES, N_STATE_ROWS * SCAN_STRIDE, LANES), F32),
            pltpu.VMEM((N_STATE_ROWS, SSM_HALF_ST), F32),
            pltpu.VMEM((N_STATE_ROWS, SSM_HALF_ST), F32),
        ],
        compiler_params=pltpu.CompilerParams(
            dimension_semantics=("arbitrary",), vmem_limit_bytes=VMEM_LIMIT),
        name="ssm",
    )(u, u, wb, wc, ar, ai)


def _ssm_params(lam_re, lam_im, log_dt, b_re, b_im, c_re, c_im):
    lr = lam_re.astype(F32)
    li = lam_im.astype(F32)
    dt = jnp.exp(log_dt.astype(F32))[..., None]
    mag = jnp.exp(lr * dt)
    ang = li * dt
    a_re = mag * jnp.cos(ang)
    a_im = mag * jnp.sin(ang)
    den = lr * lr + li * li
    k_re = ((a_re - 1.0) * lr + a_im * li) / den
    k_im = (a_im * lr - (a_re - 1.0) * li) / den
    bb_re = k_re[..., None] * b_re - k_im[..., None] * b_im
    bb_im = k_re[..., None] * b_im + k_im[..., None] * b_re
    eye = jnp.eye(SSM_GROUPS // 2, dtype=F32)

    def drive(bb):
        blk = jnp.swapaxes(bb, -1, -2).reshape(2, 2, SSM_GROUPS // 2, SSM_GROUP, SSM_STATE)
        return jnp.einsum('dsghp,gk->dsghkp', blk, eye).reshape(2, 2, SSM_HALF_IN, SSM_HALF_ST)

    def readout(c):
        blk = jnp.swapaxes(c, -1, -2).reshape(2, 2, SSM_GROUPS // 2, SSM_STATE, SSM_GROUP)
        return jnp.einsum('dsgph,gk->dsgpkh', blk, eye).reshape(2, 2, SSM_HALF_ST, SSM_HALF_IN)

    wb = jnp.stack([drive(bb_re), drive(bb_im)], axis=1).astype(BF16)
    wc = jnp.stack([readout(c_re.astype(F32)), readout(-c_im.astype(F32))], axis=1).astype(BF16)

    def rows(a, sign_re, sign_im):
        halves = a.reshape(2, 2, SSM_HALF_ST)
        per = []
        for reim in range(2):
            for half in range(2):
                for _ in range(BATCH):
                    per.append(halves[:, half] * (sign_re if reim == 0 else sign_im))
        return jnp.stack(per, axis=1)

    ar = rows(a_re, 1.0, 1.0)
    ai = rows(a_im, -1.0, 1.0)
    return wb, wc, ar, ai


def _attn_kernel(sink_ref, band_ref, q_ref, kc_ref, kp_ref, kn0_ref, kn_ref, o_ref):
    j = pl.program_id(1)
    n = j - CTX_LEN // ATTN_BLOCK
    q = q_ref[0]
    kvc = kc_ref[0]
    band = jnp.concatenate([kp_ref[0], kn0_ref[0], kn_ref[0]], axis=0)
    nb = band.shape[0]
    last = SEQ // ATTN_BLOCK - 1
    col = lax.broadcasted_iota(jnp.int32, (1, nb), 1)
    lo_c = jnp.where(n < 0, nb, jnp.where(n == 0, ATTN_BLOCK, 0))
    hi_c = jnp.where(n == last, 2 * ATTN_BLOCK, nb)
    bias = band_ref[...] + jnp.where((col >= lo_c) & (col < hi_c), 0.0, MASK_VALUE)
    bias = jnp.concatenate([bias] * ATTN_STACK, axis=0)
    nt = (((1,), (1,)), ((), ()))
    outs = []
    for hk in range(N_KV_HEADS):
        k_b = band[:, hk * HEAD_DIM:(hk + 1) * HEAD_DIM]
        v_b = band[:, KV_WIDTH + hk * HEAD_DIM:KV_WIDTH + (hk + 1) * HEAD_DIM]
        k_c = kvc[:, hk * HEAD_DIM:(hk + 1) * HEAD_DIM]
        v_c = kvc[:, KV_WIDTH + hk * HEAD_DIM:KV_WIDTH + (hk + 1) * HEAD_DIM]
        for g0 in range(0, GQA_GROUP, ATTN_STACK):
            heads = range(hk * GQA_GROUP + g0, hk * GQA_GROUP + g0 + ATTN_STACK)
            qs = jnp.concatenate([q[:, hq * HEAD_DIM:(hq + 1) * HEAD_DIM] for hq in heads], axis=0)
            sink = jnp.concatenate([jnp.full((ATTN_BLOCK, 1), sink_ref[hq], F32) for hq in heads], axis=0)
            s_w = lax.dot_general(qs, k_b, nt, preferred_element_type=F32) + bias
            s_c = lax.dot_general(qs, k_c, nt, preferred_element_type=F32)
            m = jnp.maximum(jnp.maximum(jnp.max(s_w, axis=-1, keepdims=True),
                                        jnp.max(s_c, axis=-1, keepdims=True)), sink)
            e_w = jnp.exp(s_w - m)
            e_c = jnp.exp(s_c - m)
            den = (jnp.sum(e_w, axis=-1, keepdims=True) + jnp.sum(e_c, axis=-1, keepdims=True)
                   + jnp.exp(sink - m))
            o = jnp.dot(e_w.astype(BF16), v_b, preferred_element_type=F32)
            o = (o + jnp.dot(e_c.astype(BF16), v_c, preferred_element_type=F32)) / den
            outs.extend(o[g * ATTN_BLOCK:(g + 1) * ATTN_BLOCK] for g in range(ATTN_STACK))
    o_ref[0] = jnp.concatenate(outs, axis=1).astype(BF16)


def _attn(sink, q, kv):
    nq = S_TOT // ATTN_BLOCK
    first = CTX_LEN // ATTN_BLOCK
    clamp = lambda j: jnp.clip(j, first, nq - 1)
    r = jnp.arange(ATTN_BLOCK)[:, None]
    c = jnp.arange(3 * ATTN_BLOCK)[None, :] - ATTN_BLOCK
    band_bias = jnp.where(jnp.abs(c - r) <= WINDOW, 0.0, MASK_VALUE).astype(F32)
    return pl.pallas_call(
        _attn_kernel,
        grid=(BATCH, nq),
        in_specs=[
            pl.BlockSpec(memory_space=pltpu.SMEM),
            pl.BlockSpec((ATTN_BLOCK, 3 * ATTN_BLOCK), lambda b, j: (0, 0)),
            pl.BlockSpec((1, ATTN_BLOCK, ATTN_WIDTH), lambda b, j: (b, j, 0)),
            pl.BlockSpec((1, CTX_LEN, 2 * KV_WIDTH), lambda b, j: (b, 0, 0)),
            pl.BlockSpec((1, ATTN_BLOCK, 2 * KV_WIDTH), lambda b, j: (b, clamp(j - 1), 0)),
            pl.BlockSpec((1, ATTN_BLOCK, 2 * KV_WIDTH), lambda b, j: (b, clamp(j), 0)),
            pl.BlockSpec((1, ATTN_BLOCK, 2 * KV_WIDTH), lambda b, j: (b, clamp(j + 1), 0)),
        ],
        out_specs=pl.BlockSpec((1, ATTN_BLOCK, ATTN_WIDTH), lambda b, j: (b, j, 0)),
        out_shape=jax.ShapeDtypeStruct((BATCH, S_TOT, ATTN_WIDTH), BF16),
        compiler_params=pltpu.CompilerParams(dimension_semantics=("parallel", "parallel")),
        name="attn",
    )(sink, band_bias, q, kv, kv, kv, kv)


def _gelu_tanh(x):
    return 0.5 * x * (1.0 + jnp.tanh(math.sqrt(2.0 / math.pi) * (x + 0.044715 * (x * x * x))))


def _pool_means(ext, i):
    T = TOK_TILE
    n_ext = T + 2 * POOL_HALO
    seq_len = jnp.where(i == 0, CTX_LEN, SEQ)
    start = jnp.where(i == 0, 0, (i - 1) * T)
    r = start - POOL_HALO + lax.broadcasted_iota(jnp.int32, (n_ext, 1), 0)
    ext = jnp.where((r >= 0) & (r < seq_len), ext, 0.0)
    tl = start + lax.broadcasted_iota(jnp.int32, (T, 1), 0)
    back = lambda a, s: pltpu.roll(a, s, 0)
    fwd = lambda a, s: pltpu.roll(a, n_ext - s, 0)
    diffs = []
    for gi, w in enumerate(POOL_WINDOWS):
        e = ext[:, gi * POOL_GROUP:(gi + 1) * POOL_GROUP]
        lo = w // 2
        hi = w - 1 - lo
        s = e + back(e, 1)
        span = 2
        while span < w:
            s = s + back(s, span)
            span *= 2
        if w > 2:
            s = fwd(s, hi)
        cnt = (jnp.clip(tl + hi + 1, 0, seq_len) - jnp.clip(tl - lo, 0, seq_len)).astype(F32)
        mean = s[POOL_HALO:POOL_HALO + T] / cnt
        diffs.append((mean - e[POOL_HALO:POOL_HALO + T]).astype(BF16))
    return diffs


def _merge_kernel(x_ref, mod_ref, g1_ref, wg_ref, yf_ref, yb_ref, u_ref, dsk_ref, wval_ref, wgate_ref,
                  o_ref, wup_ref, pc_ref, pp_ref, pn_ref, pw_ref, ps_ref, wout_ref, g2_ref, rw_ref, rb_ref,
                  xo_ref, f_ref, te_ref, tp_ref):
    b = pl.program_id(0)
    i = pl.program_id(1)
    row = _mod_row(i, b)
    mods = [mod_ref[j, pl.ds(row, 1), :] for j in range(6)]
    x = x_ref[0]
    h = (_rms(x, g1_ref[...]) * (1.0 + mods[1]) + mods[0]).astype(BF16)
    gates = _sigmoid(jnp.dot(h, wg_ref[...], preferred_element_type=F32))

    z = yf_ref[0] + yb_ref[0] + dsk_ref[...] * u_ref[0]
    z = _gelu_tanh(z).astype(BF16)
    y_ssm = (jnp.dot(z, wval_ref[...], preferred_element_type=F32)
             * _sigmoid(jnp.dot(z, wgate_ref[...], preferred_element_type=F32)))
    y_attn = jnp.dot(o_ref[0], wup_ref[...], preferred_element_type=F32)

    ext = jnp.concatenate([pp_ref[0], pc_ref[0], pn_ref[0]], axis=0)
    diffs = _pool_means(ext, i)
    y_pool = jnp.concatenate(
        [jnp.dot(diffs[gi], pw_ref[gi], preferred_element_type=F32) for gi in range(len(POOL_WINDOWS))],
        axis=1) * ps_ref[...]

    m = (gates[:, :D_MODEL] * y_ssm + gates[:, D_MODEL:2 * D_MODEL] * y_attn
         + gates[:, 2 * D_MODEL:] * y_pool)
    xo = x + mods[2] * jnp.dot(m.astype(BF16), wout_ref[...], preferred_element_type=F32)
    xo_ref[0] = xo
    f = _rms(xo, g2_ref[...]) * (1.0 + mods[4]) + mods[3]
    f_ref[0] = f

    f_hi = f.astype(BF16)
    f_lo = (f - f_hi.astype(F32)).astype(BF16)
    r_hi = jnp.dot(f_hi, rw_ref[...], preferred_element_type=F32)
    r_lo = jnp.dot(f_lo, rw_ref[...], preferred_element_type=F32)
    logits = r_hi[:, :N_EXPERTS] + r_hi[:, N_EXPERTS:] + r_lo[:, :N_EXPERTS] + rb_ref[...]
    lane = lax.broadcasted_iota(jnp.int32, logits.shape, 1)
    vals, idxs = [], []
    for _ in range(TOP_K):
        mx = jnp.max(logits, axis=-1, keepdims=True)
        ix = jnp.min(jnp.where(logits == mx, lane, N_EXPERTS), axis=-1, keepdims=True)
        vals.append(mx)
        idxs.append(ix)
        logits = jnp.where(lane == ix, -jnp.inf, logits)
    ex = [jnp.exp(v - vals[0]) for v in vals]
    tot = ex[0] + ex[1] + ex[2] + ex[3]
    te_ref[0] = jnp.concatenate(idxs, axis=1)
    tp_ref[0] = jnp.concatenate(ex, axis=1) / tot


def _merge(xa, mod, g1, wg, yf, yb, u, dsk, wval, wgate, o, wup, pool_in, pw, ps, wout, g2, rw, rb):
    tok = lambda b, i: (b, i, 0)
    const2 = lambda b, i: (0, 0)
    const3 = lambda b, i: (0, 0, 0)
    per_tile = TOK_TILE // POOL_HALO
    n_halo = S_TOT // POOL_HALO
    return pl.pallas_call(
        _merge_kernel,
        grid=(BATCH, N_TILES),
        in_specs=[
            pl.BlockSpec((1, TOK_TILE, D_MODEL), tok),
            pl.BlockSpec((6, 8, D_MODEL), const3),
            pl.BlockSpec((1, D_MODEL), const2),
            pl.BlockSpec((D_MODEL, 3 * D_MODEL), const2),
            pl.BlockSpec((1, TOK_TILE, SSM_WIDTH), tok),
            pl.BlockSpec((1, TOK_TILE, SSM_WIDTH), tok),
            pl.BlockSpec((1, TOK_TILE, SSM_WIDTH), tok),
            pl.BlockSpec((1, SSM_WIDTH), const2),
            pl.BlockSpec((SSM_WIDTH, D_MODEL), const2),
            pl.BlockSpec((SSM_WIDTH, D_MODEL), const2),
            pl.BlockSpec((1, TOK_TILE, ATTN_WIDTH), tok),
            pl.BlockSpec((ATTN_WIDTH, D_MODEL), const2),
            pl.BlockSpec((1, TOK_TILE, POOL_WIDTH), tok),
            pl.BlockSpec((1, POOL_HALO, POOL_WIDTH), lambda b, i: (b, jnp.maximum(i * per_tile - 1, 0), 0)),
            pl.BlockSpec((1, POOL_HALO, POOL_WIDTH),
                         lambda b, i: (b, jnp.minimum((i + 1) * per_tile, n_halo - 1), 0)),
            pl.BlockSpec((len(POOL_WINDOWS), POOL_GROUP, POOL_OUT_GROUP), const3),
            pl.BlockSpec((1, D_MODEL), const2),
            pl.BlockSpec((D_MODEL, D_MODEL), const2),
            pl.BlockSpec((1, D_MODEL), const2),
            pl.BlockSpec((D_MODEL, 2 * N_EXPERTS), const2),
            pl.BlockSpec((1, N_EXPERTS), const2),
        ],
        out_specs=[
            pl.BlockSpec((1, TOK_TILE, D_MODEL), tok),
            pl.BlockSpec((1, TOK_TILE, D_MODEL), tok),
            pl.BlockSpec((1, TOK_TILE, TOP_K), tok),
            pl.BlockSpec((1, TOK_TILE, TOP_K), tok),
        ],
        out_shape=[
            jax.ShapeDtypeStruct((BATCH, S_TOT, D_MODEL), F32),
            jax.ShapeDtypeStruct((BATCH, S_TOT, D_MODEL), F32),
            jax.ShapeDtypeStruct((BATCH, S_TOT, TOP_K), jnp.int32),
            jax.ShapeDtypeStruct((BATCH, S_TOT, TOP_K), F32),
        ],
        compiler_params=pltpu.CompilerParams(
            dimension_semantics=("parallel", "parallel"), vmem_limit_bytes=VMEM_LIMIT),
        name="merge",
    )(xa, mod, g1, wg, yf, yb, u, dsk, wval, wgate, o, wup, pool_in, pool_in, pool_in, pw, ps, wout, g2, rw, rb)


def _route_kernel(te_ref, dest_ref, be_ref, cnt_ref, carry_ref):
    ph = pl.program_id(0)
    i = pl.program_id(1)
    te = te_ref[...]
    lane = lax.broadcasted_iota(jnp.int32, (ROUTE_TILE, LANES), 1)
    ohs = [(te[:, k:k + 1] == lane).astype(F32) for k in range(TOP_K)]
    oh_all = ohs[0] + ohs[1] + ohs[2] + ohs[3]
    tile_cnt = jnp.sum(oh_all, axis=0, keepdims=True)

    @pl.when((ph == 0) & (i == 0))
    def _():
        cnt_ref[...] = jnp.zeros_like(cnt_ref)

    @pl.when(ph == 0)
    def _():
        cnt_ref[...] += tile_cnt

    @pl.when((ph == 1) & (i == 0))
    def _():
        cnt = jnp.broadcast_to(cnt_ref[...], (8, LANES))
        padded = jnp.floor((cnt + (ROW_BLOCK - 1)) / ROW_BLOCK) * ROW_BLOCK
        l8 = lax.broadcasted_iota(jnp.int32, (8, LANES), 1)
        end = padded
        s = 1
        while s < N_EXPERTS:
            end = end + jnp.where(l8 >= s, pltpu.roll(end, s, 1), 0.0)
            s *= 2
        carry_ref[...] = (end - padded)[0:1]
        blk_row = (lax.broadcasted_iota(jnp.int32, (N_BLOCKS_PAD, LANES), 0) * ROW_BLOCK).astype(F32)
        lb = lax.broadcasted_iota(jnp.int32, (N_BLOCKS_PAD, LANES), 1)
        hit = jnp.where((lb < N_EXPERTS) & (jnp.broadcast_to(end[0:1], blk_row.shape) <= blk_row), 1.0, 0.0)
        be = jnp.sum(hit, axis=-1, keepdims=True)
        be_ref[...] = jnp.broadcast_to(be, (N_BLOCKS_PAD, LANES)).astype(jnp.int32)

    @pl.when(ph == 1)
    def _():
        r = lax.broadcasted_iota(jnp.int32, (ROUTE_TILE, ROUTE_TILE), 0)
        c = lax.broadcasted_iota(jnp.int32, (ROUTE_TILE, ROUTE_TILE), 1)
        below = jnp.where(r > c, 1.0, 0.0).astype(BF16)
        rank = carry_ref[...] + jnp.dot(below, oh_all.astype(BF16), preferred_element_type=F32)
        dests = [jnp.sum(ohs[k] * rank, axis=-1, keepdims=True) for k in range(TOP_K)]
        dest_ref[...] = jnp.concatenate(dests, axis=1).astype(jnp.int32)
        carry_ref[...] += tile_cnt


def _route(top_e):
    n_steps = N_TOK // ROUTE_TILE
    return pl.pallas_call(
        _route_kernel,
        grid=(2, n_steps),
        in_specs=[pl.BlockSpec((ROUTE_TILE, TOP_K), lambda ph, i: (i, 0))],
        out_specs=[
            pl.BlockSpec((ROUTE_TILE, TOP_K), lambda ph, i: (ph * i, 0)),
            pl.BlockSpec((N_BLOCKS_PAD, LANES), lambda ph, i: (0, 0)),
        ],
        out_shape=[
            jax.ShapeDtypeStruct((N_TOK, TOP_K), jnp.int32),
            jax.ShapeDtypeStruct((N_BLOCKS_PAD, LANES), jnp.int32),
        ],
        scratch_shapes=[pltpu.VMEM((1, LANES), F32), pltpu.VMEM((1, LANES), F32)],
        compiler_params=pltpu.CompilerParams(dimension_semantics=("arbitrary", "arbitrary")),
        name="route",
    )(top_e)


def _sc_mesh():
    return plsc.VectorSubcoreMesh(core_axis_name="c", subcore_axis_name="s")


def _sc_worker():
    return lax.axis_index("s") * SC_CORES + lax.axis_index("c")


def _dispatch(f, dest_t):
    per_w = N_TOK // SC_WORKERS
    n_chunks = per_w // SC_SCATTER_ROWS

    @functools.partial(
        pl.kernel,
        out_type=jax.ShapeDtypeStruct((N_ROWS, D_MODEL), F32),
        mesh=_sc_mesh(),
        scratch_types=[
            pltpu.VMEM((SC_SCATTER_ROWS,), jnp.int32),
            pltpu.VMEM((SC_SCATTER_ROWS, D_MODEL), F32),
            pltpu.SemaphoreType.DMA,
        ],
        name="dispatch",
    )
    def k(f_hbm, i_hbm, o_hbm, idx_v, rows_v, sem):
        base = _sc_worker() * per_w

        @pl.loop(0, n_chunks)
        def _(j):
            t0 = pl.multiple_of(base + j * SC_SCATTER_ROWS, 8)
            pltpu.sync_copy(f_hbm.at[pl.ds(t0, SC_SCATTER_ROWS)], rows_v)
            for kk in range(TOP_K):
                pltpu.sync_copy(i_hbm.at[pl.ds(kk * N_TOK + t0, SC_SCATTER_ROWS)], idx_v)
                pltpu.async_copy(rows_v, o_hbm.at[idx_v], sem).wait()

    return k(f, dest_t.reshape(TOP_K * N_TOK))


def _combine_gather(yb, dest_t):
    n = TOP_K * N_TOK
    per_w = n // SC_WORKERS
    n_chunks = per_w // SC_GATHER_ROWS

    @functools.partial(
        pl.kernel,
        out_type=jax.ShapeDtypeStruct((n, D_MODEL), F32),
        mesh=_sc_mesh(),
        scratch_types=[
            pltpu.VMEM((SC_GATHER_ROWS,), jnp.int32),
            pltpu.VMEM((SC_GATHER_ROWS, D_MODEL), F32),
            pltpu.SemaphoreType.DMA,
        ],
        name="combine_gather",
    )
    def k(y_hbm, i_hbm, o_hbm, idx_v, rows_v, sem):
        base = _sc_worker() * per_w

        @pl.loop(0, n_chunks)
        def _(j):
            r0 = pl.multiple_of(base + j * SC_GATHER_ROWS, 8)
            pltpu.sync_copy(i_hbm.at[pl.ds(r0, SC_GATHER_ROWS)], idx_v)
            pltpu.async_copy(y_hbm.at[idx_v], rows_v, sem).wait()
            pltpu.sync_copy(rows_v, o_hbm.at[pl.ds(r0, SC_GATHER_ROWS)])

    return k(yb, dest_t.reshape(n))


def _expert_kernel(be_ref, x_ref, w1_ref, b1_ref, w2_ref, b2_ref, y_ref, w1b, w2b):
    j = pl.program_id(0)
    e = be_ref[j]
    prev = be_ref[jnp.maximum(j - 1, 0)]

    @pl.when((j == 0) | (e != prev))
    def _():
        w1b[...] = w1_ref[0].astype(BF16)
        w2b[...] = w2_ref[0].astype(BF16)

    @pl.when(e < N_EXPERTS)
    def _():
        gu = jnp.dot(x_ref[...].astype(BF16), w1b[...], preferred_element_type=F32) + b1_ref[0]
        gate = jnp.minimum(gu[:, :D_EXPERT], SWIGLU_LIMIT)
        lin = jnp.clip(gu[:, D_EXPERT:], -SWIGLU_LIMIT, SWIGLU_LIMIT)
        act = gate * _sigmoid(SWIGLU_ALPHA * gate) * (lin + 1.0)
        y_ref[...] = jnp.dot(act.astype(BF16), w2b[...], preferred_element_type=F32) + b2_ref[0]


def _experts(block_e, rows, w1, b1, w2, b2, layer):
    ex = lambda j, be: layer * N_EXPERTS + jnp.minimum(be[j], N_EXPERTS - 1)
    return pl.pallas_call(
        _expert_kernel,
        grid_spec=pltpu.PrefetchScalarGridSpec(
            num_scalar_prefetch=1,
            grid=(N_BLOCKS,),
            in_specs=[
                pl.BlockSpec((ROW_BLOCK, D_MODEL), lambda j, be: (j, 0)),
                pl.BlockSpec((1, D_MODEL, 2 * D_EXPERT), lambda j, be: (ex(j, be), 0, 0)),
                pl.BlockSpec((1, 1, 2 * D_EXPERT), lambda j, be: (ex(j, be), 0, 0)),
                pl.BlockSpec((1, D_EXPERT, D_MODEL), lambda j, be: (ex(j, be), 0, 0)),
                pl.BlockSpec((1, 1, D_MODEL), lambda j, be: (ex(j, be), 0, 0)),
            ],
            out_specs=pl.BlockSpec((ROW_BLOCK, D_MODEL), lambda j, be: (j, 0)),
            scratch_shapes=[
                pltpu.VMEM((D_MODEL, 2 * D_EXPERT), BF16),
                pltpu.VMEM((D_EXPERT, D_MODEL), BF16),
            ],
        ),
        out_shape=jax.ShapeDtypeStruct((N_ROWS, D_MODEL), F32),
        compiler_params=pltpu.CompilerParams(
            dimension_semantics=("arbitrary",), vmem_limit_bytes=VMEM_LIMIT),
        name="experts",
    )(block_e, rows, w1, b1, w2, b2)


def _combine_kernel(x_ref, mod_ref, g_ref, p_ref, o_ref, *, tile_off):
    b = pl.program_id(0)
    i = pl.program_id(1) + tile_off
    row = _mod_row(i, b)
    p = p_ref[0]
    y = p[:, 0:1] * g_ref[0, 0]
    for k in range(1, TOP_K):
        y = y + p[:, k:k + 1] * g_ref[k, 0]
    o_ref[0] = x_ref[0] + mod_ref[5, pl.ds(row, 1), :] * y


def _combine_final_kernel(x_ref, mod_ref, g_ref, p_ref, fg_ref, o_ref, *, tile_off):
    _combine_kernel(x_ref, mod_ref, g_ref, p_ref, o_ref, tile_off=tile_off)
    o_ref[0] = _rms(o_ref[0], fg_ref[...])


def _combine(xa, mod, gathered, probs, final_g=None):
    last = final_g is not None
    tile_off = CTX_LEN // TOK_TILE if last else 0
    n_t = N_TILES - tile_off
    tok = lambda b, i: (b, i + tile_off, 0)
    in_specs = [
        pl.BlockSpec((1, TOK_TILE, D_MODEL), tok),
        pl.BlockSpec((6, 8, D_MODEL), lambda b, i: (0, 0, 0)),
        pl.BlockSpec((TOP_K, 1, TOK_TILE, D_MODEL), lambda b, i: (0, b, i + tile_off, 0)),
        pl.BlockSpec((1, TOK_TILE, TOP_K), tok),
    ]
    args = [xa, mod, gathered.reshape(TOP_K, BATCH, S_TOT, D_MODEL), probs]
    if last:
        in_specs.append(pl.BlockSpec((1, D_MODEL), lambda b, i: (0, 0)))
        args.append(final_g)
        body = functools.partial(_combine_final_kernel, tile_off=tile_off)
    else:
        body = functools.partial(_combine_kernel, tile_off=tile_off)
    return pl.pallas_call(
        body,
        grid=(BATCH, n_t),
        in_specs=in_specs,
        out_specs=pl.BlockSpec((1, TOK_TILE, D_MODEL), lambda b, i: (b, i, 0)),
        out_shape=jax.ShapeDtypeStruct((BATCH, n_t * TOK_TILE, D_MODEL), F32),
        compiler_params=pltpu.CompilerParams(dimension_semantics=("parallel", "parallel")),
        name="combine",
    )(*args)


def _split_bf16(w):
    bits = lax.bitcast_convert_type(w.astype(F32), jnp.uint32) & jnp.uint32(0xFFFF0000)
    hi = lax.bitcast_convert_type(bits, F32)
    return jnp.concatenate([hi.astype(BF16), (w - hi).astype(BF16)], axis=-1)


def _rope_tables():
    n_rows = SEQ // GRID_W
    row = jnp.repeat(jnp.arange(n_rows), GRID_W).astype(F32)
    col = (jnp.arange(SEQ) % GRID_W).astype(F32)
    axis_dim = HEAD_DIM // 2
    inv_freq = 1.0 / (ROPE_BASE ** (jnp.arange(0, axis_dim, 2, dtype=F32) / axis_dim))
    ang_r = row[:, None] * inv_freq[None]
    ang_c = col[:, None] * inv_freq[None]
    ang = jnp.concatenate([ang_r, ang_r, ang_c, ang_c] * (LANES // HEAD_DIM), axis=-1)
    cos = jnp.concatenate([jnp.ones((CTX_LEN, LANES), F32), jnp.cos(ang)], axis=0)
    sin = jnp.concatenate([jnp.zeros((CTX_LEN, LANES), F32), jnp.sin(ang)], axis=0)
    return cos, sin


def kernel(x, c, ctx, c_ctx, w_ada, b_ada, norm1_g, norm2_g, w_in, ssm_lam_re, ssm_lam_im, ssm_log_dt, ssm_b_re, ssm_b_im, ssm_c_re, ssm_c_im, ssm_d, ssm_w_val, ssm_w_gate, attn_sink, attn_w_up, pool_w, pool_scale, w_out, router_w, router_b, exp_w1, exp_b1, exp_w2, exp_b2, final_g):
    assert x.shape == (BATCH, SEQ, D_MODEL) and ctx.shape == (BATCH, CTX_LEN, D_MODEL)
    cos_t, sin_t = _rope_tables()
    cv = jnp.concatenate([c, c_ctx[None], jnp.zeros((8 - BATCH - 1, D_MODEL), F32)], axis=0)
    mods = _ada(cv, w_ada, b_ada)
    xa = jnp.concatenate([ctx, x], axis=1)
    ew1 = exp_w1.reshape(DEPTH * N_EXPERTS, D_MODEL, 2 * D_EXPERT)
    eb1 = exp_b1.reshape(DEPTH * N_EXPERTS, 1, 2 * D_EXPERT)
    ew2 = exp_w2.reshape(DEPTH * N_EXPERTS, D_EXPERT, D_MODEL)
    eb2 = exp_b2.reshape(DEPTH * N_EXPERTS, 1, D_MODEL)
    out = None
    for l in range(DEPTH):
        mod = mods[l]
        w_in_b = w_in[l].astype(BF16)
        u, kv, q, pool_in = _inproj(xa, mod, norm1_g[l][None], w_in_b[:, :OFF_GATE], cos_t, sin_t)
        wb, wc, ar, ai = _ssm_params(ssm_lam_re[l], ssm_lam_im[l], ssm_log_dt[l], ssm_b_re[l], ssm_b_im[l],
                                     ssm_c_re[l], ssm_c_im[l])
        yf, yb = _ssm(u, wb, wc, ar, ai)
        o = _attn(attn_sink[l], q, kv)
        xa, f, top_e, top_p = _merge(
            xa, mod, norm1_g[l][None], w_in_b[:, OFF_GATE:], yf, yb, u, ssm_d[l][None],
            ssm_w_val[l].astype(BF16), ssm_w_gate[l].astype(BF16), o, attn_w_up[l].astype(BF16),
            pool_in, pool_w[l].astype(BF16), pool_scale[l][None], w_out[l].astype(BF16), norm2_g[l][None],
            _split_bf16(router_w[l]), router_b[l][None])
        dest, block_e = _route(top_e.reshape(N_TOK, TOP_K))
        dest_t = dest.T
        rows = _dispatch(f.reshape(N_TOK, D_MODEL), dest_t)
        yrows = _experts(block_e[:N_BLOCKS, 0], rows, ew1, eb1, ew2, eb2, l)
        gathered = _combine_gather(yrows, dest_t)
        if l < DEPTH - 1:
            xa = _combine(xa, mod, gathered, top_p)
        else:
            out = _combine(xa, mod, gathered, top_p, final_g[None])
    return out
```

```python
import functools
import math

import jax
import jax.numpy as jnp
from jax import lax
from jax.experimental import pallas as pl
from jax.experimental.pallas import tpu as pltpu
from jax.experimental.pallas import tpu_sc as plsc

F32 = jnp.float32
BF16 = jnp.bfloat16
HIGHEST = lax.Precision.HIGHEST

D_MODEL = 1024
BATCH = 2
SEQ = 8192
DEPTH = 2
GRID_W = 64
CTX_LEN = 256
NORM_EPS = 1e-6
SSM_WIDTH = 512
SSM_GROUP = 16
SSM_GROUPS = 32
SSM_STATE = 64
HEAD_DIM = 64
N_Q_HEADS = 8
N_KV_HEADS = 2
GQA_GROUP = 4
ATTN_WIDTH = 512
KV_WIDTH = 128
WINDOW = 128
ROPE_BASE = 10000.0
ATTN_SCALE = HEAD_DIM ** -0.5
POOL_WINDOWS = (2, 4, 8, 16)
POOL_WIDTH = 512
POOL_GROUP = 128
POOL_OUT_GROUP = 256
OFF_K = 512
OFF_V = 640
OFF_Q = 768
OFF_POOL = 1280
OFF_GATE = 1792
N_EXPERTS = 32
TOP_K = 4
D_EXPERT = 1024
SWIGLU_LIMIT = 7.0
SWIGLU_ALPHA = 1.702

S_TOT = CTX_LEN + SEQ
N_TOK = BATCH * S_TOT
TOK_TILE = 256
N_TILES = S_TOT // TOK_TILE
ATTN_BLOCK = 128
ATTN_STACK = 2
POOL_HALO = 8
ROUTE_TILE = 512
ROW_BLOCK = 256
N_ROWS = -(-(N_TOK * TOP_K + N_EXPERTS * (ROW_BLOCK - 1)) // ROW_BLOCK) * ROW_BLOCK
N_BLOCKS = N_ROWS // ROW_BLOCK
N_BLOCKS_PAD = -(-N_BLOCKS // 8) * 8
SC_CORES = 2
SC_WORKERS = SC_CORES * 16
PACKED = D_MODEL // 2
SC_SCATTER_ROWS = 88
SC_GATHER_ROWS = 88
assert N_TOK % (SC_WORKERS * SC_SCATTER_ROWS) == 0 and (N_TOK * TOP_K) % (SC_WORKERS * SC_GATHER_ROWS) == 0
VMEM_LIMIT = 56 * 1024 * 1024
LANES = 128
MASK_VALUE = -1e30


def _rms(x, g):
    return x * lax.rsqrt(jnp.mean(x * x, axis=-1, keepdims=True) + NORM_EPS) * g


def _sigmoid(x):
    return 0.5 * jnp.tanh(0.5 * x) + 0.5


def _pack_rows(x):
    half = x.shape[1] // 2
    lo = lax.bitcast_convert_type(x[:, :half].astype(BF16).astype(F32), jnp.uint32)
    hi = lax.bitcast_convert_type(x[:, half:].astype(BF16).astype(F32), jnp.uint32)
    return (lo >> 16) | hi


def _unpack_rows(p):
    lo = lax.bitcast_convert_type(p << 16, F32)
    hi = lax.bitcast_convert_type(p & jnp.uint32(0xFFFF0000), F32)
    return lo, hi


def _mod_row(i, b):
    return jnp.where(i == 0, 2, b)


def _ada_kernel(cv_ref, w_ref, b_ref, o_ref):
    cv = cv_ref[...]
    s = cv * jax.nn.sigmoid(cv)
    o_ref[0, 0] = jnp.dot(s, w_ref[0], preferred_element_type=F32, precision=HIGHEST) + b_ref[0, 0]


def _ada(cv, w_ada, b_ada):
    return pl.pallas_call(
        _ada_kernel,
        grid=(DEPTH, 6),
        in_specs=[
            pl.BlockSpec((8, D_MODEL), lambda l, j: (0, 0)),
            pl.BlockSpec((1, D_MODEL, D_MODEL), lambda l, j: (l, 0, j)),
            pl.BlockSpec((1, 1, 1, D_MODEL), lambda l, j: (l, j, 0, 0)),
        ],
        out_specs=pl.BlockSpec((1, 1, 8, D_MODEL), lambda l, j: (l, j, 0, 0)),
        out_shape=jax.ShapeDtypeStruct((DEPTH, 6, 8, D_MODEL), F32),
        name="ada",
    )(cv, w_ada, b_ada.reshape(DEPTH, 6, 1, D_MODEL))


def _rope(t, cos, sin):
    lane = lax.broadcasted_iota(jnp.int32, t.shape, 1)
    rot = jnp.where((lane % 32) < 16, -pltpu.roll(t, LANES - 16, 1), pltpu.roll(t, 16, 1))
    return t * cos + rot * sin


def _inproj_kernel(x_ref, mod_ref, g_ref, w_ref, cos_ref, sin_ref, u_ref, kv_ref, q_ref, pool_ref):
    b = pl.program_id(0)
    i = pl.program_id(1)
    row = _mod_row(i, b)
    shift = mod_ref[0, pl.ds(row, 1), :]
    scale = mod_ref[1, pl.ds(row, 1), :]
    h = _rms(x_ref[0], g_ref[...]) * (1.0 + scale) + shift
    p = jnp.dot(h.astype(BF16), w_ref[...], preferred_element_type=F32)
    cos = cos_ref[...]
    sin = sin_ref[...]
    u_ref[0] = p[:, :OFF_K]
    kv_ref[0, :, :KV_WIDTH] = _rope(p[:, OFF_K:OFF_V], cos, sin).astype(BF16)
    kv_ref[0, :, KV_WIDTH:] = p[:, OFF_V:OFF_Q].astype(BF16)
    for c in range(ATTN_WIDTH // LANES):
        qc = p[:, OFF_Q + c * LANES:OFF_Q + (c + 1) * LANES]
        q_ref[0, :, c * LANES:(c + 1) * LANES] = (_rope(qc, cos, sin) * ATTN_SCALE).astype(BF16)
    pool_ref[0] = p[:, OFF_POOL:OFF_GATE]


def _inproj(xa, mod, g1, w1, cos_t, sin_t):
    tok = lambda b, i: (b, i, 0)
    return pl.pallas_call(
        _inproj_kernel,
        grid=(BATCH, N_TILES),
        in_specs=[
            pl.BlockSpec((1, TOK_TILE, D_MODEL), tok),
            pl.BlockSpec((6, 8, D_MODEL), lambda b, i: (0, 0, 0)),
            pl.BlockSpec((1, D_MODEL), lambda b, i: (0, 0)),
            pl.BlockSpec((D_MODEL, OFF_GATE), lambda b, i: (0, 0)),
            pl.BlockSpec((TOK_TILE, LANES), lambda b, i: (i, 0)),
            pl.BlockSpec((TOK_TILE, LANES), lambda b, i: (i, 0)),
        ],
        out_specs=[
            pl.BlockSpec((1, TOK_TILE, SSM_WIDTH), tok),
            pl.BlockSpec((1, TOK_TILE, 2 * KV_WIDTH), tok),
            pl.BlockSpec((1, TOK_TILE, ATTN_WIDTH), tok),
            pl.BlockSpec((1, TOK_TILE, POOL_WIDTH), tok),
        ],
        out_shape=[
            jax.ShapeDtypeStruct((BATCH, S_TOT, SSM_WIDTH), F32),
            jax.ShapeDtypeStruct((BATCH, S_TOT, 2 * KV_WIDTH), BF16),
            jax.ShapeDtypeStruct((BATCH, S_TOT, ATTN_WIDTH), BF16),
            jax.ShapeDtypeStruct((BATCH, S_TOT, POOL_WIDTH), F32),
        ],
        compiler_params=pltpu.CompilerParams(
            dimension_semantics=("parallel", "parallel"), vmem_limit_bytes=VMEM_LIMIT),
        name="inproj",
    )(xa, mod, g1, w1, cos_t, sin_t)


SSM_HALF_IN = SSM_WIDTH // 2
SSM_HALF_ST = SSM_GROUPS // 2 * SSM_STATE
N_STATE_ROWS = 8
N_LANE_TILES = SSM_HALF_ST // LANES
SCAN_STRIDE = TOK_TILE + 8


def _ssm_kernel(uf_ref, ub_ref, wb_ref, wc_ref, ar_ref, ai_ref, yf_ref, yb_ref, buf_f, buf_b, hf_ref, hb_ref):
    i = pl.program_id(0)
    T = TOK_TILE

    @pl.when(i == 0)
    def _():
        hf_ref[...] = jnp.zeros_like(hf_ref)
        hb_ref[...] = jnp.zeros_like(hb_ref)

    for d, (u_ref, buf) in enumerate(((uf_ref, buf_f), (ub_ref, buf_b))):
        for b in range(BATCH):
            for half in range(2):
                ub16 = u_ref[b, :, half * SSM_HALF_IN:(half + 1) * SSM_HALF_IN].astype(BF16)
                for reim in range(2):
                    k = reim * 4 + half * 2 + b
                    bu = jnp.dot(ub16, wb_ref[d, reim, half], preferred_element_type=F32)
                    for l in range(N_LANE_TILES):
                        buf[l, k * SCAN_STRIDE:k * SCAN_STRIDE + T, :] = bu[:, l * LANES:(l + 1) * LANES]

    def step(t, carry):
        hf, hb = carry
        tb = T - 1 - t
        nf, nb = [], []
        for l in range(N_LANE_TILES):
            sl = pl.ds(l * LANES, LANES)
            rows_f = pl.ds(t, N_STATE_ROWS, stride=SCAN_STRIDE)
            h = ar_ref[0, :, sl] * hf[l] + ai_ref[0, :, sl] * pltpu.roll(hf[l], 4, 0) + buf_f[l, rows_f, :]
            buf_f[l, rows_f, :] = h
            nf.append(h)
            rows_b = pl.ds(tb, N_STATE_ROWS, stride=SCAN_STRIDE)
            h = ar_ref[1, :, sl] * hb[l] + ai_ref[1, :, sl] * pltpu.roll(hb[l], 4, 0) + buf_b[l, rows_b, :]
            buf_b[l, rows_b, :] = h
            nb.append(h)
        return tuple(nf), tuple(nb)

    hf0 = tuple(hf_ref[:, l * LANES:(l + 1) * LANES] for l in range(N_LANE_TILES))
    hb0 = tuple(hb_ref[:, l * LANES:(l + 1) * LANES] for l in range(N_LANE_TILES))
    hf1, hb1 = lax.fori_loop(0, T, step, (hf0, hb0), unroll=2)
    for l in range(N_LANE_TILES):
        hf_ref[:, l * LANES:(l + 1) * LANES] = hf1[l]
        hb_ref[:, l * LANES:(l + 1) * LANES] = hb1[l]

    for d, (buf, y_ref) in enumerate(((buf_f, yf_ref), (buf_b, yb_ref))):
        for b in range(BATCH):
            for half in range(2):
                kre = half * 2 + b
                kim = 4 + half * 2 + b
                state = lambda k: jnp.concatenate(
                    [buf[l, k * SCAN_STRIDE:k * SCAN_STRIDE + T, :] for l in range(N_LANE_TILES)],
                    axis=1).astype(BF16)
                y = jnp.dot(state(kre), wc_ref[d, 0, half], preferred_element_type=F32)
                y = y + jnp.dot(state(kim), wc_ref[d, 1, half], preferred_element_type=F32)
                y_ref[b, :, half * SSM_HALF_IN:(half + 1) * SSM_HALF_IN] = y


def _ssm(u, wb, wc, ar, ai):
    fwd = lambda i: (0, i, 0)
    bwd = lambda i: (0, jnp.where(i == 0, 0, N_TILES - i), 0)
    return pl.pallas_call(
        _ssm_kernel,
        grid=(N_TILES,),
        in_specs=[
            pl.BlockSpec((BATCH, TOK_TILE, SSM_WIDTH), fwd),
            pl.BlockSpec((BATCH, TOK_TILE, SSM_WIDTH), bwd),
            pl.BlockSpec((2, 2, 2, SSM_HALF_IN, SSM_HALF_ST), lambda i: (0, 0, 0, 0, 0)),
            pl.BlockSpec((2, 2, 2, SSM_HALF_ST, SSM_HALF_IN), lambda i: (0, 0, 0, 0, 0)),
            pl.BlockSpec((2, N_STATE_ROWS, SSM_HALF_ST), lambda i: (0, 0, 0)),
            pl.BlockSpec((2, N_STATE_ROWS, SSM_HALF_ST), lambda i: (0, 0, 0)),
        ],
        out_specs=[
            pl.BlockSpec((BATCH, TOK_TILE, SSM_WIDTH), fwd),
            pl.BlockSpec((BATCH, TOK_TILE, SSM_WIDTH), bwd),
        ],
        out_shape=[jax.ShapeDtypeStruct((BATCH, S_TOT, SSM_WIDTH), F32)] * 2,
        scratch_shapes=[
            pltpu.VMEM((N_LANE_TILES, N_STATE_ROWS * SCAN_STRIDE, LANES), F32),
            pltpu.VMEM((N_LANE_TILES, N_STATE_ROWS * SCAN_STRIDE, LANES), F32),
            pltpu.VMEM((N_STATE_ROWS, SSM_HALF_ST), F32),
            pltpu.VMEM((N_STATE_ROWS, SSM_HALF_ST), F32),
        ],
        compiler_params=pltpu.CompilerParams(
            dimension_semantics=("arbitrary",), vmem_limit_bytes=VMEM_LIMIT),
        name="ssm",
    )(u, u, wb, wc, ar, ai)


def _ssm_params(lam_re, lam_im, log_dt, b_re, b_im, c_re, c_im):
    lr = lam_re.astype(F32)
    li = lam_im.astype(F32)
    dt = jnp.exp(log_dt.astype(F32))[..., None]
    mag = jnp.exp(lr * dt)
    ang = li * dt
    a_re = mag * jnp.cos(ang)
    a_im = mag * jnp.sin(ang)
    den = lr * lr + li * li
    k_re = ((a_re - 1.0) * lr + a_im * li) / den
    k_im = (a_im * lr - (a_re - 1.0) * li) / den
    bb_re = k_re[..., None] * b_re - k_im[..., None] * b_im
    bb_im = k_re[..., None] * b_im + k_im[..., None] * b_re
    eye = jnp.eye(SSM_GROUPS // 2, dtype=F32)

    def drive(bb):
        blk = jnp.swapaxes(bb, -1, -2).reshape(2, 2, SSM_GROUPS // 2, SSM_GROUP, SSM_STATE)
        return jnp.einsum('dsghp,gk->dsghkp', blk, eye).reshape(2, 2, SSM_HALF_IN, SSM_HALF_ST)

    def readout(c):
        blk = jnp.swapaxes(c, -1, -2).reshape(2, 2, SSM_GROUPS // 2, SSM_STATE, SSM_GROUP)
        return jnp.einsum('dsgph,gk->dsgpkh', blk, eye).reshape(2, 2, SSM_HALF_ST, SSM_HALF_IN)

    wb = jnp.stack([drive(bb_re), drive(bb_im)], axis=1).astype(BF16)
    wc = jnp.stack([readout(c_re.astype(F32)), readout(-c_im.astype(F32))], axis=1).astype(BF16)

    def rows(a, sign_re, sign_im):
        halves = a.reshape(2, 2, SSM_HALF_ST)
        per = []
        for reim in range(2):
            for half in range(2):
                for _ in range(BATCH):
                    per.append(halves[:, half] * (sign_re if reim == 0 else sign_im))
        return jnp.stack(per, axis=1)

    ar = rows(a_re, 1.0, 1.0)
    ai = rows(a_im, -1.0, 1.0)
    return wb, wc, ar, ai


def _attn_kernel(sink_ref, band_ref, q_ref, kc_ref, kp_ref, kn0_ref, kn_ref, o_ref):
    j = pl.program_id(1)
    n = j - CTX_LEN // ATTN_BLOCK
    q = q_ref[0]
    kvc = kc_ref[0]
    band = jnp.concatenate([kp_ref[0], kn0_ref[0], kn_ref[0]], axis=0)
    nb = band.shape[0]
    last = SEQ // ATTN_BLOCK - 1
    col = lax.broadcasted_iota(jnp.int32, (1, nb), 1)
    lo_c = jnp.where(n < 0, nb, jnp.where(n == 0, ATTN_BLOCK, 0))
    hi_c = jnp.where(n == last, 2 * ATTN_BLOCK, nb)
    bias = band_ref[...] + jnp.where((col >= lo_c) & (col < hi_c), 0.0, MASK_VALUE)
    bias = jnp.concatenate([bias] * ATTN_STACK, axis=0)
    nt = (((1,), (1,)), ((), ()))
    outs = []
    for hk in range(N_KV_HEADS):
        k_b = band[:, hk * HEAD_DIM:(hk + 1) * HEAD_DIM]
        v_b = band[:, KV_WIDTH + hk * HEAD_DIM:KV_WIDTH + (hk + 1) * HEAD_DIM]
        k_c = kvc[:, hk * HEAD_DIM:(hk + 1) * HEAD_DIM]
        v_c = kvc[:, KV_WIDTH + hk * HEAD_DIM:KV_WIDTH + (hk + 1) * HEAD_DIM]
        for g0 in range(0, GQA_GROUP, ATTN_STACK):
            heads = range(hk * GQA_GROUP + g0, hk * GQA_GROUP + g0 + ATTN_STACK)
            qs = jnp.concatenate([q[:, hq * HEAD_DIM:(hq + 1) * HEAD_DIM] for hq in heads], axis=0)
            sink = jnp.concatenate([jnp.full((ATTN_BLOCK, 1), sink_ref[hq], F32) for hq in heads], axis=0)
            s_w = lax.dot_general(qs, k_b, nt, preferred_element_type=F32) + bias
            s_c = lax.dot_general(qs, k_c, nt, preferred_element_type=F32)
            m = jnp.maximum(jnp.maximum(jnp.max(s_w, axis=-1, keepdims=True),
                                        jnp.max(s_c, axis=-1, keepdims=True)), sink)
            e_w = jnp.exp(s_w - m)
            e_c = jnp.exp(s_c - m)
            den = (jnp.sum(e_w, axis=-1, keepdims=True) + jnp.sum(e_c, axis=-1, keepdims=True)
                   + jnp.exp(sink - m))
            o = jnp.dot(e_w.astype(BF16), v_b, preferred_element_type=F32)
            o = (o + jnp.dot(e_c.astype(BF16), v_c, preferred_element_type=F32)) / den
            outs.extend(o[g * ATTN_BLOCK:(g + 1) * ATTN_BLOCK] for g in range(ATTN_STACK))
    o_ref[0] = jnp.concatenate(outs, axis=1).astype(BF16)


def _attn(sink, q, kv):
    nq = S_TOT // ATTN_BLOCK
    first = CTX_LEN // ATTN_BLOCK
    clamp = lambda j: jnp.clip(j, first, nq - 1)
    r = jnp.arange(ATTN_BLOCK)[:, None]
    c = jnp.arange(3 * ATTN_BLOCK)[None, :] - ATTN_BLOCK
    band_bias = jnp.where(jnp.abs(c - r) <= WINDOW, 0.0, MASK_VALUE).astype(F32)
    return pl.pallas_call(
        _attn_kernel,
        grid=(BATCH, nq),
        in_specs=[
            pl.BlockSpec(memory_space=pltpu.SMEM),
            pl.BlockSpec((ATTN_BLOCK, 3 * ATTN_BLOCK), lambda b, j: (0, 0)),
            pl.BlockSpec((1, ATTN_BLOCK, ATTN_WIDTH), lambda b, j: (b, j, 0)),
            pl.BlockSpec((1, CTX_LEN, 2 * KV_WIDTH), lambda b, j: (b, 0, 0)),
            pl.BlockSpec((1, ATTN_BLOCK, 2 * KV_WIDTH), lambda b, j: (b, clamp(j - 1), 0)),
            pl.BlockSpec((1, ATTN_BLOCK, 2 * KV_WIDTH), lambda b, j: (b, clamp(j), 0)),
            pl.BlockSpec((1, ATTN_BLOCK, 2 * KV_WIDTH), lambda b, j: (b, clamp(j + 1), 0)),
        ],
        out_specs=pl.BlockSpec((1, ATTN_BLOCK, ATTN_WIDTH), lambda b, j: (b, j, 0)),
        out_shape=jax.ShapeDtypeStruct((BATCH, S_TOT, ATTN_WIDTH), BF16),
        compiler_params=pltpu.CompilerParams(dimension_semantics=("parallel", "parallel")),
        name="attn",
    )(sink, band_bias, q, kv, kv, kv, kv)


def _gelu_tanh(x):
    return 0.5 * x * (1.0 + jnp.tanh(math.sqrt(2.0 / math.pi) * (x + 0.044715 * (x * x * x))))


def _pool_means(ext, i):
    T = TOK_TILE
    n_ext = T + 2 * POOL_HALO
    seq_len = jnp.where(i == 0, CTX_LEN, SEQ)
    start = jnp.where(i == 0, 0, (i - 1) * T)
    r = start - POOL_HALO + lax.broadcasted_iota(jnp.int32, (n_ext, 1), 0)
    ext = jnp.where((r >= 0) & (r < seq_len), ext, 0.0)
    tl = start + lax.broadcasted_iota(jnp.int32, (T, 1), 0)
    back = lambda a, s: pltpu.roll(a, s, 0)
    fwd = lambda a, s: pltpu.roll(a, n_ext - s, 0)
    diffs = []
    for gi, w in enumerate(POOL_WINDOWS):
        e = ext[:, gi * POOL_GROUP:(gi + 1) * POOL_GROUP]
        lo = w // 2
        hi = w - 1 - lo
        s = e + back(e, 1)
        span = 2
        while span < w:
            s = s + back(s, span)
            span *= 2
        if w > 2:
            s = fwd(s, hi)
        cnt = (jnp.clip(tl + hi + 1, 0, seq_len) - jnp.clip(tl - lo, 0, seq_len)).astype(F32)
        mean = s[POOL_HALO:POOL_HALO + T] / cnt
        diffs.append((mean - e[POOL_HALO:POOL_HALO + T]).astype(BF16))
    return diffs


def _merge_kernel(x_ref, mod_ref, g1_ref, wg_ref, yf_ref, yb_ref, u_ref, dsk_ref, wval_ref, wgate_ref,
                  o_ref, wup_ref, pc_ref, pp_ref, pn_ref, pw_ref, ps_ref, wout_ref, g2_ref, rw_ref, rb_ref,
                  xo_ref, f_ref, te_ref, tp_ref):
    b = pl.program_id(0)
    i = pl.program_id(1)
    row = _mod_row(i, b)
    mods = [mod_ref[j, pl.ds(row, 1), :] for j in range(6)]
    x = x_ref[0]
    h = (_rms(x, g1_ref[...]) * (1.0 + mods[1]) + mods[0]).astype(BF16)
    gates = _sigmoid(jnp.dot(h, wg_ref[...], preferred_element_type=F32))

    z = yf_ref[0] + yb_ref[0] + dsk_ref[...] * u_ref[0]
    z = _gelu_tanh(z).astype(BF16)
    y_ssm = (jnp.dot(z, wval_ref[...], preferred_element_type=F32)
             * _sigmoid(jnp.dot(z, wgate_ref[...], preferred_element_type=F32)))
    y_attn = jnp.dot(o_ref[0], wup_ref[...], preferred_element_type=F32)

    ext = jnp.concatenate([pp_ref[0], pc_ref[0], pn_ref[0]], axis=0)
    diffs = _pool_means(ext, i)
    y_pool = jnp.concatenate(
        [jnp.dot(diffs[gi], pw_ref[gi], preferred_element_type=F32) for gi in range(len(POOL_WINDOWS))],
        axis=1) * ps_ref[...]

    m = (gates[:, :D_MODEL] * y_ssm + gates[:, D_MODEL:2 * D_MODEL] * y_attn
         + gates[:, 2 * D_MODEL:] * y_pool)
    xo = x + mods[2] * jnp.dot(m.astype(BF16), wout_ref[...], preferred_element_type=F32)
    xo_ref[0] = xo
    f = _rms(xo, g2_ref[...]) * (1.0 + mods[4]) + mods[3]
    f_ref[0] = _pack_rows(f)

    f_hi = f.astype(BF16)
    f_lo = (f - f_hi.astype(F32)).astype(BF16)
    r_hi = jnp.dot(f_hi, rw_ref[...], preferred_element_type=F32)
    r_lo = jnp.dot(f_lo, rw_ref[...], preferred_element_type=F32)
    logits = r_hi[:, :N_EXPERTS] + r_hi[:, N_EXPERTS:] + r_lo[:, :N_EXPERTS] + rb_ref[...]
    lane = lax.broadcasted_iota(jnp.int32, logits.shape, 1)
    vals, idxs = [], []
    for _ in range(TOP_K):
        mx = jnp.max(logits, axis=-1, keepdims=True)
        ix = jnp.min(jnp.where(logits == mx, lane, N_EXPERTS), axis=-1, keepdims=True)
        vals.append(mx)
        idxs.append(ix)
        logits = jnp.where(lane == ix, -jnp.inf, logits)
    ex = [jnp.exp(v - vals[0]) for v in vals]
    tot = ex[0] + ex[1] + ex[2] + ex[3]
    te_ref[0] = jnp.concatenate(idxs, axis=1)
    tp_ref[0] = jnp.concatenate(ex, axis=1) / tot


def _merge(xa, mod, g1, wg, yf, yb, u, dsk, wval, wgate, o, wup, pool_in, pw, ps, wout, g2, rw, rb):
    tok = lambda b, i: (b, i, 0)
    const2 = lambda b, i: (0, 0)
    const3 = lambda b, i: (0, 0, 0)
    per_tile = TOK_TILE // POOL_HALO
    n_halo = S_TOT // POOL_HALO
    return pl.pallas_call(
        _merge_kernel,
        grid=(BATCH, N_TILES),
        in_specs=[
            pl.BlockSpec((1, TOK_TILE, D_MODEL), tok),
            pl.BlockSpec((6, 8, D_MODEL), const3),
            pl.BlockSpec((1, D_MODEL), const2),
            pl.BlockSpec((D_MODEL, 3 * D_MODEL), const2),
            pl.BlockSpec((1, TOK_TILE, SSM_WIDTH), tok),
            pl.BlockSpec((1, TOK_TILE, SSM_WIDTH), tok),
            pl.BlockSpec((1, TOK_TILE, SSM_WIDTH), tok),
            pl.BlockSpec((1, SSM_WIDTH), const2),
            pl.BlockSpec((SSM_WIDTH, D_MODEL), const2),
            pl.BlockSpec((SSM_WIDTH, D_MODEL), const2),
            pl.BlockSpec((1, TOK_TILE, ATTN_WIDTH), tok),
            pl.BlockSpec((ATTN_WIDTH, D_MODEL), const2),
            pl.BlockSpec((1, TOK_TILE, POOL_WIDTH), tok),
            pl.BlockSpec((1, POOL_HALO, POOL_WIDTH), lambda b, i: (b, jnp.maximum(i * per_tile - 1, 0), 0)),
            pl.BlockSpec((1, POOL_HALO, POOL_WIDTH),
                         lambda b, i: (b, jnp.minimum((i + 1) * per_tile, n_halo - 1), 0)),
            pl.BlockSpec((len(POOL_WINDOWS), POOL_GROUP, POOL_OUT_GROUP), const3),
            pl.BlockSpec((1, D_MODEL), const2),
            pl.BlockSpec((D_MODEL, D_MODEL), const2),
            pl.BlockSpec((1, D_MODEL), const2),
            pl.BlockSpec((D_MODEL, 2 * N_EXPERTS), const2),
            pl.BlockSpec((1, N_EXPERTS), const2),
        ],
        out_specs=[
            pl.BlockSpec((1, TOK_TILE, D_MODEL), tok),
            pl.BlockSpec((1, TOK_TILE, PACKED), tok),
            pl.BlockSpec((1, TOK_TILE, TOP_K), tok),
            pl.BlockSpec((1, TOK_TILE, TOP_K), tok),
        ],
        out_shape=[
            jax.ShapeDtypeStruct((BATCH, S_TOT, D_MODEL), F32),
            jax.ShapeDtypeStruct((BATCH, S_TOT, PACKED), jnp.uint32),
            jax.ShapeDtypeStruct((BATCH, S_TOT, TOP_K), jnp.int32),
            jax.ShapeDtypeStruct((BATCH, S_TOT, TOP_K), F32),
        ],
        compiler_params=pltpu.CompilerParams(
            dimension_semantics=("parallel", "parallel"), vmem_limit_bytes=VMEM_LIMIT),
        name="merge",
    )(xa, mod, g1, wg, yf, yb, u, dsk, wval, wgate, o, wup, pool_in, pool_in, pool_in, pw, ps, wout, g2, rw, rb)


def _route_kernel(te_ref, dest_ref, be_ref, cnt_ref, carry_ref):
    ph = pl.program_id(0)
    i = pl.program_id(1)
    te = te_ref[...]
    lane = lax.broadcasted_iota(jnp.int32, (ROUTE_TILE, LANES), 1)
    ohs = [(te[:, k:k + 1] == lane).astype(F32) for k in range(TOP_K)]
    oh_all = ohs[0] + ohs[1] + ohs[2] + ohs[3]
    tile_cnt = jnp.sum(oh_all, axis=0, keepdims=True)

    @pl.when((ph == 0) & (i == 0))
    def _():
        cnt_ref[...] = jnp.zeros_like(cnt_ref)

    @pl.when(ph == 0)
    def _():
        cnt_ref[...] += tile_cnt

    @pl.when((ph == 1) & (i == 0))
    def _():
        cnt = jnp.broadcast_to(cnt_ref[...], (8, LANES))
        padded = jnp.floor((cnt + (ROW_BLOCK - 1)) / ROW_BLOCK) * ROW_BLOCK
        l8 = lax.broadcasted_iota(jnp.int32, (8, LANES), 1)
        end = padded
        s = 1
        while s < N_EXPERTS:
            end = end + jnp.where(l8 >= s, pltpu.roll(end, s, 1), 0.0)
            s *= 2
        carry_ref[...] = (end - padded)[0:1]
        blk_row = (lax.broadcasted_iota(jnp.int32, (N_BLOCKS_PAD, LANES), 0) * ROW_BLOCK).astype(F32)
        lb = lax.broadcasted_iota(jnp.int32, (N_BLOCKS_PAD, LANES), 1)
        hit = jnp.where((lb < N_EXPERTS) & (jnp.broadcast_to(end[0:1], blk_row.shape) <= blk_row), 1.0, 0.0)
        be = jnp.sum(hit, axis=-1, keepdims=True)
        be_ref[...] = jnp.broadcast_to(be, (N_BLOCKS_PAD, LANES)).astype(jnp.int32)

    @pl.when(ph == 1)
    def _():
        r = lax.broadcasted_iota(jnp.int32, (ROUTE_TILE, ROUTE_TILE), 0)
        c = lax.broadcasted_iota(jnp.int32, (ROUTE_TILE, ROUTE_TILE), 1)
        below = jnp.where(r > c, 1.0, 0.0).astype(BF16)
        rank = carry_ref[...] + jnp.dot(below, oh_all.astype(BF16), preferred_element_type=F32)
        dests = [jnp.sum(ohs[k] * rank, axis=-1, keepdims=True) for k in range(TOP_K)]
        dest_ref[...] = jnp.concatenate(dests, axis=1).astype(jnp.int32)
        carry_ref[...] += tile_cnt


def _route(top_e):
    n_steps = N_TOK // ROUTE_TILE
    return pl.pallas_call(
        _route_kernel,
        grid=(2, n_steps),
        in_specs=[pl.BlockSpec((ROUTE_TILE, TOP_K), lambda ph, i: (i, 0))],
        out_specs=[
            pl.BlockSpec((ROUTE_TILE, TOP_K), lambda ph, i: (ph * i, 0)),
            pl.BlockSpec((N_BLOCKS_PAD, LANES), lambda ph, i: (0, 0)),
        ],
        out_shape=[
            jax.ShapeDtypeStruct((N_TOK, TOP_K), jnp.int32),
            jax.ShapeDtypeStruct((N_BLOCKS_PAD, LANES), jnp.int32),
        ],
        scratch_shapes=[pltpu.VMEM((1, LANES), F32), pltpu.VMEM((1, LANES), F32)],
        compiler_params=pltpu.CompilerParams(dimension_semantics=("arbitrary", "arbitrary")),
        name="route",
    )(top_e)


def _sc_mesh():
    return plsc.VectorSubcoreMesh(core_axis_name="c", subcore_axis_name="s")


def _sc_worker():
    return lax.axis_index("s") * SC_CORES + lax.axis_index("c")


def _dispatch(f, dest_t):
    per_w = N_TOK // SC_WORKERS
    n_chunks = per_w // SC_SCATTER_ROWS

    @functools.partial(
        pl.kernel,
        out_type=jax.ShapeDtypeStruct((N_ROWS, PACKED), jnp.uint32),
        mesh=_sc_mesh(),
        scratch_types=[
            pltpu.VMEM((SC_SCATTER_ROWS,), jnp.int32),
            pltpu.VMEM((SC_SCATTER_ROWS, PACKED), jnp.uint32),
            pltpu.SemaphoreType.DMA,
        ],
        name="dispatch",
    )
    def k(f_hbm, i_hbm, o_hbm, idx_v, rows_v, sem):
        base = _sc_worker() * per_w

        @pl.loop(0, n_chunks)
        def _(j):
            t0 = pl.multiple_of(base + j * SC_SCATTER_ROWS, 8)
            pltpu.sync_copy(f_hbm.at[pl.ds(t0, SC_SCATTER_ROWS)], rows_v)
            for kk in range(TOP_K):
                pltpu.sync_copy(i_hbm.at[pl.ds(kk * N_TOK + t0, SC_SCATTER_ROWS)], idx_v)
                pltpu.async_copy(rows_v, o_hbm.at[idx_v], sem).wait()

    return k(f, dest_t.reshape(TOP_K * N_TOK))


def _combine_gather(yb, dest_t):
    n = TOP_K * N_TOK
    per_w = n // SC_WORKERS
    n_chunks = per_w // SC_GATHER_ROWS

    @functools.partial(
        pl.kernel,
        out_type=jax.ShapeDtypeStruct((n, PACKED), jnp.uint32),
        mesh=_sc_mesh(),
        scratch_types=[
            pltpu.VMEM((SC_GATHER_ROWS,), jnp.int32),
            pltpu.VMEM((SC_GATHER_ROWS, PACKED), jnp.uint32),
            pltpu.SemaphoreType.DMA,
        ],
        name="combine_gather",
    )
    def k(y_hbm, i_hbm, o_hbm, idx_v, rows_v, sem):
        base = _sc_worker() * per_w

        @pl.loop(0, n_chunks)
        def _(j):
            r0 = pl.multiple_of(base + j * SC_GATHER_ROWS, 8)
            pltpu.sync_copy(i_hbm.at[pl.ds(r0, SC_GATHER_ROWS)], idx_v)
            pltpu.async_copy(y_hbm.at[idx_v], rows_v, sem).wait()
            pltpu.sync_copy(rows_v, o_hbm.at[pl.ds(r0, SC_GATHER_ROWS)])

    return k(yb, dest_t.reshape(n))


def _weight_copies(w1_hbm, w2_hbm, w1f, w2f, sem, idx):
    return (pltpu.make_async_copy(w1_hbm.at[idx], w1f, sem.at[0]),
            pltpu.make_async_copy(w2_hbm.at[idx], w2f, sem.at[1]))


def _expert_kernel(be_ref, x_ref, w1_hbm, b1_ref, w2_hbm, b2_ref, y_ref, w1f, w2f, w1b, w2b, sem, *, layer):
    j = pl.program_id(0)
    e = be_ref[j]
    prev = be_ref[jnp.maximum(j - 1, 0)]
    live = e < N_EXPERTS
    first = live & ((j == 0) | (e != prev))
    copies = functools.partial(_weight_copies, w1_hbm, w2_hbm, w1f, w2f, sem)

    @pl.when(first & (j == 0))
    def _():
        for c in copies(layer * N_EXPERTS + e):
            c.start()

    @pl.when(first)
    def _():
        for c in copies(layer * N_EXPERTS + e):
            c.wait()
        w1b[...] = w1f[...].astype(BF16)
        w2b[...] = w2f[...].astype(BF16)
        nxt = lax.while_loop(
            lambda k: (k < N_BLOCKS) & (be_ref[jnp.minimum(k, N_BLOCKS - 1)] == e), lambda k: k + 1, j + 1)
        e_next = be_ref[jnp.minimum(nxt, N_BLOCKS - 1)]

        @pl.when((nxt < N_BLOCKS) & (e_next < N_EXPERTS))
        def _():
            for c in copies(layer * N_EXPERTS + e_next):
                c.start()

    @pl.when(live)
    def _():
        x = jnp.concatenate(_unpack_rows(x_ref[...]), axis=1).astype(BF16)
        gu = jnp.dot(x, w1b[...], preferred_element_type=F32) + b1_ref[0]
        gate = jnp.minimum(gu[:, :D_EXPERT], SWIGLU_LIMIT)
        lin = jnp.clip(gu[:, D_EXPERT:], -SWIGLU_LIMIT, SWIGLU_LIMIT)
        act = gate * _sigmoid(SWIGLU_ALPHA * gate) * (lin + 1.0)
        y_ref[...] = _pack_rows(jnp.dot(act.astype(BF16), w2b[...], preferred_element_type=F32) + b2_ref[0])


def _experts(block_e, rows, w1, b1, w2, b2, layer):
    ex = lambda j, be: layer * N_EXPERTS + jnp.minimum(be[j], N_EXPERTS - 1)
    return pl.pallas_call(
        functools.partial(_expert_kernel, layer=layer),
        grid_spec=pltpu.PrefetchScalarGridSpec(
            num_scalar_prefetch=1,
            grid=(N_BLOCKS,),
            in_specs=[
                pl.BlockSpec((ROW_BLOCK, PACKED), lambda j, be: (j, 0)),
                pl.BlockSpec(memory_space=pl.ANY),
                pl.BlockSpec((1, 1, 2 * D_EXPERT), lambda j, be: (ex(j, be), 0, 0)),
                pl.BlockSpec(memory_space=pl.ANY),
                pl.BlockSpec((1, 1, D_MODEL), lambda j, be: (ex(j, be), 0, 0)),
            ],
            out_specs=pl.BlockSpec((ROW_BLOCK, PACKED), lambda j, be: (j, 0)),
            scratch_shapes=[
                pltpu.VMEM((D_MODEL, 2 * D_EXPERT), F32),
                pltpu.VMEM((D_EXPERT, D_MODEL), F32),
                pltpu.VMEM((D_MODEL, 2 * D_EXPERT), BF16),
                pltpu.VMEM((D_EXPERT, D_MODEL), BF16),
                pltpu.SemaphoreType.DMA((2,)),
            ],
        ),
        out_shape=jax.ShapeDtypeStruct((N_ROWS, PACKED), jnp.uint32),
        compiler_params=pltpu.CompilerParams(
            dimension_semantics=("arbitrary",), vmem_limit_bytes=VMEM_LIMIT),
        name="experts",
    )(block_e, rows, w1, b1, w2, b2)


def _combine_kernel(x_ref, mod_ref, g_ref, p_ref, o_ref, *, tile_off):
    b = pl.program_id(0)
    i = pl.program_id(1) + tile_off
    row = _mod_row(i, b)
    p = p_ref[0]
    lo, hi = _unpack_rows(g_ref[0, 0])
    y_lo = p[:, 0:1] * lo
    y_hi = p[:, 0:1] * hi
    for k in range(1, TOP_K):
        lo, hi = _unpack_rows(g_ref[k, 0])
        y_lo = y_lo + p[:, k:k + 1] * lo
        y_hi = y_hi + p[:, k:k + 1] * hi
    o_ref[0] = x_ref[0] + mod_ref[5, pl.ds(row, 1), :] * jnp.concatenate([y_lo, y_hi], axis=1)


def _combine_final_kernel(x_ref, mod_ref, g_ref, p_ref, fg_ref, o_ref, *, tile_off):
    _combine_kernel(x_ref, mod_ref, g_ref, p_ref, o_ref, tile_off=tile_off)
    o_ref[0] = _rms(o_ref[0], fg_ref[...])


def _combine(xa, mod, gathered, probs, final_g=None):
    last = final_g is not None
    tile_off = CTX_LEN // TOK_TILE if last else 0
    n_t = N_TILES - tile_off
    tok = lambda b, i: (b, i + tile_off, 0)
    in_specs = [
        pl.BlockSpec((1, TOK_TILE, D_MODEL), tok),
        pl.BlockSpec((6, 8, D_MODEL), lambda b, i: (0, 0, 0)),
        pl.BlockSpec((TOP_K, 1, TOK_TILE, PACKED), lambda b, i: (0, b, i + tile_off, 0)),
        pl.BlockSpec((1, TOK_TILE, TOP_K), tok),
    ]
    args = [xa, mod, gathered.reshape(TOP_K, BATCH, S_TOT, PACKED), probs]
    if last:
        in_specs.append(pl.BlockSpec((1, D_MODEL), lambda b, i: (0, 0)))
        args.append(final_g)
        body = functools.partial(_combine_final_kernel, tile_off=tile_off)
    else:
        body = functools.partial(_combine_kernel, tile_off=tile_off)
    return pl.pallas_call(
        body,
        grid=(BATCH, n_t),
        in_specs=in_specs,
        out_specs=pl.BlockSpec((1, TOK_TILE, D_MODEL), lambda b, i: (b, i, 0)),
        out_shape=jax.ShapeDtypeStruct((BATCH, n_t * TOK_TILE, D_MODEL), F32),
        compiler_params=pltpu.CompilerParams(dimension_semantics=("parallel", "parallel")),
        name="combine",
    )(*args)


def _split_bf16(w):
    bits = lax.bitcast_convert_type(w.astype(F32), jnp.uint32) & jnp.uint32(0xFFFF0000)
    hi = lax.bitcast_convert_type(bits, F32)
    return jnp.concatenate([hi.astype(BF16), (w - hi).astype(BF16)], axis=-1)


def _rope_tables():
    n_rows = SEQ // GRID_W
    row = jnp.repeat(jnp.arange(n_rows), GRID_W).astype(F32)
    col = (jnp.arange(SEQ) % GRID_W).astype(F32)
    axis_dim = HEAD_DIM // 2
    inv_freq = 1.0 / (ROPE_BASE ** (jnp.arange(0, axis_dim, 2, dtype=F32) / axis_dim))
    ang_r = row[:, None] * inv_freq[None]
    ang_c = col[:, None] * inv_freq[None]
    ang = jnp.concatenate([ang_r, ang_r, ang_c, ang_c] * (LANES // HEAD_DIM), axis=-1)
    cos = jnp.concatenate([jnp.ones((CTX_LEN, LANES), F32), jnp.cos(ang)], axis=0)
    sin = jnp.concatenate([jnp.zeros((CTX_LEN, LANES), F32), jnp.sin(ang)], axis=0)
    return cos, sin


def kernel(x, c, ctx, c_ctx, w_ada, b_ada, norm1_g, norm2_g, w_in, ssm_lam_re, ssm_lam_im, ssm_log_dt, ssm_b_re, ssm_b_im, ssm_c_re, ssm_c_im, ssm_d, ssm_w_val, ssm_w_gate, attn_sink, attn_w_up, pool_w, pool_scale, w_out, router_w, router_b, exp_w1, exp_b1, exp_w2, exp_b2, final_g):
    assert x.shape == (BATCH, SEQ, D_MODEL) and ctx.shape == (BATCH, CTX_LEN, D_MODEL)
    cos_t, sin_t = _rope_tables()
    cv = jnp.concatenate([c, c_ctx[None], jnp.zeros((8 - BATCH - 1, D_MODEL), F32)], axis=0)
    mods = _ada(cv, w_ada, b_ada)
    xa = jnp.concatenate([ctx, x], axis=1)
    ew1 = exp_w1.reshape(DEPTH * N_EXPERTS, D_MODEL, 2 * D_EXPERT)
    eb1 = exp_b1.reshape(DEPTH * N_EXPERTS, 1, 2 * D_EXPERT)
    ew2 = exp_w2.reshape(DEPTH * N_EXPERTS, D_EXPERT, D_MODEL)
    eb2 = exp_b2.reshape(DEPTH * N_EXPERTS, 1, D_MODEL)
    out = None
    for l in range(DEPTH):
        mod = mods[l]
        w_in_b = w_in[l].astype(BF16)
        u, kv, q, pool_in = _inproj(xa, mod, norm1_g[l][None], w_in_b[:, :OFF_GATE], cos_t, sin_t)
        wb, wc, ar, ai = _ssm_params(ssm_lam_re[l], ssm_lam_im[l], ssm_log_dt[l], ssm_b_re[l], ssm_b_im[l],
                                     ssm_c_re[l], ssm_c_im[l])
        yf, yb = _ssm(u, wb, wc, ar, ai)
        o = _attn(attn_sink[l], q, kv)
        xa, f, top_e, top_p = _merge(
            xa, mod, norm1_g[l][None], w_in_b[:, OFF_GATE:], yf, yb, u, ssm_d[l][None],
            ssm_w_val[l].astype(BF16), ssm_w_gate[l].astype(BF16), o, attn_w_up[l].astype(BF16),
            pool_in, pool_w[l].astype(BF16), pool_scale[l][None], w_out[l].astype(BF16), norm2_g[l][None],
            _split_bf16(router_w[l]), router_b[l][None])
        dest, block_e = _route(top_e.reshape(N_TOK, TOP_K))
        dest_t = dest.T
        rows = _dispatch(f.reshape(N_TOK, PACKED), dest_t)
        yrows = _experts(block_e[:N_BLOCKS, 0], rows, ew1, eb1, ew2, eb2, l)
        gathered = _combine_gather(yrows, dest_t)
        if l < DEPTH - 1:
            xa = _combine(xa, mod, gathered, top_p)
        else:
            out = _combine(xa, mod, gathered, top_p, final_g[None])
    return out
```

```python
import functools
import math

import jax
import jax.numpy as jnp
import numpy as np
from jax import lax
from jax.experimental import pallas as pl
from jax.experimental.pallas import tpu as pltpu
from jax.experimental.pallas import tpu_sc as plsc

F32 = jnp.float32
BF16 = jnp.bfloat16
HIGHEST = lax.Precision.HIGHEST

D_MODEL = 1024
BATCH = 2
SEQ = 8192
DEPTH = 2
GRID_W = 64
CTX_LEN = 256
NORM_EPS = 1e-6
SSM_WIDTH = 512
SSM_GROUP = 16
SSM_GROUPS = 32
SSM_STATE = 64
HEAD_DIM = 64
N_Q_HEADS = 8
N_KV_HEADS = 2
GQA_GROUP = 4
ATTN_WIDTH = 512
KV_WIDTH = 128
WINDOW = 128
ROPE_BASE = 10000.0
ATTN_SCALE = HEAD_DIM ** -0.5
POOL_WINDOWS = (2, 4, 8, 16)
POOL_WIDTH = 512
POOL_GROUP = 128
POOL_OUT_GROUP = 256
OFF_K = 512
OFF_V = 640
OFF_Q = 768
OFF_POOL = 1280
OFF_GATE = 1792
IN_WIDTH = OFF_GATE + 3 * D_MODEL
N_EXPERTS = 32
TOP_K = 4
D_EXPERT = 1024
SWIGLU_LIMIT = 7.0
SWIGLU_ALPHA = 1.702

S_TOT = CTX_LEN + SEQ
N_TOK = BATCH * S_TOT
TOK_TILE = 256
N_TILES = S_TOT // TOK_TILE
assert CTX_LEN == TOK_TILE
ATTN_BLOCK = 128
ATTN_STACK = 2
MERGE_SPLIT = 1
POOL_HALO = 8
ROUTE_TILE = 512
ROW_BLOCK = 256
N_ROWS = -(-(N_TOK * TOP_K + N_EXPERTS * (ROW_BLOCK - 1)) // ROW_BLOCK) * ROW_BLOCK
N_BLOCKS = N_ROWS // ROW_BLOCK
N_BLOCKS_PAD = -(-N_BLOCKS // 8) * 8
SC_CORES = 2
SC_WORKERS = SC_CORES * 16
PACKED = D_MODEL // 2
SC_SCATTER_ROWS = 88
SC_GATHER_ROWS = 88
assert N_TOK % (SC_WORKERS * SC_SCATTER_ROWS) == 0 and (N_TOK * TOP_K) % (SC_WORKERS * SC_GATHER_ROWS) == 0
VMEM_LIMIT = 56 * 1024 * 1024
LANES = 128
MASK_VALUE = -1e30


def _rms(x, g):
    return x * lax.rsqrt(jnp.mean(x * x, axis=-1, keepdims=True) + NORM_EPS) * g


def _sigmoid(x):
    return 0.5 * jnp.tanh(0.5 * x) + 0.5


def _pack_rows(x):
    half = x.shape[1] // 2
    lo = lax.bitcast_convert_type(x[:, :half].astype(BF16).astype(F32), jnp.uint32)
    hi = lax.bitcast_convert_type(x[:, half:].astype(BF16).astype(F32), jnp.uint32)
    return (lo >> 16) | hi


def _unpack_rows(p):
    lo = lax.bitcast_convert_type(p << 16, F32)
    hi = lax.bitcast_convert_type(p & jnp.uint32(0xFFFF0000), F32)
    return lo, hi


def _token_specs(xs):
    ctx_tiles = CTX_LEN // TOK_TILE
    off = 0 if xs[1].shape[1] == S_TOT else ctx_tiles
    return [pl.BlockSpec((1, TOK_TILE, D_MODEL), lambda b, i: (b, 0, 0)),
            pl.BlockSpec((1, TOK_TILE, D_MODEL), lambda b, i: (b, jnp.maximum(i, ctx_tiles) - off, 0))]


def _token_tile(i, xc_ref, xl_ref):
    return jnp.where(i == 0, xc_ref[0], xl_ref[0])


def _mod_row(i, b):
    return jnp.where(i == 0, 2, b)


def _ada_kernel(cv_ref, w_ref, b_ref, o_ref):
    cv = cv_ref[...]
    s = cv * jax.nn.sigmoid(cv)
    o_ref[0, 0] = jnp.dot(s, w_ref[0], preferred_element_type=F32, precision=HIGHEST) + b_ref[0, 0]


def _ada(cv, w_ada, b_ada):
    return pl.pallas_call(
        _ada_kernel,
        grid=(DEPTH, 6),
        in_specs=[
            pl.BlockSpec((8, D_MODEL), lambda l, j: (0, 0)),
            pl.BlockSpec((1, D_MODEL, D_MODEL), lambda l, j: (l, 0, j)),
            pl.BlockSpec((1, 1, 1, D_MODEL), lambda l, j: (l, j, 0, 0)),
        ],
        out_specs=pl.BlockSpec((1, 1, 8, D_MODEL), lambda l, j: (l, j, 0, 0)),
        out_shape=jax.ShapeDtypeStruct((DEPTH, 6, 8, D_MODEL), F32),
        name="ada",
    )(cv, w_ada, b_ada.reshape(DEPTH, 6, 1, D_MODEL))


def _rope(t, cos, sin):
    lane = lax.broadcasted_iota(jnp.int32, t.shape, 1)
    rot = jnp.where((lane % 32) < 16, -pltpu.roll(t, LANES - 16, 1), pltpu.roll(t, 16, 1))
    return t * cos + rot * sin


def _inproj_kernel(xc_ref, xl_ref, mod_ref, g_ref, w_ref, cos_ref, sin_ref, u_ref, kv_ref, q_ref, pool_ref):
    b = pl.program_id(0)
    i = pl.program_id(1)
    row = _mod_row(i, b)
    shift = mod_ref[0, pl.ds(row, 1), :]
    scale = mod_ref[1, pl.ds(row, 1), :]
    h = _rms(_token_tile(i, xc_ref, xl_ref), g_ref[...]) * (1.0 + scale) + shift
    p = jnp.dot(h.astype(BF16), w_ref[0, :, :OFF_GATE], preferred_element_type=F32)
    cos = cos_ref[...]
    sin = sin_ref[...]
    u_ref[0] = p[:, :OFF_K]
    kv_ref[0, :, :KV_WIDTH] = _rope(p[:, OFF_K:OFF_V], cos, sin).astype(BF16)
    kv_ref[0, :, KV_WIDTH:] = p[:, OFF_V:OFF_Q].astype(BF16)
    for c in range(ATTN_WIDTH // LANES):
        qc = p[:, OFF_Q + c * LANES:OFF_Q + (c + 1) * LANES]
        q_ref[0, :, c * LANES:(c + 1) * LANES] = (_rope(qc, cos, sin) * ATTN_SCALE).astype(BF16)
    pool_ref[0] = p[:, OFF_POOL:OFF_GATE]


def _inproj(xs, mod, g1, w_in, layer, cos_t, sin_t):
    tok = lambda b, i: (b, i, 0)
    return pl.pallas_call(
        _inproj_kernel,
        grid=(BATCH, N_TILES),
        in_specs=[
            *_token_specs(xs),
            pl.BlockSpec((6, 8, D_MODEL), lambda b, i: (0, 0, 0)),
            pl.BlockSpec((1, D_MODEL), lambda b, i: (0, 0)),
            pl.BlockSpec((1, D_MODEL, IN_WIDTH), lambda b, i: (layer, 0, 0)),
            pl.BlockSpec((TOK_TILE, LANES), lambda b, i: (i, 0)),
            pl.BlockSpec((TOK_TILE, LANES), lambda b, i: (i, 0)),
        ],
        out_specs=[
            pl.BlockSpec((1, TOK_TILE, SSM_WIDTH), tok),
            pl.BlockSpec((1, TOK_TILE, 2 * KV_WIDTH), tok),
            pl.BlockSpec((1, TOK_TILE, ATTN_WIDTH), tok),
            pl.BlockSpec((1, TOK_TILE, POOL_WIDTH), tok),
        ],
        out_shape=[
            jax.ShapeDtypeStruct((BATCH, S_TOT, SSM_WIDTH), F32),
            jax.ShapeDtypeStruct((BATCH, S_TOT, 2 * KV_WIDTH), BF16),
            jax.ShapeDtypeStruct((BATCH, S_TOT, ATTN_WIDTH), BF16),
            jax.ShapeDtypeStruct((BATCH, S_TOT, POOL_WIDTH), F32),
        ],
        compiler_params=pltpu.CompilerParams(
            dimension_semantics=("parallel", "parallel"), vmem_limit_bytes=VMEM_LIMIT),
        name="inproj",
    )(*xs, mod, g1, w_in, cos_t, sin_t)


SSM_HALF_IN = SSM_WIDTH // 2
SSM_HALF_ST = SSM_GROUPS // 2 * SSM_STATE
N_STATE_ROWS = 8
N_LANE_TILES = SSM_HALF_ST // LANES
SCAN_STRIDE = TOK_TILE + 8


def _ssm_kernel(uf_ref, ub_ref, wb_ref, wc_ref, ar_ref, ai_ref, yf_ref, yb_ref, buf_f, buf_b, hf_ref, hb_ref):
    i = pl.program_id(0)
    T = TOK_TILE

    @pl.when(i == 0)
    def _():
        hf_ref[...] = jnp.zeros_like(hf_ref)
        hb_ref[...] = jnp.zeros_like(hb_ref)

    for d, (u_ref, buf) in enumerate(((uf_ref, buf_f), (ub_ref, buf_b))):
        for b in range(BATCH):
            for half in range(2):
                ub16 = u_ref[b, :, half * SSM_HALF_IN:(half + 1) * SSM_HALF_IN].astype(BF16)
                for reim in range(2):
                    k = reim * 4 + half * 2 + b
                    bu = jnp.dot(ub16, wb_ref[d, reim, half], preferred_element_type=F32)
                    for l in range(N_LANE_TILES):
                        buf[l, k * SCAN_STRIDE:k * SCAN_STRIDE + T, :] = bu[:, l * LANES:(l + 1) * LANES]

    def step(t, carry):
        hf, hb = carry
        tb = T - 1 - t
        nf, nb = [], []
        for l in range(N_LANE_TILES):
            sl = pl.ds(l * LANES, LANES)
            rows_f = pl.ds(t, N_STATE_ROWS, stride=SCAN_STRIDE)
            h = ar_ref[0, :, sl] * hf[l] + ai_ref[0, :, sl] * pltpu.roll(hf[l], 4, 0) + buf_f[l, rows_f, :]
            buf_f[l, rows_f, :] = h
            nf.append(h)
            rows_b = pl.ds(tb, N_STATE_ROWS, stride=SCAN_STRIDE)
            h = ar_ref[1, :, sl] * hb[l] + ai_ref[1, :, sl] * pltpu.roll(hb[l], 4, 0) + buf_b[l, rows_b, :]
            buf_b[l, rows_b, :] = h
            nb.append(h)
        return tuple(nf), tuple(nb)

    hf0 = tuple(hf_ref[:, l * LANES:(l + 1) * LANES] for l in range(N_LANE_TILES))
    hb0 = tuple(hb_ref[:, l * LANES:(l + 1) * LANES] for l in range(N_LANE_TILES))
    hf1, hb1 = lax.fori_loop(0, T, step, (hf0, hb0), unroll=2)
    for l in range(N_LANE_TILES):
        hf_ref[:, l * LANES:(l + 1) * LANES] = hf1[l]
        hb_ref[:, l * LANES:(l + 1) * LANES] = hb1[l]

    for d, (buf, y_ref) in enumerate(((buf_f, yf_ref), (buf_b, yb_ref))):
        for b in range(BATCH):
            for half in range(2):
                kre = half * 2 + b
                kim = 4 + half * 2 + b
                state = lambda k: jnp.concatenate(
                    [buf[l, k * SCAN_STRIDE:k * SCAN_STRIDE + T, :] for l in range(N_LANE_TILES)],
                    axis=1).astype(BF16)
                y = jnp.dot(state(kre), wc_ref[d, 0, half], preferred_element_type=F32)
                y = y + jnp.dot(state(kim), wc_ref[d, 1, half], preferred_element_type=F32)
                y_ref[b, :, half * SSM_HALF_IN:(half + 1) * SSM_HALF_IN] = y


def _ssm(u, wb, wc, ar, ai):
    fwd = lambda i: (0, i, 0)
    bwd = lambda i: (0, jnp.where(i == 0, 0, N_TILES - i), 0)
    return pl.pallas_call(
        _ssm_kernel,
        grid=(N_TILES,),
        in_specs=[
            pl.BlockSpec((BATCH, TOK_TILE, SSM_WIDTH), fwd),
            pl.BlockSpec((BATCH, TOK_TILE, SSM_WIDTH), bwd),
            pl.BlockSpec((2, 2, 2, SSM_HALF_IN, SSM_HALF_ST), lambda i: (0, 0, 0, 0, 0)),
            pl.BlockSpec((2, 2, 2, SSM_HALF_ST, SSM_HALF_IN), lambda i: (0, 0, 0, 0, 0)),
            pl.BlockSpec((2, N_STATE_ROWS, SSM_HALF_ST), lambda i: (0, 0, 0)),
            pl.BlockSpec((2, N_STATE_ROWS, SSM_HALF_ST), lambda i: (0, 0, 0)),
        ],
        out_specs=[
            pl.BlockSpec((BATCH, TOK_TILE, SSM_WIDTH), fwd),
            pl.BlockSpec((BATCH, TOK_TILE, SSM_WIDTH), bwd),
        ],
        out_shape=[jax.ShapeDtypeStruct((BATCH, S_TOT, SSM_WIDTH), F32)] * 2,
        scratch_shapes=[
            pltpu.VMEM((N_LANE_TILES, N_STATE_ROWS * SCAN_STRIDE, LANES), F32),
            pltpu.VMEM((N_LANE_TILES, N_STATE_ROWS * SCAN_STRIDE, LANES), F32),
            pltpu.VMEM((N_STATE_ROWS, SSM_HALF_ST), F32),
            pltpu.VMEM((N_STATE_ROWS, SSM_HALF_ST), F32),
        ],
        compiler_params=pltpu.CompilerParams(
            dimension_semantics=("arbitrary",), vmem_limit_bytes=VMEM_LIMIT),
        name="ssm",
    )(u, u, wb, wc, ar, ai)


def _ssm_params(lam_re, lam_im, log_dt, b_re, b_im, c_re, c_im):
    lr = lam_re.astype(F32)
    li = lam_im.astype(F32)
    dt = jnp.exp(log_dt.astype(F32))[..., None]
    mag = jnp.exp(lr * dt)
    ang = li * dt
    a_re = mag * jnp.cos(ang)
    a_im = mag * jnp.sin(ang)
    den = lr * lr + li * li
    k_re = ((a_re - 1.0) * lr + a_im * li) / den
    k_im = (a_im * lr - (a_re - 1.0) * li) / den
    bb_re = k_re[..., None] * b_re - k_im[..., None] * b_im
    bb_im = k_re[..., None] * b_im + k_im[..., None] * b_re
    n_g = SSM_GROUPS // 2
    drive_mask = np.equal.outer(np.arange(SSM_HALF_IN) // SSM_GROUP, np.arange(SSM_HALF_ST) // SSM_STATE)
    read_mask = drive_mask.T

    def drive(bb):
        rows = jnp.swapaxes(bb, -1, -2).reshape(2, 2, SSM_HALF_IN, SSM_STATE)
        return jnp.tile(rows, (1, 1, 1, n_g)) * drive_mask.astype(np.float32)

    def readout(c):
        rows = jnp.swapaxes(c, -1, -2).reshape(2, 2, SSM_HALF_ST, SSM_GROUP)
        return jnp.tile(rows, (1, 1, 1, n_g)) * read_mask.astype(np.float32)

    wb = jnp.stack([drive(bb_re), drive(bb_im)], axis=1).astype(BF16)
    wc = jnp.stack([readout(c_re.astype(F32)), readout(-c_im.astype(F32))], axis=1).astype(BF16)

    def rows(a, sign_re, sign_im):
        halves = a.reshape(2, 2, SSM_HALF_ST)
        per = []
        for reim in range(2):
            for half in range(2):
                for _ in range(BATCH):
                    per.append(halves[:, half] * (sign_re if reim == 0 else sign_im))
        return jnp.stack(per, axis=1)

    ar = rows(a_re, 1.0, 1.0)
    ai = rows(a_im, -1.0, 1.0)
    return wb, wc, ar, ai


def _attn_kernel(sink_ref, band_ref, q_ref, kc_ref, kp_ref, kn0_ref, kn_ref, o_ref):
    j = pl.program_id(1)
    n = j - CTX_LEN // ATTN_BLOCK
    q = q_ref[0]
    kvc = kc_ref[0]
    band = jnp.concatenate([kp_ref[0], kn0_ref[0], kn_ref[0]], axis=0)
    nb = band.shape[0]
    last = SEQ // ATTN_BLOCK - 1
    col = lax.broadcasted_iota(jnp.int32, (1, nb), 1)
    lo_c = jnp.where(n < 0, nb, jnp.where(n == 0, ATTN_BLOCK, 0))
    hi_c = jnp.where(n == last, 2 * ATTN_BLOCK, nb)
    bias = band_ref[...] + jnp.where((col >= lo_c) & (col < hi_c), 0.0, MASK_VALUE)
    bias = jnp.concatenate([bias] * ATTN_STACK, axis=0)
    nt = (((1,), (1,)), ((), ()))
    outs = []
    for hk in range(N_KV_HEADS):
        k_b = band[:, hk * HEAD_DIM:(hk + 1) * HEAD_DIM]
        v_b = band[:, KV_WIDTH + hk * HEAD_DIM:KV_WIDTH + (hk + 1) * HEAD_DIM]
        k_c = kvc[:, hk * HEAD_DIM:(hk + 1) * HEAD_DIM]
        v_c = kvc[:, KV_WIDTH + hk * HEAD_DIM:KV_WIDTH + (hk + 1) * HEAD_DIM]
        for g0 in range(0, GQA_GROUP, ATTN_STACK):
            heads = range(hk * GQA_GROUP + g0, hk * GQA_GROUP + g0 + ATTN_STACK)
            qs = jnp.concatenate([q[:, hq * HEAD_DIM:(hq + 1) * HEAD_DIM] for hq in heads], axis=0)
            sink = jnp.concatenate([jnp.full((ATTN_BLOCK, 1), sink_ref[hq], F32) for hq in heads], axis=0)
            s_w = lax.dot_general(qs, k_b, nt, preferred_element_type=F32) + bias
            s_c = lax.dot_general(qs, k_c, nt, preferred_element_type=F32)
            m = jnp.maximum(jnp.maximum(jnp.max(s_w, axis=-1, keepdims=True),
                                        jnp.max(s_c, axis=-1, keepdims=True)), sink)
            e_w = jnp.exp(s_w - m)
            e_c = jnp.exp(s_c - m)
            den = (jnp.sum(e_w, axis=-1, keepdims=True) + jnp.sum(e_c, axis=-1, keepdims=True)
                   + jnp.exp(sink - m))
            o = jnp.dot(e_w.astype(BF16), v_b, preferred_element_type=F32)
            o = (o + jnp.dot(e_c.astype(BF16), v_c, preferred_element_type=F32)) / den
            outs.extend(o[g * ATTN_BLOCK:(g + 1) * ATTN_BLOCK] for g in range(ATTN_STACK))
    o_ref[0] = jnp.concatenate(outs, axis=1).astype(BF16)


def _attn(sink, q, kv):
    nq = S_TOT // ATTN_BLOCK
    first = CTX_LEN // ATTN_BLOCK
    clamp = lambda j: jnp.clip(j, first, nq - 1)
    r = jnp.arange(ATTN_BLOCK)[:, None]
    c = jnp.arange(3 * ATTN_BLOCK)[None, :] - ATTN_BLOCK
    band_bias = jnp.where(jnp.abs(c - r) <= WINDOW, 0.0, MASK_VALUE).astype(F32)
    return pl.pallas_call(
        _attn_kernel,
        grid=(BATCH, nq),
        in_specs=[
            pl.BlockSpec(memory_space=pltpu.SMEM),
            pl.BlockSpec((ATTN_BLOCK, 3 * ATTN_BLOCK), lambda b, j: (0, 0)),
            pl.BlockSpec((1, ATTN_BLOCK, ATTN_WIDTH), lambda b, j: (b, j, 0)),
            pl.BlockSpec((1, CTX_LEN, 2 * KV_WIDTH), lambda b, j: (b, 0, 0)),
            pl.BlockSpec((1, ATTN_BLOCK, 2 * KV_WIDTH), lambda b, j: (b, clamp(j - 1), 0)),
            pl.BlockSpec((1, ATTN_BLOCK, 2 * KV_WIDTH), lambda b, j: (b, clamp(j), 0)),
            pl.BlockSpec((1, ATTN_BLOCK, 2 * KV_WIDTH), lambda b, j: (b, clamp(j + 1), 0)),
        ],
        out_specs=pl.BlockSpec((1, ATTN_BLOCK, ATTN_WIDTH), lambda b, j: (b, j, 0)),
        out_shape=jax.ShapeDtypeStruct((BATCH, S_TOT, ATTN_WIDTH), BF16),
        compiler_params=pltpu.CompilerParams(dimension_semantics=("parallel", "parallel")),
        name="attn",
    )(sink, band_bias, q, kv, kv, kv, kv)


def _gelu_tanh(x):
    return 0.5 * x * (1.0 + jnp.tanh(math.sqrt(2.0 / math.pi) * (x + 0.044715 * (x * x * x))))


def _pool_means(ext, i):
    T = TOK_TILE
    n_ext = T + 2 * POOL_HALO
    seq_len = jnp.where(i == 0, CTX_LEN, SEQ)
    start = jnp.where(i == 0, 0, (i - 1) * T)
    r = start - POOL_HALO + lax.broadcasted_iota(jnp.int32, (n_ext, 1), 0)
    ext = jnp.where((r >= 0) & (r < seq_len), ext, 0.0)
    tl = start + lax.broadcasted_iota(jnp.int32, (T, 1), 0)
    back = lambda a, s: pltpu.roll(a, s, 0)
    fwd = lambda a, s: pltpu.roll(a, n_ext - s, 0)
    diffs = []
    for gi, w in enumerate(POOL_WINDOWS):
        e = ext[:, gi * POOL_GROUP:(gi + 1) * POOL_GROUP]
        lo = w // 2
        hi = w - 1 - lo
        s = e + back(e, 1)
        span = 2
        while span < w:
            s = s + back(s, span)
            span *= 2
        if w > 2:
            s = fwd(s, hi)
        cnt = (jnp.clip(tl + hi + 1, 0, seq_len) - jnp.clip(tl - lo, 0, seq_len)).astype(F32)
        mean = s[POOL_HALO:POOL_HALO + T] / cnt
        diffs.append((mean - e[POOL_HALO:POOL_HALO + T]).astype(BF16))
    return diffs


def _merge_kernel(xc_ref, xl_ref, mod_ref, g1_ref, w_ref, yf_ref, yb_ref, u_ref, dsk_ref, wval_ref, wgate_ref,
                  o_ref, wup_ref, pc_ref, pp_ref, pn_ref, pw_ref, ps_ref, wout_ref, g2_ref, rw_ref, rb_ref,
                  xo_ref, f_ref, te_ref, tp_ref):
    b = pl.program_id(0)
    i = pl.program_id(1)
    row = _mod_row(i, b)
    mods = [mod_ref[j, pl.ds(row, 1), :] for j in range(6)]
    ext = jnp.concatenate([pp_ref[0], pc_ref[0], pn_ref[0]], axis=0)
    diffs = _pool_means(ext, i)
    x_tile = _token_tile(i, xc_ref, xl_ref)
    R = TOK_TILE // MERGE_SPLIT
    for r0 in range(0, TOK_TILE, R):
        rs = slice(r0, r0 + R)
        x = x_tile[rs]
        h = (_rms(x, g1_ref[...]) * (1.0 + mods[1]) + mods[0]).astype(BF16)
        gates = _sigmoid(jnp.dot(h, w_ref[0, :, OFF_GATE:], preferred_element_type=F32))

        z = yf_ref[0, rs, :] + yb_ref[0, rs, :] + dsk_ref[...] * u_ref[0, rs, :]
        z = _gelu_tanh(z).astype(BF16)
        y_ssm = (jnp.dot(z, wval_ref[...], preferred_element_type=F32)
                 * _sigmoid(jnp.dot(z, wgate_ref[...], preferred_element_type=F32)))
        y_attn = jnp.dot(o_ref[0, rs, :], wup_ref[...], preferred_element_type=F32)
        y_pool = jnp.concatenate(
            [jnp.dot(diffs[gi][rs], pw_ref[gi], preferred_element_type=F32)
             for gi in range(len(POOL_WINDOWS))], axis=1) * ps_ref[...]

        m = (gates[:, :D_MODEL] * y_ssm + gates[:, D_MODEL:2 * D_MODEL] * y_attn
             + gates[:, 2 * D_MODEL:] * y_pool)
        xo = x + mods[2] * jnp.dot(m.astype(BF16), wout_ref[...], preferred_element_type=F32)
        xo_ref[0, rs, :] = xo
        f = _rms(xo, g2_ref[...]) * (1.0 + mods[4]) + mods[3]
        f_ref[0, rs, :] = _pack_rows(f)

        f_hi = f.astype(BF16)
        f_lo = (f - f_hi.astype(F32)).astype(BF16)
        r_hi = jnp.dot(f_hi, rw_ref[...], preferred_element_type=F32)
        r_lo = jnp.dot(f_lo, rw_ref[...], preferred_element_type=F32)
        logits = r_hi[:, :N_EXPERTS] + r_hi[:, N_EXPERTS:] + r_lo[:, :N_EXPERTS] + rb_ref[...]
        lane = lax.broadcasted_iota(jnp.int32, logits.shape, 1)
        vals, idxs = [], []
        for _ in range(TOP_K):
            mx = jnp.max(logits, axis=-1, keepdims=True)
            ix = jnp.min(jnp.where(logits == mx, lane, N_EXPERTS), axis=-1, keepdims=True)
            vals.append(mx)
            idxs.append(ix)
            logits = jnp.where(lane == ix, -jnp.inf, logits)
        ex = [jnp.exp(v - vals[0]) for v in vals]
        tot = ex[0] + ex[1] + ex[2] + ex[3]
        te_ref[0, rs, :] = jnp.concatenate(idxs, axis=1)
        tp_ref[0, rs, :] = jnp.concatenate(ex, axis=1) / tot


def _merge(xs, mod, g1, w_in, layer, yf, yb, u, dsk, wval, wgate, o, wup, pool_in, pw, ps, wout, g2, rw, rb):
    tok = lambda b, i: (b, i, 0)
    const2 = lambda b, i: (0, 0)
    const3 = lambda b, i: (0, 0, 0)
    per_tile = TOK_TILE // POOL_HALO
    n_halo = S_TOT // POOL_HALO
    return pl.pallas_call(
        _merge_kernel,
        grid=(BATCH, N_TILES),
        in_specs=[
            *_token_specs(xs),
            pl.BlockSpec((6, 8, D_MODEL), const3),
            pl.BlockSpec((1, D_MODEL), const2),
            pl.BlockSpec((1, D_MODEL, IN_WIDTH), lambda b, i: (layer, 0, 0)),
            pl.BlockSpec((1, TOK_TILE, SSM_WIDTH), tok),
            pl.BlockSpec((1, TOK_TILE, SSM_WIDTH), tok),
            pl.BlockSpec((1, TOK_TILE, SSM_WIDTH), tok),
            pl.BlockSpec((1, SSM_WIDTH), const2),
            pl.BlockSpec((SSM_WIDTH, D_MODEL), const2),
            pl.BlockSpec((SSM_WIDTH, D_MODEL), const2),
            pl.BlockSpec((1, TOK_TILE, ATTN_WIDTH), tok),
            pl.BlockSpec((ATTN_WIDTH, D_MODEL), const2),
            pl.BlockSpec((1, TOK_TILE, POOL_WIDTH), tok),
            pl.BlockSpec((1, POOL_HALO, POOL_WIDTH), lambda b, i: (b, jnp.maximum(i * per_tile - 1, 0), 0)),
            pl.BlockSpec((1, POOL_HALO, POOL_WIDTH),
                         lambda b, i: (b, jnp.minimum((i + 1) * per_tile, n_halo - 1), 0)),
            pl.BlockSpec((len(POOL_WINDOWS), POOL_GROUP, POOL_OUT_GROUP), const3),
            pl.BlockSpec((1, D_MODEL), const2),
            pl.BlockSpec((D_MODEL, D_MODEL), const2),
            pl.BlockSpec((1, D_MODEL), const2),
            pl.BlockSpec((D_MODEL, 2 * N_EXPERTS), const2),
            pl.BlockSpec((1, N_EXPERTS), const2),
        ],
        out_specs=[
            pl.BlockSpec((1, TOK_TILE, D_MODEL), tok),
            pl.BlockSpec((1, TOK_TILE, PACKED), tok),
            pl.BlockSpec((1, TOK_TILE, TOP_K), tok),
            pl.BlockSpec((1, TOK_TILE, TOP_K), tok),
        ],
        out_shape=[
            jax.ShapeDtypeStruct((BATCH, S_TOT, D_MODEL), F32),
            jax.ShapeDtypeStruct((BATCH, S_TOT, PACKED), jnp.uint32),
            jax.ShapeDtypeStruct((BATCH, S_TOT, TOP_K), jnp.int32),
            jax.ShapeDtypeStruct((BATCH, S_TOT, TOP_K), F32),
        ],
        compiler_params=pltpu.CompilerParams(
            dimension_semantics=("parallel", "parallel"), vmem_limit_bytes=VMEM_LIMIT),
        name="merge",
    )(*xs, mod, g1, w_in, yf, yb, u, dsk, wval, wgate, o, wup, pool_in, pool_in, pool_in, pw, ps, wout, g2, rw, rb)


def _route_kernel(te_ref, below_ref, dest_ref, be_ref, cnt_ref, carry_ref):
    ph = pl.program_id(0)
    i = pl.program_id(1)
    te = te_ref[...]
    lane = lax.broadcasted_iota(jnp.int32, (ROUTE_TILE, LANES), 1)
    ohs = [(te[:, k:k + 1] == lane).astype(F32) for k in range(TOP_K)]
    oh_all = ohs[0] + ohs[1] + ohs[2] + ohs[3]
    tile_cnt = jnp.sum(oh_all, axis=0, keepdims=True)

    @pl.when((ph == 0) & (i == 0))
    def _():
        cnt_ref[...] = jnp.zeros_like(cnt_ref)

    @pl.when(ph == 0)
    def _():
        cnt_ref[...] += tile_cnt

    @pl.when((ph == 1) & (i == 0))
    def _():
        cnt = jnp.broadcast_to(cnt_ref[...], (8, LANES))
        padded = jnp.floor((cnt + (ROW_BLOCK - 1)) / ROW_BLOCK) * ROW_BLOCK
        l8 = lax.broadcasted_iota(jnp.int32, (8, LANES), 1)
        end = padded
        s = 1
        while s < N_EXPERTS:
            end = end + jnp.where(l8 >= s, pltpu.roll(end, s, 1), 0.0)
            s *= 2
        carry_ref[...] = (end - padded)[0:1]
        blk_row = (lax.broadcasted_iota(jnp.int32, (N_BLOCKS_PAD, LANES), 0) * ROW_BLOCK).astype(F32)
        lb = lax.broadcasted_iota(jnp.int32, (N_BLOCKS_PAD, LANES), 1)
        hit = jnp.where((lb < N_EXPERTS) & (jnp.broadcast_to(end[0:1], blk_row.shape) <= blk_row), 1.0, 0.0)
        be = jnp.sum(hit, axis=-1, keepdims=True)
        be_ref[...] = jnp.broadcast_to(be, (N_BLOCKS_PAD, LANES)).astype(jnp.int32)

    @pl.when(ph == 1)
    def _():
        rank = carry_ref[...] + jnp.dot(below_ref[...], oh_all.astype(BF16), preferred_element_type=F32)
        dests = [jnp.sum(ohs[k] * rank, axis=-1, keepdims=True) for k in range(TOP_K)]
        dest_ref[...] = jnp.concatenate(dests, axis=1).astype(jnp.int32)
        carry_ref[...] += tile_cnt


def _route(top_e):
    n_steps = N_TOK // ROUTE_TILE
    return pl.pallas_call(
        _route_kernel,
        grid=(2, n_steps),
        in_specs=[pl.BlockSpec((ROUTE_TILE, TOP_K), lambda ph, i: (i, 0)),
                  pl.BlockSpec((ROUTE_TILE, ROUTE_TILE), lambda ph, i: (0, 0))],
        out_specs=[
            pl.BlockSpec((ROUTE_TILE, TOP_K), lambda ph, i: (ph * i, 0)),
            pl.BlockSpec((N_BLOCKS_PAD, LANES), lambda ph, i: (0, 0)),
        ],
        out_shape=[
            jax.ShapeDtypeStruct((N_TOK, TOP_K), jnp.int32),
            jax.ShapeDtypeStruct((N_BLOCKS_PAD, LANES), jnp.int32),
        ],
        scratch_shapes=[pltpu.VMEM((1, LANES), F32), pltpu.VMEM((1, LANES), F32)],
        compiler_params=pltpu.CompilerParams(dimension_semantics=("arbitrary", "arbitrary")),
        name="route",
    )(top_e, jnp.asarray(np.tril(np.ones((ROUTE_TILE, ROUTE_TILE), np.float32), -1), BF16))


def _sc_mesh():
    return plsc.VectorSubcoreMesh(core_axis_name="c", subcore_axis_name="s")


def _sc_worker():
    return lax.axis_index("s") * SC_CORES + lax.axis_index("c")


def _dispatch(f, dest_t):
    per_w = N_TOK // SC_WORKERS
    n_chunks = per_w // SC_SCATTER_ROWS

    @functools.partial(
        pl.kernel,
        out_type=jax.ShapeDtypeStruct((N_ROWS, PACKED), jnp.uint32),
        mesh=_sc_mesh(),
        scratch_types=[
            pltpu.VMEM((SC_SCATTER_ROWS,), jnp.int32),
            pltpu.VMEM((SC_SCATTER_ROWS, PACKED), jnp.uint32),
            pltpu.SemaphoreType.DMA,
        ],
        name="dispatch",
    )
    def k(f_hbm, i_hbm, o_hbm, idx_v, rows_v, sem):
        base = _sc_worker() * per_w

        @pl.loop(0, n_chunks)
        def _(j):
            t0 = pl.multiple_of(base + j * SC_SCATTER_ROWS, 8)
            pltpu.sync_copy(f_hbm.at[pl.ds(t0, SC_SCATTER_ROWS)], rows_v)
            for kk in range(TOP_K):
                pltpu.sync_copy(i_hbm.at[pl.ds(kk * N_TOK + t0, SC_SCATTER_ROWS)], idx_v)
                pltpu.async_copy(rows_v, o_hbm.at[idx_v], sem).wait()

    return k(f, dest_t.reshape(TOP_K * N_TOK))


def _combine_gather(yb, dest_t):
    n = TOP_K * N_TOK
    per_w = n // SC_WORKERS
    n_chunks = per_w // SC_GATHER_ROWS

    @functools.partial(
        pl.kernel,
        out_type=jax.ShapeDtypeStruct((n, PACKED), jnp.uint32),
        mesh=_sc_mesh(),
        scratch_types=[
            pltpu.VMEM((SC_GATHER_ROWS,), jnp.int32),
            pltpu.VMEM((SC_GATHER_ROWS, PACKED), jnp.uint32),
            pltpu.SemaphoreType.DMA,
        ],
        name="combine_gather",
    )
    def k(y_hbm, i_hbm, o_hbm, idx_v, rows_v, sem):
        base = _sc_worker() * per_w

        @pl.loop(0, n_chunks)
        def _(j):
            r0 = pl.multiple_of(base + j * SC_GATHER_ROWS, 8)
            pltpu.sync_copy(i_hbm.at[pl.ds(r0, SC_GATHER_ROWS)], idx_v)
            pltpu.async_copy(y_hbm.at[idx_v], rows_v, sem).wait()
            pltpu.sync_copy(rows_v, o_hbm.at[pl.ds(r0, SC_GATHER_ROWS)])

    return k(yb, dest_t.reshape(n))


def _weight_copies(w1_hbm, w2_hbm, w1f, w2f, sem, idx):
    return (pltpu.make_async_copy(w1_hbm.at[idx], w1f, sem.at[0]),
            pltpu.make_async_copy(w2_hbm.at[idx], w2f, sem.at[1]))


def _expert_kernel(be_ref, x_ref, w1_hbm, b1_ref, w2_hbm, b2_ref, y_ref, w1f, w2f, w1b, w2b, sem, *, layer):
    j = pl.program_id(0)
    e = be_ref[j]
    prev = be_ref[jnp.maximum(j - 1, 0)]
    live = e < N_EXPERTS
    first = live & ((j == 0) | (e != prev))
    copies = functools.partial(_weight_copies, w1_hbm, w2_hbm, w1f, w2f, sem)

    @pl.when(first & (j == 0))
    def _():
        for c in copies(layer * N_EXPERTS + e):
            c.start()

    @pl.when(first)
    def _():
        for c in copies(layer * N_EXPERTS + e):
            c.wait()
        w1b[...] = w1f[...].astype(BF16)
        w2b[...] = w2f[...].astype(BF16)
        nxt = lax.while_loop(
            lambda k: (k < N_BLOCKS) & (be_ref[jnp.minimum(k, N_BLOCKS - 1)] == e), lambda k: k + 1, j + 1)
        e_next = be_ref[jnp.minimum(nxt, N_BLOCKS - 1)]

        @pl.when((nxt < N_BLOCKS) & (e_next < N_EXPERTS))
        def _():
            for c in copies(layer * N_EXPERTS + e_next):
                c.start()

    @pl.when(live)
    def _():
        x = jnp.concatenate(_unpack_rows(x_ref[...]), axis=1).astype(BF16)
        gu = jnp.dot(x, w1b[...], preferred_element_type=F32) + b1_ref[0]
        gate = jnp.minimum(gu[:, :D_EXPERT], SWIGLU_LIMIT)
        lin = jnp.clip(gu[:, D_EXPERT:], -SWIGLU_LIMIT, SWIGLU_LIMIT)
        act = gate * _sigmoid(SWIGLU_ALPHA * gate) * (lin + 1.0)
        y_ref[...] = _pack_rows(jnp.dot(act.astype(BF16), w2b[...], preferred_element_type=F32) + b2_ref[0])


def _experts(block_e, rows, w1, b1, w2, b2, layer):
    ex = lambda j, be: layer * N_EXPERTS + jnp.minimum(be[j], N_EXPERTS - 1)
    return pl.pallas_call(
        functools.partial(_expert_kernel, layer=layer),
        grid_spec=pltpu.PrefetchScalarGridSpec(
            num_scalar_prefetch=1,
            grid=(N_BLOCKS,),
            in_specs=[
                pl.BlockSpec((ROW_BLOCK, PACKED), lambda j, be: (j, 0)),
                pl.BlockSpec(memory_space=pl.ANY),
                pl.BlockSpec((1, 1, 2 * D_EXPERT), lambda j, be: (ex(j, be), 0, 0)),
                pl.BlockSpec(memory_space=pl.ANY),
                pl.BlockSpec((1, 1, D_MODEL), lambda j, be: (ex(j, be), 0, 0)),
            ],
            out_specs=pl.BlockSpec((ROW_BLOCK, PACKED), lambda j, be: (j, 0)),
            scratch_shapes=[
                pltpu.VMEM((D_MODEL, 2 * D_EXPERT), F32),
                pltpu.VMEM((D_EXPERT, D_MODEL), F32),
                pltpu.VMEM((D_MODEL, 2 * D_EXPERT), BF16),
                pltpu.VMEM((D_EXPERT, D_MODEL), BF16),
                pltpu.SemaphoreType.DMA((2,)),
            ],
        ),
        out_shape=jax.ShapeDtypeStruct((N_ROWS, PACKED), jnp.uint32),
        compiler_params=pltpu.CompilerParams(
            dimension_semantics=("arbitrary",), vmem_limit_bytes=VMEM_LIMIT),
        name="experts",
    )(block_e, rows, w1, b1, w2, b2)


def _combined(x_ref, mod_ref, g_ref, p_ref, tile_off):
    b = pl.program_id(0)
    i = pl.program_id(1) + tile_off
    row = _mod_row(i, b)
    p = p_ref[0]
    lo, hi = _unpack_rows(g_ref[0, 0])
    y_lo = p[:, 0:1] * lo
    y_hi = p[:, 0:1] * hi
    for k in range(1, TOP_K):
        lo, hi = _unpack_rows(g_ref[k, 0])
        y_lo = y_lo + p[:, k:k + 1] * lo
        y_hi = y_hi + p[:, k:k + 1] * hi
    return x_ref[0] + mod_ref[5, pl.ds(row, 1), :] * jnp.concatenate([y_lo, y_hi], axis=1)


def _combine_kernel(x_ref, mod_ref, g_ref, p_ref, o_ref, *, tile_off):
    o_ref[0] = _combined(x_ref, mod_ref, g_ref, p_ref, tile_off)


def _combine_final_kernel(x_ref, mod_ref, g_ref, p_ref, fg_ref, o_ref, *, tile_off):
    o_ref[0] = _rms(_combined(x_ref, mod_ref, g_ref, p_ref, tile_off), fg_ref[...])


def _combine(xa, mod, gathered, probs, final_g=None):
    last = final_g is not None
    tile_off = CTX_LEN // TOK_TILE if last else 0
    n_t = N_TILES - tile_off
    tok = lambda b, i: (b, i + tile_off, 0)
    in_specs = [
        pl.BlockSpec((1, TOK_TILE, D_MODEL), tok),
        pl.BlockSpec((6, 8, D_MODEL), lambda b, i: (0, 0, 0)),
        pl.BlockSpec((TOP_K, 1, TOK_TILE, PACKED), lambda b, i: (0, b, i + tile_off, 0)),
        pl.BlockSpec((1, TOK_TILE, TOP_K), tok),
    ]
    args = [xa, mod, gathered.reshape(TOP_K, BATCH, S_TOT, PACKED), probs]
    if last:
        in_specs.append(pl.BlockSpec((1, D_MODEL), lambda b, i: (0, 0)))
        args.append(final_g)
        body = functools.partial(_combine_final_kernel, tile_off=tile_off)
    else:
        body = functools.partial(_combine_kernel, tile_off=tile_off)
    return pl.pallas_call(
        body,
        grid=(BATCH, n_t),
        in_specs=in_specs,
        out_specs=pl.BlockSpec((1, TOK_TILE, D_MODEL), lambda b, i: (b, i, 0)),
        out_shape=jax.ShapeDtypeStruct((BATCH, n_t * TOK_TILE, D_MODEL), F32),
        compiler_params=pltpu.CompilerParams(dimension_semantics=("parallel", "parallel")),
        name="combine",
    )(*args)


def _split_bf16(w):
    bits = lax.bitcast_convert_type(w.astype(F32), jnp.uint32) & jnp.uint32(0xFFFF0000)
    hi = lax.bitcast_convert_type(bits, F32)
    return jnp.concatenate([hi.astype(BF16), (w - hi).astype(BF16)], axis=-1)


def _rope_tables():
    t = np.arange(SEQ)
    axis_dim = HEAD_DIM // 2
    inv_freq = (1.0 / (ROPE_BASE ** (np.arange(0, axis_dim, 2, dtype=np.float32) / axis_dim))).astype(np.float32)
    ang_r = (t // GRID_W).astype(np.float32)[:, None] * inv_freq[None]
    ang_c = (t % GRID_W).astype(np.float32)[:, None] * inv_freq[None]
    ang = np.concatenate([ang_r, ang_r, ang_c, ang_c] * (LANES // HEAD_DIM), axis=-1)
    cos = np.concatenate([np.ones((CTX_LEN, LANES), np.float32), np.cos(ang)], axis=0)
    sin = np.concatenate([np.zeros((CTX_LEN, LANES), np.float32), np.sin(ang)], axis=0)
    return jnp.asarray(cos, F32), jnp.asarray(sin, F32)


def kernel(x, c, ctx, c_ctx, w_ada, b_ada, norm1_g, norm2_g, w_in, ssm_lam_re, ssm_lam_im, ssm_log_dt, ssm_b_re, ssm_b_im, ssm_c_re, ssm_c_im, ssm_d, ssm_w_val, ssm_w_gate, attn_sink, attn_w_up, pool_w, pool_scale, w_out, router_w, router_b, exp_w1, exp_b1, exp_w2, exp_b2, final_g):
    assert x.shape == (BATCH, SEQ, D_MODEL) and ctx.shape == (BATCH, CTX_LEN, D_MODEL)
    cos_t, sin_t = _rope_tables()
    cv = jnp.concatenate([c, c_ctx[None], jnp.zeros((8 - BATCH - 1, D_MODEL), F32)], axis=0)
    mods = _ada(cv, w_ada, b_ada)
    xs = (ctx, x)
    ew1 = exp_w1.reshape(DEPTH * N_EXPERTS, D_MODEL, 2 * D_EXPERT)
    eb1 = exp_b1.reshape(DEPTH * N_EXPERTS, 1, 2 * D_EXPERT)
    ew2 = exp_w2.reshape(DEPTH * N_EXPERTS, D_EXPERT, D_MODEL)
    eb2 = exp_b2.reshape(DEPTH * N_EXPERTS, 1, D_MODEL)
    w_in_b = w_in.astype(BF16)
    out = None
    for l in range(DEPTH):
        mod = mods[l]
        u, kv, q, pool_in = _inproj(xs, mod, norm1_g[l][None], w_in_b, l, cos_t, sin_t)
        wb, wc, ar, ai = _ssm_params(ssm_lam_re[l], ssm_lam_im[l], ssm_log_dt[l], ssm_b_re[l], ssm_b_im[l],
                                     ssm_c_re[l], ssm_c_im[l])
        yf, yb = _ssm(u, wb, wc, ar, ai)
        o = _attn(attn_sink[l], q, kv)
        xa, f, top_e, top_p = _merge(
            xs, mod, norm1_g[l][None], w_in_b, l, yf, yb, u, ssm_d[l][None],
            ssm_w_val[l].astype(BF16), ssm_w_gate[l].astype(BF16), o, attn_w_up[l].astype(BF16),
            pool_in, pool_w[l].astype(BF16), pool_scale[l][None], w_out[l].astype(BF16), norm2_g[l][None],
            _split_bf16(router_w[l]), router_b[l][None])
        dest, block_e = _route(top_e.reshape(N_TOK, TOP_K))
        dest_t = dest.T
        rows = _dispatch(f.reshape(N_TOK, PACKED), dest_t)
        yrows = _experts(block_e[:N_BLOCKS, 0], rows, ew1, eb1, ew2, eb2, l)
        gathered = _combine_gather(yrows, dest_t)
        if l < DEPTH - 1:
            xa = _combine(xa, mod, gathered, top_p)
            xs = (xa, xa)
        else:
            out = _combine(xa, mod, gathered, top_p, final_g[None])
    return out
```

```python
import functools
import math

import jax
import jax.numpy as jnp
import numpy as np
from jax import lax
from jax.experimental import pallas as pl
from jax.experimental.pallas import tpu as pltpu
from jax.experimental.pallas import tpu_sc as plsc

F32 = jnp.float32
BF16 = jnp.bfloat16
HIGHEST = lax.Precision.HIGHEST

D_MODEL = 1024
BATCH = 2
SEQ = 8192
DEPTH = 2
GRID_W = 64
CTX_LEN = 256
NORM_EPS = 1e-6
SSM_WIDTH = 512
SSM_GROUP = 16
SSM_GROUPS = 32
SSM_STATE = 64
HEAD_DIM = 64
N_Q_HEADS = 8
N_KV_HEADS = 2
GQA_GROUP = 4
ATTN_WIDTH = 512
KV_WIDTH = 128
WINDOW = 128
ROPE_BASE = 10000.0
ATTN_SCALE = HEAD_DIM ** -0.5
POOL_WINDOWS = (2, 4, 8, 16)
POOL_WIDTH = 512
POOL_GROUP = 128
POOL_OUT_GROUP = 256
OFF_K = 512
OFF_V = 640
OFF_Q = 768
OFF_POOL = 1280
OFF_GATE = 1792
IN_WIDTH = OFF_GATE + 3 * D_MODEL
N_EXPERTS = 32
TOP_K = 4
D_EXPERT = 1024
SWIGLU_LIMIT = 7.0
SWIGLU_ALPHA = 1.702

S_TOT = CTX_LEN + SEQ
N_TOK = BATCH * S_TOT
TOK_TILE = 256
N_TILES = S_TOT // TOK_TILE
assert CTX_LEN == TOK_TILE
ATTN_BLOCK = 128
ATTN_STACK = 2
MERGE_SPLIT = 1
POOL_HALO = 8
ROUTE_TILE = 512
ROW_BLOCK = 256
N_ROWS = -(-(N_TOK * TOP_K + N_EXPERTS * (ROW_BLOCK - 1)) // ROW_BLOCK) * ROW_BLOCK
N_BLOCKS = N_ROWS // ROW_BLOCK
N_BLOCKS_PAD = -(-N_BLOCKS // 8) * 8
SC_CORES = 2
SC_WORKERS = SC_CORES * 16
PACKED = D_MODEL // 2
SC_SCATTER_ROWS = 88
SC_GATHER_ROWS = 88
assert N_TOK % (SC_WORKERS * SC_SCATTER_ROWS) == 0 and (N_TOK * TOP_K) % (SC_WORKERS * SC_GATHER_ROWS) == 0
VMEM_LIMIT = 56 * 1024 * 1024
LANES = 128
MASK_VALUE = -1e30


def _rms(x, g):
    return x * lax.rsqrt(jnp.mean(x * x, axis=-1, keepdims=True) + NORM_EPS) * g


def _sigmoid(x):
    return 0.5 * jnp.tanh(0.5 * x) + 0.5


def _pack_rows(x):
    half = x.shape[1] // 2
    lo = lax.bitcast_convert_type(x[:, :half].astype(BF16).astype(F32), jnp.uint32)
    hi = lax.bitcast_convert_type(x[:, half:].astype(BF16).astype(F32), jnp.uint32)
    return (lo >> 16) | hi


def _unpack_rows(p):
    lo = lax.bitcast_convert_type(p << 16, F32)
    hi = lax.bitcast_convert_type(p & jnp.uint32(0xFFFF0000), F32)
    return lo, hi


def _token_specs(xs):
    ctx_tiles = CTX_LEN // TOK_TILE
    off = 0 if xs[1].shape[1] == S_TOT else ctx_tiles
    return [pl.BlockSpec((1, TOK_TILE, D_MODEL), lambda b, i: (b, 0, 0)),
            pl.BlockSpec((1, TOK_TILE, D_MODEL), lambda b, i: (b, jnp.maximum(i, ctx_tiles) - off, 0))]


def _token_tile(i, xc_ref, xl_ref):
    return jnp.where(i == 0, xc_ref[0], xl_ref[0])


def _mod_row(i, b):
    return jnp.where(i == 0, 2, b)


def _ada_kernel(cv_ref, w_ref, b_ref, o_ref):
    cv = cv_ref[...]
    s = cv * jax.nn.sigmoid(cv)
    o_ref[0, 0] = jnp.dot(s, w_ref[0], preferred_element_type=F32, precision=HIGHEST) + b_ref[0, 0]


def _ada(cv, w_ada, b_ada):
    return pl.pallas_call(
        _ada_kernel,
        grid=(DEPTH, 6),
        in_specs=[
            pl.BlockSpec((8, D_MODEL), lambda l, j: (0, 0)),
            pl.BlockSpec((1, D_MODEL, D_MODEL), lambda l, j: (l, 0, j)),
            pl.BlockSpec((1, 1, 1, D_MODEL), lambda l, j: (l, j, 0, 0)),
        ],
        out_specs=pl.BlockSpec((1, 1, 8, D_MODEL), lambda l, j: (l, j, 0, 0)),
        out_shape=jax.ShapeDtypeStruct((DEPTH, 6, 8, D_MODEL), F32),
        name="ada",
    )(cv, w_ada, b_ada.reshape(DEPTH, 6, 1, D_MODEL))


def _rope(t, cos, sin):
    lane = lax.broadcasted_iota(jnp.int32, t.shape, 1)
    rot = jnp.where((lane % 32) < 16, -pltpu.roll(t, LANES - 16, 1), pltpu.roll(t, 16, 1))
    return t * cos + rot * sin


def _inproj_kernel(xc_ref, xl_ref, mod_ref, g_ref, w_ref, cos_ref, sin_ref, u_ref, kv_ref, q_ref, pool_ref):
    b = pl.program_id(0)
    i = pl.program_id(1)
    row = _mod_row(i, b)
    shift = mod_ref[0, pl.ds(row, 1), :]
    scale = mod_ref[1, pl.ds(row, 1), :]
    h = _rms(_token_tile(i, xc_ref, xl_ref), g_ref[...]) * (1.0 + scale) + shift
    p = jnp.dot(h.astype(BF16), w_ref[0, :, :OFF_GATE], preferred_element_type=F32)
    cos = cos_ref[...]
    sin = sin_ref[...]
    u_ref[0] = p[:, :OFF_K]
    kv_ref[0, :, :KV_WIDTH] = _rope(p[:, OFF_K:OFF_V], cos, sin).astype(BF16)
    kv_ref[0, :, KV_WIDTH:] = p[:, OFF_V:OFF_Q].astype(BF16)
    for c in range(ATTN_WIDTH // LANES):
        qc = p[:, OFF_Q + c * LANES:OFF_Q + (c + 1) * LANES]
        q_ref[0, :, c * LANES:(c + 1) * LANES] = (_rope(qc, cos, sin) * ATTN_SCALE).astype(BF16)
    pool_ref[0] = p[:, OFF_POOL:OFF_GATE]


def _inproj(xs, mod, g1, w_in, layer, cos_t, sin_t):
    tok = lambda b, i: (b, i, 0)
    return pl.pallas_call(
        _inproj_kernel,
        grid=(BATCH, N_TILES),
        in_specs=[
            *_token_specs(xs),
            pl.BlockSpec((6, 8, D_MODEL), lambda b, i: (0, 0, 0)),
            pl.BlockSpec((1, D_MODEL), lambda b, i: (0, 0)),
            pl.BlockSpec((1, D_MODEL, IN_WIDTH), lambda b, i: (layer, 0, 0)),
            pl.BlockSpec((TOK_TILE, LANES), lambda b, i: (i, 0)),
            pl.BlockSpec((TOK_TILE, LANES), lambda b, i: (i, 0)),
        ],
        out_specs=[
            pl.BlockSpec((1, TOK_TILE, SSM_WIDTH), tok),
            pl.BlockSpec((1, TOK_TILE, 2 * KV_WIDTH), tok),
            pl.BlockSpec((1, TOK_TILE, ATTN_WIDTH), tok),
            pl.BlockSpec((1, TOK_TILE, POOL_WIDTH), tok),
        ],
        out_shape=[
            jax.ShapeDtypeStruct((BATCH, S_TOT, SSM_WIDTH), F32),
            jax.ShapeDtypeStruct((BATCH, S_TOT, 2 * KV_WIDTH), BF16),
            jax.ShapeDtypeStruct((BATCH, S_TOT, ATTN_WIDTH), BF16),
            jax.ShapeDtypeStruct((BATCH, S_TOT, POOL_WIDTH), F32),
        ],
        compiler_params=pltpu.CompilerParams(
            dimension_semantics=("parallel", "parallel"), vmem_limit_bytes=VMEM_LIMIT),
        name="inproj",
    )(*xs, mod, g1, w_in, cos_t, sin_t)


SSM_HALF_IN = SSM_WIDTH // 2
SSM_HALF_ST = SSM_GROUPS // 2 * SSM_STATE
N_STATE_ROWS = 8
N_LANE_TILES = SSM_HALF_ST // LANES


def _ssm_kernel(uf_ref, ub_ref, wb_ref, wc_ref, ar_ref, ai_ref, yf_ref, yb_ref,
                bu_f, bu_b, hs_f, hs_b, hf_ref, hb_ref):
    i = pl.program_id(0)
    T = TOK_TILE

    @pl.when(i == 0)
    def _():
        hf_ref[...] = jnp.zeros_like(hf_ref)
        hb_ref[...] = jnp.zeros_like(hb_ref)

    for d, (u_ref, buf) in enumerate(((uf_ref, bu_f), (ub_ref, bu_b))):
        for b in range(BATCH):
            for half in range(2):
                ub16 = u_ref[b, :, half * SSM_HALF_IN:(half + 1) * SSM_HALF_IN].astype(BF16)
                for reim in range(2):
                    k = reim * 4 + half * 2 + b
                    bu = jnp.dot(ub16, wb_ref[d, reim, half], preferred_element_type=F32)
                    for l in range(N_LANE_TILES):
                        buf[l, pl.ds(k, T, stride=N_STATE_ROWS), :] = bu[:, l * LANES:(l + 1) * LANES]

    def step(t, carry):
        hf, hb = carry
        rows_f = pl.ds(pl.multiple_of(t * N_STATE_ROWS, N_STATE_ROWS), N_STATE_ROWS)
        rows_b = pl.ds(pl.multiple_of((T - 1 - t) * N_STATE_ROWS, N_STATE_ROWS), N_STATE_ROWS)
        nf, nb = [], []
        for l in range(N_LANE_TILES):
            sl = pl.ds(l * LANES, LANES)
            h = ar_ref[0, :, sl] * hf[l] + ai_ref[0, :, sl] * pltpu.roll(hf[l], 4, 0) + bu_f[l, rows_f, :]
            hs_f[l, rows_f, :] = h
            nf.append(h)
            h = ar_ref[1, :, sl] * hb[l] + ai_ref[1, :, sl] * pltpu.roll(hb[l], 4, 0) + bu_b[l, rows_b, :]
            hs_b[l, rows_b, :] = h
            nb.append(h)
        return tuple(nf), tuple(nb)

    hf0 = tuple(hf_ref[:, l * LANES:(l + 1) * LANES] for l in range(N_LANE_TILES))
    hb0 = tuple(hb_ref[:, l * LANES:(l + 1) * LANES] for l in range(N_LANE_TILES))
    hf1, hb1 = lax.fori_loop(0, T, step, (hf0, hb0), unroll=2)
    for l in range(N_LANE_TILES):
        hf_ref[:, l * LANES:(l + 1) * LANES] = hf1[l]
        hb_ref[:, l * LANES:(l + 1) * LANES] = hb1[l]

    for d, (buf, y_ref) in enumerate(((hs_f, yf_ref), (hs_b, yb_ref))):
        for b in range(BATCH):
            for half in range(2):
                kre = half * 2 + b
                kim = 4 + half * 2 + b
                state = lambda k: jnp.concatenate(
                    [buf[l, pl.ds(k, T, stride=N_STATE_ROWS), :] for l in range(N_LANE_TILES)],
                    axis=1).astype(BF16)
                y = jnp.dot(state(kre), wc_ref[d, 0, half], preferred_element_type=F32)
                y = y + jnp.dot(state(kim), wc_ref[d, 1, half], preferred_element_type=F32)
                y_ref[b, :, half * SSM_HALF_IN:(half + 1) * SSM_HALF_IN] = y


def _ssm(u, wb, wc, ar, ai):
    fwd = lambda i: (0, i, 0)
    bwd = lambda i: (0, jnp.where(i == 0, 0, N_TILES - i), 0)
    return pl.pallas_call(
        _ssm_kernel,
        grid=(N_TILES,),
        in_specs=[
            pl.BlockSpec((BATCH, TOK_TILE, SSM_WIDTH), fwd),
            pl.BlockSpec((BATCH, TOK_TILE, SSM_WIDTH), bwd),
            pl.BlockSpec((2, 2, 2, SSM_HALF_IN, SSM_HALF_ST), lambda i: (0, 0, 0, 0, 0)),
            pl.BlockSpec((2, 2, 2, SSM_HALF_ST, SSM_HALF_IN), lambda i: (0, 0, 0, 0, 0)),
            pl.BlockSpec((2, N_STATE_ROWS, SSM_HALF_ST), lambda i: (0, 0, 0)),
            pl.BlockSpec((2, N_STATE_ROWS, SSM_HALF_ST), lambda i: (0, 0, 0)),
        ],
        out_specs=[
            pl.BlockSpec((BATCH, TOK_TILE, SSM_WIDTH), fwd),
            pl.BlockSpec((BATCH, TOK_TILE, SSM_WIDTH), bwd),
        ],
        out_shape=[jax.ShapeDtypeStruct((BATCH, S_TOT, SSM_WIDTH), F32)] * 2,
        scratch_shapes=[
            *[pltpu.VMEM((N_LANE_TILES, N_STATE_ROWS * TOK_TILE, LANES), F32)] * 4,
            pltpu.VMEM((N_STATE_ROWS, SSM_HALF_ST), F32),
            pltpu.VMEM((N_STATE_ROWS, SSM_HALF_ST), F32),
        ],
        compiler_params=pltpu.CompilerParams(
            dimension_semantics=("arbitrary",), vmem_limit_bytes=VMEM_LIMIT),
        name="ssm",
    )(u, u, wb, wc, ar, ai)


def _ssm_params(lam_re, lam_im, log_dt, b_re, b_im, c_re, c_im):
    lr = lam_re.astype(F32)
    li = lam_im.astype(F32)
    dt = jnp.exp(log_dt.astype(F32))[..., None]
    mag = jnp.exp(lr * dt)
    ang = li * dt
    a_re = mag * jnp.cos(ang)
    a_im = mag * jnp.sin(ang)
    den = lr * lr + li * li
    k_re = ((a_re - 1.0) * lr + a_im * li) / den
    k_im = (a_im * lr - (a_re - 1.0) * li) / den
    bb_re = k_re[..., None] * b_re - k_im[..., None] * b_im
    bb_im = k_re[..., None] * b_im + k_im[..., None] * b_re
    n_g = SSM_GROUPS // 2
    drive_mask = np.equal.outer(np.arange(SSM_HALF_IN) // SSM_GROUP, np.arange(SSM_HALF_ST) // SSM_STATE)
    read_mask = drive_mask.T

    def drive(bb):
        rows = jnp.swapaxes(bb, -1, -2).reshape(2, 2, SSM_HALF_IN, SSM_STATE)
        return jnp.tile(rows, (1, 1, 1, n_g)) * drive_mask.astype(np.float32)

    def readout(c):
        rows = jnp.swapaxes(c, -1, -2).reshape(2, 2, SSM_HALF_ST, SSM_GROUP)
        return jnp.tile(rows, (1, 1, 1, n_g)) * read_mask.astype(np.float32)

    wb = jnp.stack([drive(bb_re), drive(bb_im)], axis=1).astype(BF16)
    wc = jnp.stack([readout(c_re.astype(F32)), readout(-c_im.astype(F32))], axis=1).astype(BF16)

    def rows(a, sign_re, sign_im):
        halves = a.reshape(2, 2, SSM_HALF_ST)
        per = []
        for reim in range(2):
            for half in range(2):
                for _ in range(BATCH):
                    per.append(halves[:, half] * (sign_re if reim == 0 else sign_im))
        return jnp.stack(per, axis=1)

    ar = rows(a_re, 1.0, 1.0)
    ai = rows(a_im, -1.0, 1.0)
    return wb, wc, ar, ai


def _attn_kernel(sink_ref, band_ref, q_ref, kc_ref, kp_ref, kn0_ref, kn_ref, o_ref):
    j = pl.program_id(1)
    n = j - CTX_LEN // ATTN_BLOCK
    q = q_ref[0]
    kvc = kc_ref[0]
    band = jnp.concatenate([kp_ref[0], kn0_ref[0], kn_ref[0]], axis=0)
    nb = band.shape[0]
    last = SEQ // ATTN_BLOCK - 1
    col = lax.broadcasted_iota(jnp.int32, (1, nb), 1)
    lo_c = jnp.where(n < 0, nb, jnp.where(n == 0, ATTN_BLOCK, 0))
    hi_c = jnp.where(n == last, 2 * ATTN_BLOCK, nb)
    bias = band_ref[...] + jnp.where((col >= lo_c) & (col < hi_c), 0.0, MASK_VALUE)
    bias = jnp.concatenate([bias] * ATTN_STACK, axis=0)
    nt = (((1,), (1,)), ((), ()))
    outs = []
    for hk in range(N_KV_HEADS):
        k_b = band[:, hk * HEAD_DIM:(hk + 1) * HEAD_DIM]
        v_b = band[:, KV_WIDTH + hk * HEAD_DIM:KV_WIDTH + (hk + 1) * HEAD_DIM]
        k_c = kvc[:, hk * HEAD_DIM:(hk + 1) * HEAD_DIM]
        v_c = kvc[:, KV_WIDTH + hk * HEAD_DIM:KV_WIDTH + (hk + 1) * HEAD_DIM]
        for g0 in range(0, GQA_GROUP, ATTN_STACK):
            heads = range(hk * GQA_GROUP + g0, hk * GQA_GROUP + g0 + ATTN_STACK)
            qs = jnp.concatenate([q[:, hq * HEAD_DIM:(hq + 1) * HEAD_DIM] for hq in heads], axis=0)
            sink = jnp.concatenate([jnp.full((ATTN_BLOCK, 1), sink_ref[hq], F32) for hq in heads], axis=0)
            s_w = lax.dot_general(qs, k_b, nt, preferred_element_type=F32) + bias
            s_c = lax.dot_general(qs, k_c, nt, preferred_element_type=F32)
            m = jnp.maximum(jnp.maximum(jnp.max(s_w, axis=-1, keepdims=True),
                                        jnp.max(s_c, axis=-1, keepdims=True)), sink)
            e_w = jnp.exp(s_w - m)
            e_c = jnp.exp(s_c - m)
            den = (jnp.sum(e_w, axis=-1, keepdims=True) + jnp.sum(e_c, axis=-1, keepdims=True)
                   + jnp.exp(sink - m))
            o = jnp.dot(e_w.astype(BF16), v_b, preferred_element_type=F32)
            o = (o + jnp.dot(e_c.astype(BF16), v_c, preferred_element_type=F32)) / den
            outs.extend(o[g * ATTN_BLOCK:(g + 1) * ATTN_BLOCK] for g in range(ATTN_STACK))
    o_ref[0] = jnp.concatenate(outs, axis=1).astype(BF16)


def _attn(sink, q, kv):
    nq = S_TOT // ATTN_BLOCK
    first = CTX_LEN // ATTN_BLOCK
    clamp = lambda j: jnp.clip(j, first, nq - 1)
    r = jnp.arange(ATTN_BLOCK)[:, None]
    c = jnp.arange(3 * ATTN_BLOCK)[None, :] - ATTN_BLOCK
    band_bias = jnp.where(jnp.abs(c - r) <= WINDOW, 0.0, MASK_VALUE).astype(F32)
    return pl.pallas_call(
        _attn_kernel,
        grid=(BATCH, nq),
        in_specs=[
            pl.BlockSpec(memory_space=pltpu.SMEM),
            pl.BlockSpec((ATTN_BLOCK, 3 * ATTN_BLOCK), lambda b, j: (0, 0)),
            pl.BlockSpec((1, ATTN_BLOCK, ATTN_WIDTH), lambda b, j: (b, j, 0)),
            pl.BlockSpec((1, CTX_LEN, 2 * KV_WIDTH), lambda b, j: (b, 0, 0)),
            pl.BlockSpec((1, ATTN_BLOCK, 2 * KV_WIDTH), lambda b, j: (b, clamp(j - 1), 0)),
            pl.BlockSpec((1, ATTN_BLOCK, 2 * KV_WIDTH), lambda b, j: (b, clamp(j), 0)),
            pl.BlockSpec((1, ATTN_BLOCK, 2 * KV_WIDTH), lambda b, j: (b, clamp(j + 1), 0)),
        ],
        out_specs=pl.BlockSpec((1, ATTN_BLOCK, ATTN_WIDTH), lambda b, j: (b, j, 0)),
        out_shape=jax.ShapeDtypeStruct((BATCH, S_TOT, ATTN_WIDTH), BF16),
        compiler_params=pltpu.CompilerParams(dimension_semantics=("parallel", "parallel")),
        name="attn",
    )(sink, band_bias, q, kv, kv, kv, kv)


def _gelu_tanh(x):
    return 0.5 * x * (1.0 + jnp.tanh(math.sqrt(2.0 / math.pi) * (x + 0.044715 * (x * x * x))))


def _pool_means(ext, i):
    T = TOK_TILE
    n_ext = T + 2 * POOL_HALO
    seq_len = jnp.where(i == 0, CTX_LEN, SEQ)
    start = jnp.where(i == 0, 0, (i - 1) * T)
    r = start - POOL_HALO + lax.broadcasted_iota(jnp.int32, (n_ext, 1), 0)
    ext = jnp.where((r >= 0) & (r < seq_len), ext, 0.0)
    tl = start + lax.broadcasted_iota(jnp.int32, (T, 1), 0)
    back = lambda a, s: pltpu.roll(a, s, 0)
    fwd = lambda a, s: pltpu.roll(a, n_ext - s, 0)
    diffs = []
    for gi, w in enumerate(POOL_WINDOWS):
        e = ext[:, gi * POOL_GROUP:(gi + 1) * POOL_GROUP]
        lo = w // 2
        hi = w - 1 - lo
        s = e + back(e, 1)
        span = 2
        while span < w:
            s = s + back(s, span)
            span *= 2
        if w > 2:
            s = fwd(s, hi)
        cnt = (jnp.clip(tl + hi + 1, 0, seq_len) - jnp.clip(tl - lo, 0, seq_len)).astype(F32)
        mean = s[POOL_HALO:POOL_HALO + T] / cnt
        diffs.append((mean - e[POOL_HALO:POOL_HALO + T]).astype(BF16))
    return diffs


def _merge_kernel(xc_ref, xl_ref, mod_ref, g1_ref, w_ref, yf_ref, yb_ref, u_ref, dsk_ref, wval_ref, wgate_ref,
                  o_ref, wup_ref, pc_ref, pp_ref, pn_ref, pw_ref, ps_ref, wout_ref, g2_ref, rw_ref, rb_ref,
                  xo_ref, f_ref, te_ref, tp_ref):
    b = pl.program_id(0)
    i = pl.program_id(1)
    row = _mod_row(i, b)
    mods = [mod_ref[j, pl.ds(row, 1), :] for j in range(6)]
    ext = jnp.concatenate([pp_ref[0], pc_ref[0], pn_ref[0]], axis=0)
    diffs = _pool_means(ext, i)
    x_tile = _token_tile(i, xc_ref, xl_ref)
    R = TOK_TILE // MERGE_SPLIT
    for r0 in range(0, TOK_TILE, R):
        rs = slice(r0, r0 + R)
        x = x_tile[rs]
        h = (_rms(x, g1_ref[...]) * (1.0 + mods[1]) + mods[0]).astype(BF16)
        t = jnp.tanh(jnp.dot(h, w_ref[0, :, OFF_GATE:], preferred_element_type=F32))

        z = yf_ref[0, rs, :] + yb_ref[0, rs, :] + dsk_ref[...] * u_ref[0, rs, :]
        z = _gelu_tanh(z).astype(BF16)
        val = jnp.dot(z, wval_ref[...], preferred_element_type=F32)
        y_ssm = val * jnp.tanh(jnp.dot(z, wgate_ref[...], preferred_element_type=F32)) + val
        y_attn = jnp.dot(o_ref[0, rs, :], wup_ref[...], preferred_element_type=F32)
        y_pool = jnp.concatenate(
            [jnp.dot(diffs[gi][rs], pw_ref[gi], preferred_element_type=F32)
             for gi in range(len(POOL_WINDOWS))], axis=1) * ps_ref[...]

        m = (t[:, :D_MODEL] * y_ssm + t[:, D_MODEL:2 * D_MODEL] * y_attn + t[:, 2 * D_MODEL:] * y_pool
             + (y_ssm + y_attn + y_pool))
        xo = x + mods[2] * jnp.dot(m.astype(BF16), wout_ref[...], preferred_element_type=F32)
        xo_ref[0, rs, :] = xo
        f = _rms(xo, g2_ref[...]) * (1.0 + mods[4]) + mods[3]
        f_ref[0, rs, :] = _pack_rows(f)

        f_hi = f.astype(BF16)
        f_lo = (f - f_hi.astype(F32)).astype(BF16)
        r_hi = jnp.dot(f_hi, rw_ref[...], preferred_element_type=F32)
        r_lo = jnp.dot(f_lo, rw_ref[...], preferred_element_type=F32)
        logits = r_hi[:, :N_EXPERTS] + r_hi[:, N_EXPERTS:] + r_lo[:, :N_EXPERTS] + rb_ref[...]
        lane = lax.broadcasted_iota(jnp.int32, logits.shape, 1)
        vals, idxs = [], []
        for _ in range(TOP_K):
            mx = jnp.max(logits, axis=-1, keepdims=True)
            ix = jnp.min(jnp.where(logits == mx, lane, N_EXPERTS), axis=-1, keepdims=True)
            vals.append(mx)
            idxs.append(ix)
            logits = jnp.where(lane == ix, -jnp.inf, logits)
        ex = [jnp.exp(v - vals[0]) for v in vals]
        tot = ex[0] + ex[1] + ex[2] + ex[3]
        te_ref[0, rs, :] = jnp.concatenate(idxs, axis=1)
        tp_ref[0, rs, :] = jnp.concatenate(ex, axis=1) / tot


def _merge(xs, mod, g1, w_in, layer, yf, yb, u, dsk, wval, wgate, o, wup, pool_in, pw, ps, wout, g2, rw, rb):
    tok = lambda b, i: (b, i, 0)
    const2 = lambda b, i: (0, 0)
    const3 = lambda b, i: (0, 0, 0)
    per_tile = TOK_TILE // POOL_HALO
    n_halo = S_TOT // POOL_HALO
    return pl.pallas_call(
        _merge_kernel,
        grid=(BATCH, N_TILES),
        in_specs=[
            *_token_specs(xs),
            pl.BlockSpec((6, 8, D_MODEL), const3),
            pl.BlockSpec((1, D_MODEL), const2),
            pl.BlockSpec((1, D_MODEL, IN_WIDTH), lambda b, i: (layer, 0, 0)),
            pl.BlockSpec((1, TOK_TILE, SSM_WIDTH), tok),
            pl.BlockSpec((1, TOK_TILE, SSM_WIDTH), tok),
            pl.BlockSpec((1, TOK_TILE, SSM_WIDTH), tok),
            pl.BlockSpec((1, SSM_WIDTH), const2),
            pl.BlockSpec((SSM_WIDTH, D_MODEL), const2),
            pl.BlockSpec((SSM_WIDTH, D_MODEL), const2),
            pl.BlockSpec((1, TOK_TILE, ATTN_WIDTH), tok),
            pl.BlockSpec((ATTN_WIDTH, D_MODEL), const2),
            pl.BlockSpec((1, TOK_TILE, POOL_WIDTH), tok),
            pl.BlockSpec((1, POOL_HALO, POOL_WIDTH), lambda b, i: (b, jnp.maximum(i * per_tile - 1, 0), 0)),
            pl.BlockSpec((1, POOL_HALO, POOL_WIDTH),
                         lambda b, i: (b, jnp.minimum((i + 1) * per_tile, n_halo - 1), 0)),
            pl.BlockSpec((len(POOL_WINDOWS), POOL_GROUP, POOL_OUT_GROUP), const3),
            pl.BlockSpec((1, D_MODEL), const2),
            pl.BlockSpec((D_MODEL, D_MODEL), const2),
            pl.BlockSpec((1, D_MODEL), const2),
            pl.BlockSpec((D_MODEL, 2 * N_EXPERTS), const2),
            pl.BlockSpec((1, N_EXPERTS), const2),
        ],
        out_specs=[
            pl.BlockSpec((1, TOK_TILE, D_MODEL), tok),
            pl.BlockSpec((1, TOK_TILE, PACKED), tok),
            pl.BlockSpec((1, TOK_TILE, TOP_K), tok),
            pl.BlockSpec((1, TOK_TILE, TOP_K), tok),
        ],
        out_shape=[
            jax.ShapeDtypeStruct((BATCH, S_TOT, D_MODEL), F32),
            jax.ShapeDtypeStruct((BATCH, S_TOT, PACKED), jnp.uint32),
            jax.ShapeDtypeStruct((BATCH, S_TOT, TOP_K), jnp.int32),
            jax.ShapeDtypeStruct((BATCH, S_TOT, TOP_K), F32),
        ],
        compiler_params=pltpu.CompilerParams(
            dimension_semantics=("parallel", "parallel"), vmem_limit_bytes=VMEM_LIMIT),
        name="merge",
    )(*xs, mod, g1, w_in, yf, yb, u, dsk, wval, wgate, o, wup, pool_in, pool_in, pool_in, pw, ps, wout, g2, rw, rb)


def _route_kernel(te_ref, below_ref, dest_ref, be_ref, cnt_ref, carry_ref):
    ph = pl.program_id(0)
    i = pl.program_id(1)
    te = te_ref[...]
    lane = lax.broadcasted_iota(jnp.int32, (ROUTE_TILE, LANES), 1)
    ohs = [(te[:, k:k + 1] == lane).astype(F32) for k in range(TOP_K)]
    oh_all = ohs[0] + ohs[1] + ohs[2] + ohs[3]
    tile_cnt = jnp.sum(oh_all, axis=0, keepdims=True)

    @pl.when((ph == 0) & (i == 0))
    def _():
        cnt_ref[...] = jnp.zeros_like(cnt_ref)

    @pl.when(ph == 0)
    def _():
        cnt_ref[...] += tile_cnt

    @pl.when((ph == 1) & (i == 0))
    def _():
        cnt = jnp.broadcast_to(cnt_ref[...], (8, LANES))
        padded = jnp.floor((cnt + (ROW_BLOCK - 1)) / ROW_BLOCK) * ROW_BLOCK
        l8 = lax.broadcasted_iota(jnp.int32, (8, LANES), 1)
        end = padded
        s = 1
        while s < N_EXPERTS:
            end = end + jnp.where(l8 >= s, pltpu.roll(end, s, 1), 0.0)
            s *= 2
        carry_ref[...] = (end - padded)[0:1]
        blk_row = (lax.broadcasted_iota(jnp.int32, (N_BLOCKS_PAD, LANES), 0) * ROW_BLOCK).astype(F32)
        lb = lax.broadcasted_iota(jnp.int32, (N_BLOCKS_PAD, LANES), 1)
        hit = jnp.where((lb < N_EXPERTS) & (jnp.broadcast_to(end[0:1], blk_row.shape) <= blk_row), 1.0, 0.0)
        be = jnp.sum(hit, axis=-1, keepdims=True)
        be_ref[...] = jnp.broadcast_to(be, (N_BLOCKS_PAD, LANES)).astype(jnp.int32)

    @pl.when(ph == 1)
    def _():
        rank = carry_ref[...] + jnp.dot(below_ref[...], oh_all.astype(BF16), preferred_element_type=F32)
        dests = [jnp.sum(ohs[k] * rank, axis=-1, keepdims=True) for k in range(TOP_K)]
        dest_ref[...] = jnp.concatenate(dests, axis=1).astype(jnp.int32)
        carry_ref[...] += tile_cnt


def _route(top_e):
    n_steps = N_TOK // ROUTE_TILE
    return pl.pallas_call(
        _route_kernel,
        grid=(2, n_steps),
        in_specs=[pl.BlockSpec((ROUTE_TILE, TOP_K), lambda ph, i: (i, 0)),
                  pl.BlockSpec((ROUTE_TILE, ROUTE_TILE), lambda ph, i: (0, 0))],
        out_specs=[
            pl.BlockSpec((ROUTE_TILE, TOP_K), lambda ph, i: (ph * i, 0)),
            pl.BlockSpec((N_BLOCKS_PAD, LANES), lambda ph, i: (0, 0)),
        ],
        out_shape=[
            jax.ShapeDtypeStruct((N_TOK, TOP_K), jnp.int32),
            jax.ShapeDtypeStruct((N_BLOCKS_PAD, LANES), jnp.int32),
        ],
        scratch_shapes=[pltpu.VMEM((1, LANES), F32), pltpu.VMEM((1, LANES), F32)],
        compiler_params=pltpu.CompilerParams(dimension_semantics=("arbitrary", "arbitrary")),
        name="route",
    )(top_e, jnp.asarray(np.tril(np.ones((ROUTE_TILE, ROUTE_TILE), np.float32), -1), BF16))


def _sc_mesh():
    return plsc.VectorSubcoreMesh(core_axis_name="c", subcore_axis_name="s")


def _sc_worker():
    return lax.axis_index("s") * SC_CORES + lax.axis_index("c")


def _dispatch(f, dest_t):
    per_w = N_TOK // SC_WORKERS
    n_chunks = per_w // SC_SCATTER_ROWS

    @functools.partial(
        pl.kernel,
        out_type=jax.ShapeDtypeStruct((N_ROWS, PACKED), jnp.uint32),
        mesh=_sc_mesh(),
        scratch_types=[
            pltpu.VMEM((SC_SCATTER_ROWS,), jnp.int32),
            pltpu.VMEM((SC_SCATTER_ROWS, PACKED), jnp.uint32),
            pltpu.SemaphoreType.DMA,
        ],
        name="dispatch",
    )
    def k(f_hbm, i_hbm, o_hbm, idx_v, rows_v, sem):
        base = _sc_worker() * per_w

        @pl.loop(0, n_chunks)
        def _(j):
            t0 = pl.multiple_of(base + j * SC_SCATTER_ROWS, 8)
            pltpu.sync_copy(f_hbm.at[pl.ds(t0, SC_SCATTER_ROWS)], rows_v)
            for kk in range(TOP_K):
                pltpu.sync_copy(i_hbm.at[pl.ds(kk * N_TOK + t0, SC_SCATTER_ROWS)], idx_v)
                pltpu.async_copy(rows_v, o_hbm.at[idx_v], sem).wait()

    return k(f, dest_t.reshape(TOP_K * N_TOK))


def _combine_gather(yb, dest_t):
    n = TOP_K * N_TOK
    per_w = n // SC_WORKERS
    n_chunks = per_w // SC_GATHER_ROWS

    @functools.partial(
        pl.kernel,
        out_type=jax.ShapeDtypeStruct((n, PACKED), jnp.uint32),
        mesh=_sc_mesh(),
        scratch_types=[
            pltpu.VMEM((SC_GATHER_ROWS,), jnp.int32),
            pltpu.VMEM((SC_GATHER_ROWS, PACKED), jnp.uint32),
            pltpu.SemaphoreType.DMA,
        ],
        name="combine_gather",
    )
    def k(y_hbm, i_hbm, o_hbm, idx_v, rows_v, sem):
        base = _sc_worker() * per_w

        @pl.loop(0, n_chunks)
        def _(j):
            r0 = pl.multiple_of(base + j * SC_GATHER_ROWS, 8)
            pltpu.sync_copy(i_hbm.at[pl.ds(r0, SC_GATHER_ROWS)], idx_v)
            pltpu.async_copy(y_hbm.at[idx_v], rows_v, sem).wait()
            pltpu.sync_copy(rows_v, o_hbm.at[pl.ds(r0, SC_GATHER_ROWS)])

    return k(yb, dest_t.reshape(n))


def _weight_copies(w1_hbm, w2_hbm, w1f, w2f, sem, idx):
    return (pltpu.make_async_copy(w1_hbm.at[idx], w1f, sem.at[0]),
            pltpu.make_async_copy(w2_hbm.at[idx], w2f, sem.at[1]))


def _expert_kernel(be_ref, x_ref, w1_hbm, b1_ref, w2_hbm, b2_ref, y_ref, w1f, w2f, w1b, w2b, sem, *, layer):
    j = pl.program_id(0)
    e = be_ref[j]
    prev = be_ref[jnp.maximum(j - 1, 0)]
    live = e < N_EXPERTS
    first = live & ((j == 0) | (e != prev))
    copies = functools.partial(_weight_copies, w1_hbm, w2_hbm, w1f, w2f, sem)

    @pl.when(first & (j == 0))
    def _():
        for c in copies(layer * N_EXPERTS + e):
            c.start()

    @pl.when(first)
    def _():
        for c in copies(layer * N_EXPERTS + e):
            c.wait()
        w1b[...] = w1f[...].astype(BF16)
        w2b[...] = w2f[...].astype(BF16)
        nxt = lax.while_loop(
            lambda k: (k < N_BLOCKS) & (be_ref[jnp.minimum(k, N_BLOCKS - 1)] == e), lambda k: k + 1, j + 1)
        e_next = be_ref[jnp.minimum(nxt, N_BLOCKS - 1)]

        @pl.when((nxt < N_BLOCKS) & (e_next < N_EXPERTS))
        def _():
            for c in copies(layer * N_EXPERTS + e_next):
                c.start()

    @pl.when(live)
    def _():
        x = jnp.concatenate(_unpack_rows(x_ref[...]), axis=1).astype(BF16)
        gu = jnp.dot(x, w1b[...], preferred_element_type=F32) + b1_ref[0]
        gate = jnp.minimum(gu[:, :D_EXPERT], SWIGLU_LIMIT)
        lin = jnp.clip(gu[:, D_EXPERT:], -SWIGLU_LIMIT, SWIGLU_LIMIT)
        act = gate * _sigmoid(SWIGLU_ALPHA * gate) * (lin + 1.0)
        y_ref[...] = _pack_rows(jnp.dot(act.astype(BF16), w2b[...], preferred_element_type=F32) + b2_ref[0])


def _experts(block_e, rows, w1, b1, w2, b2, layer):
    ex = lambda j, be: layer * N_EXPERTS + jnp.minimum(be[j], N_EXPERTS - 1)
    return pl.pallas_call(
        functools.partial(_expert_kernel, layer=layer),
        grid_spec=pltpu.PrefetchScalarGridSpec(
            num_scalar_prefetch=1,
            grid=(N_BLOCKS,),
            in_specs=[
                pl.BlockSpec((ROW_BLOCK, PACKED), lambda j, be: (j, 0)),
                pl.BlockSpec(memory_space=pl.ANY),
                pl.BlockSpec((1, 1, 2 * D_EXPERT), lambda j, be: (ex(j, be), 0, 0)),
                pl.BlockSpec(memory_space=pl.ANY),
                pl.BlockSpec((1, 1, D_MODEL), lambda j, be: (ex(j, be), 0, 0)),
            ],
            out_specs=pl.BlockSpec((ROW_BLOCK, PACKED), lambda j, be: (j, 0)),
            scratch_shapes=[
                pltpu.VMEM((D_MODEL, 2 * D_EXPERT), F32),
                pltpu.VMEM((D_EXPERT, D_MODEL), F32),
                pltpu.VMEM((D_MODEL, 2 * D_EXPERT), BF16),
                pltpu.VMEM((D_EXPERT, D_MODEL), BF16),
                pltpu.SemaphoreType.DMA((2,)),
            ],
        ),
        out_shape=jax.ShapeDtypeStruct((N_ROWS, PACKED), jnp.uint32),
        compiler_params=pltpu.CompilerParams(
            dimension_semantics=("arbitrary",), vmem_limit_bytes=VMEM_LIMIT),
        name="experts",
    )(block_e, rows, w1, b1, w2, b2)


def _combined(x_ref, mod_ref, g_ref, p_ref, tile_off):
    b = pl.program_id(0)
    i = pl.program_id(1) + tile_off
    row = _mod_row(i, b)
    p = p_ref[0]
    lo, hi = _unpack_rows(g_ref[0, 0])
    y_lo = p[:, 0:1] * lo
    y_hi = p[:, 0:1] * hi
    for k in range(1, TOP_K):
        lo, hi = _unpack_rows(g_ref[k, 0])
        y_lo = y_lo + p[:, k:k + 1] * lo
        y_hi = y_hi + p[:, k:k + 1] * hi
    return x_ref[0] + mod_ref[5, pl.ds(row, 1), :] * jnp.concatenate([y_lo, y_hi], axis=1)


def _combine_kernel(x_ref, mod_ref, g_ref, p_ref, o_ref, *, tile_off):
    o_ref[0] = _combined(x_ref, mod_ref, g_ref, p_ref, tile_off)


def _combine_final_kernel(x_ref, mod_ref, g_ref, p_ref, fg_ref, o_ref, *, tile_off):
    o_ref[0] = _rms(_combined(x_ref, mod_ref, g_ref, p_ref, tile_off), fg_ref[...])


def _combine(xa, mod, gathered, probs, final_g=None):
    last = final_g is not None
    tile_off = CTX_LEN // TOK_TILE if last else 0
    n_t = N_TILES - tile_off
    tok = lambda b, i: (b, i + tile_off, 0)
    in_specs = [
        pl.BlockSpec((1, TOK_TILE, D_MODEL), tok),
        pl.BlockSpec((6, 8, D_MODEL), lambda b, i: (0, 0, 0)),
        pl.BlockSpec((TOP_K, 1, TOK_TILE, PACKED), lambda b, i: (0, b, i + tile_off, 0)),
        pl.BlockSpec((1, TOK_TILE, TOP_K), tok),
    ]
    args = [xa, mod, gathered.reshape(TOP_K, BATCH, S_TOT, PACKED), probs]
    if last:
        in_specs.append(pl.BlockSpec((1, D_MODEL), lambda b, i: (0, 0)))
        args.append(final_g)
        body = functools.partial(_combine_final_kernel, tile_off=tile_off)
    else:
        body = functools.partial(_combine_kernel, tile_off=tile_off)
    return pl.pallas_call(
        body,
        grid=(BATCH, n_t),
        in_specs=in_specs,
        out_specs=pl.BlockSpec((1, TOK_TILE, D_MODEL), lambda b, i: (b, i, 0)),
        out_shape=jax.ShapeDtypeStruct((BATCH, n_t * TOK_TILE, D_MODEL), F32),
        compiler_params=pltpu.CompilerParams(dimension_semantics=("parallel", "parallel")),
        name="combine",
    )(*args)


def _split_bf16(w):
    bits = lax.bitcast_convert_type(w.astype(F32), jnp.uint32) & jnp.uint32(0xFFFF0000)
    hi = lax.bitcast_convert_type(bits, F32)
    return jnp.concatenate([hi.astype(BF16), (w - hi).astype(BF16)], axis=-1)


def _rope_tables():
    t = np.arange(SEQ)
    axis_dim = HEAD_DIM // 2
    inv_freq = (1.0 / (ROPE_BASE ** (np.arange(0, axis_dim, 2, dtype=np.float32) / axis_dim))).astype(np.float32)
    ang_r = (t // GRID_W).astype(np.float32)[:, None] * inv_freq[None]
    ang_c = (t % GRID_W).astype(np.float32)[:, None] * inv_freq[None]
    ang = np.concatenate([ang_r, ang_r, ang_c, ang_c] * (LANES // HEAD_DIM), axis=-1)
    cos = np.concatenate([np.ones((CTX_LEN, LANES), np.float32), np.cos(ang)], axis=0)
    sin = np.concatenate([np.zeros((CTX_LEN, LANES), np.float32), np.sin(ang)], axis=0)
    return jnp.asarray(cos, F32), jnp.asarray(sin, F32)


def kernel(x, c, ctx, c_ctx, w_ada, b_ada, norm1_g, norm2_g, w_in, ssm_lam_re, ssm_lam_im, ssm_log_dt, ssm_b_re, ssm_b_im, ssm_c_re, ssm_c_im, ssm_d, ssm_w_val, ssm_w_gate, attn_sink, attn_w_up, pool_w, pool_scale, w_out, router_w, router_b, exp_w1, exp_b1, exp_w2, exp_b2, final_g):
    assert x.shape == (BATCH, SEQ, D_MODEL) and ctx.shape == (BATCH, CTX_LEN, D_MODEL)
    cos_t, sin_t = _rope_tables()
    cv = jnp.concatenate([c, c_ctx[None], jnp.zeros((8 - BATCH - 1, D_MODEL), F32)], axis=0)
    mods = _ada(cv, w_ada, b_ada)
    xs = (ctx, x)
    ew1 = exp_w1.reshape(DEPTH * N_EXPERTS, D_MODEL, 2 * D_EXPERT)
    eb1 = exp_b1.reshape(DEPTH * N_EXPERTS, 1, 2 * D_EXPERT)
    ew2 = exp_w2.reshape(DEPTH * N_EXPERTS, D_EXPERT, D_MODEL)
    eb2 = exp_b2.reshape(DEPTH * N_EXPERTS, 1, D_MODEL)
    half_gate = jnp.where(jnp.arange(IN_WIDTH) >= OFF_GATE, 0.5, 1.0).astype(F32)
    w_in_b = (w_in * half_gate).astype(BF16)
    out = None
    for l in range(DEPTH):
        mod = mods[l]
        u, kv, q, pool_in = _inproj(xs, mod, norm1_g[l][None], w_in_b, l, cos_t, sin_t)
        wb, wc, ar, ai = _ssm_params(ssm_lam_re[l], ssm_lam_im[l], ssm_log_dt[l], ssm_b_re[l], ssm_b_im[l],
                                     ssm_c_re[l], ssm_c_im[l])
        yf, yb = _ssm(u, wb, wc, ar, ai)
        o = _attn(attn_sink[l], q, kv)
        xa, f, top_e, top_p = _merge(
            xs, mod, norm1_g[l][None], w_in_b, l, yf, yb, u, ssm_d[l][None],
            (0.5 * ssm_w_val[l]).astype(BF16), (0.5 * ssm_w_gate[l]).astype(BF16), o, attn_w_up[l].astype(BF16),
            pool_in, pool_w[l].astype(BF16), pool_scale[l][None], (0.5 * w_out[l]).astype(BF16), norm2_g[l][None],
            _split_bf16(router_w[l]), router_b[l][None])
        dest, block_e = _route(top_e.reshape(N_TOK, TOP_K))
        dest_t = dest.T
        rows = _dispatch(f.reshape(N_TOK, PACKED), dest_t)
        yrows = _experts(block_e[:N_BLOCKS, 0], rows, ew1, eb1, ew2, eb2, l)
        gathered = _combine_gather(yrows, dest_t)
        if l < DEPTH - 1:
            xa = _combine(xa, mod, gathered, top_p)
            xs = (xa, xa)
        else:
            out = _combine(xa, mod, gathered, top_p, final_g[None])
    return out
```

```python
import functools
import math

import jax
import jax.numpy as jnp
import numpy as np
from jax import lax
from jax.experimental import pallas as pl
from jax.experimental.pallas import tpu as pltpu
from jax.experimental.pallas import tpu_sc as plsc

F32 = jnp.float32
BF16 = jnp.bfloat16
HIGHEST = lax.Precision.HIGHEST

D_MODEL = 1024
BATCH = 2
SEQ = 8192
DEPTH = 2
GRID_W = 64
CTX_LEN = 256
NORM_EPS = 1e-6
SSM_WIDTH = 512
SSM_GROUP = 16
SSM_GROUPS = 32
SSM_STATE = 64
HEAD_DIM = 64
N_Q_HEADS = 8
N_KV_HEADS = 2
GQA_GROUP = 4
ATTN_WIDTH = 512
KV_WIDTH = 128
WINDOW = 128
ROPE_BASE = 10000.0
ATTN_SCALE = HEAD_DIM ** -0.5
POOL_WINDOWS = (2, 4, 8, 16)
POOL_WIDTH = 512
POOL_GROUP = 128
POOL_OUT_GROUP = 256
OFF_K = 512
OFF_V = 640
OFF_Q = 768
OFF_POOL = 1280
OFF_GATE = 1792
IN_WIDTH = OFF_GATE + 3 * D_MODEL
N_EXPERTS = 32
TOP_K = 4
D_EXPERT = 1024
SWIGLU_LIMIT = 7.0
SWIGLU_ALPHA = 1.702

S_TOT = CTX_LEN + SEQ
N_TOK = BATCH * S_TOT
TOK_TILE = 256
N_TILES = S_TOT // TOK_TILE
assert CTX_LEN == TOK_TILE
ATTN_BLOCK = 128
ATTN_STACK = 2
POOL_HALO = 8
ROUTE_TILE = 512
ROW_BLOCK = 256
N_ROWS = -(-(N_TOK * TOP_K + N_EXPERTS * (ROW_BLOCK - 1)) // ROW_BLOCK) * ROW_BLOCK
N_BLOCKS = N_ROWS // ROW_BLOCK
N_BLOCKS_PAD = -(-N_BLOCKS // 8) * 8
SC_CORES = 2
SC_WORKERS = SC_CORES * 16
PACKED = D_MODEL // 2
SC_SCATTER_ROWS = 88
SC_GATHER_ROWS = 88
assert N_TOK % (SC_WORKERS * SC_SCATTER_ROWS) == 0 and (N_TOK * TOP_K) % (SC_WORKERS * SC_GATHER_ROWS) == 0
VMEM_LIMIT = 56 * 1024 * 1024
LANES = 128
MASK_VALUE = -1e30


def _rms(x, g):
    return x * lax.rsqrt(jnp.mean(x * x, axis=-1, keepdims=True) + NORM_EPS) * g


def _sigmoid(x):
    return 0.5 * jnp.tanh(0.5 * x) + 0.5


def _pack_rows(x):
    half = x.shape[1] // 2
    lo = lax.bitcast_convert_type(x[:, :half].astype(BF16).astype(F32), jnp.uint32)
    hi = lax.bitcast_convert_type(x[:, half:].astype(BF16).astype(F32), jnp.uint32)
    return (lo >> 16) | hi


def _unpack_rows(p):
    lo = lax.bitcast_convert_type(p << 16, F32)
    hi = lax.bitcast_convert_type(p & jnp.uint32(0xFFFF0000), F32)
    return lo, hi


def _token_specs(xs, last=N_TILES - 1):
    ctx_tiles = CTX_LEN // TOK_TILE
    off = 0 if xs[1].shape[1] == S_TOT else ctx_tiles
    return [pl.BlockSpec((1, TOK_TILE, D_MODEL), lambda b, i: (b, 0, 0)),
            pl.BlockSpec((1, TOK_TILE, D_MODEL), lambda b, i: (b, jnp.clip(i, ctx_tiles, last) - off, 0))]


def _token_tile(i, xc_ref, xl_ref):
    return jnp.where(i == 0, xc_ref[0], xl_ref[0])


def _mod_row(i, b):
    return jnp.where(i == 0, 2, b)


def _ada_kernel(cv_ref, w_ref, b_ref, o_ref):
    cv = cv_ref[...]
    s = cv * jax.nn.sigmoid(cv)
    o_ref[0, 0] = jnp.dot(s, w_ref[0], preferred_element_type=F32, precision=HIGHEST) + b_ref[0, 0]


def _ada(cv, w_ada, b_ada):
    return pl.pallas_call(
        _ada_kernel,
        grid=(DEPTH, 6),
        in_specs=[
            pl.BlockSpec((8, D_MODEL), lambda l, j: (0, 0)),
            pl.BlockSpec((1, D_MODEL, D_MODEL), lambda l, j: (l, 0, j)),
            pl.BlockSpec((1, 1, 1, D_MODEL), lambda l, j: (l, j, 0, 0)),
        ],
        out_specs=pl.BlockSpec((1, 1, 8, D_MODEL), lambda l, j: (l, j, 0, 0)),
        out_shape=jax.ShapeDtypeStruct((DEPTH, 6, 8, D_MODEL), F32),
        name="ada",
    )(cv, w_ada, b_ada.reshape(DEPTH, 6, 1, D_MODEL))


def _rope(t, cos, sin):
    lane = lax.broadcasted_iota(jnp.int32, t.shape, 1)
    rot = jnp.where((lane % 32) < 16, -pltpu.roll(t, LANES - 16, 1), pltpu.roll(t, 16, 1))
    return t * cos + rot * sin


def _inproj_kernel(xc_ref, xl_ref, mod_ref, g_ref, w_ref, cos_ref, sin_ref, u_ref, kv_ref, q_ref, pool_ref):
    b = pl.program_id(0)
    i = pl.program_id(1)
    row = _mod_row(i, b)
    shift = mod_ref[0, pl.ds(row, 1), :]
    scale = mod_ref[1, pl.ds(row, 1), :]
    h = _rms(_token_tile(i, xc_ref, xl_ref), g_ref[...]) * (1.0 + scale) + shift
    p = jnp.dot(h.astype(BF16), w_ref[0, :, :OFF_GATE], preferred_element_type=F32)
    cos = cos_ref[...]
    sin = sin_ref[...]
    u_ref[0] = p[:, :OFF_K]
    kv_ref[0, :, :KV_WIDTH] = _rope(p[:, OFF_K:OFF_V], cos, sin).astype(BF16)
    kv_ref[0, :, KV_WIDTH:] = p[:, OFF_V:OFF_Q].astype(BF16)
    for c in range(ATTN_WIDTH // LANES):
        qc = p[:, OFF_Q + c * LANES:OFF_Q + (c + 1) * LANES]
        q_ref[0, :, c * LANES:(c + 1) * LANES] = (_rope(qc, cos, sin) * ATTN_SCALE).astype(BF16)
    pool_ref[0] = p[:, OFF_POOL:OFF_GATE]


def _inproj(xs, mod, g1, w_in, layer, cos_t, sin_t):
    tok = lambda b, i: (b, i, 0)
    return pl.pallas_call(
        _inproj_kernel,
        grid=(BATCH, N_TILES),
        in_specs=[
            *_token_specs(xs),
            pl.BlockSpec((6, 8, D_MODEL), lambda b, i: (0, 0, 0)),
            pl.BlockSpec((1, D_MODEL), lambda b, i: (0, 0)),
            pl.BlockSpec((1, D_MODEL, IN_WIDTH), lambda b, i: (layer, 0, 0)),
            pl.BlockSpec((TOK_TILE, LANES), lambda b, i: (i, 0)),
            pl.BlockSpec((TOK_TILE, LANES), lambda b, i: (i, 0)),
        ],
        out_specs=[
            pl.BlockSpec((1, TOK_TILE, SSM_WIDTH), tok),
            pl.BlockSpec((1, TOK_TILE, 2 * KV_WIDTH), tok),
            pl.BlockSpec((1, TOK_TILE, ATTN_WIDTH), tok),
            pl.BlockSpec((1, TOK_TILE, POOL_WIDTH), tok),
        ],
        out_shape=[
            jax.ShapeDtypeStruct((BATCH, S_TOT, SSM_WIDTH), F32),
            jax.ShapeDtypeStruct((BATCH, S_TOT, 2 * KV_WIDTH), BF16),
            jax.ShapeDtypeStruct((BATCH, S_TOT, ATTN_WIDTH), BF16),
            jax.ShapeDtypeStruct((BATCH, S_TOT, POOL_WIDTH), F32),
        ],
        compiler_params=pltpu.CompilerParams(
            dimension_semantics=("parallel", "parallel"), vmem_limit_bytes=VMEM_LIMIT),
        name="inproj",
    )(*xs, mod, g1, w_in, cos_t, sin_t)


SSM_HALF_IN = SSM_WIDTH // 2
SSM_HALF_ST = SSM_GROUPS // 2 * SSM_STATE
N_STATE_ROWS = 8
N_LANE_TILES = SSM_HALF_ST // LANES
SCAN_STRIDE = TOK_TILE + 8


def _ssm_kernel(uf_ref, ub_ref, wb_ref, wc_ref, ar_ref, ai_ref, yf_ref, yb_ref, buf_f, buf_b, hf_ref, hb_ref):
    i = pl.program_id(0)
    T = TOK_TILE

    @pl.when(i == 0)
    def _():
        hf_ref[...] = jnp.zeros_like(hf_ref)
        hb_ref[...] = jnp.zeros_like(hb_ref)

    for d, (u_ref, buf) in enumerate(((uf_ref, buf_f), (ub_ref, buf_b))):
        for b in range(BATCH):
            for half in range(2):
                ub16 = u_ref[b, :, half * SSM_HALF_IN:(half + 1) * SSM_HALF_IN].astype(BF16)
                for reim in range(2):
                    k = reim * 4 + half * 2 + b
                    bu = jnp.dot(ub16, wb_ref[d, reim, half], preferred_element_type=F32)
                    for l in range(N_LANE_TILES):
                        buf[l, k * SCAN_STRIDE:k * SCAN_STRIDE + T, :] = bu[:, l * LANES:(l + 1) * LANES]

    def step(t, carry):
        hf, hb = carry
        tb = T - 1 - t
        nf, nb = [], []
        for l in range(N_LANE_TILES):
            sl = pl.ds(l * LANES, LANES)
            rows_f = pl.ds(t, N_STATE_ROWS, stride=SCAN_STRIDE)
            h = ar_ref[0, :, sl] * hf[l] + ai_ref[0, :, sl] * pltpu.roll(hf[l], 4, 0) + buf_f[l, rows_f, :]
            buf_f[l, rows_f, :] = h
            nf.append(h)
            rows_b = pl.ds(tb, N_STATE_ROWS, stride=SCAN_STRIDE)
            h = ar_ref[1, :, sl] * hb[l] + ai_ref[1, :, sl] * pltpu.roll(hb[l], 4, 0) + buf_b[l, rows_b, :]
            buf_b[l, rows_b, :] = h
            nb.append(h)
        return tuple(nf), tuple(nb)

    hf0 = tuple(hf_ref[:, l * LANES:(l + 1) * LANES] for l in range(N_LANE_TILES))
    hb0 = tuple(hb_ref[:, l * LANES:(l + 1) * LANES] for l in range(N_LANE_TILES))
    hf1, hb1 = lax.fori_loop(0, T, step, (hf0, hb0), unroll=2)
    for l in range(N_LANE_TILES):
        hf_ref[:, l * LANES:(l + 1) * LANES] = hf1[l]
        hb_ref[:, l * LANES:(l + 1) * LANES] = hb1[l]

    for d, (buf, y_ref) in enumerate(((buf_f, yf_ref), (buf_b, yb_ref))):
        for b in range(BATCH):
            for half in range(2):
                kre = half * 2 + b
                kim = 4 + half * 2 + b
                state = lambda k: jnp.concatenate(
                    [buf[l, k * SCAN_STRIDE:k * SCAN_STRIDE + T, :] for l in range(N_LANE_TILES)],
                    axis=1).astype(BF16)
                y = jnp.dot(state(kre), wc_ref[d, 0, half], preferred_element_type=F32)
                y = y + jnp.dot(state(kim), wc_ref[d, 1, half], preferred_element_type=F32)
                y_ref[b, :, half * SSM_HALF_IN:(half + 1) * SSM_HALF_IN] = y


def _ssm(u, wb, wc, ar, ai):
    fwd = lambda i: (0, i, 0)
    bwd = lambda i: (0, jnp.where(i == 0, 0, N_TILES - i), 0)
    return pl.pallas_call(
        _ssm_kernel,
        grid=(N_TILES,),
        in_specs=[
            pl.BlockSpec((BATCH, TOK_TILE, SSM_WIDTH), fwd),
            pl.BlockSpec((BATCH, TOK_TILE, SSM_WIDTH), bwd),
            pl.BlockSpec((2, 2, 2, SSM_HALF_IN, SSM_HALF_ST), lambda i: (0, 0, 0, 0, 0)),
            pl.BlockSpec((2, 2, 2, SSM_HALF_ST, SSM_HALF_IN), lambda i: (0, 0, 0, 0, 0)),
            pl.BlockSpec((2, N_STATE_ROWS, SSM_HALF_ST), lambda i: (0, 0, 0)),
            pl.BlockSpec((2, N_STATE_ROWS, SSM_HALF_ST), lambda i: (0, 0, 0)),
        ],
        out_specs=[
            pl.BlockSpec((BATCH, TOK_TILE, SSM_WIDTH), fwd),
            pl.BlockSpec((BATCH, TOK_TILE, SSM_WIDTH), bwd),
        ],
        out_shape=[jax.ShapeDtypeStruct((BATCH, S_TOT, SSM_WIDTH), F32)] * 2,
        scratch_shapes=[
            pltpu.VMEM((N_LANE_TILES, N_STATE_ROWS * SCAN_STRIDE, LANES), F32),
            pltpu.VMEM((N_LANE_TILES, N_STATE_ROWS * SCAN_STRIDE, LANES), F32),
            pltpu.VMEM((N_STATE_ROWS, SSM_HALF_ST), F32),
            pltpu.VMEM((N_STATE_ROWS, SSM_HALF_ST), F32),
        ],
        compiler_params=pltpu.CompilerParams(
            dimension_semantics=("arbitrary",), vmem_limit_bytes=VMEM_LIMIT),
        name="ssm",
    )(u, u, wb, wc, ar, ai)


def _ssm_params(lam_re, lam_im, log_dt, b_re, b_im, c_re, c_im):
    lr = lam_re.astype(F32)
    li = lam_im.astype(F32)
    dt = jnp.exp(log_dt.astype(F32))[..., None]
    mag = jnp.exp(lr * dt)
    ang = li * dt
    a_re = mag * jnp.cos(ang)
    a_im = mag * jnp.sin(ang)
    den = lr * lr + li * li
    k_re = ((a_re - 1.0) * lr + a_im * li) / den
    k_im = (a_im * lr - (a_re - 1.0) * li) / den
    bb_re = k_re[..., None] * b_re - k_im[..., None] * b_im
    bb_im = k_re[..., None] * b_im + k_im[..., None] * b_re
    n_g = SSM_GROUPS // 2
    drive_mask = np.equal.outer(np.arange(SSM_HALF_IN) // SSM_GROUP, np.arange(SSM_HALF_ST) // SSM_STATE)
    read_mask = drive_mask.T

    def drive(bb):
        rows = jnp.swapaxes(bb, -1, -2).reshape(2, 2, SSM_HALF_IN, SSM_STATE)
        return jnp.tile(rows, (1, 1, 1, n_g)) * drive_mask.astype(np.float32)

    def readout(c):
        rows = jnp.swapaxes(c, -1, -2).reshape(2, 2, SSM_HALF_ST, SSM_GROUP)
        return jnp.tile(rows, (1, 1, 1, n_g)) * read_mask.astype(np.float32)

    wb = jnp.stack([drive(bb_re), drive(bb_im)], axis=1).astype(BF16)
    wc = jnp.stack([readout(c_re.astype(F32)), readout(-c_im.astype(F32))], axis=1).astype(BF16)

    def rows(a, sign_re, sign_im):
        halves = a.reshape(2, 2, SSM_HALF_ST)
        per = []
        for reim in range(2):
            for half in range(2):
                for _ in range(BATCH):
                    per.append(halves[:, half] * (sign_re if reim == 0 else sign_im))
        return jnp.stack(per, axis=1)

    ar = rows(a_re, 1.0, 1.0)
    ai = rows(a_im, -1.0, 1.0)
    return wb, wc, ar, ai


def _attn_kernel(sink_ref, band_ref, q_ref, kc_ref, kp_ref, kn0_ref, kn_ref, o_ref):
    j = pl.program_id(1)
    n = j - CTX_LEN // ATTN_BLOCK
    q = q_ref[0]
    kvc = kc_ref[0]
    band = jnp.concatenate([kp_ref[0], kn0_ref[0], kn_ref[0]], axis=0)
    nb = band.shape[0]
    last = SEQ // ATTN_BLOCK - 1
    col = lax.broadcasted_iota(jnp.int32, (1, nb), 1)
    lo_c = jnp.where(n < 0, nb, jnp.where(n == 0, ATTN_BLOCK, 0))
    hi_c = jnp.where(n == last, 2 * ATTN_BLOCK, nb)
    bias = band_ref[...] + jnp.where((col >= lo_c) & (col < hi_c), 0.0, MASK_VALUE)
    bias = jnp.concatenate([bias] * ATTN_STACK, axis=0)
    nt = (((1,), (1,)), ((), ()))
    outs = []
    for hk in range(N_KV_HEADS):
        k_b = band[:, hk * HEAD_DIM:(hk + 1) * HEAD_DIM]
        v_b = band[:, KV_WIDTH + hk * HEAD_DIM:KV_WIDTH + (hk + 1) * HEAD_DIM]
        k_c = kvc[:, hk * HEAD_DIM:(hk + 1) * HEAD_DIM]
        v_c = kvc[:, KV_WIDTH + hk * HEAD_DIM:KV_WIDTH + (hk + 1) * HEAD_DIM]
        for g0 in range(0, GQA_GROUP, ATTN_STACK):
            heads = range(hk * GQA_GROUP + g0, hk * GQA_GROUP + g0 + ATTN_STACK)
            qs = jnp.concatenate([q[:, hq * HEAD_DIM:(hq + 1) * HEAD_DIM] for hq in heads], axis=0)
            sink = jnp.concatenate([jnp.full((ATTN_BLOCK, 1), sink_ref[hq], F32) for hq in heads], axis=0)
            s_w = lax.dot_general(qs, k_b, nt, preferred_element_type=F32) + bias
            s_c = lax.dot_general(qs, k_c, nt, preferred_element_type=F32)
            m = jnp.maximum(jnp.maximum(jnp.max(s_w, axis=-1, keepdims=True),
                                        jnp.max(s_c, axis=-1, keepdims=True)), sink)
            e_w = jnp.exp(s_w - m)
            e_c = jnp.exp(s_c - m)
            den = (jnp.sum(e_w, axis=-1, keepdims=True) + jnp.sum(e_c, axis=-1, keepdims=True)
                   + jnp.exp(sink - m))
            o = jnp.dot(e_w.astype(BF16), v_b, preferred_element_type=F32)
            o = (o + jnp.dot(e_c.astype(BF16), v_c, preferred_element_type=F32)) / den
            outs.extend(o[g * ATTN_BLOCK:(g + 1) * ATTN_BLOCK] for g in range(ATTN_STACK))
    o_ref[0] = jnp.concatenate(outs, axis=1).astype(BF16)


def _attn(sink, q, kv):
    nq = S_TOT // ATTN_BLOCK
    first = CTX_LEN // ATTN_BLOCK
    clamp = lambda j: jnp.clip(j, first, nq - 1)
    r = jnp.arange(ATTN_BLOCK)[:, None]
    c = jnp.arange(3 * ATTN_BLOCK)[None, :] - ATTN_BLOCK
    band_bias = jnp.where(jnp.abs(c - r) <= WINDOW, 0.0, MASK_VALUE).astype(F32)
    return pl.pallas_call(
        _attn_kernel,
        grid=(BATCH, nq),
        in_specs=[
            pl.BlockSpec(memory_space=pltpu.SMEM),
            pl.BlockSpec((ATTN_BLOCK, 3 * ATTN_BLOCK), lambda b, j: (0, 0)),
            pl.BlockSpec((1, ATTN_BLOCK, ATTN_WIDTH), lambda b, j: (b, j, 0)),
            pl.BlockSpec((1, CTX_LEN, 2 * KV_WIDTH), lambda b, j: (b, 0, 0)),
            pl.BlockSpec((1, ATTN_BLOCK, 2 * KV_WIDTH), lambda b, j: (b, clamp(j - 1), 0)),
            pl.BlockSpec((1, ATTN_BLOCK, 2 * KV_WIDTH), lambda b, j: (b, clamp(j), 0)),
            pl.BlockSpec((1, ATTN_BLOCK, 2 * KV_WIDTH), lambda b, j: (b, clamp(j + 1), 0)),
        ],
        out_specs=pl.BlockSpec((1, ATTN_BLOCK, ATTN_WIDTH), lambda b, j: (b, j, 0)),
        out_shape=jax.ShapeDtypeStruct((BATCH, S_TOT, ATTN_WIDTH), BF16),
        compiler_params=pltpu.CompilerParams(dimension_semantics=("parallel", "parallel")),
        name="attn",
    )(sink, band_bias, q, kv, kv, kv, kv)


def _gelu_tanh(x):
    return 0.5 * x * (1.0 + jnp.tanh(math.sqrt(2.0 / math.pi) * (x + 0.044715 * (x * x * x))))


def _pool_means(ext, i):
    T = TOK_TILE
    n_ext = T + 2 * POOL_HALO
    seq_len = jnp.where(i == 0, CTX_LEN, SEQ)
    start = jnp.where(i == 0, 0, (i - 1) * T)
    r = start - POOL_HALO + lax.broadcasted_iota(jnp.int32, (n_ext, 1), 0)
    ext = jnp.where((r >= 0) & (r < seq_len), ext, 0.0)
    tl = start + lax.broadcasted_iota(jnp.int32, (T, 1), 0)
    back = lambda a, s: pltpu.roll(a, s, 0)
    fwd = lambda a, s: pltpu.roll(a, n_ext - s, 0)
    diffs = []
    for gi, w in enumerate(POOL_WINDOWS):
        e = ext[:, gi * POOL_GROUP:(gi + 1) * POOL_GROUP]
        lo = w // 2
        hi = w - 1 - lo
        s = e + back(e, 1)
        span = 2
        while span < w:
            s = s + back(s, span)
            span *= 2
        if w > 2:
            s = fwd(s, hi)
        cnt = (jnp.clip(tl + hi + 1, 0, seq_len) - jnp.clip(tl - lo, 0, seq_len)).astype(F32)
        mean = s[POOL_HALO:POOL_HALO + T] / cnt
        diffs.append((mean - e[POOL_HALO:POOL_HALO + T]).astype(BF16))
    return diffs


def _merge_kernel(xc_ref, xl_ref, mod_ref, g1_ref, w_ref, yf_ref, yb_ref, u_ref, dsk_ref, wval_ref, wgate_ref,
                  o_ref, wup_ref, pc_ref, pp_ref, pn_ref, pw_ref, ps_ref, wout_ref, g2_ref, rw_ref, rb_ref,
                  xo_ref, f_ref, te_ref, tp_ref, xprev_ref):
    b = pl.program_id(0)
    s = pl.program_id(1)

    @pl.when(s == 0)
    def _():
        xprev_ref[...] = jnp.zeros_like(xprev_ref)

    rowp = _mod_row(jnp.maximum(s - 1, 0), b)
    f = _rms(xprev_ref[...], g2_ref[...]) * (1.0 + mod_ref[4, pl.ds(rowp, 1), :]) + mod_ref[3, pl.ds(rowp, 1), :]
    f_ref[0] = _pack_rows(f)
    f_hi = f.astype(BF16)
    f_lo = (f - f_hi.astype(F32)).astype(BF16)
    r_hi = jnp.dot(f_hi, rw_ref[...], preferred_element_type=F32)
    r_lo = jnp.dot(f_lo, rw_ref[...], preferred_element_type=F32)
    logits = r_hi[:, :N_EXPERTS] + r_hi[:, N_EXPERTS:] + r_lo[:, :N_EXPERTS] + rb_ref[...]
    lane = lax.broadcasted_iota(jnp.int32, logits.shape, 1)
    vals, idxs = [], []
    for _ in range(TOP_K):
        mx = jnp.max(logits, axis=-1, keepdims=True)
        ix = jnp.min(jnp.where(logits == mx, lane, N_EXPERTS), axis=-1, keepdims=True)
        vals.append(mx)
        idxs.append(ix)
        logits = jnp.where(lane == ix, -jnp.inf, logits)
    ex = [jnp.exp(v - vals[0]) for v in vals]
    tot = ex[0] + ex[1] + ex[2] + ex[3]
    te_ref[0] = jnp.concatenate(idxs, axis=1)
    tp_ref[0] = jnp.concatenate(ex, axis=1) / tot

    i = jnp.minimum(s, N_TILES - 1)
    row = _mod_row(i, b)
    shift, scale, gate = (mod_ref[j, pl.ds(row, 1), :] for j in range(3))
    ext = jnp.concatenate([pp_ref[0], pc_ref[0], pn_ref[0]], axis=0)
    diffs = _pool_means(ext, i)
    x = _token_tile(i, xc_ref, xl_ref)
    h = (_rms(x, g1_ref[...]) * (1.0 + scale) + shift).astype(BF16)
    t = jnp.tanh(jnp.dot(h, w_ref[0, :, OFF_GATE:], preferred_element_type=F32))

    z = yf_ref[0] + yb_ref[0] + dsk_ref[...] * u_ref[0]
    z = _gelu_tanh(z).astype(BF16)
    val = jnp.dot(z, wval_ref[...], preferred_element_type=F32)
    y_ssm = val * jnp.tanh(jnp.dot(z, wgate_ref[...], preferred_element_type=F32)) + val
    y_attn = jnp.dot(o_ref[0], wup_ref[...], preferred_element_type=F32)
    y_pool = jnp.concatenate(
        [jnp.dot(diffs[gi], pw_ref[gi], preferred_element_type=F32) for gi in range(len(POOL_WINDOWS))],
        axis=1) * ps_ref[...]

    m = (t[:, :D_MODEL] * y_ssm + t[:, D_MODEL:2 * D_MODEL] * y_attn + t[:, 2 * D_MODEL:] * y_pool
         + (y_ssm + y_attn + y_pool))
    xo = x + gate * jnp.dot(m.astype(BF16), wout_ref[...], preferred_element_type=F32)
    xo_ref[0] = xo
    xprev_ref[...] = xo


def _merge(xs, mod, g1, w_in, layer, yf, yb, u, dsk, wval, wgate, o, wup, pool_in, pw, ps, wout, g2, rw, rb):
    last = N_TILES - 1
    tok = lambda b, s: (b, jnp.minimum(s, last), 0)
    prev = lambda b, s: (b, jnp.maximum(s - 1, 0), 0)
    const2 = lambda b, s: (0, 0)
    const3 = lambda b, s: (0, 0, 0)
    per_tile = TOK_TILE // POOL_HALO
    n_halo = S_TOT // POOL_HALO
    return pl.pallas_call(
        _merge_kernel,
        grid=(BATCH, N_TILES + 1),
        in_specs=[
            *_token_specs(xs, last),
            pl.BlockSpec((6, 8, D_MODEL), const3),
            pl.BlockSpec((1, D_MODEL), const2),
            pl.BlockSpec((1, D_MODEL, IN_WIDTH), lambda b, s: (layer, 0, 0)),
            pl.BlockSpec((1, TOK_TILE, SSM_WIDTH), tok),
            pl.BlockSpec((1, TOK_TILE, SSM_WIDTH), tok),
            pl.BlockSpec((1, TOK_TILE, SSM_WIDTH), tok),
            pl.BlockSpec((1, SSM_WIDTH), const2),
            pl.BlockSpec((SSM_WIDTH, D_MODEL), const2),
            pl.BlockSpec((SSM_WIDTH, D_MODEL), const2),
            pl.BlockSpec((1, TOK_TILE, ATTN_WIDTH), tok),
            pl.BlockSpec((ATTN_WIDTH, D_MODEL), const2),
            pl.BlockSpec((1, TOK_TILE, POOL_WIDTH), tok),
            pl.BlockSpec((1, POOL_HALO, POOL_WIDTH),
                         lambda b, s: (b, jnp.maximum(jnp.minimum(s, last) * per_tile - 1, 0), 0)),
            pl.BlockSpec((1, POOL_HALO, POOL_WIDTH),
                         lambda b, s: (b, jnp.minimum((jnp.minimum(s, last) + 1) * per_tile, n_halo - 1), 0)),
            pl.BlockSpec((len(POOL_WINDOWS), POOL_GROUP, POOL_OUT_GROUP), const3),
            pl.BlockSpec((1, D_MODEL), const2),
            pl.BlockSpec((D_MODEL, D_MODEL), const2),
            pl.BlockSpec((1, D_MODEL), const2),
            pl.BlockSpec((D_MODEL, 2 * N_EXPERTS), const2),
            pl.BlockSpec((1, N_EXPERTS), const2),
        ],
        out_specs=[
            pl.BlockSpec((1, TOK_TILE, D_MODEL), tok),
            pl.BlockSpec((1, TOK_TILE, PACKED), prev),
            pl.BlockSpec((1, TOK_TILE, TOP_K), prev),
            pl.BlockSpec((1, TOK_TILE, TOP_K), prev),
        ],
        out_shape=[
            jax.ShapeDtypeStruct((BATCH, S_TOT, D_MODEL), F32),
            jax.ShapeDtypeStruct((BATCH, S_TOT, PACKED), jnp.uint32),
            jax.ShapeDtypeStruct((BATCH, S_TOT, TOP_K), jnp.int32),
            jax.ShapeDtypeStruct((BATCH, S_TOT, TOP_K), F32),
        ],
        scratch_shapes=[pltpu.VMEM((TOK_TILE, D_MODEL), F32)],
        compiler_params=pltpu.CompilerParams(
            dimension_semantics=("parallel", "arbitrary"), vmem_limit_bytes=VMEM_LIMIT),
        name="merge",
    )(*xs, mod, g1, w_in, yf, yb, u, dsk, wval, wgate, o, wup, pool_in, pool_in, pool_in, pw, ps, wout, g2, rw, rb)


def _route_kernel(te_ref, below_ref, dest_ref, be_ref, cnt_ref, carry_ref):
    ph = pl.program_id(0)
    i = pl.program_id(1)
    te = te_ref[...]
    lane = lax.broadcasted_iota(jnp.int32, (ROUTE_TILE, LANES), 1)
    ohs = [(te[:, k:k + 1] == lane).astype(F32) for k in range(TOP_K)]
    oh_all = ohs[0] + ohs[1] + ohs[2] + ohs[3]
    tile_cnt = jnp.sum(oh_all, axis=0, keepdims=True)

    @pl.when((ph == 0) & (i == 0))
    def _():
        cnt_ref[...] = jnp.zeros_like(cnt_ref)

    @pl.when(ph == 0)
    def _():
        cnt_ref[...] += tile_cnt

    @pl.when((ph == 1) & (i == 0))
    def _():
        cnt = jnp.broadcast_to(cnt_ref[...], (8, LANES))
        padded = jnp.floor((cnt + (ROW_BLOCK - 1)) / ROW_BLOCK) * ROW_BLOCK
        l8 = lax.broadcasted_iota(jnp.int32, (8, LANES), 1)
        end = padded
        s = 1
        while s < N_EXPERTS:
            end = end + jnp.where(l8 >= s, pltpu.roll(end, s, 1), 0.0)
            s *= 2
        carry_ref[...] = (end - padded)[0:1]
        blk_row = (lax.broadcasted_iota(jnp.int32, (N_BLOCKS_PAD, LANES), 0) * ROW_BLOCK).astype(F32)
        lb = lax.broadcasted_iota(jnp.int32, (N_BLOCKS_PAD, LANES), 1)
        hit = jnp.where((lb < N_EXPERTS) & (jnp.broadcast_to(end[0:1], blk_row.shape) <= blk_row), 1.0, 0.0)
        be = jnp.sum(hit, axis=-1, keepdims=True)
        be_ref[...] = jnp.broadcast_to(be, (N_BLOCKS_PAD, LANES)).astype(jnp.int32)

    @pl.when(ph == 1)
    def _():
        rank = carry_ref[...] + jnp.dot(below_ref[...], oh_all.astype(BF16), preferred_element_type=F32)
        dests = [jnp.sum(ohs[k] * rank, axis=-1, keepdims=True) for k in range(TOP_K)]
        dest_ref[...] = jnp.concatenate(dests, axis=1).astype(jnp.int32)
        carry_ref[...] += tile_cnt


def _route(top_e):
    n_steps = N_TOK // ROUTE_TILE
    return pl.pallas_call(
        _route_kernel,
        grid=(2, n_steps),
        in_specs=[pl.BlockSpec((ROUTE_TILE, TOP_K), lambda ph, i: (i, 0)),
                  pl.BlockSpec((ROUTE_TILE, ROUTE_TILE), lambda ph, i: (0, 0))],
        out_specs=[
            pl.BlockSpec((ROUTE_TILE, TOP_K), lambda ph, i: (ph * i, 0)),
            pl.BlockSpec((N_BLOCKS_PAD, LANES), lambda ph, i: (0, 0)),
        ],
        out_shape=[
            jax.ShapeDtypeStruct((N_TOK, TOP_K), jnp.int32),
            jax.ShapeDtypeStruct((N_BLOCKS_PAD, LANES), jnp.int32),
        ],
        scratch_shapes=[pltpu.VMEM((1, LANES), F32), pltpu.VMEM((1, LANES), F32)],
        compiler_params=pltpu.CompilerParams(dimension_semantics=("arbitrary", "arbitrary")),
        name="route",
    )(top_e, jnp.asarray(np.tril(np.ones((ROUTE_TILE, ROUTE_TILE), np.float32), -1), BF16))


def _sc_mesh():
    return plsc.VectorSubcoreMesh(core_axis_name="c", subcore_axis_name="s")


def _sc_worker():
    return lax.axis_index("s") * SC_CORES + lax.axis_index("c")


def _dispatch(f, dest_t):
    per_w = N_TOK // SC_WORKERS
    n_chunks = per_w // SC_SCATTER_ROWS

    @functools.partial(
        pl.kernel,
        out_type=jax.ShapeDtypeStruct((N_ROWS, PACKED), jnp.uint32),
        mesh=_sc_mesh(),
        scratch_types=[
            pltpu.VMEM((SC_SCATTER_ROWS,), jnp.int32),
            pltpu.VMEM((SC_SCATTER_ROWS, PACKED), jnp.uint32),
            pltpu.SemaphoreType.DMA,
        ],
        name="dispatch",
    )
    def k(f_hbm, i_hbm, o_hbm, idx_v, rows_v, sem):
        base = _sc_worker() * per_w

        @pl.loop(0, n_chunks)
        def _(j):
            t0 = pl.multiple_of(base + j * SC_SCATTER_ROWS, 8)
            pltpu.sync_copy(f_hbm.at[pl.ds(t0, SC_SCATTER_ROWS)], rows_v)
            for kk in range(TOP_K):
                pltpu.sync_copy(i_hbm.at[pl.ds(kk * N_TOK + t0, SC_SCATTER_ROWS)], idx_v)
                pltpu.async_copy(rows_v, o_hbm.at[idx_v], sem).wait()

    return k(f, dest_t.reshape(TOP_K * N_TOK))


def _combine_gather(yb, dest_t):
    n = TOP_K * N_TOK
    per_w = n // SC_WORKERS
    n_chunks = per_w // SC_GATHER_ROWS

    @functools.partial(
        pl.kernel,
        out_type=jax.ShapeDtypeStruct((n, PACKED), jnp.uint32),
        mesh=_sc_mesh(),
        scratch_types=[
            pltpu.VMEM((SC_GATHER_ROWS,), jnp.int32),
            pltpu.VMEM((SC_GATHER_ROWS, PACKED), jnp.uint32),
            pltpu.SemaphoreType.DMA,
        ],
        name="combine_gather",
    )
    def k(y_hbm, i_hbm, o_hbm, idx_v, rows_v, sem):
        base = _sc_worker() * per_w

        @pl.loop(0, n_chunks)
        def _(j):
            r0 = pl.multiple_of(base + j * SC_GATHER_ROWS, 8)
            pltpu.sync_copy(i_hbm.at[pl.ds(r0, SC_GATHER_ROWS)], idx_v)
            pltpu.async_copy(y_hbm.at[idx_v], rows_v, sem).wait()
            pltpu.sync_copy(rows_v, o_hbm.at[pl.ds(r0, SC_GATHER_ROWS)])

    return k(yb, dest_t.reshape(n))


def _weight_copies(w1_hbm, w2_hbm, w1f, w2f, sem, idx):
    return (pltpu.make_async_copy(w1_hbm.at[idx], w1f, sem.at[0]),
            pltpu.make_async_copy(w2_hbm.at[idx], w2f, sem.at[1]))


def _expert_kernel(be_ref, x_ref, w1_hbm, b1_ref, w2_hbm, b2_ref, y_ref, w1f, w2f, w1b, w2b, sem, *, layer):
    j = pl.program_id(0)
    e = be_ref[j]
    prev = be_ref[jnp.maximum(j - 1, 0)]
    live = e < N_EXPERTS
    first = live & ((j == 0) | (e != prev))
    copies = functools.partial(_weight_copies, w1_hbm, w2_hbm, w1f, w2f, sem)

    @pl.when(first & (j == 0))
    def _():
        for c in copies(layer * N_EXPERTS + e):
            c.start()

    @pl.when(first)
    def _():
        for c in copies(layer * N_EXPERTS + e):
            c.wait()
        w1b[...] = w1f[...].astype(BF16)
        w2b[...] = w2f[...].astype(BF16)
        nxt = lax.while_loop(
            lambda k: (k < N_BLOCKS) & (be_ref[jnp.minimum(k, N_BLOCKS - 1)] == e), lambda k: k + 1, j + 1)
        e_next = be_ref[jnp.minimum(nxt, N_BLOCKS - 1)]

        @pl.when((nxt < N_BLOCKS) & (e_next < N_EXPERTS))
        def _():
            for c in copies(layer * N_EXPERTS + e_next):
                c.start()

    @pl.when(live)
    def _():
        x = jnp.concatenate(_unpack_rows(x_ref[...]), axis=1).astype(BF16)
        gu = jnp.dot(x, w1b[...], preferred_element_type=F32) + b1_ref[0]
        gate = jnp.minimum(gu[:, :D_EXPERT], SWIGLU_LIMIT)
        lin = jnp.clip(gu[:, D_EXPERT:], -SWIGLU_LIMIT, SWIGLU_LIMIT)
        act = gate * _sigmoid(SWIGLU_ALPHA * gate) * (lin + 1.0)
        y_ref[...] = _pack_rows(jnp.dot(act.astype(BF16), w2b[...], preferred_element_type=F32) + b2_ref[0])


def _experts(block_e, rows, w1, b1, w2, b2, layer):
    ex = lambda j, be: layer * N_EXPERTS + jnp.minimum(be[j], N_EXPERTS - 1)
    return pl.pallas_call(
        functools.partial(_expert_kernel, layer=layer),
        grid_spec=pltpu.PrefetchScalarGridSpec(
            num_scalar_prefetch=1,
            grid=(N_BLOCKS,),
            in_specs=[
                pl.BlockSpec((ROW_BLOCK, PACKED), lambda j, be: (j, 0)),
                pl.BlockSpec(memory_space=pl.ANY),
                pl.BlockSpec((1, 1, 2 * D_EXPERT), lambda j, be: (ex(j, be), 0, 0)),
                pl.BlockSpec(memory_space=pl.ANY),
                pl.BlockSpec((1, 1, D_MODEL), lambda j, be: (ex(j, be), 0, 0)),
            ],
            out_specs=pl.BlockSpec((ROW_BLOCK, PACKED), lambda j, be: (j, 0)),
            scratch_shapes=[
                pltpu.VMEM((D_MODEL, 2 * D_EXPERT), F32),
                pltpu.VMEM((D_EXPERT, D_MODEL), F32),
                pltpu.VMEM((D_MODEL, 2 * D_EXPERT), BF16),
                pltpu.VMEM((D_EXPERT, D_MODEL), BF16),
                pltpu.SemaphoreType.DMA((2,)),
            ],
        ),
        out_shape=jax.ShapeDtypeStruct((N_ROWS, PACKED), jnp.uint32),
        compiler_params=pltpu.CompilerParams(
            dimension_semantics=("arbitrary",), vmem_limit_bytes=VMEM_LIMIT),
        name="experts",
    )(block_e, rows, w1, b1, w2, b2)


def _combined(x_ref, mod_ref, g_ref, p_ref, tile_off):
    b = pl.program_id(0)
    i = pl.program_id(1) + tile_off
    row = _mod_row(i, b)
    p = p_ref[0]
    lo, hi = _unpack_rows(g_ref[0, 0])
    y_lo = p[:, 0:1] * lo
    y_hi = p[:, 0:1] * hi
    for k in range(1, TOP_K):
        lo, hi = _unpack_rows(g_ref[k, 0])
        y_lo = y_lo + p[:, k:k + 1] * lo
        y_hi = y_hi + p[:, k:k + 1] * hi
    return x_ref[0] + mod_ref[5, pl.ds(row, 1), :] * jnp.concatenate([y_lo, y_hi], axis=1)


def _combine_kernel(x_ref, mod_ref, g_ref, p_ref, o_ref, *, tile_off):
    o_ref[0] = _combined(x_ref, mod_ref, g_ref, p_ref, tile_off)


def _combine_final_kernel(x_ref, mod_ref, g_ref, p_ref, fg_ref, o_ref, *, tile_off):
    o_ref[0] = _rms(_combined(x_ref, mod_ref, g_ref, p_ref, tile_off), fg_ref[...])


def _combine(xa, mod, gathered, probs, final_g=None):
    last = final_g is not None
    tile_off = CTX_LEN // TOK_TILE if last else 0
    n_t = N_TILES - tile_off
    tok = lambda b, i: (b, i + tile_off, 0)
    in_specs = [
        pl.BlockSpec((1, TOK_TILE, D_MODEL), tok),
        pl.BlockSpec((6, 8, D_MODEL), lambda b, i: (0, 0, 0)),
        pl.BlockSpec((TOP_K, 1, TOK_TILE, PACKED), lambda b, i: (0, b, i + tile_off, 0)),
        pl.BlockSpec((1, TOK_TILE, TOP_K), tok),
    ]
    args = [xa, mod, gathered.reshape(TOP_K, BATCH, S_TOT, PACKED), probs]
    if last:
        in_specs.append(pl.BlockSpec((1, D_MODEL), lambda b, i: (0, 0)))
        args.append(final_g)
        body = functools.partial(_combine_final_kernel, tile_off=tile_off)
    else:
        body = functools.partial(_combine_kernel, tile_off=tile_off)
    return pl.pallas_call(
        body,
        grid=(BATCH, n_t),
        in_specs=in_specs,
        out_specs=pl.BlockSpec((1, TOK_TILE, D_MODEL), lambda b, i: (b, i, 0)),
        out_shape=jax.ShapeDtypeStruct((BATCH, n_t * TOK_TILE, D_MODEL), F32),
        compiler_params=pltpu.CompilerParams(dimension_semantics=("parallel", "parallel")),
        name="combine",
    )(*args)


def _split_bf16(w):
    bits = lax.bitcast_convert_type(w.astype(F32), jnp.uint32) & jnp.uint32(0xFFFF0000)
    hi = lax.bitcast_convert_type(bits, F32)
    return jnp.concatenate([hi.astype(BF16), (w - hi).astype(BF16)], axis=-1)


def _rope_tables():
    t = np.arange(SEQ)
    axis_dim = HEAD_DIM // 2
    inv_freq = (1.0 / (ROPE_BASE ** (np.arange(0, axis_dim, 2, dtype=np.float32) / axis_dim))).astype(np.float32)
    ang_r = (t // GRID_W).astype(np.float32)[:, None] * inv_freq[None]
    ang_c = (t % GRID_W).astype(np.float32)[:, None] * inv_freq[None]
    ang = np.concatenate([ang_r, ang_r, ang_c, ang_c] * (LANES // HEAD_DIM), axis=-1)
    cos = np.concatenate([np.ones((CTX_LEN, LANES), np.float32), np.cos(ang)], axis=0)
    sin = np.concatenate([np.zeros((CTX_LEN, LANES), np.float32), np.sin(ang)], axis=0)
    return jnp.asarray(cos, F32), jnp.asarray(sin, F32)


def kernel(x, c, ctx, c_ctx, w_ada, b_ada, norm1_g, norm2_g, w_in, ssm_lam_re, ssm_lam_im, ssm_log_dt, ssm_b_re, ssm_b_im, ssm_c_re, ssm_c_im, ssm_d, ssm_w_val, ssm_w_gate, attn_sink, attn_w_up, pool_w, pool_scale, w_out, router_w, router_b, exp_w1, exp_b1, exp_w2, exp_b2, final_g):
    assert x.shape == (BATCH, SEQ, D_MODEL) and ctx.shape == (BATCH, CTX_LEN, D_MODEL)
    cos_t, sin_t = _rope_tables()
    cv = jnp.concatenate([c, c_ctx[None], jnp.zeros((8 - BATCH - 1, D_MODEL), F32)], axis=0)
    mods = _ada(cv, w_ada, b_ada)
    xs = (ctx, x)
    ew1 = exp_w1.reshape(DEPTH * N_EXPERTS, D_MODEL, 2 * D_EXPERT)
    eb1 = exp_b1.reshape(DEPTH * N_EXPERTS, 1, 2 * D_EXPERT)
    ew2 = exp_w2.reshape(DEPTH * N_EXPERTS, D_EXPERT, D_MODEL)
    eb2 = exp_b2.reshape(DEPTH * N_EXPERTS, 1, D_MODEL)
    half_gate = jnp.where(jnp.arange(IN_WIDTH) >= OFF_GATE, 0.5, 1.0).astype(F32)
    w_in_b = (w_in * half_gate).astype(BF16)
    out = None
    for l in range(DEPTH):
        mod = mods[l]
        u, kv, q, pool_in = _inproj(xs, mod, norm1_g[l][None], w_in_b, l, cos_t, sin_t)
        wb, wc, ar, ai = _ssm_params(ssm_lam_re[l], ssm_lam_im[l], ssm_log_dt[l], ssm_b_re[l], ssm_b_im[l],
                                     ssm_c_re[l], ssm_c_im[l])
        yf, yb = _ssm(u, wb, wc, ar, ai)
        o = _attn(attn_sink[l], q, kv)
        xa, f, top_e, top_p = _merge(
            xs, mod, norm1_g[l][None], w_in_b, l, yf, yb, u, ssm_d[l][None],
            (0.5 * ssm_w_val[l]).astype(BF16), (0.5 * ssm_w_gate[l]).astype(BF16), o, attn_w_up[l].astype(BF16),
            pool_in, pool_w[l].astype(BF16), pool_scale[l][None], (0.5 * w_out[l]).astype(BF16), norm2_g[l][None],
            _split_bf16(router_w[l]), router_b[l][None])
        dest, block_e = _route(top_e.reshape(N_TOK, TOP_K))
        dest_t = dest.T
        rows = _dispatch(f.reshape(N_TOK, PACKED), dest_t)
        yrows = _experts(block_e[:N_BLOCKS, 0], rows, ew1, eb1, ew2, eb2, l)
        gathered = _combine_gather(yrows, dest_t)
        if l < DEPTH - 1:
            xa = _combine(xa, mod, gathered, top_p)
            xs = (xa, xa)
        else:
            out = _combine(xa, mod, gathered, top_p, final_g[None])
    return out
```

```python
import functools
import math

import jax
import jax.numpy as jnp
import numpy as np
from jax import lax
from jax.experimental import pallas as pl
from jax.experimental.pallas import tpu as pltpu
from jax.experimental.pallas import tpu_sc as plsc

F32 = jnp.float32
BF16 = jnp.bfloat16
HIGHEST = lax.Precision.HIGHEST

D_MODEL = 1024
BATCH = 2
SEQ = 8192
DEPTH = 2
GRID_W = 64
CTX_LEN = 256
NORM_EPS = 1e-6
SSM_WIDTH = 512
SSM_GROUP = 16
SSM_GROUPS = 32
SSM_STATE = 64
HEAD_DIM = 64
N_Q_HEADS = 8
N_KV_HEADS = 2
GQA_GROUP = 4
ATTN_WIDTH = 512
KV_WIDTH = 128
WINDOW = 128
ROPE_BASE = 10000.0
ATTN_SCALE = HEAD_DIM ** -0.5
POOL_WINDOWS = (2, 4, 8, 16)
POOL_WIDTH = 512
POOL_GROUP = 128
POOL_OUT_GROUP = 256
OFF_K = 512
OFF_V = 640
OFF_Q = 768
OFF_POOL = 1280
OFF_GATE = 1792
IN_WIDTH = OFF_GATE + 3 * D_MODEL
N_EXPERTS = 32
TOP_K = 4
D_EXPERT = 1024
SWIGLU_LIMIT = 7.0
SWIGLU_ALPHA = 1.702

S_TOT = CTX_LEN + SEQ
N_TOK = BATCH * S_TOT
TOK_TILE = 256
N_TILES = S_TOT // TOK_TILE
assert CTX_LEN == TOK_TILE
ATTN_BLOCK = 128
POOL_HALO = 8
ROUTE_TILE = 512
ROW_BLOCK = 256
N_ROWS = -(-(N_TOK * TOP_K + N_EXPERTS * (ROW_BLOCK - 1)) // ROW_BLOCK) * ROW_BLOCK
N_BLOCKS = N_ROWS // ROW_BLOCK
N_BLOCKS_PAD = -(-N_BLOCKS // 8) * 8
SC_CORES = 2
SC_WORKERS = SC_CORES * 16
PACKED = D_MODEL // 2
SC_SCATTER_ROWS = 88
SC_GATHER_ROWS = 88
assert N_TOK % (SC_WORKERS * SC_SCATTER_ROWS) == 0 and (N_TOK * TOP_K) % (SC_WORKERS * SC_GATHER_ROWS) == 0
VMEM_LIMIT = 56 * 1024 * 1024
LANES = 128
MASK_VALUE = -1e30


def _rms(x, g):
    return x * lax.rsqrt(jnp.mean(x * x, axis=-1, keepdims=True) + NORM_EPS) * g


def _sigmoid(x):
    return 0.5 * jnp.tanh(0.5 * x) + 0.5


def _pack_rows(x):
    half = x.shape[1] // 2
    lo = lax.bitcast_convert_type(x[:, :half].astype(BF16).astype(F32), jnp.uint32)
    hi = lax.bitcast_convert_type(x[:, half:].astype(BF16).astype(F32), jnp.uint32)
    return (lo >> 16) | hi


def _unpack_rows(p):
    lo = lax.bitcast_convert_type(p << 16, F32)
    hi = lax.bitcast_convert_type(p & jnp.uint32(0xFFFF0000), F32)
    return lo, hi


def _token_specs(xs, last=N_TILES - 1):
    ctx_tiles = CTX_LEN // TOK_TILE
    off = 0 if xs[1].shape[1] == S_TOT else ctx_tiles
    return [pl.BlockSpec((1, TOK_TILE, D_MODEL), lambda b, i: (b, 0, 0)),
            pl.BlockSpec((1, TOK_TILE, D_MODEL), lambda b, i: (b, jnp.clip(i, ctx_tiles, last) - off, 0))]


def _token_tile(i, xc_ref, xl_ref):
    return jnp.where(i == 0, xc_ref[0], xl_ref[0])


def _mod_row(i, b):
    return jnp.where(i == 0, 2, b)


def _ada_kernel(cv_ref, w_ref, b_ref, o_ref):
    cv = cv_ref[...]
    s = cv * jax.nn.sigmoid(cv)
    o_ref[0, 0] = jnp.dot(s, w_ref[0], preferred_element_type=F32, precision=HIGHEST) + b_ref[0, 0]


def _ada(cv, w_ada, b_ada):
    return pl.pallas_call(
        _ada_kernel,
        grid=(DEPTH, 6),
        in_specs=[
            pl.BlockSpec((8, D_MODEL), lambda l, j: (0, 0)),
            pl.BlockSpec((1, D_MODEL, D_MODEL), lambda l, j: (l, 0, j)),
            pl.BlockSpec((1, 1, 1, D_MODEL), lambda l, j: (l, j, 0, 0)),
        ],
        out_specs=pl.BlockSpec((1, 1, 8, D_MODEL), lambda l, j: (l, j, 0, 0)),
        out_shape=jax.ShapeDtypeStruct((DEPTH, 6, 8, D_MODEL), F32),
        name="ada",
    )(cv, w_ada, b_ada.reshape(DEPTH, 6, 1, D_MODEL))


def _routed_sum(g_ref, p_ref):
    p = p_ref[0]
    lo, hi = _unpack_rows(g_ref[0, 0])
    y_lo = p[:, 0:1] * lo
    y_hi = p[:, 0:1] * hi
    for k in range(1, TOP_K):
        lo, hi = _unpack_rows(g_ref[k, 0])
        y_lo = y_lo + p[:, k:k + 1] * lo
        y_hi = y_hi + p[:, k:k + 1] * hi
    return jnp.concatenate([y_lo, y_hi], axis=1)


def _rope(t, cos, sin):
    lane = lax.broadcasted_iota(jnp.int32, t.shape, 1)
    rot = jnp.where((lane % 32) < 16, -pltpu.roll(t, LANES - 16, 1), pltpu.roll(t, 16, 1))
    return t * cos + rot * sin


def _inproj_kernel(*refs, pending):
    if pending:
        (xc_ref, xl_ref, mod_ref, g_ref, w_ref, cos_ref, sin_ref, pmod_ref, rows_ref, probs_ref,
         u_ref, kv_ref, q_ref, pool_ref, xo_ref) = refs
    else:
        xc_ref, xl_ref, mod_ref, g_ref, w_ref, cos_ref, sin_ref, u_ref, kv_ref, q_ref, pool_ref = refs
    b = pl.program_id(0)
    i = pl.program_id(1)
    row = _mod_row(i, b)
    shift = mod_ref[0, pl.ds(row, 1), :]
    scale = mod_ref[1, pl.ds(row, 1), :]
    x = _token_tile(i, xc_ref, xl_ref)
    if pending:
        x = x + pmod_ref[5, pl.ds(row, 1), :] * _routed_sum(rows_ref, probs_ref)
        xo_ref[0] = x
    h = _rms(x, g_ref[...]) * (1.0 + scale) + shift
    p = jnp.dot(h.astype(BF16), w_ref[0, :, :OFF_GATE], preferred_element_type=F32)
    cos = cos_ref[...]
    sin = sin_ref[...]
    u_ref[0] = p[:, :OFF_K]
    kv_ref[0, :, :KV_WIDTH] = _rope(p[:, OFF_K:OFF_V], cos, sin).astype(BF16)
    kv_ref[0, :, KV_WIDTH:] = p[:, OFF_V:OFF_Q].astype(BF16)
    for c in range(ATTN_WIDTH // LANES):
        qc = p[:, OFF_Q + c * LANES:OFF_Q + (c + 1) * LANES]
        q_ref[0, :, c * LANES:(c + 1) * LANES] = (_rope(qc, cos, sin) * ATTN_SCALE).astype(BF16)
    pool_ref[0] = p[:, OFF_POOL:OFF_GATE]


def _inproj(xs, mod, g1, w_in, layer, cos_t, sin_t, pending=None):
    tok = lambda b, i: (b, i, 0)
    in_specs = [
        *_token_specs(xs),
        pl.BlockSpec((6, 8, D_MODEL), lambda b, i: (0, 0, 0)),
        pl.BlockSpec((1, D_MODEL), lambda b, i: (0, 0)),
        pl.BlockSpec((1, D_MODEL, IN_WIDTH), lambda b, i: (layer, 0, 0)),
        pl.BlockSpec((TOK_TILE, LANES), lambda b, i: (i, 0)),
        pl.BlockSpec((TOK_TILE, LANES), lambda b, i: (i, 0)),
    ]
    out_specs = [
        pl.BlockSpec((1, TOK_TILE, SSM_WIDTH), tok),
        pl.BlockSpec((1, TOK_TILE, 2 * KV_WIDTH), tok),
        pl.BlockSpec((1, TOK_TILE, ATTN_WIDTH), tok),
        pl.BlockSpec((1, TOK_TILE, POOL_WIDTH), tok),
    ]
    out_shape = [
        jax.ShapeDtypeStruct((BATCH, S_TOT, SSM_WIDTH), F32),
        jax.ShapeDtypeStruct((BATCH, S_TOT, 2 * KV_WIDTH), BF16),
        jax.ShapeDtypeStruct((BATCH, S_TOT, ATTN_WIDTH), BF16),
        jax.ShapeDtypeStruct((BATCH, S_TOT, POOL_WIDTH), F32),
    ]
    args = [*xs, mod, g1, w_in, cos_t, sin_t]
    if pending is not None:
        pmod, gathered, probs = pending
        in_specs += [
            pl.BlockSpec((6, 8, D_MODEL), lambda b, i: (0, 0, 0)),
            pl.BlockSpec((TOP_K, 1, TOK_TILE, PACKED), lambda b, i: (0, b, i, 0)),
            pl.BlockSpec((1, TOK_TILE, TOP_K), tok),
        ]
        out_specs.append(pl.BlockSpec((1, TOK_TILE, D_MODEL), tok))
        out_shape.append(jax.ShapeDtypeStruct((BATCH, S_TOT, D_MODEL), F32))
        args += [pmod, gathered.reshape(TOP_K, BATCH, S_TOT, PACKED), probs]
    return pl.pallas_call(
        functools.partial(_inproj_kernel, pending=pending is not None),
        grid=(BATCH, N_TILES),
        in_specs=in_specs,
        out_specs=out_specs,
        out_shape=out_shape,
        compiler_params=pltpu.CompilerParams(
            dimension_semantics=("parallel", "parallel"), vmem_limit_bytes=VMEM_LIMIT),
        name="inproj",
    )(*args)


SSM_HALF_IN = SSM_WIDTH // 2
SSM_HALF_ST = SSM_GROUPS // 2 * SSM_STATE
N_STATE_ROWS = 8
N_LANE_TILES = SSM_HALF_ST // LANES
SCAN_STRIDE = TOK_TILE + 8


def _ssm_kernel(uf_ref, ub_ref, wb_ref, wc_ref, ar_ref, ai_ref, yf_ref, yb_ref, buf_f, buf_b, hf_ref, hb_ref):
    i = pl.program_id(0)
    T = TOK_TILE

    @pl.when(i == 0)
    def _():
        hf_ref[...] = jnp.zeros_like(hf_ref)
        hb_ref[...] = jnp.zeros_like(hb_ref)

    for d, (u_ref, buf) in enumerate(((uf_ref, buf_f), (ub_ref, buf_b))):
        for b in range(BATCH):
            for half in range(2):
                ub16 = u_ref[b, :, half * SSM_HALF_IN:(half + 1) * SSM_HALF_IN].astype(BF16)
                for reim in range(2):
                    k = reim * 4 + half * 2 + b
                    bu = jnp.dot(ub16, wb_ref[d, reim, half], preferred_element_type=F32)
                    for l in range(N_LANE_TILES):
                        buf[l, k * SCAN_STRIDE:k * SCAN_STRIDE + T, :] = bu[:, l * LANES:(l + 1) * LANES]

    def step(t, carry):
        hf, hb = carry
        tb = T - 1 - t
        nf, nb = [], []
        for l in range(N_LANE_TILES):
            sl = pl.ds(l * LANES, LANES)
            rows_f = pl.ds(t, N_STATE_ROWS, stride=SCAN_STRIDE)
            h = ar_ref[0, :, sl] * hf[l] + ai_ref[0, :, sl] * pltpu.roll(hf[l], 4, 0) + buf_f[l, rows_f, :]
            buf_f[l, rows_f, :] = h
            nf.append(h)
            rows_b = pl.ds(tb, N_STATE_ROWS, stride=SCAN_STRIDE)
            h = ar_ref[1, :, sl] * hb[l] + ai_ref[1, :, sl] * pltpu.roll(hb[l], 4, 0) + buf_b[l, rows_b, :]
            buf_b[l, rows_b, :] = h
            nb.append(h)
        return tuple(nf), tuple(nb)

    hf0 = tuple(hf_ref[:, l * LANES:(l + 1) * LANES] for l in range(N_LANE_TILES))
    hb0 = tuple(hb_ref[:, l * LANES:(l + 1) * LANES] for l in range(N_LANE_TILES))
    hf1, hb1 = lax.fori_loop(0, T, step, (hf0, hb0), unroll=2)
    for l in range(N_LANE_TILES):
        hf_ref[:, l * LANES:(l + 1) * LANES] = hf1[l]
        hb_ref[:, l * LANES:(l + 1) * LANES] = hb1[l]

    for d, (buf, y_ref) in enumerate(((buf_f, yf_ref), (buf_b, yb_ref))):
        for b in range(BATCH):
            for half in range(2):
                kre = half * 2 + b
                kim = 4 + half * 2 + b
                state = lambda k: jnp.concatenate(
                    [buf[l, k * SCAN_STRIDE:k * SCAN_STRIDE + T, :] for l in range(N_LANE_TILES)],
                    axis=1).astype(BF16)
                y = jnp.dot(state(kre), wc_ref[d, 0, half], preferred_element_type=F32)
                y = y + jnp.dot(state(kim), wc_ref[d, 1, half], preferred_element_type=F32)
                y_ref[b, :, half * SSM_HALF_IN:(half + 1) * SSM_HALF_IN] = y


def _ssm(u, wb, wc, ar, ai):
    fwd = lambda i: (0, i, 0)
    bwd = lambda i: (0, jnp.where(i == 0, 0, N_TILES - i), 0)
    return pl.pallas_call(
        _ssm_kernel,
        grid=(N_TILES,),
        in_specs=[
            pl.BlockSpec((BATCH, TOK_TILE, SSM_WIDTH), fwd),
            pl.BlockSpec((BATCH, TOK_TILE, SSM_WIDTH), bwd),
            pl.BlockSpec((2, 2, 2, SSM_HALF_IN, SSM_HALF_ST), lambda i: (0, 0, 0, 0, 0)),
            pl.BlockSpec((2, 2, 2, SSM_HALF_ST, SSM_HALF_IN), lambda i: (0, 0, 0, 0, 0)),
            pl.BlockSpec((2, N_STATE_ROWS, SSM_HALF_ST), lambda i: (0, 0, 0)),
            pl.BlockSpec((2, N_STATE_ROWS, SSM_HALF_ST), lambda i: (0, 0, 0)),
        ],
        out_specs=[
            pl.BlockSpec((BATCH, TOK_TILE, SSM_WIDTH), fwd),
            pl.BlockSpec((BATCH, TOK_TILE, SSM_WIDTH), bwd),
        ],
        out_shape=[jax.ShapeDtypeStruct((BATCH, S_TOT, SSM_WIDTH), F32)] * 2,
        scratch_shapes=[
            pltpu.VMEM((N_LANE_TILES, N_STATE_ROWS * SCAN_STRIDE, LANES), F32),
            pltpu.VMEM((N_LANE_TILES, N_STATE_ROWS * SCAN_STRIDE, LANES), F32),
            pltpu.VMEM((N_STATE_ROWS, SSM_HALF_ST), F32),
            pltpu.VMEM((N_STATE_ROWS, SSM_HALF_ST), F32),
        ],
        compiler_params=pltpu.CompilerParams(
            dimension_semantics=("arbitrary",), vmem_limit_bytes=VMEM_LIMIT),
        name="ssm",
    )(u, u, wb, wc, ar, ai)


def _ssm_params(lam_re, lam_im, log_dt, b_re, b_im, c_re, c_im):
    lr = lam_re.astype(F32)
    li = lam_im.astype(F32)
    dt = jnp.exp(log_dt.astype(F32))[..., None]
    mag = jnp.exp(lr * dt)
    ang = li * dt
    a_re = mag * jnp.cos(ang)
    a_im = mag * jnp.sin(ang)
    den = lr * lr + li * li
    k_re = ((a_re - 1.0) * lr + a_im * li) / den
    k_im = (a_im * lr - (a_re - 1.0) * li) / den
    bb_re = k_re[..., None] * b_re - k_im[..., None] * b_im
    bb_im = k_re[..., None] * b_im + k_im[..., None] * b_re
    n_g = SSM_GROUPS // 2
    drive_mask = np.equal.outer(np.arange(SSM_HALF_IN) // SSM_GROUP, np.arange(SSM_HALF_ST) // SSM_STATE)
    read_mask = drive_mask.T

    def drive(bb):
        rows = jnp.swapaxes(bb, -1, -2).reshape(2, 2, SSM_HALF_IN, SSM_STATE)
        return jnp.tile(rows, (1, 1, 1, n_g)) * drive_mask.astype(np.float32)

    def readout(c):
        rows = jnp.swapaxes(c, -1, -2).reshape(2, 2, SSM_HALF_ST, SSM_GROUP)
        return jnp.tile(rows, (1, 1, 1, n_g)) * read_mask.astype(np.float32)

    wb = jnp.stack([drive(bb_re), drive(bb_im)], axis=1).astype(BF16)
    wc = jnp.stack([readout(c_re.astype(F32)), readout(-c_im.astype(F32))], axis=1).astype(BF16)

    def rows(a, sign_re, sign_im):
        halves = a.reshape(2, 2, SSM_HALF_ST)
        per = []
        for reim in range(2):
            for half in range(2):
                for _ in range(BATCH):
                    per.append(halves[:, half] * (sign_re if reim == 0 else sign_im))
        return jnp.stack(per, axis=1)

    ar = rows(a_re, 1.0, 1.0)
    ai = rows(a_im, -1.0, 1.0)
    return wb, wc, ar, ai


def _attn_kernel(sink_ref, band_ref, q_ref, kc_ref, kp_ref, kn0_ref, kn_ref, o_ref):
    j = pl.program_id(1)
    n = j - CTX_LEN // ATTN_BLOCK
    q = q_ref[0]
    kv_all = jnp.concatenate([kp_ref[0], kn0_ref[0], kn_ref[0], kc_ref[0]], axis=0)
    nb = 3 * ATTN_BLOCK
    last = SEQ // ATTN_BLOCK - 1
    krow = lax.broadcasted_iota(jnp.int32, (nb, 1), 0)
    lo_r = jnp.where(n < 0, nb, jnp.where(n == 0, ATTN_BLOCK, 0))
    hi_r = jnp.where(n == last, 2 * ATTN_BLOCK, nb)
    bias = band_ref[...] + jnp.where((krow >= lo_r) & (krow < hi_r), 0.0, MASK_VALUE)
    bias = jnp.concatenate([bias] * GQA_GROUP, axis=1)
    nt = (((1,), (1,)), ((), ()))
    outs = []
    for hk in range(N_KV_HEADS):
        k = kv_all[:, hk * HEAD_DIM:(hk + 1) * HEAD_DIM]
        v_t = kv_all[:, KV_WIDTH + hk * HEAD_DIM:KV_WIDTH + (hk + 1) * HEAD_DIM].astype(F32).T.astype(BF16)
        heads = range(hk * GQA_GROUP, (hk + 1) * GQA_GROUP)
        qs = jnp.concatenate([q[:, hq * HEAD_DIM:(hq + 1) * HEAD_DIM] for hq in heads], axis=0)
        sink = jnp.concatenate([jnp.full((1, ATTN_BLOCK), sink_ref[hq], F32) for hq in heads], axis=1)
        s_t = lax.dot_general(k, qs, nt, preferred_element_type=F32)
        s_w = s_t[:nb] + bias
        s_c = s_t[nb:]
        m = jnp.maximum(jnp.maximum(jnp.max(s_w, axis=0, keepdims=True),
                                    jnp.max(s_c, axis=0, keepdims=True)), sink)
        e_w = jnp.exp(s_w - m)
        e_c = jnp.exp(s_c - m)
        den = jnp.sum(e_w, axis=0, keepdims=True) + jnp.sum(e_c, axis=0, keepdims=True) + jnp.exp(sink - m)
        e = jnp.concatenate([e_w, e_c], axis=0).astype(BF16)
        o_t = jnp.dot(v_t, e, preferred_element_type=F32) / den
        outs.extend(o_t[:, g * ATTN_BLOCK:(g + 1) * ATTN_BLOCK].T for g in range(GQA_GROUP))
    o_ref[0] = jnp.concatenate(outs, axis=1).astype(BF16)


def _attn(sink, q, kv):
    nq = S_TOT // ATTN_BLOCK
    first = CTX_LEN // ATTN_BLOCK
    clamp = lambda j: jnp.clip(j, first, nq - 1)
    r = jnp.arange(ATTN_BLOCK)[None, :]
    c = jnp.arange(3 * ATTN_BLOCK)[:, None] - ATTN_BLOCK
    band_bias = jnp.where(jnp.abs(c - r) <= WINDOW, 0.0, MASK_VALUE).astype(F32)
    return pl.pallas_call(
        _attn_kernel,
        grid=(BATCH, nq),
        in_specs=[
            pl.BlockSpec(memory_space=pltpu.SMEM),
            pl.BlockSpec((3 * ATTN_BLOCK, ATTN_BLOCK), lambda b, j: (0, 0)),
            pl.BlockSpec((1, ATTN_BLOCK, ATTN_WIDTH), lambda b, j: (b, j, 0)),
            pl.BlockSpec((1, CTX_LEN, 2 * KV_WIDTH), lambda b, j: (b, 0, 0)),
            pl.BlockSpec((1, ATTN_BLOCK, 2 * KV_WIDTH), lambda b, j: (b, clamp(j - 1), 0)),
            pl.BlockSpec((1, ATTN_BLOCK, 2 * KV_WIDTH), lambda b, j: (b, clamp(j), 0)),
            pl.BlockSpec((1, ATTN_BLOCK, 2 * KV_WIDTH), lambda b, j: (b, clamp(j + 1), 0)),
        ],
        out_specs=pl.BlockSpec((1, ATTN_BLOCK, ATTN_WIDTH), lambda b, j: (b, j, 0)),
        out_shape=jax.ShapeDtypeStruct((BATCH, S_TOT, ATTN_WIDTH), BF16),
        compiler_params=pltpu.CompilerParams(dimension_semantics=("parallel", "parallel")),
        name="attn",
    )(sink, band_bias, q, kv, kv, kv, kv)


def _gelu_tanh(x):
    return 0.5 * x * (1.0 + jnp.tanh(math.sqrt(2.0 / math.pi) * (x + 0.044715 * (x * x * x))))


def _pool_means(ext, i):
    T = TOK_TILE
    n_ext = T + 2 * POOL_HALO
    seq_len = jnp.where(i == 0, CTX_LEN, SEQ)
    start = jnp.where(i == 0, 0, (i - 1) * T)
    r = start - POOL_HALO + lax.broadcasted_iota(jnp.int32, (n_ext, 1), 0)
    ext = jnp.where((r >= 0) & (r < seq_len), ext, 0.0)
    tl = start + lax.broadcasted_iota(jnp.int32, (T, 1), 0)
    back = lambda a, s: pltpu.roll(a, s, 0)
    fwd = lambda a, s: pltpu.roll(a, n_ext - s, 0)
    diffs = []
    for gi, w in enumerate(POOL_WINDOWS):
        e = ext[:, gi * POOL_GROUP:(gi + 1) * POOL_GROUP]
        lo = w // 2
        hi = w - 1 - lo
        s = e + back(e, 1)
        span = 2
        while span < w:
            s = s + back(s, span)
            span *= 2
        if w > 2:
            s = fwd(s, hi)
        cnt = (jnp.clip(tl + hi + 1, 0, seq_len) - jnp.clip(tl - lo, 0, seq_len)).astype(F32)
        mean = s[POOL_HALO:POOL_HALO + T] / cnt
        diffs.append((mean - e[POOL_HALO:POOL_HALO + T]).astype(BF16))
    return diffs


def _merge_kernel(xc_ref, xl_ref, mod_ref, g1_ref, w_ref, yf_ref, yb_ref, u_ref, dsk_ref, wval_ref, wgate_ref,
                  o_ref, wup_ref, pc_ref, pp_ref, pn_ref, pw_ref, ps_ref, wout_ref, g2_ref, rw_ref, rb_ref,
                  xo_ref, f_ref, te_ref, tp_ref, xprev_ref):
    b = pl.program_id(0)
    s = pl.program_id(1)

    @pl.when(s == 0)
    def _():
        xprev_ref[...] = jnp.zeros_like(xprev_ref)

    rowp = _mod_row(jnp.maximum(s - 1, 0), b)
    f = _rms(xprev_ref[...], g2_ref[...]) * (1.0 + mod_ref[4, pl.ds(rowp, 1), :]) + mod_ref[3, pl.ds(rowp, 1), :]
    f_ref[0] = _pack_rows(f)
    f_hi = f.astype(BF16)
    f_lo = (f - f_hi.astype(F32)).astype(BF16)
    r_hi = jnp.dot(f_hi, rw_ref[...], preferred_element_type=F32)
    r_lo = jnp.dot(f_lo, rw_ref[...], preferred_element_type=F32)
    logits = r_hi[:, :N_EXPERTS] + r_hi[:, N_EXPERTS:] + r_lo[:, :N_EXPERTS] + rb_ref[...]
    lane = lax.broadcasted_iota(jnp.int32, logits.shape, 1)
    vals, idxs = [], []
    for _ in range(TOP_K):
        mx = jnp.max(logits, axis=-1, keepdims=True)
        ix = jnp.min(jnp.where(logits == mx, lane, N_EXPERTS), axis=-1, keepdims=True)
        vals.append(mx)
        idxs.append(ix)
        logits = jnp.where(lane == ix, -jnp.inf, logits)
    ex = [jnp.exp(v - vals[0]) for v in vals]
    tot = ex[0] + ex[1] + ex[2] + ex[3]
    te_ref[0] = jnp.concatenate(idxs, axis=1)
    tp_ref[0] = jnp.concatenate(ex, axis=1) / tot

    i = jnp.minimum(s, N_TILES - 1)
    row = _mod_row(i, b)
    shift, scale, gate = (mod_ref[j, pl.ds(row, 1), :] for j in range(3))
    ext = jnp.concatenate([pp_ref[0], pc_ref[0], pn_ref[0]], axis=0)
    diffs = _pool_means(ext, i)
    x = _token_tile(i, xc_ref, xl_ref)
    h = (_rms(x, g1_ref[...]) * (1.0 + scale) + shift).astype(BF16)
    t = jnp.tanh(jnp.dot(h, w_ref[0, :, OFF_GATE:], preferred_element_type=F32))

    z = yf_ref[0] + yb_ref[0] + dsk_ref[...] * u_ref[0]
    z = _gelu_tanh(z).astype(BF16)
    val = jnp.dot(z, wval_ref[...], preferred_element_type=F32)
    y_ssm = val * jnp.tanh(jnp.dot(z, wgate_ref[...], preferred_element_type=F32)) + val
    y_attn = jnp.dot(o_ref[0], wup_ref[...], preferred_element_type=F32)
    y_pool = jnp.concatenate(
        [jnp.dot(diffs[gi], pw_ref[gi], preferred_element_type=F32) for gi in range(len(POOL_WINDOWS))],
        axis=1) * ps_ref[...]

    m = (t[:, :D_MODEL] * y_ssm + t[:, D_MODEL:2 * D_MODEL] * y_attn + t[:, 2 * D_MODEL:] * y_pool
         + (y_ssm + y_attn + y_pool))
    xo = x + gate * jnp.dot(m.astype(BF16), wout_ref[...], preferred_element_type=F32)
    xo_ref[0] = xo
    xprev_ref[...] = xo


def _merge(xs, mod, g1, w_in, layer, yf, yb, u, dsk, wval, wgate, o, wup, pool_in, pw, ps, wout, g2, rw, rb):
    last = N_TILES - 1
    tok = lambda b, s: (b, jnp.minimum(s, last), 0)
    prev = lambda b, s: (b, jnp.maximum(s - 1, 0), 0)
    const2 = lambda b, s: (0, 0)
    const3 = lambda b, s: (0, 0, 0)
    per_tile = TOK_TILE // POOL_HALO
    n_halo = S_TOT // POOL_HALO
    return pl.pallas_call(
        _merge_kernel,
        grid=(BATCH, N_TILES + 1),
        in_specs=[
            *_token_specs(xs, last),
            pl.BlockSpec((6, 8, D_MODEL), const3),
            pl.BlockSpec((1, D_MODEL), const2),
            pl.BlockSpec((1, D_MODEL, IN_WIDTH), lambda b, s: (layer, 0, 0)),
            pl.BlockSpec((1, TOK_TILE, SSM_WIDTH), tok),
            pl.BlockSpec((1, TOK_TILE, SSM_WIDTH), tok),
            pl.BlockSpec((1, TOK_TILE, SSM_WIDTH), tok),
            pl.BlockSpec((1, SSM_WIDTH), const2),
            pl.BlockSpec((SSM_WIDTH, D_MODEL), const2),
            pl.BlockSpec((SSM_WIDTH, D_MODEL), const2),
            pl.BlockSpec((1, TOK_TILE, ATTN_WIDTH), tok),
            pl.BlockSpec((ATTN_WIDTH, D_MODEL), const2),
            pl.BlockSpec((1, TOK_TILE, POOL_WIDTH), tok),
            pl.BlockSpec((1, POOL_HALO, POOL_WIDTH),
                         lambda b, s: (b, jnp.maximum(jnp.minimum(s, last) * per_tile - 1, 0), 0)),
            pl.BlockSpec((1, POOL_HALO, POOL_WIDTH),
                         lambda b, s: (b, jnp.minimum((jnp.minimum(s, last) + 1) * per_tile, n_halo - 1), 0)),
            pl.BlockSpec((len(POOL_WINDOWS), POOL_GROUP, POOL_OUT_GROUP), const3),
            pl.BlockSpec((1, D_MODEL), const2),
            pl.BlockSpec((D_MODEL, D_MODEL), const2),
            pl.BlockSpec((1, D_MODEL), const2),
            pl.BlockSpec((D_MODEL, 2 * N_EXPERTS), const2),
            pl.BlockSpec((1, N_EXPERTS), const2),
        ],
        out_specs=[
            pl.BlockSpec((1, TOK_TILE, D_MODEL), tok),
            pl.BlockSpec((1, TOK_TILE, PACKED), prev),
            pl.BlockSpec((1, TOK_TILE, TOP_K), prev),
            pl.BlockSpec((1, TOK_TILE, TOP_K), prev),
        ],
        out_shape=[
            jax.ShapeDtypeStruct((BATCH, S_TOT, D_MODEL), F32),
            jax.ShapeDtypeStruct((BATCH, S_TOT, PACKED), jnp.uint32),
            jax.ShapeDtypeStruct((BATCH, S_TOT, TOP_K), jnp.int32),
            jax.ShapeDtypeStruct((BATCH, S_TOT, TOP_K), F32),
        ],
        scratch_shapes=[pltpu.VMEM((TOK_TILE, D_MODEL), F32)],
        compiler_params=pltpu.CompilerParams(
            dimension_semantics=("parallel", "arbitrary"), vmem_limit_bytes=VMEM_LIMIT),
        name="merge",
    )(*xs, mod, g1, w_in, yf, yb, u, dsk, wval, wgate, o, wup, pool_in, pool_in, pool_in, pw, ps, wout, g2, rw, rb)


def _route_kernel(te_ref, below_ref, dest_ref, be_ref, cnt_ref, carry_ref):
    ph = pl.program_id(0)
    i = pl.program_id(1)
    te = te_ref[...]
    lane = lax.broadcasted_iota(jnp.int32, (ROUTE_TILE, LANES), 1)
    ohs = [(te[:, k:k + 1] == lane).astype(F32) for k in range(TOP_K)]
    oh_all = ohs[0] + ohs[1] + ohs[2] + ohs[3]
    tile_cnt = jnp.sum(oh_all, axis=0, keepdims=True)

    @pl.when((ph == 0) & (i == 0))
    def _():
        cnt_ref[...] = jnp.zeros_like(cnt_ref)

    @pl.when(ph == 0)
    def _():
        cnt_ref[...] += tile_cnt

    @pl.when((ph == 1) & (i == 0))
    def _():
        cnt = jnp.broadcast_to(cnt_ref[...], (8, LANES))
        padded = jnp.floor((cnt + (ROW_BLOCK - 1)) / ROW_BLOCK) * ROW_BLOCK
        l8 = lax.broadcasted_iota(jnp.int32, (8, LANES), 1)
        end = padded
        s = 1
        while s < N_EXPERTS:
            end = end + jnp.where(l8 >= s, pltpu.roll(end, s, 1), 0.0)
            s *= 2
        carry_ref[...] = (end - padded)[0:1]
        blk_row = (lax.broadcasted_iota(jnp.int32, (N_BLOCKS_PAD, LANES), 0) * ROW_BLOCK).astype(F32)
        lb = lax.broadcasted_iota(jnp.int32, (N_BLOCKS_PAD, LANES), 1)
        hit = jnp.where((lb < N_EXPERTS) & (jnp.broadcast_to(end[0:1], blk_row.shape) <= blk_row), 1.0, 0.0)
        be = jnp.sum(hit, axis=-1, keepdims=True)
        be_ref[...] = jnp.broadcast_to(be, (N_BLOCKS_PAD, LANES)).astype(jnp.int32)

    @pl.when(ph == 1)
    def _():
        rank = carry_ref[...] + jnp.dot(below_ref[...], oh_all.astype(BF16), preferred_element_type=F32)
        dests = [jnp.sum(ohs[k] * rank, axis=-1, keepdims=True) for k in range(TOP_K)]
        dest_ref[...] = jnp.concatenate(dests, axis=1).astype(jnp.int32)
        carry_ref[...] += tile_cnt


def _route(top_e):
    n_steps = N_TOK // ROUTE_TILE
    return pl.pallas_call(
        _route_kernel,
        grid=(2, n_steps),
        in_specs=[pl.BlockSpec((ROUTE_TILE, TOP_K), lambda ph, i: (i, 0)),
                  pl.BlockSpec((ROUTE_TILE, ROUTE_TILE), lambda ph, i: (0, 0))],
        out_specs=[
            pl.BlockSpec((ROUTE_TILE, TOP_K), lambda ph, i: (ph * i, 0)),
            pl.BlockSpec((N_BLOCKS_PAD, LANES), lambda ph, i: (0, 0)),
        ],
        out_shape=[
            jax.ShapeDtypeStruct((N_TOK, TOP_K), jnp.int32),
            jax.ShapeDtypeStruct((N_BLOCKS_PAD, LANES), jnp.int32),
        ],
        scratch_shapes=[pltpu.VMEM((1, LANES), F32), pltpu.VMEM((1, LANES), F32)],
        compiler_params=pltpu.CompilerParams(dimension_semantics=("arbitrary", "arbitrary")),
        name="route",
    )(top_e, jnp.asarray(np.tril(np.ones((ROUTE_TILE, ROUTE_TILE), np.float32), -1), BF16))


def _sc_mesh():
    return plsc.VectorSubcoreMesh(core_axis_name="c", subcore_axis_name="s")


def _sc_worker():
    return lax.axis_index("s") * SC_CORES + lax.axis_index("c")


def _dispatch(f, dest_t):
    per_w = N_TOK // SC_WORKERS
    n_chunks = per_w // SC_SCATTER_ROWS

    @functools.partial(
        pl.kernel,
        out_type=jax.ShapeDtypeStruct((N_ROWS, PACKED), jnp.uint32),
        mesh=_sc_mesh(),
        scratch_types=[
            pltpu.VMEM((SC_SCATTER_ROWS,), jnp.int32),
            pltpu.VMEM((SC_SCATTER_ROWS, PACKED), jnp.uint32),
            pltpu.SemaphoreType.DMA,
        ],
        name="dispatch",
    )
    def k(f_hbm, i_hbm, o_hbm, idx_v, rows_v, sem):
        base = _sc_worker() * per_w

        @pl.loop(0, n_chunks)
        def _(j):
            t0 = pl.multiple_of(base + j * SC_SCATTER_ROWS, 8)
            pltpu.sync_copy(f_hbm.at[pl.ds(t0, SC_SCATTER_ROWS)], rows_v)
            for kk in range(TOP_K):
                pltpu.sync_copy(i_hbm.at[pl.ds(kk * N_TOK + t0, SC_SCATTER_ROWS)], idx_v)
                pltpu.async_copy(rows_v, o_hbm.at[idx_v], sem).wait()

    return k(f, dest_t.reshape(TOP_K * N_TOK))


def _combine_gather(yb, dest_t):
    n = TOP_K * N_TOK
    per_w = n // SC_WORKERS
    n_chunks = per_w // SC_GATHER_ROWS

    @functools.partial(
        pl.kernel,
        out_type=jax.ShapeDtypeStruct((n, PACKED), jnp.uint32),
        mesh=_sc_mesh(),
        scratch_types=[
            pltpu.VMEM((SC_GATHER_ROWS,), jnp.int32),
            pltpu.VMEM((SC_GATHER_ROWS, PACKED), jnp.uint32),
            pltpu.SemaphoreType.DMA,
        ],
        name="combine_gather",
    )
    def k(y_hbm, i_hbm, o_hbm, idx_v, rows_v, sem):
        base = _sc_worker() * per_w

        @pl.loop(0, n_chunks)
        def _(j):
            r0 = pl.multiple_of(base + j * SC_GATHER_ROWS, 8)
            pltpu.sync_copy(i_hbm.at[pl.ds(r0, SC_GATHER_ROWS)], idx_v)
            pltpu.async_copy(y_hbm.at[idx_v], rows_v, sem).wait()
            pltpu.sync_copy(rows_v, o_hbm.at[pl.ds(r0, SC_GATHER_ROWS)])

    return k(yb, dest_t.reshape(n))


def _weight_copies(w1_hbm, w2_hbm, w1f, w2f, sem, idx):
    return (pltpu.make_async_copy(w1_hbm.at[idx], w1f, sem.at[0]),
            pltpu.make_async_copy(w2_hbm.at[idx], w2f, sem.at[1]))


def _expert_kernel(be_ref, x_ref, w1_hbm, b1_ref, w2_hbm, b2_ref, y_ref, w1f, w2f, w1b, w2b, sem, *, layer):
    j = pl.program_id(0)
    e = be_ref[j]
    prev = be_ref[jnp.maximum(j - 1, 0)]
    live = e < N_EXPERTS
    first = live & ((j == 0) | (e != prev))
    copies = functools.partial(_weight_copies, w1_hbm, w2_hbm, w1f, w2f, sem)

    @pl.when(first & (j == 0))
    def _():
        for c in copies(layer * N_EXPERTS + e):
            c.start()

    @pl.when(first)
    def _():
        for c in copies(layer * N_EXPERTS + e):
            c.wait()
        w1b[...] = w1f[...].astype(BF16)
        w2b[...] = w2f[...].astype(BF16)
        nxt = lax.while_loop(
            lambda k: (k < N_BLOCKS) & (be_ref[jnp.minimum(k, N_BLOCKS - 1)] == e), lambda k: k + 1, j + 1)
        e_next = be_ref[jnp.minimum(nxt, N_BLOCKS - 1)]

        @pl.when((nxt < N_BLOCKS) & (e_next < N_EXPERTS))
        def _():
            for c in copies(layer * N_EXPERTS + e_next):
                c.start()

    @pl.when(live)
    def _():
        x = jnp.concatenate(_unpack_rows(x_ref[...]), axis=1).astype(BF16)
        gu = jnp.dot(x, w1b[...], preferred_element_type=F32) + b1_ref[0]
        gate = jnp.minimum(gu[:, :D_EXPERT], SWIGLU_LIMIT)
        lin = jnp.clip(gu[:, D_EXPERT:], -SWIGLU_LIMIT, SWIGLU_LIMIT)
        act = gate * _sigmoid(SWIGLU_ALPHA * gate) * (lin + 1.0)
        y_ref[...] = _pack_rows(jnp.dot(act.astype(BF16), w2b[...], preferred_element_type=F32) + b2_ref[0])


def _experts(block_e, rows, w1, b1, w2, b2, layer):
    ex = lambda j, be: layer * N_EXPERTS + jnp.minimum(be[j], N_EXPERTS - 1)
    return pl.pallas_call(
        functools.partial(_expert_kernel, layer=layer),
        grid_spec=pltpu.PrefetchScalarGridSpec(
            num_scalar_prefetch=1,
            grid=(N_BLOCKS,),
            in_specs=[
                pl.BlockSpec((ROW_BLOCK, PACKED), lambda j, be: (j, 0)),
                pl.BlockSpec(memory_space=pl.ANY),
                pl.BlockSpec((1, 1, 2 * D_EXPERT), lambda j, be: (ex(j, be), 0, 0)),
                pl.BlockSpec(memory_space=pl.ANY),
                pl.BlockSpec((1, 1, D_MODEL), lambda j, be: (ex(j, be), 0, 0)),
            ],
            out_specs=pl.BlockSpec((ROW_BLOCK, PACKED), lambda j, be: (j, 0)),
            scratch_shapes=[
                pltpu.VMEM((D_MODEL, 2 * D_EXPERT), F32),
                pltpu.VMEM((D_EXPERT, D_MODEL), F32),
                pltpu.VMEM((D_MODEL, 2 * D_EXPERT), BF16),
                pltpu.VMEM((D_EXPERT, D_MODEL), BF16),
                pltpu.SemaphoreType.DMA((2,)),
            ],
        ),
        out_shape=jax.ShapeDtypeStruct((N_ROWS, PACKED), jnp.uint32),
        compiler_params=pltpu.CompilerParams(
            dimension_semantics=("arbitrary",), vmem_limit_bytes=VMEM_LIMIT),
        name="experts",
    )(block_e, rows, w1, b1, w2, b2)


def _final_kernel(x_ref, mod_ref, g_ref, p_ref, fg_ref, o_ref):
    b = pl.program_id(0)
    gate = mod_ref[5, pl.ds(b, 1), :]
    o_ref[0] = _rms(x_ref[0] + gate * _routed_sum(g_ref, p_ref), fg_ref[...])


def _final(xa, mod, gathered, probs, final_g):
    ctx_tiles = CTX_LEN // TOK_TILE
    tok = lambda b, i: (b, i + ctx_tiles, 0)
    return pl.pallas_call(
        _final_kernel,
        grid=(BATCH, N_TILES - ctx_tiles),
        in_specs=[
            pl.BlockSpec((1, TOK_TILE, D_MODEL), tok),
            pl.BlockSpec((6, 8, D_MODEL), lambda b, i: (0, 0, 0)),
            pl.BlockSpec((TOP_K, 1, TOK_TILE, PACKED), lambda b, i: (0, b, i + ctx_tiles, 0)),
            pl.BlockSpec((1, TOK_TILE, TOP_K), tok),
            pl.BlockSpec((1, D_MODEL), lambda b, i: (0, 0)),
        ],
        out_specs=pl.BlockSpec((1, TOK_TILE, D_MODEL), lambda b, i: (b, i, 0)),
        out_shape=jax.ShapeDtypeStruct((BATCH, SEQ, D_MODEL), F32),
        compiler_params=pltpu.CompilerParams(dimension_semantics=("parallel", "parallel")),
        name="final",
    )(xa, mod, gathered.reshape(TOP_K, BATCH, S_TOT, PACKED), probs, final_g)


def _split_bf16(w):
    bits = lax.bitcast_convert_type(w.astype(F32), jnp.uint32) & jnp.uint32(0xFFFF0000)
    hi = lax.bitcast_convert_type(bits, F32)
    return jnp.concatenate([hi.astype(BF16), (w - hi).astype(BF16)], axis=-1)


def _rope_tables():
    t = np.arange(SEQ)
    axis_dim = HEAD_DIM // 2
    inv_freq = (1.0 / (ROPE_BASE ** (np.arange(0, axis_dim, 2, dtype=np.float32) / axis_dim))).astype(np.float32)
    ang_r = (t // GRID_W).astype(np.float32)[:, None] * inv_freq[None]
    ang_c = (t % GRID_W).astype(np.float32)[:, None] * inv_freq[None]
    ang = np.concatenate([ang_r, ang_r, ang_c, ang_c] * (LANES // HEAD_DIM), axis=-1)
    cos = np.concatenate([np.ones((CTX_LEN, LANES), np.float32), np.cos(ang)], axis=0)
    sin = np.concatenate([np.zeros((CTX_LEN, LANES), np.float32), np.sin(ang)], axis=0)
    return jnp.asarray(cos, F32), jnp.asarray(sin, F32)


def kernel(x, c, ctx, c_ctx, w_ada, b_ada, norm1_g, norm2_g, w_in, ssm_lam_re, ssm_lam_im, ssm_log_dt, ssm_b_re, ssm_b_im, ssm_c_re, ssm_c_im, ssm_d, ssm_w_val, ssm_w_gate, attn_sink, attn_w_up, pool_w, pool_scale, w_out, router_w, router_b, exp_w1, exp_b1, exp_w2, exp_b2, final_g):
    assert x.shape == (BATCH, SEQ, D_MODEL) and ctx.shape == (BATCH, CTX_LEN, D_MODEL)
    cos_t, sin_t = _rope_tables()
    cv = jnp.concatenate([c, c_ctx[None], jnp.zeros((8 - BATCH - 1, D_MODEL), F32)], axis=0)
    mods = _ada(cv, w_ada, b_ada)
    xs = (ctx, x)
    ew1 = exp_w1.reshape(DEPTH * N_EXPERTS, D_MODEL, 2 * D_EXPERT)
    eb1 = exp_b1.reshape(DEPTH * N_EXPERTS, 1, 2 * D_EXPERT)
    ew2 = exp_w2.reshape(DEPTH * N_EXPERTS, D_EXPERT, D_MODEL)
    eb2 = exp_b2.reshape(DEPTH * N_EXPERTS, 1, D_MODEL)
    half_gate = jnp.where(jnp.arange(IN_WIDTH) >= OFF_GATE, 0.5, 1.0).astype(F32)
    w_in_b = (w_in * half_gate).astype(BF16)
    pending = None
    for l in range(DEPTH):
        mod = mods[l]
        u, kv, q, pool_in, *updated = _inproj(xs, mod, norm1_g[l][None], w_in_b, l, cos_t, sin_t, pending)
        if updated:
            xs = (updated[0], updated[0])
        wb, wc, ar, ai = _ssm_params(ssm_lam_re[l], ssm_lam_im[l], ssm_log_dt[l], ssm_b_re[l], ssm_b_im[l],
                                     ssm_c_re[l], ssm_c_im[l])
        yf, yb = _ssm(u, wb, wc, ar, ai)
        o = _attn(attn_sink[l], q, kv)
        xa, f, top_e, top_p = _merge(
            xs, mod, norm1_g[l][None], w_in_b, l, yf, yb, u, ssm_d[l][None],
            (0.5 * ssm_w_val[l]).astype(BF16), (0.5 * ssm_w_gate[l]).astype(BF16), o, attn_w_up[l].astype(BF16),
            pool_in, pool_w[l].astype(BF16), pool_scale[l][None], (0.5 * w_out[l]).astype(BF16), norm2_g[l][None],
            _split_bf16(router_w[l]), router_b[l][None])
        dest, block_e = _route(top_e.reshape(N_TOK, TOP_K))
        dest_t = dest.T
        rows = _dispatch(f.reshape(N_TOK, PACKED), dest_t)
        yrows = _experts(block_e[:N_BLOCKS, 0], rows, ew1, eb1, ew2, eb2, l)
        gathered = _combine_gather(yrows, dest_t)
        xs = (xa, xa)
        pending = (mod, gathered, top_p)
    return _final(xs[1], *pending, final_g[None])
```

```python
import functools
import math

import jax
import jax.numpy as jnp
import numpy as np
from jax import lax
from jax.experimental import pallas as pl
from jax.experimental.pallas import tpu as pltpu
from jax.experimental.pallas import tpu_sc as plsc

F32 = jnp.float32
BF16 = jnp.bfloat16
HIGHEST = lax.Precision.HIGHEST

D_MODEL = 1024
BATCH = 2
SEQ = 8192
DEPTH = 2
GRID_W = 64
CTX_LEN = 256
NORM_EPS = 1e-6
SSM_WIDTH = 512
SSM_GROUP = 16
SSM_GROUPS = 32
SSM_STATE = 64
HEAD_DIM = 64
N_Q_HEADS = 8
N_KV_HEADS = 2
GQA_GROUP = 4
ATTN_WIDTH = 512
KV_WIDTH = 128
WINDOW = 128
ROPE_BASE = 10000.0
ATTN_SCALE = HEAD_DIM ** -0.5
POOL_WINDOWS = (2, 4, 8, 16)
POOL_WIDTH = 512
POOL_GROUP = 128
POOL_OUT_GROUP = 256
OFF_K = 512
OFF_V = 640
OFF_Q = 768
OFF_POOL = 1280
OFF_GATE = 1792
IN_WIDTH = OFF_GATE + 3 * D_MODEL
N_EXPERTS = 32
TOP_K = 4
D_EXPERT = 1024
SWIGLU_LIMIT = 7.0
SWIGLU_ALPHA = 1.702

S_TOT = CTX_LEN + SEQ
N_TOK = BATCH * S_TOT
TOK_TILE = 256
N_TILES = S_TOT // TOK_TILE
assert CTX_LEN == TOK_TILE
ATTN_BLOCK = 128
ATTN_STEP = 3
POOL_HALO = 8
ROUTE_TILE = 1056
ROW_BLOCK = 256
N_ROWS = -(-(N_TOK * TOP_K + N_EXPERTS * (ROW_BLOCK - 1)) // ROW_BLOCK) * ROW_BLOCK
N_BLOCKS = N_ROWS // ROW_BLOCK
EXPERT_STEP = 4
assert N_BLOCKS % EXPERT_STEP == 0
N_BLOCKS_PAD = -(-N_BLOCKS // 8) * 8
SC_CORES = 2
SC_WORKERS = SC_CORES * 16
PACKED = D_MODEL // 2
SC_SCATTER_ROWS = 88
SC_GATHER_ROWS = 88
assert N_TOK % (SC_WORKERS * SC_SCATTER_ROWS) == 0 and (N_TOK * TOP_K) % (SC_WORKERS * SC_GATHER_ROWS) == 0
VMEM_LIMIT = 56 * 1024 * 1024
LANES = 128
MASK_VALUE = -1e30


def _rms(x, g):
    return x * lax.rsqrt(jnp.mean(x * x, axis=-1, keepdims=True) + NORM_EPS) * g


def _sigmoid(x):
    return 0.5 * jnp.tanh(0.5 * x) + 0.5


def _pack_rows(x):
    half = x.shape[1] // 2
    lo = lax.bitcast_convert_type(x[:, :half].astype(BF16).astype(F32), jnp.uint32)
    hi = lax.bitcast_convert_type(x[:, half:].astype(BF16).astype(F32), jnp.uint32)
    return (lo >> 16) | hi


def _unpack_rows(p):
    lo = lax.bitcast_convert_type(p << 16, F32)
    hi = lax.bitcast_convert_type(p & jnp.uint32(0xFFFF0000), F32)
    return lo, hi


def _token_specs(xs, last=N_TILES - 1):
    ctx_tiles = CTX_LEN // TOK_TILE
    off = 0 if xs[1].shape[1] == S_TOT else ctx_tiles
    return [pl.BlockSpec((1, TOK_TILE, D_MODEL), lambda b, i: (b, 0, 0)),
            pl.BlockSpec((1, TOK_TILE, D_MODEL), lambda b, i: (b, jnp.clip(i, ctx_tiles, last) - off, 0))]


def _token_tile(i, xc_ref, xl_ref):
    return jnp.where(i == 0, xc_ref[0], xl_ref[0])


def _mod_row(i, b):
    return jnp.where(i == 0, 2, b)


def _ada_kernel(cv_ref, w_ref, b_ref, o_ref):
    cv = cv_ref[...]
    s = cv * jax.nn.sigmoid(cv)
    o_ref[0, 0] = jnp.dot(s, w_ref[0], preferred_element_type=F32, precision=HIGHEST) + b_ref[0, 0]


def _ada(cv, w_ada, b_ada):
    return pl.pallas_call(
        _ada_kernel,
        grid=(DEPTH, 6),
        in_specs=[
            pl.BlockSpec((8, D_MODEL), lambda l, j: (0, 0)),
            pl.BlockSpec((1, D_MODEL, D_MODEL), lambda l, j: (l, 0, j)),
            pl.BlockSpec((1, 1, 1, D_MODEL), lambda l, j: (l, j, 0, 0)),
        ],
        out_specs=pl.BlockSpec((1, 1, 8, D_MODEL), lambda l, j: (l, j, 0, 0)),
        out_shape=jax.ShapeDtypeStruct((DEPTH, 6, 8, D_MODEL), F32),
        name="ada",
    )(cv, w_ada, b_ada.reshape(DEPTH, 6, 1, D_MODEL))


def _routed_sum(g_ref, p_ref):
    p = p_ref[0]
    lo, hi = _unpack_rows(g_ref[0, 0])
    y_lo = p[:, 0:1] * lo
    y_hi = p[:, 0:1] * hi
    for k in range(1, TOP_K):
        lo, hi = _unpack_rows(g_ref[k, 0])
        y_lo = y_lo + p[:, k:k + 1] * lo
        y_hi = y_hi + p[:, k:k + 1] * hi
    return jnp.concatenate([y_lo, y_hi], axis=1)


def _rope(t, cos, sin):
    lane = lax.broadcasted_iota(jnp.int32, t.shape, 1)
    rot = jnp.where((lane % 32) < 16, -pltpu.roll(t, LANES - 16, 1), pltpu.roll(t, 16, 1))
    return t * cos + rot * sin


def _inproj_kernel(*refs, pending):
    if pending:
        (xc_ref, xl_ref, mod_ref, g_ref, w_ref, cos_ref, sin_ref, pmod_ref, rows_ref, probs_ref,
         u_ref, kv_ref, q_ref, pool_ref, xo_ref) = refs
    else:
        xc_ref, xl_ref, mod_ref, g_ref, w_ref, cos_ref, sin_ref, u_ref, kv_ref, q_ref, pool_ref = refs
    b = pl.program_id(0)
    i = pl.program_id(1)
    row = _mod_row(i, b)
    shift = mod_ref[0, pl.ds(row, 1), :]
    scale = mod_ref[1, pl.ds(row, 1), :]
    x = _token_tile(i, xc_ref, xl_ref)
    if pending:
        x = x + pmod_ref[5, pl.ds(row, 1), :] * _routed_sum(rows_ref, probs_ref)
        xo_ref[0] = x
    h = _rms(x, g_ref[...]) * (1.0 + scale) + shift
    p = jnp.dot(h.astype(BF16), w_ref[0, :, :OFF_GATE], preferred_element_type=F32)
    cos = cos_ref[...]
    sin = sin_ref[...]
    u_ref[0] = p[:, :OFF_K]
    kv_ref[0, :, :KV_WIDTH] = _rope(p[:, OFF_K:OFF_V], cos, sin).astype(BF16)
    kv_ref[0, :, KV_WIDTH:] = p[:, OFF_V:OFF_Q].astype(BF16)
    for c in range(ATTN_WIDTH // LANES):
        qc = p[:, OFF_Q + c * LANES:OFF_Q + (c + 1) * LANES]
        q_ref[0, :, c * LANES:(c + 1) * LANES] = (_rope(qc, cos, sin) * ATTN_SCALE).astype(BF16)
    pool_ref[0] = p[:, OFF_POOL:OFF_GATE]


def _inproj(xs, mod, g1, w_in, layer, cos_t, sin_t, pending=None):
    tok = lambda b, i: (b, i, 0)
    in_specs = [
        *_token_specs(xs),
        pl.BlockSpec((6, 8, D_MODEL), lambda b, i: (0, 0, 0)),
        pl.BlockSpec((1, D_MODEL), lambda b, i: (0, 0)),
        pl.BlockSpec((1, D_MODEL, IN_WIDTH), lambda b, i: (layer, 0, 0)),
        pl.BlockSpec((TOK_TILE, LANES), lambda b, i: (i, 0)),
        pl.BlockSpec((TOK_TILE, LANES), lambda b, i: (i, 0)),
    ]
    out_specs = [
        pl.BlockSpec((1, TOK_TILE, SSM_WIDTH), tok),
        pl.BlockSpec((1, TOK_TILE, 2 * KV_WIDTH), tok),
        pl.BlockSpec((1, TOK_TILE, ATTN_WIDTH), tok),
        pl.BlockSpec((1, TOK_TILE, POOL_WIDTH), tok),
    ]
    out_shape = [
        jax.ShapeDtypeStruct((BATCH, S_TOT, SSM_WIDTH), F32),
        jax.ShapeDtypeStruct((BATCH, S_TOT, 2 * KV_WIDTH), BF16),
        jax.ShapeDtypeStruct((BATCH, S_TOT, ATTN_WIDTH), BF16),
        jax.ShapeDtypeStruct((BATCH, S_TOT, POOL_WIDTH), F32),
    ]
    args = [*xs, mod, g1, w_in, cos_t, sin_t]
    if pending is not None:
        pmod, gathered, probs = pending
        in_specs += [
            pl.BlockSpec((6, 8, D_MODEL), lambda b, i: (0, 0, 0)),
            pl.BlockSpec((TOP_K, 1, TOK_TILE, PACKED), lambda b, i: (0, b, i, 0)),
            pl.BlockSpec((1, TOK_TILE, TOP_K), tok),
        ]
        out_specs.append(pl.BlockSpec((1, TOK_TILE, D_MODEL), tok))
        out_shape.append(jax.ShapeDtypeStruct((BATCH, S_TOT, D_MODEL), F32))
        args += [pmod, gathered.reshape(TOP_K, BATCH, S_TOT, PACKED), probs]
    return pl.pallas_call(
        functools.partial(_inproj_kernel, pending=pending is not None),
        grid=(BATCH, N_TILES),
        in_specs=in_specs,
        out_specs=out_specs,
        out_shape=out_shape,
        compiler_params=pltpu.CompilerParams(
            dimension_semantics=("parallel", "parallel"), vmem_limit_bytes=VMEM_LIMIT),
        name="inproj",
    )(*args)


SSM_HALF_IN = SSM_WIDTH // 2
SSM_HALF_ST = SSM_GROUPS // 2 * SSM_STATE
N_STATE_ROWS = 8
N_LANE_TILES = SSM_HALF_ST // LANES
SCAN_STRIDE = TOK_TILE + 8


def _ssm_kernel(uf_ref, ub_ref, wb_ref, wc_ref, ar_ref, ai_ref, yf_ref, yb_ref, buf_f, buf_b, hf_ref, hb_ref):
    i = pl.program_id(0)
    T = TOK_TILE

    @pl.when(i == 0)
    def _():
        hf_ref[...] = jnp.zeros_like(hf_ref)
        hb_ref[...] = jnp.zeros_like(hb_ref)

    for d, (u_ref, buf) in enumerate(((uf_ref, buf_f), (ub_ref, buf_b))):
        for b in range(BATCH):
            for half in range(2):
                ub16 = u_ref[b, :, half * SSM_HALF_IN:(half + 1) * SSM_HALF_IN].astype(BF16)
                for reim in range(2):
                    k = reim * 4 + half * 2 + b
                    bu = jnp.dot(ub16, wb_ref[d, reim, half], preferred_element_type=F32)
                    for l in range(N_LANE_TILES):
                        buf[l, k * SCAN_STRIDE:k * SCAN_STRIDE + T, :] = bu[:, l * LANES:(l + 1) * LANES]

    def step(t, carry):
        hf, hb = carry
        tb = T - 1 - t
        nf, nb = [], []
        for l in range(N_LANE_TILES):
            sl = pl.ds(l * LANES, LANES)
            rows_f = pl.ds(t, N_STATE_ROWS, stride=SCAN_STRIDE)
            h = ar_ref[0, :, sl] * hf[l] + ai_ref[0, :, sl] * pltpu.roll(hf[l], 4, 0) + buf_f[l, rows_f, :]
            buf_f[l, rows_f, :] = h
            nf.append(h)
            rows_b = pl.ds(tb, N_STATE_ROWS, stride=SCAN_STRIDE)
            h = ar_ref[1, :, sl] * hb[l] + ai_ref[1, :, sl] * pltpu.roll(hb[l], 4, 0) + buf_b[l, rows_b, :]
            buf_b[l, rows_b, :] = h
            nb.append(h)
        return tuple(nf), tuple(nb)

    hf0 = tuple(hf_ref[:, l * LANES:(l + 1) * LANES] for l in range(N_LANE_TILES))
    hb0 = tuple(hb_ref[:, l * LANES:(l + 1) * LANES] for l in range(N_LANE_TILES))
    hf1, hb1 = lax.fori_loop(0, T, step, (hf0, hb0), unroll=2)
    for l in range(N_LANE_TILES):
        hf_ref[:, l * LANES:(l + 1) * LANES] = hf1[l]
        hb_ref[:, l * LANES:(l + 1) * LANES] = hb1[l]

    for d, (buf, y_ref) in enumerate(((buf_f, yf_ref), (buf_b, yb_ref))):
        for b in range(BATCH):
            for half in range(2):
                kre = half * 2 + b
                kim = 4 + half * 2 + b
                state = lambda k: jnp.concatenate(
                    [buf[l, k * SCAN_STRIDE:k * SCAN_STRIDE + T, :] for l in range(N_LANE_TILES)],
                    axis=1).astype(BF16)
                y = jnp.dot(state(kre), wc_ref[d, 0, half], preferred_element_type=F32)
                y = y + jnp.dot(state(kim), wc_ref[d, 1, half], preferred_element_type=F32)
                y_ref[b, :, half * SSM_HALF_IN:(half + 1) * SSM_HALF_IN] = y


def _ssm(u, wb, wc, ar, ai):
    fwd = lambda i: (0, i, 0)
    bwd = lambda i: (0, jnp.where(i == 0, 0, N_TILES - i), 0)
    return pl.pallas_call(
        _ssm_kernel,
        grid=(N_TILES,),
        in_specs=[
            pl.BlockSpec((BATCH, TOK_TILE, SSM_WIDTH), fwd),
            pl.BlockSpec((BATCH, TOK_TILE, SSM_WIDTH), bwd),
            pl.BlockSpec((2, 2, 2, SSM_HALF_IN, SSM_HALF_ST), lambda i: (0, 0, 0, 0, 0)),
            pl.BlockSpec((2, 2, 2, SSM_HALF_ST, SSM_HALF_IN), lambda i: (0, 0, 0, 0, 0)),
            pl.BlockSpec((2, N_STATE_ROWS, SSM_HALF_ST), lambda i: (0, 0, 0)),
            pl.BlockSpec((2, N_STATE_ROWS, SSM_HALF_ST), lambda i: (0, 0, 0)),
        ],
        out_specs=[
            pl.BlockSpec((BATCH, TOK_TILE, SSM_WIDTH), fwd),
            pl.BlockSpec((BATCH, TOK_TILE, SSM_WIDTH), bwd),
        ],
        out_shape=[jax.ShapeDtypeStruct((BATCH, S_TOT, SSM_WIDTH), F32)] * 2,
        scratch_shapes=[
            pltpu.VMEM((N_LANE_TILES, N_STATE_ROWS * SCAN_STRIDE, LANES), F32),
            pltpu.VMEM((N_LANE_TILES, N_STATE_ROWS * SCAN_STRIDE, LANES), F32),
            pltpu.VMEM((N_STATE_ROWS, SSM_HALF_ST), F32),
            pltpu.VMEM((N_STATE_ROWS, SSM_HALF_ST), F32),
        ],
        compiler_params=pltpu.CompilerParams(
            dimension_semantics=("arbitrary",), vmem_limit_bytes=VMEM_LIMIT),
        name="ssm",
    )(u, u, wb, wc, ar, ai)


def _ssm_params(lam_re, lam_im, log_dt, b_re, b_im, c_re, c_im):
    lr = lam_re.astype(F32)
    li = lam_im.astype(F32)
    dt = jnp.exp(log_dt.astype(F32))[..., None]
    mag = jnp.exp(lr * dt)
    ang = li * dt
    a_re = mag * jnp.cos(ang)
    a_im = mag * jnp.sin(ang)
    den = lr * lr + li * li
    k_re = ((a_re - 1.0) * lr + a_im * li) / den
    k_im = (a_im * lr - (a_re - 1.0) * li) / den
    bb_re = k_re[..., None] * b_re - k_im[..., None] * b_im
    bb_im = k_re[..., None] * b_im + k_im[..., None] * b_re
    n_g = SSM_GROUPS // 2
    drive_mask = np.equal.outer(np.arange(SSM_HALF_IN) // SSM_GROUP, np.arange(SSM_HALF_ST) // SSM_STATE)
    read_mask = drive_mask.T

    def drive(bb):
        rows = jnp.swapaxes(bb, -1, -2).reshape(2, 2, SSM_HALF_IN, SSM_STATE)
        return jnp.tile(rows, (1, 1, 1, n_g)) * drive_mask.astype(np.float32)

    def readout(c):
        rows = jnp.swapaxes(c, -1, -2).reshape(2, 2, SSM_HALF_ST, SSM_GROUP)
        return jnp.tile(rows, (1, 1, 1, n_g)) * read_mask.astype(np.float32)

    wb = jnp.stack([drive(bb_re), drive(bb_im)], axis=1).astype(BF16)
    wc = jnp.stack([readout(c_re.astype(F32)), readout(-c_im.astype(F32))], axis=1).astype(BF16)

    def rows(a, sign_re, sign_im):
        halves = a.reshape(2, 2, SSM_HALF_ST)
        per = []
        for reim in range(2):
            for half in range(2):
                for _ in range(BATCH):
                    per.append(halves[:, half] * (sign_re if reim == 0 else sign_im))
        return jnp.stack(per, axis=1)

    ar = rows(a_re, 1.0, 1.0)
    ai = rows(a_im, -1.0, 1.0)
    return wb, wc, ar, ai


def _attn_block(n, q, kv_band, kv_ctx, sink_ref, band_bias):
    kv_all = jnp.concatenate([kv_band, kv_ctx], axis=0)
    nb = 3 * ATTN_BLOCK
    last = SEQ // ATTN_BLOCK - 1
    krow = lax.broadcasted_iota(jnp.int32, (nb, 1), 0)
    lo_r = jnp.where(n < 0, nb, jnp.where(n == 0, ATTN_BLOCK, 0))
    hi_r = jnp.where(n == last, 2 * ATTN_BLOCK, nb)
    bias = band_bias + jnp.where((krow >= lo_r) & (krow < hi_r), 0.0, MASK_VALUE)
    bias = jnp.concatenate([bias] * GQA_GROUP, axis=1)
    nt = (((1,), (1,)), ((), ()))
    outs = []
    for hk in range(N_KV_HEADS):
        k = kv_all[:, hk * HEAD_DIM:(hk + 1) * HEAD_DIM]
        v_t = kv_all[:, KV_WIDTH + hk * HEAD_DIM:KV_WIDTH + (hk + 1) * HEAD_DIM].astype(F32).T.astype(BF16)
        heads = range(hk * GQA_GROUP, (hk + 1) * GQA_GROUP)
        qs = jnp.concatenate([q[:, hq * HEAD_DIM:(hq + 1) * HEAD_DIM] for hq in heads], axis=0)
        sink = jnp.concatenate([jnp.full((1, ATTN_BLOCK), sink_ref[hq], F32) for hq in heads], axis=1)
        s_t = lax.dot_general(k, qs, nt, preferred_element_type=F32)
        s_w = s_t[:nb] + bias
        s_c = s_t[nb:]
        m = jnp.maximum(jnp.maximum(jnp.max(s_w, axis=0, keepdims=True),
                                    jnp.max(s_c, axis=0, keepdims=True)), sink)
        e_w = jnp.exp(s_w - m)
        e_c = jnp.exp(s_c - m)
        den = jnp.sum(e_w, axis=0, keepdims=True) + jnp.sum(e_c, axis=0, keepdims=True) + jnp.exp(sink - m)
        e = jnp.concatenate([e_w, e_c], axis=0).astype(BF16)
        o_t = jnp.dot(v_t, e, preferred_element_type=F32) / den
        outs.extend(o_t[:, g * ATTN_BLOCK:(g + 1) * ATTN_BLOCK].T for g in range(GQA_GROUP))
    return jnp.concatenate(outs, axis=1).astype(BF16)


def _attn_kernel(sink_ref, band_ref, q_ref, kc_ref, *rest):
    kv_refs, o_ref = rest[:-1], rest[-1]
    j0 = pl.program_id(1) * ATTN_STEP
    for a in range(ATTN_STEP):
        n = j0 + a - CTX_LEN // ATTN_BLOCK
        band = jnp.concatenate([kv_refs[a + d][0] for d in range(3)], axis=0)
        rows = slice(a * ATTN_BLOCK, (a + 1) * ATTN_BLOCK)
        o_ref[0, rows, :] = _attn_block(n, q_ref[0, rows, :], band, kc_ref[0], sink_ref, band_ref[...])


def _attn(sink, q, kv):
    nq = S_TOT // ATTN_BLOCK
    first = CTX_LEN // ATTN_BLOCK
    r = jnp.arange(ATTN_BLOCK)[None, :]
    c = jnp.arange(3 * ATTN_BLOCK)[:, None] - ATTN_BLOCK
    band_bias = jnp.where(jnp.abs(c - r) <= WINDOW, 0.0, MASK_VALUE).astype(F32)

    def kv_spec(d):
        return pl.BlockSpec((1, ATTN_BLOCK, 2 * KV_WIDTH),
                            lambda b, j: (b, jnp.clip(j * ATTN_STEP + d - 1, first, nq - 1), 0))

    return pl.pallas_call(
        _attn_kernel,
        grid=(BATCH, nq // ATTN_STEP),
        in_specs=[
            pl.BlockSpec(memory_space=pltpu.SMEM),
            pl.BlockSpec((3 * ATTN_BLOCK, ATTN_BLOCK), lambda b, j: (0, 0)),
            pl.BlockSpec((1, ATTN_STEP * ATTN_BLOCK, ATTN_WIDTH), lambda b, j: (b, j, 0)),
            pl.BlockSpec((1, CTX_LEN, 2 * KV_WIDTH), lambda b, j: (b, 0, 0)),
            *[kv_spec(d) for d in range(ATTN_STEP + 2)],
        ],
        out_specs=pl.BlockSpec((1, ATTN_STEP * ATTN_BLOCK, ATTN_WIDTH), lambda b, j: (b, j, 0)),
        out_shape=jax.ShapeDtypeStruct((BATCH, S_TOT, ATTN_WIDTH), BF16),
        compiler_params=pltpu.CompilerParams(dimension_semantics=("parallel", "parallel")),
        name="attn",
    )(sink, band_bias, q, kv, *[kv] * (ATTN_STEP + 2))


def _gelu_tanh(x):
    return 0.5 * x * (1.0 + jnp.tanh(math.sqrt(2.0 / math.pi) * (x + 0.044715 * (x * x * x))))


def _pool_means(ext, i):
    T = TOK_TILE
    n_ext = T + 2 * POOL_HALO
    seq_len = jnp.where(i == 0, CTX_LEN, SEQ)
    start = jnp.where(i == 0, 0, (i - 1) * T)
    r = start - POOL_HALO + lax.broadcasted_iota(jnp.int32, (n_ext, 1), 0)
    ext = jnp.where((r >= 0) & (r < seq_len), ext, 0.0)
    tl = start + lax.broadcasted_iota(jnp.int32, (T, 1), 0)
    back = lambda a, s: pltpu.roll(a, s, 0)
    fwd = lambda a, s: pltpu.roll(a, n_ext - s, 0)
    diffs = []
    for gi, w in enumerate(POOL_WINDOWS):
        e = ext[:, gi * POOL_GROUP:(gi + 1) * POOL_GROUP]
        lo = w // 2
        hi = w - 1 - lo
        s = e + back(e, 1)
        span = 2
        while span < w:
            s = s + back(s, span)
            span *= 2
        if w > 2:
            s = fwd(s, hi)
        cnt = (jnp.clip(tl + hi + 1, 0, seq_len) - jnp.clip(tl - lo, 0, seq_len)).astype(F32)
        mean = s[POOL_HALO:POOL_HALO + T] / cnt
        diffs.append((mean - e[POOL_HALO:POOL_HALO + T]).astype(BF16))
    return diffs


def _merge_kernel(xc_ref, xl_ref, mod_ref, g1_ref, w_ref, yf_ref, yb_ref, u_ref, dsk_ref, wval_ref, wgate_ref,
                  o_ref, wup_ref, pc_ref, pp_ref, pn_ref, pw_ref, ps_ref, wout_ref, g2_ref, rw_ref, rb_ref,
                  xo_ref, f_ref, te_ref, tp_ref, xprev_ref):
    b = pl.program_id(0)
    s = pl.program_id(1)

    @pl.when(s == 0)
    def _():
        xprev_ref[...] = jnp.zeros_like(xprev_ref)

    rowp = _mod_row(jnp.maximum(s - 1, 0), b)
    f = _rms(xprev_ref[...], g2_ref[...]) * (1.0 + mod_ref[4, pl.ds(rowp, 1), :]) + mod_ref[3, pl.ds(rowp, 1), :]
    f_ref[0] = _pack_rows(f)
    f_hi = f.astype(BF16)
    f_lo = (f - f_hi.astype(F32)).astype(BF16)
    r_hi = jnp.dot(f_hi, rw_ref[...], preferred_element_type=F32)
    r_lo = jnp.dot(f_lo, rw_ref[...], preferred_element_type=F32)
    logits = r_hi[:, :N_EXPERTS] + r_hi[:, N_EXPERTS:] + r_lo[:, :N_EXPERTS] + rb_ref[...]
    lane = lax.broadcasted_iota(jnp.int32, logits.shape, 1)
    vals, idxs = [], []
    for _ in range(TOP_K):
        mx = jnp.max(logits, axis=-1, keepdims=True)
        ix = jnp.min(jnp.where(logits == mx, lane, N_EXPERTS), axis=-1, keepdims=True)
        vals.append(mx)
        idxs.append(ix)
        logits = jnp.where(lane == ix, -jnp.inf, logits)
    ex = [jnp.exp(v - vals[0]) for v in vals]
    tot = ex[0] + ex[1] + ex[2] + ex[3]
    te_ref[0] = jnp.concatenate(idxs, axis=1)
    tp_ref[0] = jnp.concatenate(ex, axis=1) / tot

    i = jnp.minimum(s, N_TILES - 1)
    row = _mod_row(i, b)
    shift, scale, gate = (mod_ref[j, pl.ds(row, 1), :] for j in range(3))
    ext = jnp.concatenate([pp_ref[0], pc_ref[0], pn_ref[0]], axis=0)
    diffs = _pool_means(ext, i)
    x = _token_tile(i, xc_ref, xl_ref)
    h = (_rms(x, g1_ref[...]) * (1.0 + scale) + shift).astype(BF16)
    t = jnp.tanh(jnp.dot(h, w_ref[0, :, OFF_GATE:], preferred_element_type=F32))

    z = yf_ref[0] + yb_ref[0] + dsk_ref[...] * u_ref[0]
    z = _gelu_tanh(z).astype(BF16)
    val = jnp.dot(z, wval_ref[...], preferred_element_type=F32)
    y_ssm = val * jnp.tanh(jnp.dot(z, wgate_ref[...], preferred_element_type=F32)) + val
    y_attn = jnp.dot(o_ref[0], wup_ref[...], preferred_element_type=F32)
    y_pool = jnp.concatenate(
        [jnp.dot(diffs[gi], pw_ref[gi], preferred_element_type=F32) for gi in range(len(POOL_WINDOWS))],
        axis=1) * ps_ref[...]

    m = (t[:, :D_MODEL] * y_ssm + t[:, D_MODEL:2 * D_MODEL] * y_attn + t[:, 2 * D_MODEL:] * y_pool
         + (y_ssm + y_attn + y_pool))
    xo = x + gate * jnp.dot(m.astype(BF16), wout_ref[...], preferred_element_type=F32)
    xo_ref[0] = xo
    xprev_ref[...] = xo


def _merge(xs, mod, g1, w_in, layer, yf, yb, u, dsk, wval, wgate, o, wup, pool_in, pw, ps, wout, g2, rw, rb):
    last = N_TILES - 1
    tok = lambda b, s: (b, jnp.minimum(s, last), 0)
    prev = lambda b, s: (b, jnp.maximum(s - 1, 0), 0)
    const2 = lambda b, s: (0, 0)
    const3 = lambda b, s: (0, 0, 0)
    per_tile = TOK_TILE // POOL_HALO
    n_halo = S_TOT // POOL_HALO
    return pl.pallas_call(
        _merge_kernel,
        grid=(BATCH, N_TILES + 1),
        in_specs=[
            *_token_specs(xs, last),
            pl.BlockSpec((6, 8, D_MODEL), const3),
            pl.BlockSpec((1, D_MODEL), const2),
            pl.BlockSpec((1, D_MODEL, IN_WIDTH), lambda b, s: (layer, 0, 0)),
            pl.BlockSpec((1, TOK_TILE, SSM_WIDTH), tok),
            pl.BlockSpec((1, TOK_TILE, SSM_WIDTH), tok),
            pl.BlockSpec((1, TOK_TILE, SSM_WIDTH), tok),
            pl.BlockSpec((1, SSM_WIDTH), const2),
            pl.BlockSpec((SSM_WIDTH, D_MODEL), const2),
            pl.BlockSpec((SSM_WIDTH, D_MODEL), const2),
            pl.BlockSpec((1, TOK_TILE, ATTN_WIDTH), tok),
            pl.BlockSpec((ATTN_WIDTH, D_MODEL), const2),
            pl.BlockSpec((1, TOK_TILE, POOL_WIDTH), tok),
            pl.BlockSpec((1, POOL_HALO, POOL_WIDTH),
                         lambda b, s: (b, jnp.maximum(jnp.minimum(s, last) * per_tile - 1, 0), 0)),
            pl.BlockSpec((1, POOL_HALO, POOL_WIDTH),
                         lambda b, s: (b, jnp.minimum((jnp.minimum(s, last) + 1) * per_tile, n_halo - 1), 0)),
            pl.BlockSpec((len(POOL_WINDOWS), POOL_GROUP, POOL_OUT_GROUP), const3),
            pl.BlockSpec((1, D_MODEL), const2),
            pl.BlockSpec((D_MODEL, D_MODEL), const2),
            pl.BlockSpec((1, D_MODEL), const2),
            pl.BlockSpec((D_MODEL, 2 * N_EXPERTS), const2),
            pl.BlockSpec((1, N_EXPERTS), const2),
        ],
        out_specs=[
            pl.BlockSpec((1, TOK_TILE, D_MODEL), tok),
            pl.BlockSpec((1, TOK_TILE, PACKED), prev),
            pl.BlockSpec((1, TOK_TILE, TOP_K), prev),
            pl.BlockSpec((1, TOK_TILE, TOP_K), prev),
        ],
        out_shape=[
            jax.ShapeDtypeStruct((BATCH, S_TOT, D_MODEL), F32),
            jax.ShapeDtypeStruct((BATCH, S_TOT, PACKED), jnp.uint32),
            jax.ShapeDtypeStruct((BATCH, S_TOT, TOP_K), jnp.int32),
            jax.ShapeDtypeStruct((BATCH, S_TOT, TOP_K), F32),
        ],
        scratch_shapes=[pltpu.VMEM((TOK_TILE, D_MODEL), F32)],
        compiler_params=pltpu.CompilerParams(
            dimension_semantics=("parallel", "arbitrary"), vmem_limit_bytes=VMEM_LIMIT),
        name="merge",
    )(*xs, mod, g1, w_in, yf, yb, u, dsk, wval, wgate, o, wup, pool_in, pool_in, pool_in, pw, ps, wout, g2, rw, rb)


def _route_kernel(te_ref, below_ref, dest_ref, be_ref, cnt_ref, carry_ref):
    ph = pl.program_id(0)
    i = pl.program_id(1)
    te = te_ref[...]
    lane = lax.broadcasted_iota(jnp.int32, (ROUTE_TILE, LANES), 1)
    ohs = [(te[:, k:k + 1] == lane).astype(F32) for k in range(TOP_K)]
    oh_all = ohs[0] + ohs[1] + ohs[2] + ohs[3]
    tile_cnt = jnp.sum(oh_all, axis=0, keepdims=True)

    @pl.when((ph == 0) & (i == 0))
    def _():
        cnt_ref[...] = jnp.zeros_like(cnt_ref)

    @pl.when(ph == 0)
    def _():
        cnt_ref[...] += tile_cnt

    @pl.when((ph == 1) & (i == 0))
    def _():
        cnt = jnp.broadcast_to(cnt_ref[...], (8, LANES))
        padded = jnp.floor((cnt + (ROW_BLOCK - 1)) / ROW_BLOCK) * ROW_BLOCK
        l8 = lax.broadcasted_iota(jnp.int32, (8, LANES), 1)
        end = padded
        s = 1
        while s < N_EXPERTS:
            end = end + jnp.where(l8 >= s, pltpu.roll(end, s, 1), 0.0)
            s *= 2
        carry_ref[...] = (end - padded)[0:1]
        blk_row = (lax.broadcasted_iota(jnp.int32, (N_BLOCKS_PAD, LANES), 0) * ROW_BLOCK).astype(F32)
        lb = lax.broadcasted_iota(jnp.int32, (N_BLOCKS_PAD, LANES), 1)
        hit = jnp.where((lb < N_EXPERTS) & (jnp.broadcast_to(end[0:1], blk_row.shape) <= blk_row), 1.0, 0.0)
        be = jnp.sum(hit, axis=-1, keepdims=True)
        be_ref[...] = jnp.broadcast_to(be, (N_BLOCKS_PAD, LANES)).astype(jnp.int32)

    @pl.when(ph == 1)
    def _():
        rank = carry_ref[...] + jnp.dot(below_ref[...], oh_all.astype(BF16), preferred_element_type=F32)
        dests = [jnp.sum(ohs[k] * rank, axis=-1, keepdims=True) for k in range(TOP_K)]
        dest_ref[...] = jnp.concatenate(dests, axis=1).astype(jnp.int32)
        carry_ref[...] += tile_cnt


def _route(top_e):
    n_steps = N_TOK // ROUTE_TILE
    return pl.pallas_call(
        _route_kernel,
        grid=(2, n_steps),
        in_specs=[pl.BlockSpec((ROUTE_TILE, TOP_K), lambda ph, i: (i, 0)),
                  pl.BlockSpec((ROUTE_TILE, ROUTE_TILE), lambda ph, i: (0, 0))],
        out_specs=[
            pl.BlockSpec((ROUTE_TILE, TOP_K), lambda ph, i: (ph * i, 0)),
            pl.BlockSpec((N_BLOCKS_PAD, LANES), lambda ph, i: (0, 0)),
        ],
        out_shape=[
            jax.ShapeDtypeStruct((N_TOK, TOP_K), jnp.int32),
            jax.ShapeDtypeStruct((N_BLOCKS_PAD, LANES), jnp.int32),
        ],
        scratch_shapes=[pltpu.VMEM((1, LANES), F32), pltpu.VMEM((1, LANES), F32)],
        compiler_params=pltpu.CompilerParams(dimension_semantics=("arbitrary", "arbitrary")),
        name="route",
    )(top_e, jnp.asarray(np.tril(np.ones((ROUTE_TILE, ROUTE_TILE), np.float32), -1), BF16))


def _sc_mesh():
    return plsc.VectorSubcoreMesh(core_axis_name="c", subcore_axis_name="s")


def _sc_worker():
    return lax.axis_index("s") * SC_CORES + lax.axis_index("c")


def _dispatch(f, dest_t):
    per_w = N_TOK // SC_WORKERS
    n_chunks = per_w // SC_SCATTER_ROWS

    @functools.partial(
        pl.kernel,
        out_type=jax.ShapeDtypeStruct((N_ROWS, PACKED), jnp.uint32),
        mesh=_sc_mesh(),
        scratch_types=[
            pltpu.VMEM((SC_SCATTER_ROWS,), jnp.int32),
            pltpu.VMEM((SC_SCATTER_ROWS, PACKED), jnp.uint32),
            pltpu.SemaphoreType.DMA,
        ],
        name="dispatch",
    )
    def k(f_hbm, i_hbm, o_hbm, idx_v, rows_v, sem):
        base = _sc_worker() * per_w

        @pl.loop(0, n_chunks)
        def _(j):
            t0 = pl.multiple_of(base + j * SC_SCATTER_ROWS, 8)
            pltpu.sync_copy(f_hbm.at[pl.ds(t0, SC_SCATTER_ROWS)], rows_v)
            for kk in range(TOP_K):
                pltpu.sync_copy(i_hbm.at[pl.ds(kk * N_TOK + t0, SC_SCATTER_ROWS)], idx_v)
                pltpu.async_copy(rows_v, o_hbm.at[idx_v], sem).wait()

    return k(f, dest_t.reshape(TOP_K * N_TOK))


def _combine_gather(yb, dest_t):
    n = TOP_K * N_TOK
    per_w = n // SC_WORKERS
    n_chunks = per_w // SC_GATHER_ROWS

    @functools.partial(
        pl.kernel,
        out_type=jax.ShapeDtypeStruct((n, PACKED), jnp.uint32),
        mesh=_sc_mesh(),
        scratch_types=[
            pltpu.VMEM((SC_GATHER_ROWS,), jnp.int32),
            pltpu.VMEM((SC_GATHER_ROWS, PACKED), jnp.uint32),
            pltpu.SemaphoreType.DMA,
        ],
        name="combine_gather",
    )
    def k(y_hbm, i_hbm, o_hbm, idx_v, rows_v, sem):
        base = _sc_worker() * per_w

        @pl.loop(0, n_chunks)
        def _(j):
            r0 = pl.multiple_of(base + j * SC_GATHER_ROWS, 8)
            pltpu.sync_copy(i_hbm.at[pl.ds(r0, SC_GATHER_ROWS)], idx_v)
            pltpu.async_copy(y_hbm.at[idx_v], rows_v, sem).wait()
            pltpu.sync_copy(rows_v, o_hbm.at[pl.ds(r0, SC_GATHER_ROWS)])

    return k(yb, dest_t.reshape(n))


def _weight_copies(w1_hbm, w2_hbm, w1f, w2f, sem, idx):
    return (pltpu.make_async_copy(w1_hbm.at[idx], w1f, sem.at[0]),
            pltpu.make_async_copy(w2_hbm.at[idx], w2f, sem.at[1]))


def _expert_kernel(be_ref, x_ref, w1_hbm, b1_ref, w2_hbm, b2_ref, y_ref, w1f, w2f, w1b, w2b, sem, *, layer):
    copies = functools.partial(_weight_copies, w1_hbm, w2_hbm, w1f, w2f, sem)
    for a in range(EXPERT_STEP):
        j = pl.program_id(0) * EXPERT_STEP + a
        e = be_ref[j]
        prev = be_ref[jnp.maximum(j - 1, 0)]
        live = e < N_EXPERTS
        first = live & ((j == 0) | (e != prev))
        rows = slice(a * ROW_BLOCK, (a + 1) * ROW_BLOCK)

        if a == 0:
            @pl.when(first & (j == 0))
            def _():
                for c in copies(layer * N_EXPERTS + e):
                    c.start()

        @pl.when(first)
        def _():
            for c in copies(layer * N_EXPERTS + e):
                c.wait()
            w1b[...] = w1f[...].astype(BF16)
            w2b[...] = w2f[...].astype(BF16)
            nxt = lax.while_loop(
                lambda k: (k < N_BLOCKS) & (be_ref[jnp.minimum(k, N_BLOCKS - 1)] == e), lambda k: k + 1, j + 1)
            e_next = be_ref[jnp.minimum(nxt, N_BLOCKS - 1)]

            @pl.when((nxt < N_BLOCKS) & (e_next < N_EXPERTS))
            def _():
                for c in copies(layer * N_EXPERTS + e_next):
                    c.start()

        @pl.when(live)
        def _():
            idx = layer * N_EXPERTS + e
            x = jnp.concatenate(_unpack_rows(x_ref[rows, :]), axis=1).astype(BF16)
            gu = jnp.dot(x, w1b[...], preferred_element_type=F32) + b1_ref[idx]
            gate = jnp.minimum(gu[:, :D_EXPERT], SWIGLU_LIMIT)
            lin = jnp.clip(gu[:, D_EXPERT:], -SWIGLU_LIMIT, SWIGLU_LIMIT)
            act = gate * _sigmoid(SWIGLU_ALPHA * gate) * (lin + 1.0)
            y_ref[rows, :] = _pack_rows(
                jnp.dot(act.astype(BF16), w2b[...], preferred_element_type=F32) + b2_ref[idx])


def _experts(block_e, rows, w1, b1, w2, b2, layer):
    step_rows = EXPERT_STEP * ROW_BLOCK
    return pl.pallas_call(
        functools.partial(_expert_kernel, layer=layer),
        grid_spec=pltpu.PrefetchScalarGridSpec(
            num_scalar_prefetch=1,
            grid=(N_BLOCKS // EXPERT_STEP,),
            in_specs=[
                pl.BlockSpec((step_rows, PACKED), lambda j, be: (j, 0)),
                pl.BlockSpec(memory_space=pl.ANY),
                pl.BlockSpec((DEPTH * N_EXPERTS, 1, 2 * D_EXPERT), lambda j, be: (0, 0, 0)),
                pl.BlockSpec(memory_space=pl.ANY),
                pl.BlockSpec((DEPTH * N_EXPERTS, 1, D_MODEL), lambda j, be: (0, 0, 0)),
            ],
            out_specs=pl.BlockSpec((step_rows, PACKED), lambda j, be: (j, 0)),
            scratch_shapes=[
                pltpu.VMEM((D_MODEL, 2 * D_EXPERT), F32),
                pltpu.VMEM((D_EXPERT, D_MODEL), F32),
                pltpu.VMEM((D_MODEL, 2 * D_EXPERT), BF16),
                pltpu.VMEM((D_EXPERT, D_MODEL), BF16),
                pltpu.SemaphoreType.DMA((2,)),
            ],
        ),
        out_shape=jax.ShapeDtypeStruct((N_ROWS, PACKED), jnp.uint32),
        compiler_params=pltpu.CompilerParams(
            dimension_semantics=("arbitrary",), vmem_limit_bytes=VMEM_LIMIT),
        name="experts",
    )(block_e, rows, w1, b1, w2, b2)


def _final_kernel(x_ref, mod_ref, g_ref, p_ref, fg_ref, o_ref):
    b = pl.program_id(0)
    gate = mod_ref[5, pl.ds(b, 1), :]
    o_ref[0] = _rms(x_ref[0] + gate * _routed_sum(g_ref, p_ref), fg_ref[...])


def _final(xa, mod, gathered, probs, final_g):
    ctx_tiles = CTX_LEN // TOK_TILE
    tok = lambda b, i: (b, i + ctx_tiles, 0)
    return pl.pallas_call(
        _final_kernel,
        grid=(BATCH, N_TILES - ctx_tiles),
        in_specs=[
            pl.BlockSpec((1, TOK_TILE, D_MODEL), tok),
            pl.BlockSpec((6, 8, D_MODEL), lambda b, i: (0, 0, 0)),
            pl.BlockSpec((TOP_K, 1, TOK_TILE, PACKED), lambda b, i: (0, b, i + ctx_tiles, 0)),
            pl.BlockSpec((1, TOK_TILE, TOP_K), tok),
            pl.BlockSpec((1, D_MODEL), lambda b, i: (0, 0)),
        ],
        out_specs=pl.BlockSpec((1, TOK_TILE, D_MODEL), lambda b, i: (b, i, 0)),
        out_shape=jax.ShapeDtypeStruct((BATCH, SEQ, D_MODEL), F32),
        compiler_params=pltpu.CompilerParams(dimension_semantics=("parallel", "parallel")),
        name="final",
    )(xa, mod, gathered.reshape(TOP_K, BATCH, S_TOT, PACKED), probs, final_g)


def _split_bf16(w):
    bits = lax.bitcast_convert_type(w.astype(F32), jnp.uint32) & jnp.uint32(0xFFFF0000)
    hi = lax.bitcast_convert_type(bits, F32)
    return jnp.concatenate([hi.astype(BF16), (w - hi).astype(BF16)], axis=-1)


def _rope_tables():
    t = np.arange(SEQ)
    axis_dim = HEAD_DIM // 2
    inv_freq = (1.0 / (ROPE_BASE ** (np.arange(0, axis_dim, 2, dtype=np.float32) / axis_dim))).astype(np.float32)
    ang_r = (t // GRID_W).astype(np.float32)[:, None] * inv_freq[None]
    ang_c = (t % GRID_W).astype(np.float32)[:, None] * inv_freq[None]
    ang = np.concatenate([ang_r, ang_r, ang_c, ang_c] * (LANES // HEAD_DIM), axis=-1)
    cos = np.concatenate([np.ones((CTX_LEN, LANES), np.float32), np.cos(ang)], axis=0)
    sin = np.concatenate([np.zeros((CTX_LEN, LANES), np.float32), np.sin(ang)], axis=0)
    return jnp.asarray(cos, F32), jnp.asarray(sin, F32)


def kernel(x, c, ctx, c_ctx, w_ada, b_ada, norm1_g, norm2_g, w_in, ssm_lam_re, ssm_lam_im, ssm_log_dt, ssm_b_re, ssm_b_im, ssm_c_re, ssm_c_im, ssm_d, ssm_w_val, ssm_w_gate, attn_sink, attn_w_up, pool_w, pool_scale, w_out, router_w, router_b, exp_w1, exp_b1, exp_w2, exp_b2, final_g):
    assert x.shape == (BATCH, SEQ, D_MODEL) and ctx.shape == (BATCH, CTX_LEN, D_MODEL)
    cos_t, sin_t = _rope_tables()
    cv = jnp.concatenate([c, c_ctx[None], jnp.zeros((8 - BATCH - 1, D_MODEL), F32)], axis=0)
    mods = _ada(cv, w_ada, b_ada)
    xs = (ctx, x)
    ew1 = exp_w1.reshape(DEPTH * N_EXPERTS, D_MODEL, 2 * D_EXPERT)
    eb1 = exp_b1.reshape(DEPTH * N_EXPERTS, 1, 2 * D_EXPERT)
    ew2 = exp_w2.reshape(DEPTH * N_EXPERTS, D_EXPERT, D_MODEL)
    eb2 = exp_b2.reshape(DEPTH * N_EXPERTS, 1, D_MODEL)
    half_gate = jnp.where(jnp.arange(IN_WIDTH) >= OFF_GATE, 0.5, 1.0).astype(F32)
    w_in_b = (w_in * half_gate).astype(BF16)
    pending = None
    for l in range(DEPTH):
        mod = mods[l]
        u, kv, q, pool_in, *updated = _inproj(xs, mod, norm1_g[l][None], w_in_b, l, cos_t, sin_t, pending)
        if updated:
            xs = (updated[0], updated[0])
        wb, wc, ar, ai = _ssm_params(ssm_lam_re[l], ssm_lam_im[l], ssm_log_dt[l], ssm_b_re[l], ssm_b_im[l],
                                     ssm_c_re[l], ssm_c_im[l])
        yf, yb = _ssm(u, wb, wc, ar, ai)
        o = _attn(attn_sink[l], q, kv)
        xa, f, top_e, top_p = _merge(
            xs, mod, norm1_g[l][None], w_in_b, l, yf, yb, u, ssm_d[l][None],
            (0.5 * ssm_w_val[l]).astype(BF16), (0.5 * ssm_w_gate[l]).astype(BF16), o, attn_w_up[l].astype(BF16),
            pool_in, pool_w[l].astype(BF16), pool_scale[l][None], (0.5 * w_out[l]).astype(BF16), norm2_g[l][None],
            _split_bf16(router_w[l]), router_b[l][None])
        dest, block_e = _route(top_e.reshape(N_TOK, TOP_K))
        dest_t = dest.T
        rows = _dispatch(f.reshape(N_TOK, PACKED), dest_t)
        yrows = _experts(block_e[:N_BLOCKS, 0], rows, ew1, eb1, ew2, eb2, l)
        gathered = _combine_gather(yrows, dest_t)
        xs = (xa, xa)
        pending = (mod, gathered, top_p)
    return _final(xs[1], *pending, final_g[None])
```

```python
import functools
import math

import jax
import jax.numpy as jnp
import numpy as np
from jax import lax
from jax.experimental import pallas as pl
from jax.experimental.pallas import tpu as pltpu
from jax.experimental.pallas import tpu_sc as plsc

F32 = jnp.float32
BF16 = jnp.bfloat16
HIGHEST = lax.Precision.HIGHEST

D_MODEL = 1024
BATCH = 2
SEQ = 8192
DEPTH = 2
GRID_W = 64
CTX_LEN = 256
NORM_EPS = 1e-6
SSM_WIDTH = 512
SSM_GROUP = 16
SSM_GROUPS = 32
SSM_STATE = 64
HEAD_DIM = 64
N_Q_HEADS = 8
N_KV_HEADS = 2
GQA_GROUP = 4
ATTN_WIDTH = 512
KV_WIDTH = 128
WINDOW = 128
ROPE_BASE = 10000.0
ATTN_SCALE = HEAD_DIM ** -0.5
POOL_WINDOWS = (2, 4, 8, 16)
POOL_WIDTH = 512
POOL_GROUP = 128
POOL_OUT_GROUP = 256
OFF_K = 512
OFF_V = 640
OFF_Q = 768
OFF_POOL = 1280
OFF_GATE = 1792
IN_WIDTH = OFF_GATE + 3 * D_MODEL
N_EXPERTS = 32
TOP_K = 4
D_EXPERT = 1024
SWIGLU_LIMIT = 7.0
SWIGLU_ALPHA = 1.702

S_TOT = CTX_LEN + SEQ
N_TOK = BATCH * S_TOT
TOK_TILE = 256
N_TILES = S_TOT // TOK_TILE
assert CTX_LEN == TOK_TILE
ATTN_BLOCK = 128
ATTN_STEP = 3
POOL_HALO = 8
ROUTE_TILE = 1056
ROW_BLOCK = 256
N_ROWS = -(-(N_TOK * TOP_K + N_EXPERTS * (ROW_BLOCK - 1)) // ROW_BLOCK) * ROW_BLOCK
N_BLOCKS = N_ROWS // ROW_BLOCK
EXPERT_STEP = 4
assert N_BLOCKS % EXPERT_STEP == 0
N_BLOCKS_PAD = -(-N_BLOCKS // 8) * 8
SC_CORES = 2
SC_WORKERS = SC_CORES * 16
PACKED = D_MODEL // 2
SC_SCATTER_ROWS = 88
SC_GATHER_ROWS = 88
assert N_TOK % (SC_WORKERS * SC_SCATTER_ROWS) == 0 and (N_TOK * TOP_K) % (SC_WORKERS * SC_GATHER_ROWS) == 0
VMEM_LIMIT = 56 * 1024 * 1024
LANES = 128
MASK_VALUE = -1e30


def _rms(x, g):
    return x * lax.rsqrt(jnp.mean(x * x, axis=-1, keepdims=True) + NORM_EPS) * g


def _sigmoid(x):
    return 0.5 * jnp.tanh(0.5 * x) + 0.5


def _pack_rows(x):
    half = x.shape[1] // 2
    lo = lax.bitcast_convert_type(x[:, :half].astype(BF16).astype(F32), jnp.uint32)
    hi = lax.bitcast_convert_type(x[:, half:].astype(BF16).astype(F32), jnp.uint32)
    return (lo >> 16) | hi


def _unpack_rows(p):
    lo = lax.bitcast_convert_type(p << 16, F32)
    hi = lax.bitcast_convert_type(p & jnp.uint32(0xFFFF0000), F32)
    return lo, hi


def _token_specs(xs, last=N_TILES - 1):
    ctx_tiles = CTX_LEN // TOK_TILE
    off = 0 if xs[1].shape[1] == S_TOT else ctx_tiles
    return [pl.BlockSpec((1, TOK_TILE, D_MODEL), lambda b, i: (b, 0, 0)),
            pl.BlockSpec((1, TOK_TILE, D_MODEL), lambda b, i: (b, jnp.clip(i, ctx_tiles, last) - off, 0))]


def _token_tile(i, xc_ref, xl_ref):
    return jnp.where(i == 0, xc_ref[0], xl_ref[0])


def _mod_row(i, b):
    return jnp.where(i == 0, 2, b)


def _ada_kernel(cv_ref, w_ref, b_ref, o_ref):
    cv = cv_ref[...]
    s = cv * jax.nn.sigmoid(cv)
    o_ref[0, 0] = jnp.dot(s, w_ref[0], preferred_element_type=F32, precision=HIGHEST) + b_ref[0, 0]


def _ada(cv, w_ada, b_ada):
    return pl.pallas_call(
        _ada_kernel,
        grid=(DEPTH, 6),
        in_specs=[
            pl.BlockSpec((8, D_MODEL), lambda l, j: (0, 0)),
            pl.BlockSpec((1, D_MODEL, D_MODEL), lambda l, j: (l, 0, j)),
            pl.BlockSpec((1, 1, 1, D_MODEL), lambda l, j: (l, j, 0, 0)),
        ],
        out_specs=pl.BlockSpec((1, 1, 8, D_MODEL), lambda l, j: (l, j, 0, 0)),
        out_shape=jax.ShapeDtypeStruct((DEPTH, 6, 8, D_MODEL), F32),
        name="ada",
    )(cv, w_ada, b_ada.reshape(DEPTH, 6, 1, D_MODEL))


def _routed_sum(g_ref, p_ref):
    p = p_ref[0]
    lo, hi = _unpack_rows(g_ref[0, 0])
    y_lo = p[:, 0:1] * lo
    y_hi = p[:, 0:1] * hi
    for k in range(1, TOP_K):
        lo, hi = _unpack_rows(g_ref[k, 0])
        y_lo = y_lo + p[:, k:k + 1] * lo
        y_hi = y_hi + p[:, k:k + 1] * hi
    return jnp.concatenate([y_lo, y_hi], axis=1)


def _rope(t, cos, sin):
    lane = lax.broadcasted_iota(jnp.int32, t.shape, 1)
    rot = jnp.where((lane % 32) < 16, -pltpu.roll(t, LANES - 16, 1), pltpu.roll(t, 16, 1))
    return t * cos + rot * sin


def _inproj_kernel(*refs, pending):
    if pending:
        (xc_ref, xl_ref, mod_ref, g_ref, w_ref, cos_ref, sin_ref, pmod_ref, rows_ref, probs_ref,
         u_ref, kv_ref, q_ref, pool_ref, xo_ref) = refs
    else:
        xc_ref, xl_ref, mod_ref, g_ref, w_ref, cos_ref, sin_ref, u_ref, kv_ref, q_ref, pool_ref = refs
    b = pl.program_id(0)
    i = pl.program_id(1)
    row = _mod_row(i, b)
    shift = mod_ref[0, pl.ds(row, 1), :]
    scale = mod_ref[1, pl.ds(row, 1), :]
    x = _token_tile(i, xc_ref, xl_ref)
    if pending:
        x = x + pmod_ref[5, pl.ds(row, 1), :] * _routed_sum(rows_ref, probs_ref)
        xo_ref[0] = x
    h = _rms(x, g_ref[...]) * (1.0 + scale) + shift
    p = jnp.dot(h.astype(BF16), w_ref[0, :, :OFF_GATE], preferred_element_type=F32)
    cos = cos_ref[...]
    sin = sin_ref[...]
    u_ref[0] = p[:, :OFF_K]
    kv_ref[0, :, :KV_WIDTH] = _rope(p[:, OFF_K:OFF_V], cos, sin).astype(BF16)
    kv_ref[0, :, KV_WIDTH:] = p[:, OFF_V:OFF_Q].astype(BF16)
    for c in range(ATTN_WIDTH // LANES):
        qc = p[:, OFF_Q + c * LANES:OFF_Q + (c + 1) * LANES]
        q_ref[0, :, c * LANES:(c + 1) * LANES] = (_rope(qc, cos, sin) * ATTN_SCALE).astype(BF16)
    pool_ref[0] = p[:, OFF_POOL:OFF_GATE]


def _inproj(xs, mod, g1, w_in, layer, cos_t, sin_t, pending=None):
    tok = lambda b, i: (b, i, 0)
    in_specs = [
        *_token_specs(xs),
        pl.BlockSpec((6, 8, D_MODEL), lambda b, i: (0, 0, 0)),
        pl.BlockSpec((1, D_MODEL), lambda b, i: (0, 0)),
        pl.BlockSpec((1, D_MODEL, IN_WIDTH), lambda b, i: (layer, 0, 0)),
        pl.BlockSpec((TOK_TILE, LANES), lambda b, i: (i, 0)),
        pl.BlockSpec((TOK_TILE, LANES), lambda b, i: (i, 0)),
    ]
    out_specs = [
        pl.BlockSpec((1, TOK_TILE, SSM_WIDTH), tok),
        pl.BlockSpec((1, TOK_TILE, 2 * KV_WIDTH), tok),
        pl.BlockSpec((1, TOK_TILE, ATTN_WIDTH), tok),
        pl.BlockSpec((1, TOK_TILE, POOL_WIDTH), tok),
    ]
    out_shape = [
        jax.ShapeDtypeStruct((BATCH, S_TOT, SSM_WIDTH), F32),
        jax.ShapeDtypeStruct((BATCH, S_TOT, 2 * KV_WIDTH), BF16),
        jax.ShapeDtypeStruct((BATCH, S_TOT, ATTN_WIDTH), BF16),
        jax.ShapeDtypeStruct((BATCH, S_TOT, POOL_WIDTH), F32),
    ]
    args = [*xs, mod, g1, w_in, cos_t, sin_t]
    if pending is not None:
        pmod, gathered, probs = pending
        in_specs += [
            pl.BlockSpec((6, 8, D_MODEL), lambda b, i: (0, 0, 0)),
            pl.BlockSpec((TOP_K, 1, TOK_TILE, PACKED), lambda b, i: (0, b, i, 0)),
            pl.BlockSpec((1, TOK_TILE, TOP_K), tok),
        ]
        out_specs.append(pl.BlockSpec((1, TOK_TILE, D_MODEL), tok))
        out_shape.append(jax.ShapeDtypeStruct((BATCH, S_TOT, D_MODEL), F32))
        args += [pmod, gathered.reshape(TOP_K, BATCH, S_TOT, PACKED), probs]
    return pl.pallas_call(
        functools.partial(_inproj_kernel, pending=pending is not None),
        grid=(BATCH, N_TILES),
        in_specs=in_specs,
        out_specs=out_specs,
        out_shape=out_shape,
        compiler_params=pltpu.CompilerParams(
            dimension_semantics=("parallel", "parallel"), vmem_limit_bytes=VMEM_LIMIT),
        name="inproj",
    )(*args)


SSM_HALF_IN = SSM_WIDTH // 2
SSM_HALF_ST = SSM_GROUPS // 2 * SSM_STATE
N_STATE_ROWS = 8
N_LANE_TILES = SSM_HALF_ST // LANES
SCAN_STRIDE = TOK_TILE + 8


def _ssm_kernel(uf_ref, ub_ref, wb_ref, wc_ref, ar_ref, ai_ref, yf_ref, yb_ref, buf_f, buf_b, hf_ref, hb_ref):
    i = pl.program_id(0)
    T = TOK_TILE

    @pl.when(i == 0)
    def _():
        hf_ref[...] = jnp.zeros_like(hf_ref)
        hb_ref[...] = jnp.zeros_like(hb_ref)

    for d, (u_ref, buf) in enumerate(((uf_ref, buf_f), (ub_ref, buf_b))):
        for b in range(BATCH):
            for half in range(2):
                ub16 = u_ref[b, :, half * SSM_HALF_IN:(half + 1) * SSM_HALF_IN].astype(BF16)
                for reim in range(2):
                    k = reim * 4 + half * 2 + b
                    bu = jnp.dot(ub16, wb_ref[d, reim, half], preferred_element_type=F32)
                    for l in range(N_LANE_TILES):
                        buf[l, k * SCAN_STRIDE:k * SCAN_STRIDE + T, :] = bu[:, l * LANES:(l + 1) * LANES]

    def step(t, carry):
        hf, hb = carry
        tb = T - 1 - t
        nf, nb = [], []
        for l in range(N_LANE_TILES):
            sl = pl.ds(l * LANES, LANES)
            rows_f = pl.ds(t, N_STATE_ROWS, stride=SCAN_STRIDE)
            h = ar_ref[0, :, sl] * hf[l] + ai_ref[0, :, sl] * pltpu.roll(hf[l], 4, 0) + buf_f[l, rows_f, :]
            buf_f[l, rows_f, :] = h
            nf.append(h)
            rows_b = pl.ds(tb, N_STATE_ROWS, stride=SCAN_STRIDE)
            h = ar_ref[1, :, sl] * hb[l] + ai_ref[1, :, sl] * pltpu.roll(hb[l], 4, 0) + buf_b[l, rows_b, :]
            buf_b[l, rows_b, :] = h
            nb.append(h)
        return tuple(nf), tuple(nb)

    hf0 = tuple(hf_ref[:, l * LANES:(l + 1) * LANES] for l in range(N_LANE_TILES))
    hb0 = tuple(hb_ref[:, l * LANES:(l + 1) * LANES] for l in range(N_LANE_TILES))
    hf1, hb1 = lax.fori_loop(0, T, step, (hf0, hb0), unroll=2)
    for l in range(N_LANE_TILES):
        hf_ref[:, l * LANES:(l + 1) * LANES] = hf1[l]
        hb_ref[:, l * LANES:(l + 1) * LANES] = hb1[l]

    for d, (buf, y_ref) in enumerate(((buf_f, yf_ref), (buf_b, yb_ref))):
        for b in range(BATCH):
            for half in range(2):
                kre = half * 2 + b
                kim = 4 + half * 2 + b
                state = lambda k: jnp.concatenate(
                    [buf[l, k * SCAN_STRIDE:k * SCAN_STRIDE + T, :] for l in range(N_LANE_TILES)],
                    axis=1).astype(BF16)
                y = jnp.dot(state(kre), wc_ref[d, 0, half], preferred_element_type=F32)
                y = y + jnp.dot(state(kim), wc_ref[d, 1, half], preferred_element_type=F32)
                y_ref[b, :, half * SSM_HALF_IN:(half + 1) * SSM_HALF_IN] = y


def _ssm(u, wb, wc, ar, ai, layer):
    fwd = lambda i: (0, i, 0)
    bwd = lambda i: (0, jnp.where(i == 0, 0, N_TILES - i), 0)
    return pl.pallas_call(
        _ssm_kernel,
        grid=(N_TILES,),
        in_specs=[
            pl.BlockSpec((BATCH, TOK_TILE, SSM_WIDTH), fwd),
            pl.BlockSpec((BATCH, TOK_TILE, SSM_WIDTH), bwd),
            pl.BlockSpec((None, 2, 2, 2, SSM_HALF_IN, SSM_HALF_ST), lambda i: (layer, 0, 0, 0, 0, 0)),
            pl.BlockSpec((None, 2, 2, 2, SSM_HALF_ST, SSM_HALF_IN), lambda i: (layer, 0, 0, 0, 0, 0)),
            pl.BlockSpec((None, 2, N_STATE_ROWS, SSM_HALF_ST), lambda i: (layer, 0, 0, 0)),
            pl.BlockSpec((None, 2, N_STATE_ROWS, SSM_HALF_ST), lambda i: (layer, 0, 0, 0)),
        ],
        out_specs=[
            pl.BlockSpec((BATCH, TOK_TILE, SSM_WIDTH), fwd),
            pl.BlockSpec((BATCH, TOK_TILE, SSM_WIDTH), bwd),
        ],
        out_shape=[jax.ShapeDtypeStruct((BATCH, S_TOT, SSM_WIDTH), F32)] * 2,
        scratch_shapes=[
            pltpu.VMEM((N_LANE_TILES, N_STATE_ROWS * SCAN_STRIDE, LANES), F32),
            pltpu.VMEM((N_LANE_TILES, N_STATE_ROWS * SCAN_STRIDE, LANES), F32),
            pltpu.VMEM((N_STATE_ROWS, SSM_HALF_ST), F32),
            pltpu.VMEM((N_STATE_ROWS, SSM_HALF_ST), F32),
        ],
        compiler_params=pltpu.CompilerParams(
            dimension_semantics=("arbitrary",), vmem_limit_bytes=VMEM_LIMIT),
        name="ssm",
    )(u, u, wb, wc, ar, ai)


def _ssm_params(lam_re, lam_im, log_dt, b_re, b_im, c_re, c_im):
    lr = lam_re.astype(F32)
    li = lam_im.astype(F32)
    dt = jnp.exp(log_dt.astype(F32))[..., None]
    mag = jnp.exp(lr * dt)
    ang = li * dt
    a_re = mag * jnp.cos(ang)
    a_im = mag * jnp.sin(ang)
    den = lr * lr + li * li
    k_re = ((a_re - 1.0) * lr + a_im * li) / den
    k_im = (a_im * lr - (a_re - 1.0) * li) / den
    bb_re = k_re[..., None] * b_re - k_im[..., None] * b_im
    bb_im = k_re[..., None] * b_im + k_im[..., None] * b_re
    n_g = SSM_GROUPS // 2
    lead = lam_re.shape[:2]
    drive_mask = np.equal.outer(np.arange(SSM_HALF_IN) // SSM_GROUP, np.arange(SSM_HALF_ST) // SSM_STATE)
    repeat_st = np.tile(np.eye(SSM_STATE, dtype=np.float32), (1, n_g))
    repeat_in = np.tile(np.eye(SSM_GROUP, dtype=np.float32), (1, n_g))

    def block_diag(parts, rows_per_half, repeat, mask):
        rows = jnp.stack([jnp.swapaxes(p, -1, -2) for p in parts], axis=2)
        rows = rows.reshape(*lead, 2, 2, rows_per_half, repeat.shape[0])
        full = jnp.einsum('...rp,pc->...rc', rows, repeat, precision=HIGHEST)
        return (full * mask.astype(np.float32)).astype(BF16)

    wb = block_diag((bb_re, bb_im), SSM_HALF_IN, repeat_st, drive_mask)
    wc = block_diag((c_re.astype(F32), -c_im.astype(F32)), SSM_HALF_ST, repeat_in, drive_mask.T)

    def rows(a, sign_re):
        halves = a.reshape(*lead, 1, 2, 1, SSM_HALF_ST)
        signs = jnp.asarray([sign_re, 1.0], F32).reshape(2, 1, 1, 1)
        return jnp.broadcast_to(halves * signs, (*lead, 2, 2, BATCH, SSM_HALF_ST)).reshape(
            *lead, N_STATE_ROWS, SSM_HALF_ST)

    return wb, wc, rows(a_re, 1.0), rows(a_im, -1.0)


def _attn_block(n, q, kv_band, kv_ctx, sink_ref, band_bias):
    kv_all = jnp.concatenate([kv_band, kv_ctx], axis=0)
    nb = 3 * ATTN_BLOCK
    last = SEQ // ATTN_BLOCK - 1
    krow = lax.broadcasted_iota(jnp.int32, (nb, 1), 0)
    lo_r = jnp.where(n < 0, nb, jnp.where(n == 0, ATTN_BLOCK, 0))
    hi_r = jnp.where(n == last, 2 * ATTN_BLOCK, nb)
    bias = band_bias + jnp.where((krow >= lo_r) & (krow < hi_r), 0.0, MASK_VALUE)
    bias = jnp.concatenate([bias] * GQA_GROUP, axis=1)
    nt = (((1,), (1,)), ((), ()))
    outs = []
    for hk in range(N_KV_HEADS):
        k = kv_all[:, hk * HEAD_DIM:(hk + 1) * HEAD_DIM]
        v_t = kv_all[:, KV_WIDTH + hk * HEAD_DIM:KV_WIDTH + (hk + 1) * HEAD_DIM].astype(F32).T.astype(BF16)
        heads = range(hk * GQA_GROUP, (hk + 1) * GQA_GROUP)
        qs = jnp.concatenate([q[:, hq * HEAD_DIM:(hq + 1) * HEAD_DIM] for hq in heads], axis=0)
        sink = jnp.concatenate([jnp.full((1, ATTN_BLOCK), sink_ref[hq], F32) for hq in heads], axis=1)
        s_t = lax.dot_general(k, qs, nt, preferred_element_type=F32)
        s_w = s_t[:nb] + bias
        s_c = s_t[nb:]
        m = jnp.maximum(jnp.maximum(jnp.max(s_w, axis=0, keepdims=True),
                                    jnp.max(s_c, axis=0, keepdims=True)), sink)
        e_w = jnp.exp(s_w - m)
        e_c = jnp.exp(s_c - m)
        den = jnp.sum(e_w, axis=0, keepdims=True) + jnp.sum(e_c, axis=0, keepdims=True) + jnp.exp(sink - m)
        e = jnp.concatenate([e_w, e_c], axis=0).astype(BF16)
        o_t = jnp.dot(v_t, e, preferred_element_type=F32) / den
        outs.extend(o_t[:, g * ATTN_BLOCK:(g + 1) * ATTN_BLOCK].T for g in range(GQA_GROUP))
    return jnp.concatenate(outs, axis=1).astype(BF16)


def _attn_kernel(sink_ref, band_ref, q_ref, kc_ref, *rest):
    kv_refs, o_ref = rest[:-1], rest[-1]
    j0 = pl.program_id(1) * ATTN_STEP
    for a in range(ATTN_STEP):
        n = j0 + a - CTX_LEN // ATTN_BLOCK
        band = jnp.concatenate([kv_refs[a + d][0] for d in range(3)], axis=0)
        rows = slice(a * ATTN_BLOCK, (a + 1) * ATTN_BLOCK)
        o_ref[0, rows, :] = _attn_block(n, q_ref[0, rows, :], band, kc_ref[0], sink_ref, band_ref[...])


def _attn(sink, q, kv):
    nq = S_TOT // ATTN_BLOCK
    first = CTX_LEN // ATTN_BLOCK
    r = jnp.arange(ATTN_BLOCK)[None, :]
    c = jnp.arange(3 * ATTN_BLOCK)[:, None] - ATTN_BLOCK
    band_bias = jnp.where(jnp.abs(c - r) <= WINDOW, 0.0, MASK_VALUE).astype(F32)

    def kv_spec(d):
        return pl.BlockSpec((1, ATTN_BLOCK, 2 * KV_WIDTH),
                            lambda b, j: (b, jnp.clip(j * ATTN_STEP + d - 1, first, nq - 1), 0))

    return pl.pallas_call(
        _attn_kernel,
        grid=(BATCH, nq // ATTN_STEP),
        in_specs=[
            pl.BlockSpec(memory_space=pltpu.SMEM),
            pl.BlockSpec((3 * ATTN_BLOCK, ATTN_BLOCK), lambda b, j: (0, 0)),
            pl.BlockSpec((1, ATTN_STEP * ATTN_BLOCK, ATTN_WIDTH), lambda b, j: (b, j, 0)),
            pl.BlockSpec((1, CTX_LEN, 2 * KV_WIDTH), lambda b, j: (b, 0, 0)),
            *[kv_spec(d) for d in range(ATTN_STEP + 2)],
        ],
        out_specs=pl.BlockSpec((1, ATTN_STEP * ATTN_BLOCK, ATTN_WIDTH), lambda b, j: (b, j, 0)),
        out_shape=jax.ShapeDtypeStruct((BATCH, S_TOT, ATTN_WIDTH), BF16),
        compiler_params=pltpu.CompilerParams(dimension_semantics=("parallel", "parallel")),
        name="attn",
    )(sink, band_bias, q, kv, *[kv] * (ATTN_STEP + 2))


def _gelu_tanh(x):
    return 0.5 * x * (1.0 + jnp.tanh(math.sqrt(2.0 / math.pi) * (x + 0.044715 * (x * x * x))))


def _pool_means(ext, i):
    T = TOK_TILE
    n_ext = T + 2 * POOL_HALO
    seq_len = jnp.where(i == 0, CTX_LEN, SEQ)
    start = jnp.where(i == 0, 0, (i - 1) * T)
    r = start - POOL_HALO + lax.broadcasted_iota(jnp.int32, (n_ext, 1), 0)
    ext = jnp.where((r >= 0) & (r < seq_len), ext, 0.0)
    tl = start + lax.broadcasted_iota(jnp.int32, (T, 1), 0)
    back = lambda a, s: pltpu.roll(a, s, 0)
    fwd = lambda a, s: pltpu.roll(a, n_ext - s, 0)
    diffs = []
    for gi, w in enumerate(POOL_WINDOWS):
        e = ext[:, gi * POOL_GROUP:(gi + 1) * POOL_GROUP]
        lo = w // 2
        hi = w - 1 - lo
        s = e + back(e, 1)
        span = 2
        while span < w:
            s = s + back(s, span)
            span *= 2
        if w > 2:
            s = fwd(s, hi)
        cnt = (jnp.clip(tl + hi + 1, 0, seq_len) - jnp.clip(tl - lo, 0, seq_len)).astype(F32)
        mean = s[POOL_HALO:POOL_HALO + T] / cnt
        diffs.append((mean - e[POOL_HALO:POOL_HALO + T]).astype(BF16))
    return diffs


def _merge_kernel(xc_ref, xl_ref, mod_ref, g1_ref, w_ref, yf_ref, yb_ref, u_ref, dsk_ref, wval_ref, wgate_ref,
                  o_ref, wup_ref, pc_ref, pp_ref, pn_ref, pw_ref, ps_ref, wout_ref, g2_ref, rw_ref, rb_ref,
                  xo_ref, f_ref, te_ref, tp_ref, xprev_ref):
    b = pl.program_id(0)
    s = pl.program_id(1)

    @pl.when(s == 0)
    def _():
        xprev_ref[...] = jnp.zeros_like(xprev_ref)

    rowp = _mod_row(jnp.maximum(s - 1, 0), b)
    f = _rms(xprev_ref[...], g2_ref[...]) * (1.0 + mod_ref[4, pl.ds(rowp, 1), :]) + mod_ref[3, pl.ds(rowp, 1), :]
    f_ref[0] = _pack_rows(f)
    f_hi = f.astype(BF16)
    f_lo = (f - f_hi.astype(F32)).astype(BF16)
    r_hi = jnp.dot(f_hi, rw_ref[...], preferred_element_type=F32)
    r_lo = jnp.dot(f_lo, rw_ref[...], preferred_element_type=F32)
    logits = r_hi[:, :N_EXPERTS] + r_hi[:, N_EXPERTS:] + r_lo[:, :N_EXPERTS] + rb_ref[...]
    lane = lax.broadcasted_iota(jnp.int32, logits.shape, 1)
    vals, idxs = [], []
    for _ in range(TOP_K):
        mx = jnp.max(logits, axis=-1, keepdims=True)
        ix = jnp.min(jnp.where(logits == mx, lane, N_EXPERTS), axis=-1, keepdims=True)
        vals.append(mx)
        idxs.append(ix)
        logits = jnp.where(lane == ix, -jnp.inf, logits)
    ex = [jnp.exp(v - vals[0]) for v in vals]
    tot = ex[0] + ex[1] + ex[2] + ex[3]
    te_ref[0] = jnp.concatenate(idxs, axis=1)
    tp_ref[0] = jnp.concatenate(ex, axis=1) / tot

    i = jnp.minimum(s, N_TILES - 1)
    row = _mod_row(i, b)
    shift, scale, gate = (mod_ref[j, pl.ds(row, 1), :] for j in range(3))
    ext = jnp.concatenate([pp_ref[0], pc_ref[0], pn_ref[0]], axis=0)
    diffs = _pool_means(ext, i)
    x = _token_tile(i, xc_ref, xl_ref)
    h = (_rms(x, g1_ref[...]) * (1.0 + scale) + shift).astype(BF16)
    t = jnp.tanh(jnp.dot(h, w_ref[0, :, OFF_GATE:], preferred_element_type=F32))

    z = yf_ref[0] + yb_ref[0] + dsk_ref[...] * u_ref[0]
    z = _gelu_tanh(z).astype(BF16)
    val = jnp.dot(z, wval_ref[...], preferred_element_type=F32)
    y_ssm = val * jnp.tanh(jnp.dot(z, wgate_ref[...], preferred_element_type=F32)) + val
    y_attn = jnp.dot(o_ref[0], wup_ref[...], preferred_element_type=F32)
    y_pool = jnp.concatenate(
        [jnp.dot(diffs[gi], pw_ref[gi], preferred_element_type=F32) for gi in range(len(POOL_WINDOWS))],
        axis=1) * ps_ref[...]

    m = (t[:, :D_MODEL] * y_ssm + t[:, D_MODEL:2 * D_MODEL] * y_attn + t[:, 2 * D_MODEL:] * y_pool
         + (y_ssm + y_attn + y_pool))
    xo = x + gate * jnp.dot(m.astype(BF16), wout_ref[...], preferred_element_type=F32)
    xo_ref[0] = xo
    xprev_ref[...] = xo


def _merge(xs, mod, g1, w_in, layer, yf, yb, u, dsk, wval, wgate, o, wup, pool_in, pw, ps, wout, g2, rw, rb):
    last = N_TILES - 1
    tok = lambda b, s: (b, jnp.minimum(s, last), 0)
    prev = lambda b, s: (b, jnp.maximum(s - 1, 0), 0)
    const2 = lambda b, s: (0, 0)
    const3 = lambda b, s: (0, 0, 0)
    per_tile = TOK_TILE // POOL_HALO
    n_halo = S_TOT // POOL_HALO
    return pl.pallas_call(
        _merge_kernel,
        grid=(BATCH, N_TILES + 1),
        in_specs=[
            *_token_specs(xs, last),
            pl.BlockSpec((6, 8, D_MODEL), const3),
            pl.BlockSpec((1, D_MODEL), const2),
            pl.BlockSpec((1, D_MODEL, IN_WIDTH), lambda b, s: (layer, 0, 0)),
            pl.BlockSpec((1, TOK_TILE, SSM_WIDTH), tok),
            pl.BlockSpec((1, TOK_TILE, SSM_WIDTH), tok),
            pl.BlockSpec((1, TOK_TILE, SSM_WIDTH), tok),
            pl.BlockSpec((1, SSM_WIDTH), const2),
            pl.BlockSpec((SSM_WIDTH, D_MODEL), const2),
            pl.BlockSpec((SSM_WIDTH, D_MODEL), const2),
            pl.BlockSpec((1, TOK_TILE, ATTN_WIDTH), tok),
            pl.BlockSpec((ATTN_WIDTH, D_MODEL), const2),
            pl.BlockSpec((1, TOK_TILE, POOL_WIDTH), tok),
            pl.BlockSpec((1, POOL_HALO, POOL_WIDTH),
                         lambda b, s: (b, jnp.maximum(jnp.minimum(s, last) * per_tile - 1, 0), 0)),
            pl.BlockSpec((1, POOL_HALO, POOL_WIDTH),
                         lambda b, s: (b, jnp.minimum((jnp.minimum(s, last) + 1) * per_tile, n_halo - 1), 0)),
            pl.BlockSpec((len(POOL_WINDOWS), POOL_GROUP, POOL_OUT_GROUP), const3),
            pl.BlockSpec((1, D_MODEL), const2),
            pl.BlockSpec((D_MODEL, D_MODEL), const2),
            pl.BlockSpec((1, D_MODEL), const2),
            pl.BlockSpec((D_MODEL, 2 * N_EXPERTS), const2),
            pl.BlockSpec((1, N_EXPERTS), const2),
        ],
        out_specs=[
            pl.BlockSpec((1, TOK_TILE, D_MODEL), tok),
            pl.BlockSpec((1, TOK_TILE, PACKED), prev),
            pl.BlockSpec((1, TOK_TILE, TOP_K), prev),
            pl.BlockSpec((1, TOK_TILE, TOP_K), prev),
        ],
        out_shape=[
            jax.ShapeDtypeStruct((BATCH, S_TOT, D_MODEL), F32),
            jax.ShapeDtypeStruct((BATCH, S_TOT, PACKED), jnp.uint32),
            jax.ShapeDtypeStruct((BATCH, S_TOT, TOP_K), jnp.int32),
            jax.ShapeDtypeStruct((BATCH, S_TOT, TOP_K), F32),
        ],
        scratch_shapes=[pltpu.VMEM((TOK_TILE, D_MODEL), F32)],
        compiler_params=pltpu.CompilerParams(
            dimension_semantics=("parallel", "arbitrary"), vmem_limit_bytes=VMEM_LIMIT),
        name="merge",
    )(*xs, mod, g1, w_in, yf, yb, u, dsk, wval, wgate, o, wup, pool_in, pool_in, pool_in, pw, ps, wout, g2, rw, rb)


def _route_kernel(te_ref, below_ref, dest_ref, be_ref, cnt_ref, carry_ref):
    ph = pl.program_id(0)
    i = pl.program_id(1)
    te = te_ref[...]
    lane = lax.broadcasted_iota(jnp.int32, (ROUTE_TILE, LANES), 1)
    ohs = [(te[:, k:k + 1] == lane).astype(F32) for k in range(TOP_K)]
    oh_all = ohs[0] + ohs[1] + ohs[2] + ohs[3]
    tile_cnt = jnp.sum(oh_all, axis=0, keepdims=True)

    @pl.when((ph == 0) & (i == 0))
    def _():
        cnt_ref[...] = jnp.zeros_like(cnt_ref)

    @pl.when(ph == 0)
    def _():
        cnt_ref[...] += tile_cnt

    @pl.when((ph == 1) & (i == 0))
    def _():
        cnt = jnp.broadcast_to(cnt_ref[...], (8, LANES))
        padded = jnp.floor((cnt + (ROW_BLOCK - 1)) / ROW_BLOCK) * ROW_BLOCK
        l8 = lax.broadcasted_iota(jnp.int32, (8, LANES), 1)
        end = padded
        s = 1
        while s < N_EXPERTS:
            end = end + jnp.where(l8 >= s, pltpu.roll(end, s, 1), 0.0)
            s *= 2
        carry_ref[...] = (end - padded)[0:1]
        blk_row = (lax.broadcasted_iota(jnp.int32, (N_BLOCKS_PAD, LANES), 0) * ROW_BLOCK).astype(F32)
        lb = lax.broadcasted_iota(jnp.int32, (N_BLOCKS_PAD, LANES), 1)
        hit = jnp.where((lb < N_EXPERTS) & (jnp.broadcast_to(end[0:1], blk_row.shape) <= blk_row), 1.0, 0.0)
        be = jnp.sum(hit, axis=-1, keepdims=True)
        be_ref[...] = jnp.broadcast_to(be, (N_BLOCKS_PAD, LANES)).astype(jnp.int32)

    @pl.when(ph == 1)
    def _():
        rank = carry_ref[...] + jnp.dot(below_ref[...], oh_all.astype(BF16), preferred_element_type=F32)
        dests = [jnp.sum(ohs[k] * rank, axis=-1, keepdims=True) for k in range(TOP_K)]
        dest_ref[...] = jnp.concatenate(dests, axis=1).astype(jnp.int32)
        carry_ref[...] += tile_cnt


def _route(top_e):
    n_steps = N_TOK // ROUTE_TILE
    return pl.pallas_call(
        _route_kernel,
        grid=(2, n_steps),
        in_specs=[pl.BlockSpec((ROUTE_TILE, TOP_K), lambda ph, i: (i, 0)),
                  pl.BlockSpec((ROUTE_TILE, ROUTE_TILE), lambda ph, i: (0, 0))],
        out_specs=[
            pl.BlockSpec((ROUTE_TILE, TOP_K), lambda ph, i: (ph * i, 0)),
            pl.BlockSpec((N_BLOCKS_PAD, LANES), lambda ph, i: (0, 0)),
        ],
        out_shape=[
            jax.ShapeDtypeStruct((N_TOK, TOP_K), jnp.int32),
            jax.ShapeDtypeStruct((N_BLOCKS_PAD, LANES), jnp.int32),
        ],
        scratch_shapes=[pltpu.VMEM((1, LANES), F32), pltpu.VMEM((1, LANES), F32)],
        compiler_params=pltpu.CompilerParams(dimension_semantics=("arbitrary", "arbitrary")),
        name="route",
    )(top_e, jnp.asarray(np.tril(np.ones((ROUTE_TILE, ROUTE_TILE), np.float32), -1), BF16))


def _sc_mesh():
    return plsc.VectorSubcoreMesh(core_axis_name="c", subcore_axis_name="s")


def _sc_worker():
    return lax.axis_index("s") * SC_CORES + lax.axis_index("c")


def _dispatch(f, dest_t):
    per_w = N_TOK // SC_WORKERS
    n_chunks = per_w // SC_SCATTER_ROWS

    @functools.partial(
        pl.kernel,
        out_type=jax.ShapeDtypeStruct((N_ROWS, PACKED), jnp.uint32),
        mesh=_sc_mesh(),
        scratch_types=[
            pltpu.VMEM((SC_SCATTER_ROWS,), jnp.int32),
            pltpu.VMEM((SC_SCATTER_ROWS, PACKED), jnp.uint32),
            pltpu.SemaphoreType.DMA,
        ],
        name="dispatch",
    )
    def k(f_hbm, i_hbm, o_hbm, idx_v, rows_v, sem):
        base = _sc_worker() * per_w

        @pl.loop(0, n_chunks)
        def _(j):
            t0 = pl.multiple_of(base + j * SC_SCATTER_ROWS, 8)
            pltpu.sync_copy(f_hbm.at[pl.ds(t0, SC_SCATTER_ROWS)], rows_v)
            for kk in range(TOP_K):
                pltpu.sync_copy(i_hbm.at[pl.ds(kk * N_TOK + t0, SC_SCATTER_ROWS)], idx_v)
                pltpu.async_copy(rows_v, o_hbm.at[idx_v], sem).wait()

    return k(f, dest_t.reshape(TOP_K * N_TOK))


def _combine_gather(yb, idx):
    n = idx.shape[0]
    assert n % (SC_WORKERS * SC_GATHER_ROWS) == 0
    per_w = n // SC_WORKERS
    n_chunks = per_w // SC_GATHER_ROWS

    @functools.partial(
        pl.kernel,
        out_type=jax.ShapeDtypeStruct((n, PACKED), jnp.uint32),
        mesh=_sc_mesh(),
        scratch_types=[
            pltpu.VMEM((SC_GATHER_ROWS,), jnp.int32),
            pltpu.VMEM((SC_GATHER_ROWS, PACKED), jnp.uint32),
            pltpu.SemaphoreType.DMA,
        ],
        name="combine_gather",
    )
    def k(y_hbm, i_hbm, o_hbm, idx_v, rows_v, sem):
        base = _sc_worker() * per_w

        @pl.loop(0, n_chunks)
        def _(j):
            r0 = pl.multiple_of(base + j * SC_GATHER_ROWS, 8)
            pltpu.sync_copy(i_hbm.at[pl.ds(r0, SC_GATHER_ROWS)], idx_v)
            pltpu.async_copy(y_hbm.at[idx_v], rows_v, sem).wait()
            pltpu.sync_copy(rows_v, o_hbm.at[pl.ds(r0, SC_GATHER_ROWS)])

    return k(yb, idx)


def _weight_copies(w1_hbm, w2_hbm, w1f, w2f, sem, idx):
    return (pltpu.make_async_copy(w1_hbm.at[idx], w1f, sem.at[0]),
            pltpu.make_async_copy(w2_hbm.at[idx], w2f, sem.at[1]))


def _expert_kernel(be_ref, x_ref, w1_hbm, b1_ref, w2_hbm, b2_ref, y_ref, w1f, w2f, w1b, w2b, sem, *, layer):
    copies = functools.partial(_weight_copies, w1_hbm, w2_hbm, w1f, w2f, sem)
    for a in range(EXPERT_STEP):
        j = pl.program_id(0) * EXPERT_STEP + a
        e = be_ref[j]
        prev = be_ref[jnp.maximum(j - 1, 0)]
        live = e < N_EXPERTS
        first = live & ((j == 0) | (e != prev))
        rows = slice(a * ROW_BLOCK, (a + 1) * ROW_BLOCK)

        if a == 0:
            @pl.when(first & (j == 0))
            def _():
                for c in copies(layer * N_EXPERTS + e):
                    c.start()

        @pl.when(first)
        def _():
            for c in copies(layer * N_EXPERTS + e):
                c.wait()
            w1b[...] = w1f[...].astype(BF16)
            w2b[...] = w2f[...].astype(BF16)
            nxt = lax.while_loop(
                lambda k: (k < N_BLOCKS) & (be_ref[jnp.minimum(k, N_BLOCKS - 1)] == e), lambda k: k + 1, j + 1)
            e_next = be_ref[jnp.minimum(nxt, N_BLOCKS - 1)]

            @pl.when((nxt < N_BLOCKS) & (e_next < N_EXPERTS))
            def _():
                for c in copies(layer * N_EXPERTS + e_next):
                    c.start()

        @pl.when(live)
        def _():
            idx = layer * N_EXPERTS + e
            x = jnp.concatenate(_unpack_rows(x_ref[rows, :]), axis=1).astype(BF16)
            gu = jnp.dot(x, w1b[...], preferred_element_type=F32) + b1_ref[idx]
            gate = jnp.minimum(gu[:, :D_EXPERT], SWIGLU_LIMIT)
            lin = jnp.clip(gu[:, D_EXPERT:], -SWIGLU_LIMIT, SWIGLU_LIMIT)
            act = gate * _sigmoid(SWIGLU_ALPHA * gate) * (lin + 1.0)
            y_ref[rows, :] = _pack_rows(
                jnp.dot(act.astype(BF16), w2b[...], preferred_element_type=F32) + b2_ref[idx])


def _experts(block_e, rows, w1, b1, w2, b2, layer):
    step_rows = EXPERT_STEP * ROW_BLOCK
    return pl.pallas_call(
        functools.partial(_expert_kernel, layer=layer),
        grid_spec=pltpu.PrefetchScalarGridSpec(
            num_scalar_prefetch=1,
            grid=(N_BLOCKS // EXPERT_STEP,),
            in_specs=[
                pl.BlockSpec((step_rows, PACKED), lambda j, be: (j, 0)),
                pl.BlockSpec(memory_space=pl.ANY),
                pl.BlockSpec((DEPTH * N_EXPERTS, 1, 2 * D_EXPERT), lambda j, be: (0, 0, 0)),
                pl.BlockSpec(memory_space=pl.ANY),
                pl.BlockSpec((DEPTH * N_EXPERTS, 1, D_MODEL), lambda j, be: (0, 0, 0)),
            ],
            out_specs=pl.BlockSpec((step_rows, PACKED), lambda j, be: (j, 0)),
            scratch_shapes=[
                pltpu.VMEM((D_MODEL, 2 * D_EXPERT), F32),
                pltpu.VMEM((D_EXPERT, D_MODEL), F32),
                pltpu.VMEM((D_MODEL, 2 * D_EXPERT), BF16),
                pltpu.VMEM((D_EXPERT, D_MODEL), BF16),
                pltpu.SemaphoreType.DMA((2,)),
            ],
        ),
        out_shape=jax.ShapeDtypeStruct((N_ROWS, PACKED), jnp.uint32),
        compiler_params=pltpu.CompilerParams(
            dimension_semantics=("arbitrary",), vmem_limit_bytes=VMEM_LIMIT),
        name="experts",
    )(block_e, rows, w1, b1, w2, b2)


def _final_kernel(x_ref, mod_ref, g_ref, p_ref, fg_ref, *rest, batch):
    o_ref = rest[-1]
    gate = mod_ref[5, pl.ds(batch, 1), :]
    o_ref[0] = _rms(x_ref[0] + gate * _routed_sum(g_ref, p_ref), fg_ref[...])


def _final(xa, mod, gathered, probs, final_g, batch, out=None):
    ctx_tiles = CTX_LEN // TOK_TILE
    tok = lambda i: (batch, i + ctx_tiles, 0)
    in_specs = [
        pl.BlockSpec((1, TOK_TILE, D_MODEL), tok),
        pl.BlockSpec((6, 8, D_MODEL), lambda i: (0, 0, 0)),
        pl.BlockSpec((TOP_K, 1, TOK_TILE, PACKED), lambda i: (0, 0, i + ctx_tiles, 0)),
        pl.BlockSpec((1, TOK_TILE, TOP_K), tok),
        pl.BlockSpec((1, D_MODEL), lambda i: (0, 0)),
    ]
    args = [xa, mod, gathered.reshape(TOP_K, 1, S_TOT, PACKED), probs, final_g]
    aliases = {}
    if out is not None:
        in_specs.append(pl.BlockSpec(memory_space=pl.ANY))
        args.append(out)
        aliases = {len(args) - 1: 0}
    return pl.pallas_call(
        functools.partial(_final_kernel, batch=batch),
        grid=(N_TILES - ctx_tiles,),
        in_specs=in_specs,
        out_specs=pl.BlockSpec((1, TOK_TILE, D_MODEL), lambda i: (batch, i, 0)),
        out_shape=jax.ShapeDtypeStruct((BATCH, SEQ, D_MODEL), F32),
        input_output_aliases=aliases,
        compiler_params=pltpu.CompilerParams(dimension_semantics=("parallel",)),
        name="final",
    )(*args)


def _split_bf16(w):
    bits = lax.bitcast_convert_type(w.astype(F32), jnp.uint32) & jnp.uint32(0xFFFF0000)
    hi = lax.bitcast_convert_type(bits, F32)
    return jnp.concatenate([hi.astype(BF16), (w - hi).astype(BF16)], axis=-1)


def _rope_tables():
    t = np.arange(SEQ)
    axis_dim = HEAD_DIM // 2
    inv_freq = (1.0 / (ROPE_BASE ** (np.arange(0, axis_dim, 2, dtype=np.float32) / axis_dim))).astype(np.float32)
    ang_r = (t // GRID_W).astype(np.float32)[:, None] * inv_freq[None]
    ang_c = (t % GRID_W).astype(np.float32)[:, None] * inv_freq[None]
    ang = np.concatenate([ang_r, ang_r, ang_c, ang_c] * (LANES // HEAD_DIM), axis=-1)
    cos = np.concatenate([np.ones((CTX_LEN, LANES), np.float32), np.cos(ang)], axis=0)
    sin = np.concatenate([np.zeros((CTX_LEN, LANES), np.float32), np.sin(ang)], axis=0)
    return jnp.asarray(cos, F32), jnp.asarray(sin, F32)


def kernel(x, c, ctx, c_ctx, w_ada, b_ada, norm1_g, norm2_g, w_in, ssm_lam_re, ssm_lam_im, ssm_log_dt, ssm_b_re, ssm_b_im, ssm_c_re, ssm_c_im, ssm_d, ssm_w_val, ssm_w_gate, attn_sink, attn_w_up, pool_w, pool_scale, w_out, router_w, router_b, exp_w1, exp_b1, exp_w2, exp_b2, final_g):
    assert x.shape == (BATCH, SEQ, D_MODEL) and ctx.shape == (BATCH, CTX_LEN, D_MODEL)
    cos_t, sin_t = _rope_tables()
    cv = jnp.concatenate([c, c_ctx[None], jnp.zeros((8 - BATCH - 1, D_MODEL), F32)], axis=0)
    mods = _ada(cv, w_ada, b_ada)
    xs = (ctx, x)
    ew1 = exp_w1.reshape(DEPTH * N_EXPERTS, D_MODEL, 2 * D_EXPERT)
    eb1 = exp_b1.reshape(DEPTH * N_EXPERTS, 1, 2 * D_EXPERT)
    ew2 = exp_w2.reshape(DEPTH * N_EXPERTS, D_EXPERT, D_MODEL)
    eb2 = exp_b2.reshape(DEPTH * N_EXPERTS, 1, D_MODEL)
    half_gate = jnp.where(jnp.arange(IN_WIDTH) >= OFF_GATE, 0.5, 1.0).astype(F32)
    w_in_b = (w_in * half_gate).astype(BF16)
    ssm_tables = _ssm_params(ssm_lam_re, ssm_lam_im, ssm_log_dt, ssm_b_re, ssm_b_im, ssm_c_re, ssm_c_im)
    pending = None
    for l in range(DEPTH):
        mod = mods[l]
        u, kv, q, pool_in, *updated = _inproj(xs, mod, norm1_g[l][None], w_in_b, l, cos_t, sin_t, pending)
        if updated:
            xs = (updated[0], updated[0])
        yf, yb = _ssm(u, *ssm_tables, l)
        o = _attn(attn_sink[l], q, kv)
        xa, f, top_e, top_p = _merge(
            xs, mod, norm1_g[l][None], w_in_b, l, yf, yb, u, ssm_d[l][None],
            (0.5 * ssm_w_val[l]).astype(BF16), (0.5 * ssm_w_gate[l]).astype(BF16), o, attn_w_up[l].astype(BF16),
            pool_in, pool_w[l].astype(BF16), pool_scale[l][None], (0.5 * w_out[l]).astype(BF16), norm2_g[l][None],
            _split_bf16(router_w[l]), router_b[l][None])
        dest, block_e = _route(top_e.reshape(N_TOK, TOP_K))
        dest_t = dest.T
        rows = _dispatch(f.reshape(N_TOK, PACKED), dest_t)
        yrows = _experts(block_e[:N_BLOCKS, 0], rows, ew1, eb1, ew2, eb2, l)
        xs = (xa, xa)
        if l < DEPTH - 1:
            pending = (mod, _combine_gather(yrows, dest_t.reshape(-1)), top_p)
    out = None
    for b in range(BATCH):
        rows_b = _combine_gather(yrows, dest_t[:, b * S_TOT:(b + 1) * S_TOT].reshape(-1))
        out = _final(xa, mod, rows_b, top_p, final_g[None], b, out)
    return out
```

```python
import functools
import math

import jax
import jax.numpy as jnp
import numpy as np
from jax import lax
from jax.experimental import pallas as pl
from jax.experimental.pallas import tpu as pltpu
from jax.experimental.pallas import tpu_sc as plsc

F32 = jnp.float32
BF16 = jnp.bfloat16
HIGHEST = lax.Precision.HIGHEST

D_MODEL = 1024
BATCH = 2
SEQ = 8192
DEPTH = 2
GRID_W = 64
CTX_LEN = 256
NORM_EPS = 1e-6
SSM_WIDTH = 512
SSM_GROUP = 16
SSM_GROUPS = 32
SSM_STATE = 64
HEAD_DIM = 64
N_Q_HEADS = 8
N_KV_HEADS = 2
GQA_GROUP = 4
ATTN_WIDTH = 512
KV_WIDTH = 128
WINDOW = 128
ROPE_BASE = 10000.0
ATTN_SCALE = HEAD_DIM ** -0.5
POOL_WINDOWS = (2, 4, 8, 16)
POOL_WIDTH = 512
POOL_GROUP = 128
POOL_OUT_GROUP = 256
OFF_K = 512
OFF_V = 640
OFF_Q = 768
OFF_POOL = 1280
OFF_GATE = 1792
IN_WIDTH = OFF_GATE + 3 * D_MODEL
N_EXPERTS = 32
TOP_K = 4
D_EXPERT = 1024
SWIGLU_LIMIT = 7.0
SWIGLU_ALPHA = 1.702

S_TOT = CTX_LEN + SEQ
N_TOK = BATCH * S_TOT
G_TOK = S_TOT
TOK_TILE = 256
N_TILES = S_TOT // TOK_TILE
assert CTX_LEN == TOK_TILE
ATTN_BLOCK = 128
ATTN_STEP = 3
POOL_HALO = 8
ROUTE_TILE = 1056
ROW_BLOCK = 256
EXPERT_STEP = 4
N_ROWS = -(-(G_TOK * TOP_K + N_EXPERTS * (ROW_BLOCK - 1)) // (ROW_BLOCK * EXPERT_STEP)) * ROW_BLOCK * EXPERT_STEP
N_BLOCKS = N_ROWS // ROW_BLOCK
N_BLOCKS_PAD = -(-N_BLOCKS // 8) * 8
SC_CORES = 2
SC_WORKERS = SC_CORES * 16
PACKED = D_MODEL // 2
SC_SCATTER_ROWS = 88
SC_GATHER_ROWS = 88
assert G_TOK % (SC_WORKERS * SC_SCATTER_ROWS) == 0 and G_TOK % ROUTE_TILE == 0
VMEM_LIMIT = 56 * 1024 * 1024
LANES = 128
MASK_VALUE = -1e30


def _rms(x, g):
    return x * lax.rsqrt(jnp.mean(x * x, axis=-1, keepdims=True) + NORM_EPS) * g


def _sigmoid(x):
    return 0.5 * jnp.tanh(0.5 * x) + 0.5


def _pack_rows(x):
    half = x.shape[1] // 2
    lo = lax.bitcast_convert_type(x[:, :half].astype(BF16).astype(F32), jnp.uint32)
    hi = lax.bitcast_convert_type(x[:, half:].astype(BF16).astype(F32), jnp.uint32)
    return (lo >> 16) | hi


def _unpack_rows(p):
    lo = lax.bitcast_convert_type(p << 16, F32)
    hi = lax.bitcast_convert_type(p & jnp.uint32(0xFFFF0000), F32)
    return lo, hi


def _token_specs(xs, last=N_TILES - 1, im=lambda f: f):
    ctx_tiles = CTX_LEN // TOK_TILE
    off = 0 if xs[1].shape[1] == S_TOT else ctx_tiles
    return [pl.BlockSpec((1, TOK_TILE, D_MODEL), im(lambda b, i: (b, 0, 0))),
            pl.BlockSpec((1, TOK_TILE, D_MODEL), im(lambda b, i: (b, jnp.clip(i, ctx_tiles, last) - off, 0)))]


def _token_tile(i, xc_ref, xl_ref):
    return jnp.where(i == 0, xc_ref[0], xl_ref[0])


def _mod_row(i, b):
    return jnp.where(i == 0, 2, b)


def _ada_kernel(cv_ref, w_ref, b_ref, o_ref):
    cv = cv_ref[...]
    s = cv * jax.nn.sigmoid(cv)
    o_ref[0, 0] = jnp.dot(s, w_ref[0], preferred_element_type=F32, precision=HIGHEST) + b_ref[0, 0]


def _ada(cv, w_ada, b_ada):
    return pl.pallas_call(
        _ada_kernel,
        grid=(DEPTH, 6),
        in_specs=[
            pl.BlockSpec((8, D_MODEL), lambda l, j: (0, 0)),
            pl.BlockSpec((1, D_MODEL, D_MODEL), lambda l, j: (l, 0, j)),
            pl.BlockSpec((1, 1, 1, D_MODEL), lambda l, j: (l, j, 0, 0)),
        ],
        out_specs=pl.BlockSpec((1, 1, 8, D_MODEL), lambda l, j: (l, j, 0, 0)),
        out_shape=jax.ShapeDtypeStruct((DEPTH, 6, 8, D_MODEL), F32),
        name="ada",
    )(cv, w_ada, b_ada.reshape(DEPTH, 6, 1, D_MODEL))


def _routed_sum(g_ref, p_ref):
    p = p_ref[0]
    lo, hi = _unpack_rows(g_ref[0, 0])
    y_lo = p[:, 0:1] * lo
    y_hi = p[:, 0:1] * hi
    for k in range(1, TOP_K):
        lo, hi = _unpack_rows(g_ref[k, 0])
        y_lo = y_lo + p[:, k:k + 1] * lo
        y_hi = y_hi + p[:, k:k + 1] * hi
    return jnp.concatenate([y_lo, y_hi], axis=1)


def _rope(t, cos, sin):
    lane = lax.broadcasted_iota(jnp.int32, t.shape, 1)
    rot = jnp.where((lane % 32) < 16, -pltpu.roll(t, LANES - 16, 1), pltpu.roll(t, 16, 1))
    return t * cos + rot * sin


def _inproj_kernel(*refs, pending, batch):
    xc_ref, xl_ref, mod_ref, g_ref, w_ref, cos_ref, sin_ref = refs[:7]
    if pending:
        pmod_ref, rows_ref, probs_ref = refs[7:10]
        u_ref, kv_ref, q_ref, pool_ref, xo_ref = refs[-5:]
    else:
        u_ref, kv_ref, q_ref, pool_ref = refs[-4:]
    b, i = (pl.program_id(0), pl.program_id(1)) if batch is None else (batch, pl.program_id(0))
    row = _mod_row(i, b)
    shift = mod_ref[0, pl.ds(row, 1), :]
    scale = mod_ref[1, pl.ds(row, 1), :]
    x = _token_tile(i, xc_ref, xl_ref)
    if pending:
        x = x + pmod_ref[5, pl.ds(row, 1), :] * _routed_sum(rows_ref, probs_ref)
        xo_ref[0] = x
    h = _rms(x, g_ref[...]) * (1.0 + scale) + shift
    p = jnp.dot(h.astype(BF16), w_ref[0, :, :OFF_GATE], preferred_element_type=F32)
    cos = cos_ref[...]
    sin = sin_ref[...]
    u_ref[0] = p[:, :OFF_K]
    kv_ref[0, :, :KV_WIDTH] = _rope(p[:, OFF_K:OFF_V], cos, sin).astype(BF16)
    kv_ref[0, :, KV_WIDTH:] = p[:, OFF_V:OFF_Q].astype(BF16)
    for c in range(ATTN_WIDTH // LANES):
        qc = p[:, OFF_Q + c * LANES:OFF_Q + (c + 1) * LANES]
        q_ref[0, :, c * LANES:(c + 1) * LANES] = (_rope(qc, cos, sin) * ATTN_SCALE).astype(BF16)
    pool_ref[0] = p[:, OFF_POOL:OFF_GATE]


def _inproj(xs, mod, g1, w_in, layer, cos_t, sin_t, pending=None, batch=None, outs=None):
    im = (lambda f: f) if batch is None else (lambda f: lambda i: f(batch, i))
    tok = im(lambda b, i: (b, i, 0))
    in_specs = [
        *_token_specs(xs, im=im),
        pl.BlockSpec((6, 8, D_MODEL), im(lambda b, i: (0, 0, 0))),
        pl.BlockSpec((1, D_MODEL), im(lambda b, i: (0, 0))),
        pl.BlockSpec((1, D_MODEL, IN_WIDTH), im(lambda b, i: (layer, 0, 0))),
        pl.BlockSpec((TOK_TILE, LANES), im(lambda b, i: (i, 0))),
        pl.BlockSpec((TOK_TILE, LANES), im(lambda b, i: (i, 0))),
    ]
    out_specs = [
        pl.BlockSpec((1, TOK_TILE, SSM_WIDTH), tok),
        pl.BlockSpec((1, TOK_TILE, 2 * KV_WIDTH), tok),
        pl.BlockSpec((1, TOK_TILE, ATTN_WIDTH), tok),
        pl.BlockSpec((1, TOK_TILE, POOL_WIDTH), tok),
    ]
    out_shape = [
        jax.ShapeDtypeStruct((BATCH, S_TOT, SSM_WIDTH), F32),
        jax.ShapeDtypeStruct((BATCH, S_TOT, 2 * KV_WIDTH), BF16),
        jax.ShapeDtypeStruct((BATCH, S_TOT, ATTN_WIDTH), BF16),
        jax.ShapeDtypeStruct((BATCH, S_TOT, POOL_WIDTH), F32),
    ]
    args = [*xs, mod, g1, w_in, cos_t, sin_t]
    if pending is not None:
        pmod, gathered, probs = pending
        in_specs += [
            pl.BlockSpec((6, 8, D_MODEL), im(lambda b, i: (0, 0, 0))),
            pl.BlockSpec((TOP_K, 1, TOK_TILE, PACKED), im(lambda b, i: (0, 0, i, 0))),
            pl.BlockSpec((1, TOK_TILE, TOP_K), tok),
        ]
        out_specs.append(pl.BlockSpec((1, TOK_TILE, D_MODEL), tok))
        out_shape.append(jax.ShapeDtypeStruct((BATCH, S_TOT, D_MODEL), F32))
        args += [pmod, gathered.reshape(TOP_K, 1, S_TOT, PACKED), probs]
    aliases = {}
    if outs is not None:
        aliases = {len(args) + k: k for k in range(len(outs))}
        in_specs += [pl.BlockSpec(memory_space=pl.ANY)] * len(outs)
        args += list(outs)
    return pl.pallas_call(
        functools.partial(_inproj_kernel, pending=pending is not None, batch=batch),
        grid=(BATCH, N_TILES) if batch is None else (N_TILES,),
        in_specs=in_specs,
        out_specs=out_specs,
        out_shape=out_shape,
        input_output_aliases=aliases,
        compiler_params=pltpu.CompilerParams(
            dimension_semantics=("parallel",) * (2 if batch is None else 1), vmem_limit_bytes=VMEM_LIMIT),
        name="inproj",
    )(*args)


SSM_HALF_IN = SSM_WIDTH // 2
SSM_HALF_ST = SSM_GROUPS // 2 * SSM_STATE
N_STATE_ROWS = 8
N_LANE_TILES = SSM_HALF_ST // LANES
SCAN_STRIDE = TOK_TILE + 8


def _ssm_kernel(uf_ref, ub_ref, wb_ref, wc_ref, ar_ref, ai_ref, yf_ref, yb_ref, buf_f, buf_b, hf_ref, hb_ref):
    i = pl.program_id(0)
    T = TOK_TILE

    @pl.when(i == 0)
    def _():
        hf_ref[...] = jnp.zeros_like(hf_ref)
        hb_ref[...] = jnp.zeros_like(hb_ref)

    for d, (u_ref, buf) in enumerate(((uf_ref, buf_f), (ub_ref, buf_b))):
        for b in range(BATCH):
            for half in range(2):
                ub16 = u_ref[b, :, half * SSM_HALF_IN:(half + 1) * SSM_HALF_IN].astype(BF16)
                for reim in range(2):
                    k = reim * 4 + half * 2 + b
                    bu = jnp.dot(ub16, wb_ref[d, reim, half], preferred_element_type=F32)
                    for l in range(N_LANE_TILES):
                        buf[l, k * SCAN_STRIDE:k * SCAN_STRIDE + T, :] = bu[:, l * LANES:(l + 1) * LANES]

    def step(t, carry):
        hf, hb = carry
        tb = T - 1 - t
        nf, nb = [], []
        for l in range(N_LANE_TILES):
            sl = pl.ds(l * LANES, LANES)
            rows_f = pl.ds(t, N_STATE_ROWS, stride=SCAN_STRIDE)
            h = ar_ref[0, :, sl] * hf[l] + ai_ref[0, :, sl] * pltpu.roll(hf[l], 4, 0) + buf_f[l, rows_f, :]
            buf_f[l, rows_f, :] = h
            nf.append(h)
            rows_b = pl.ds(tb, N_STATE_ROWS, stride=SCAN_STRIDE)
            h = ar_ref[1, :, sl] * hb[l] + ai_ref[1, :, sl] * pltpu.roll(hb[l], 4, 0) + buf_b[l, rows_b, :]
            buf_b[l, rows_b, :] = h
            nb.append(h)
        return tuple(nf), tuple(nb)

    hf0 = tuple(hf_ref[:, l * LANES:(l + 1) * LANES] for l in range(N_LANE_TILES))
    hb0 = tuple(hb_ref[:, l * LANES:(l + 1) * LANES] for l in range(N_LANE_TILES))
    hf1, hb1 = lax.fori_loop(0, T, step, (hf0, hb0), unroll=2)
    for l in range(N_LANE_TILES):
        hf_ref[:, l * LANES:(l + 1) * LANES] = hf1[l]
        hb_ref[:, l * LANES:(l + 1) * LANES] = hb1[l]

    for d, (buf, y_ref) in enumerate(((buf_f, yf_ref), (buf_b, yb_ref))):
        for b in range(BATCH):
            for half in range(2):
                kre = half * 2 + b
                kim = 4 + half * 2 + b
                state = lambda k: jnp.concatenate(
                    [buf[l, k * SCAN_STRIDE:k * SCAN_STRIDE + T, :] for l in range(N_LANE_TILES)],
                    axis=1).astype(BF16)
                y = jnp.dot(state(kre), wc_ref[d, 0, half], preferred_element_type=F32)
                y = y + jnp.dot(state(kim), wc_ref[d, 1, half], preferred_element_type=F32)
                y_ref[b, :, half * SSM_HALF_IN:(half + 1) * SSM_HALF_IN] = y


def _ssm(u, wb, wc, ar, ai, layer):
    fwd = lambda i: (0, i, 0)
    bwd = lambda i: (0, jnp.where(i == 0, 0, N_TILES - i), 0)
    return pl.pallas_call(
        _ssm_kernel,
        grid=(N_TILES,),
        in_specs=[
            pl.BlockSpec((BATCH, TOK_TILE, SSM_WIDTH), fwd),
            pl.BlockSpec((BATCH, TOK_TILE, SSM_WIDTH), bwd),
            pl.BlockSpec((None, 2, 2, 2, SSM_HALF_IN, SSM_HALF_ST), lambda i: (layer, 0, 0, 0, 0, 0)),
            pl.BlockSpec((None, 2, 2, 2, SSM_HALF_ST, SSM_HALF_IN), lambda i: (layer, 0, 0, 0, 0, 0)),
            pl.BlockSpec((None, 2, N_STATE_ROWS, SSM_HALF_ST), lambda i: (layer, 0, 0, 0)),
            pl.BlockSpec((None, 2, N_STATE_ROWS, SSM_HALF_ST), lambda i: (layer, 0, 0, 0)),
        ],
        out_specs=[
            pl.BlockSpec((BATCH, TOK_TILE, SSM_WIDTH), fwd),
            pl.BlockSpec((BATCH, TOK_TILE, SSM_WIDTH), bwd),
        ],
        out_shape=[jax.ShapeDtypeStruct((BATCH, S_TOT, SSM_WIDTH), F32)] * 2,
        scratch_shapes=[
            pltpu.VMEM((N_LANE_TILES, N_STATE_ROWS * SCAN_STRIDE, LANES), F32),
            pltpu.VMEM((N_LANE_TILES, N_STATE_ROWS * SCAN_STRIDE, LANES), F32),
            pltpu.VMEM((N_STATE_ROWS, SSM_HALF_ST), F32),
            pltpu.VMEM((N_STATE_ROWS, SSM_HALF_ST), F32),
        ],
        compiler_params=pltpu.CompilerParams(
            dimension_semantics=("arbitrary",), vmem_limit_bytes=VMEM_LIMIT),
        name="ssm",
    )(u, u, wb, wc, ar, ai)


def _ssm_params(lam_re, lam_im, log_dt, b_re, b_im, c_re, c_im):
    lr = lam_re.astype(F32)
    li = lam_im.astype(F32)
    dt = jnp.exp(log_dt.astype(F32))[..., None]
    mag = jnp.exp(lr * dt)
    ang = li * dt
    a_re = mag * jnp.cos(ang)
    a_im = mag * jnp.sin(ang)
    den = lr * lr + li * li
    k_re = ((a_re - 1.0) * lr + a_im * li) / den
    k_im = (a_im * lr - (a_re - 1.0) * li) / den
    bb_re = k_re[..., None] * b_re - k_im[..., None] * b_im
    bb_im = k_re[..., None] * b_im + k_im[..., None] * b_re
    n_g = SSM_GROUPS // 2
    lead = lam_re.shape[:2]
    drive_mask = np.equal.outer(np.arange(SSM_HALF_IN) // SSM_GROUP, np.arange(SSM_HALF_ST) // SSM_STATE)
    repeat_st = np.tile(np.eye(SSM_STATE, dtype=np.float32), (1, n_g))
    repeat_in = np.tile(np.eye(SSM_GROUP, dtype=np.float32), (1, n_g))

    def block_diag(parts, rows_per_half, repeat, mask):
        rows = jnp.stack([jnp.swapaxes(p, -1, -2) for p in parts], axis=2)
        rows = rows.reshape(*lead, 2, 2, rows_per_half, repeat.shape[0])
        full = jnp.einsum('...rp,pc->...rc', rows, repeat, precision=HIGHEST)
        return (full * mask.astype(np.float32)).astype(BF16)

    wb = block_diag((bb_re, bb_im), SSM_HALF_IN, repeat_st, drive_mask)
    wc = block_diag((c_re.astype(F32), -c_im.astype(F32)), SSM_HALF_ST, repeat_in, drive_mask.T)

    def rows(a, sign_re):
        halves = a.reshape(*lead, 1, 2, 1, SSM_HALF_ST)
        signs = jnp.asarray([sign_re, 1.0], F32).reshape(2, 1, 1, 1)
        return jnp.broadcast_to(halves * signs, (*lead, 2, 2, BATCH, SSM_HALF_ST)).reshape(
            *lead, N_STATE_ROWS, SSM_HALF_ST)

    return wb, wc, rows(a_re, 1.0), rows(a_im, -1.0)


def _attn_block(n, q, kv_band, kv_ctx, sink_ref, band_bias):
    kv_all = jnp.concatenate([kv_band, kv_ctx], axis=0)
    nb = 3 * ATTN_BLOCK
    last = SEQ // ATTN_BLOCK - 1
    krow = lax.broadcasted_iota(jnp.int32, (nb, 1), 0)
    lo_r = jnp.where(n < 0, nb, jnp.where(n == 0, ATTN_BLOCK, 0))
    hi_r = jnp.where(n == last, 2 * ATTN_BLOCK, nb)
    bias = band_bias + jnp.where((krow >= lo_r) & (krow < hi_r), 0.0, MASK_VALUE)
    bias = jnp.concatenate([bias] * GQA_GROUP, axis=1)
    nt = (((1,), (1,)), ((), ()))
    outs = []
    for hk in range(N_KV_HEADS):
        k = kv_all[:, hk * HEAD_DIM:(hk + 1) * HEAD_DIM]
        v_t = kv_all[:, KV_WIDTH + hk * HEAD_DIM:KV_WIDTH + (hk + 1) * HEAD_DIM].astype(F32).T.astype(BF16)
        heads = range(hk * GQA_GROUP, (hk + 1) * GQA_GROUP)
        qs = jnp.concatenate([q[:, hq * HEAD_DIM:(hq + 1) * HEAD_DIM] for hq in heads], axis=0)
        sink = jnp.concatenate([jnp.full((1, ATTN_BLOCK), sink_ref[hq], F32) for hq in heads], axis=1)
        s_t = lax.dot_general(k, qs, nt, preferred_element_type=F32)
        s_w = s_t[:nb] + bias
        s_c = s_t[nb:]
        m = jnp.maximum(jnp.maximum(jnp.max(s_w, axis=0, keepdims=True),
                                    jnp.max(s_c, axis=0, keepdims=True)), sink)
        e_w = jnp.exp(s_w - m)
        e_c = jnp.exp(s_c - m)
        den = jnp.sum(e_w, axis=0, keepdims=True) + jnp.sum(e_c, axis=0, keepdims=True) + jnp.exp(sink - m)
        e = jnp.concatenate([e_w, e_c], axis=0).astype(BF16)
        o_t = jnp.dot(v_t, e, preferred_element_type=F32) / den
        outs.extend(o_t[:, g * ATTN_BLOCK:(g + 1) * ATTN_BLOCK].T for g in range(GQA_GROUP))
    return jnp.concatenate(outs, axis=1).astype(BF16)


def _attn_kernel(sink_ref, band_ref, q_ref, kc_ref, *rest):
    kv_refs, o_ref = rest[:-1], rest[-1]
    j0 = pl.program_id(1) * ATTN_STEP
    for a in range(ATTN_STEP):
        n = j0 + a - CTX_LEN // ATTN_BLOCK
        band = jnp.concatenate([kv_refs[a + d][0] for d in range(3)], axis=0)
        rows = slice(a * ATTN_BLOCK, (a + 1) * ATTN_BLOCK)
        o_ref[0, rows, :] = _attn_block(n, q_ref[0, rows, :], band, kc_ref[0], sink_ref, band_ref[...])


def _attn(sink, q, kv):
    nq = S_TOT // ATTN_BLOCK
    first = CTX_LEN // ATTN_BLOCK
    r = jnp.arange(ATTN_BLOCK)[None, :]
    c = jnp.arange(3 * ATTN_BLOCK)[:, None] - ATTN_BLOCK
    band_bias = jnp.where(jnp.abs(c - r) <= WINDOW, 0.0, MASK_VALUE).astype(F32)

    def kv_spec(d):
        return pl.BlockSpec((1, ATTN_BLOCK, 2 * KV_WIDTH),
                            lambda b, j: (b, jnp.clip(j * ATTN_STEP + d - 1, first, nq - 1), 0))

    return pl.pallas_call(
        _attn_kernel,
        grid=(BATCH, nq // ATTN_STEP),
        in_specs=[
            pl.BlockSpec(memory_space=pltpu.SMEM),
            pl.BlockSpec((3 * ATTN_BLOCK, ATTN_BLOCK), lambda b, j: (0, 0)),
            pl.BlockSpec((1, ATTN_STEP * ATTN_BLOCK, ATTN_WIDTH), lambda b, j: (b, j, 0)),
            pl.BlockSpec((1, CTX_LEN, 2 * KV_WIDTH), lambda b, j: (b, 0, 0)),
            *[kv_spec(d) for d in range(ATTN_STEP + 2)],
        ],
        out_specs=pl.BlockSpec((1, ATTN_STEP * ATTN_BLOCK, ATTN_WIDTH), lambda b, j: (b, j, 0)),
        out_shape=jax.ShapeDtypeStruct((BATCH, S_TOT, ATTN_WIDTH), BF16),
        compiler_params=pltpu.CompilerParams(dimension_semantics=("parallel", "parallel")),
        name="attn",
    )(sink, band_bias, q, kv, *[kv] * (ATTN_STEP + 2))


def _gelu_tanh(x):
    return 0.5 * x * (1.0 + jnp.tanh(math.sqrt(2.0 / math.pi) * (x + 0.044715 * (x * x * x))))


def _pool_means(ext, i):
    T = TOK_TILE
    n_ext = T + 2 * POOL_HALO
    seq_len = jnp.where(i == 0, CTX_LEN, SEQ)
    start = jnp.where(i == 0, 0, (i - 1) * T)
    r = start - POOL_HALO + lax.broadcasted_iota(jnp.int32, (n_ext, 1), 0)
    ext = jnp.where((r >= 0) & (r < seq_len), ext, 0.0)
    tl = start + lax.broadcasted_iota(jnp.int32, (T, 1), 0)
    back = lambda a, s: pltpu.roll(a, s, 0)
    fwd = lambda a, s: pltpu.roll(a, n_ext - s, 0)
    diffs = []
    for gi, w in enumerate(POOL_WINDOWS):
        e = ext[:, gi * POOL_GROUP:(gi + 1) * POOL_GROUP]
        lo = w // 2
        hi = w - 1 - lo
        s = e + back(e, 1)
        span = 2
        while span < w:
            s = s + back(s, span)
            span *= 2
        if w > 2:
            s = fwd(s, hi)
        cnt = (jnp.clip(tl + hi + 1, 0, seq_len) - jnp.clip(tl - lo, 0, seq_len)).astype(F32)
        mean = s[POOL_HALO:POOL_HALO + T] / cnt
        diffs.append((mean - e[POOL_HALO:POOL_HALO + T]).astype(BF16))
    return diffs


def _merge_kernel(xc_ref, xl_ref, mod_ref, g1_ref, w_ref, yf_ref, yb_ref, u_ref, dsk_ref, wval_ref, wgate_ref,
                  o_ref, wup_ref, pc_ref, pp_ref, pn_ref, pw_ref, ps_ref, wout_ref, g2_ref, rw_ref, rb_ref,
                  xo_ref, f_ref, te_ref, tp_ref, xprev_ref):
    b = pl.program_id(0)
    s = pl.program_id(1)

    @pl.when(s == 0)
    def _():
        xprev_ref[...] = jnp.zeros_like(xprev_ref)

    rowp = _mod_row(jnp.maximum(s - 1, 0), b)
    f = _rms(xprev_ref[...], g2_ref[...]) * (1.0 + mod_ref[4, pl.ds(rowp, 1), :]) + mod_ref[3, pl.ds(rowp, 1), :]
    f_ref[0] = _pack_rows(f)
    f_hi = f.astype(BF16)
    f_lo = (f - f_hi.astype(F32)).astype(BF16)
    r_hi = jnp.dot(f_hi, rw_ref[...], preferred_element_type=F32)
    r_lo = jnp.dot(f_lo, rw_ref[...], preferred_element_type=F32)
    logits = r_hi[:, :N_EXPERTS] + r_hi[:, N_EXPERTS:] + r_lo[:, :N_EXPERTS] + rb_ref[...]
    lane = lax.broadcasted_iota(jnp.int32, logits.shape, 1)
    vals, idxs = [], []
    for _ in range(TOP_K):
        mx = jnp.max(logits, axis=-1, keepdims=True)
        ix = jnp.min(jnp.where(logits == mx, lane, N_EXPERTS), axis=-1, keepdims=True)
        vals.append(mx)
        idxs.append(ix)
        logits = jnp.where(lane == ix, -jnp.inf, logits)
    ex = [jnp.exp(v - vals[0]) for v in vals]
    tot = ex[0] + ex[1] + ex[2] + ex[3]
    te_ref[0] = jnp.concatenate(idxs, axis=1)
    tp_ref[0] = jnp.concatenate(ex, axis=1) / tot

    i = jnp.minimum(s, N_TILES - 1)
    row = _mod_row(i, b)
    shift, scale, gate = (mod_ref[j, pl.ds(row, 1), :] for j in range(3))
    ext = jnp.concatenate([pp_ref[0], pc_ref[0], pn_ref[0]], axis=0)
    diffs = _pool_means(ext, i)
    x = _token_tile(i, xc_ref, xl_ref)
    h = (_rms(x, g1_ref[...]) * (1.0 + scale) + shift).astype(BF16)
    t = jnp.tanh(jnp.dot(h, w_ref[0, :, OFF_GATE:], preferred_element_type=F32))

    z = yf_ref[0] + yb_ref[0] + dsk_ref[...] * u_ref[0]
    z = _gelu_tanh(z).astype(BF16)
    val = jnp.dot(z, wval_ref[...], preferred_element_type=F32)
    y_ssm = val * jnp.tanh(jnp.dot(z, wgate_ref[...], preferred_element_type=F32)) + val
    y_attn = jnp.dot(o_ref[0], wup_ref[...], preferred_element_type=F32)
    y_pool = jnp.concatenate(
        [jnp.dot(diffs[gi], pw_ref[gi], preferred_element_type=F32) for gi in range(len(POOL_WINDOWS))],
        axis=1) * ps_ref[...]

    m = (t[:, :D_MODEL] * y_ssm + t[:, D_MODEL:2 * D_MODEL] * y_attn + t[:, 2 * D_MODEL:] * y_pool
         + (y_ssm + y_attn + y_pool))
    xo = x + gate * jnp.dot(m.astype(BF16), wout_ref[...], preferred_element_type=F32)
    xo_ref[0] = xo
    xprev_ref[...] = xo


def _merge(xs, mod, g1, w_in, layer, yf, yb, u, dsk, wval, wgate, o, wup, pool_in, pw, ps, wout, g2, rw, rb):
    last = N_TILES - 1
    tok = lambda b, s: (b, jnp.minimum(s, last), 0)
    prev = lambda b, s: (b, jnp.maximum(s - 1, 0), 0)
    const2 = lambda b, s: (0, 0)
    const3 = lambda b, s: (0, 0, 0)
    per_tile = TOK_TILE // POOL_HALO
    n_halo = S_TOT // POOL_HALO
    return pl.pallas_call(
        _merge_kernel,
        grid=(BATCH, N_TILES + 1),
        in_specs=[
            *_token_specs(xs, last),
            pl.BlockSpec((6, 8, D_MODEL), const3),
            pl.BlockSpec((1, D_MODEL), const2),
            pl.BlockSpec((1, D_MODEL, IN_WIDTH), lambda b, s: (layer, 0, 0)),
            pl.BlockSpec((1, TOK_TILE, SSM_WIDTH), tok),
            pl.BlockSpec((1, TOK_TILE, SSM_WIDTH), tok),
            pl.BlockSpec((1, TOK_TILE, SSM_WIDTH), tok),
            pl.BlockSpec((1, SSM_WIDTH), const2),
            pl.BlockSpec((SSM_WIDTH, D_MODEL), const2),
            pl.BlockSpec((SSM_WIDTH, D_MODEL), const2),
            pl.BlockSpec((1, TOK_TILE, ATTN_WIDTH), tok),
            pl.BlockSpec((ATTN_WIDTH, D_MODEL), const2),
            pl.BlockSpec((1, TOK_TILE, POOL_WIDTH), tok),
            pl.BlockSpec((1, POOL_HALO, POOL_WIDTH),
                         lambda b, s: (b, jnp.maximum(jnp.minimum(s, last) * per_tile - 1, 0), 0)),
            pl.BlockSpec((1, POOL_HALO, POOL_WIDTH),
                         lambda b, s: (b, jnp.minimum((jnp.minimum(s, last) + 1) * per_tile, n_halo - 1), 0)),
            pl.BlockSpec((len(POOL_WINDOWS), POOL_GROUP, POOL_OUT_GROUP), const3),
            pl.BlockSpec((1, D_MODEL), const2),
            pl.BlockSpec((D_MODEL, D_MODEL), const2),
            pl.BlockSpec((1, D_MODEL), const2),
            pl.BlockSpec((D_MODEL, 2 * N_EXPERTS), const2),
            pl.BlockSpec((1, N_EXPERTS), const2),
        ],
        out_specs=[
            pl.BlockSpec((1, TOK_TILE, D_MODEL), tok),
            pl.BlockSpec((1, TOK_TILE, PACKED), prev),
            pl.BlockSpec((1, TOK_TILE, TOP_K), prev),
            pl.BlockSpec((1, TOK_TILE, TOP_K), prev),
        ],
        out_shape=[
            jax.ShapeDtypeStruct((BATCH, S_TOT, D_MODEL), F32),
            jax.ShapeDtypeStruct((BATCH, S_TOT, PACKED), jnp.uint32),
            jax.ShapeDtypeStruct((BATCH, S_TOT, TOP_K), jnp.int32),
            jax.ShapeDtypeStruct((BATCH, S_TOT, TOP_K), F32),
        ],
        scratch_shapes=[pltpu.VMEM((TOK_TILE, D_MODEL), F32)],
        compiler_params=pltpu.CompilerParams(
            dimension_semantics=("parallel", "arbitrary"), vmem_limit_bytes=VMEM_LIMIT),
        name="merge",
    )(*xs, mod, g1, w_in, yf, yb, u, dsk, wval, wgate, o, wup, pool_in, pool_in, pool_in, pw, ps, wout, g2, rw, rb)


def _route_kernel(te_ref, below_ref, dest_ref, be_ref, cnt_ref, carry_ref):
    ph = pl.program_id(0)
    i = pl.program_id(1)
    te = te_ref[...]
    lane = lax.broadcasted_iota(jnp.int32, (ROUTE_TILE, LANES), 1)
    ohs = [(te[:, k:k + 1] == lane).astype(F32) for k in range(TOP_K)]
    oh_all = ohs[0] + ohs[1] + ohs[2] + ohs[3]
    tile_cnt = jnp.sum(oh_all, axis=0, keepdims=True)

    @pl.when((ph == 0) & (i == 0))
    def _():
        cnt_ref[...] = jnp.zeros_like(cnt_ref)

    @pl.when(ph == 0)
    def _():
        cnt_ref[...] += tile_cnt

    @pl.when((ph == 1) & (i == 0))
    def _():
        cnt = jnp.broadcast_to(cnt_ref[...], (8, LANES))
        padded = jnp.floor((cnt + (ROW_BLOCK - 1)) / ROW_BLOCK) * ROW_BLOCK
        l8 = lax.broadcasted_iota(jnp.int32, (8, LANES), 1)
        end = padded
        s = 1
        while s < N_EXPERTS:
            end = end + jnp.where(l8 >= s, pltpu.roll(end, s, 1), 0.0)
            s *= 2
        carry_ref[...] = (end - padded)[0:1]
        blk_row = (lax.broadcasted_iota(jnp.int32, (N_BLOCKS_PAD, LANES), 0) * ROW_BLOCK).astype(F32)
        lb = lax.broadcasted_iota(jnp.int32, (N_BLOCKS_PAD, LANES), 1)
        hit = jnp.where((lb < N_EXPERTS) & (jnp.broadcast_to(end[0:1], blk_row.shape) <= blk_row), 1.0, 0.0)
        be = jnp.sum(hit, axis=-1, keepdims=True)
        be_ref[...] = jnp.broadcast_to(be, (N_BLOCKS_PAD, LANES)).astype(jnp.int32)

    @pl.when(ph == 1)
    def _():
        rank = carry_ref[...] + jnp.dot(below_ref[...], oh_all.astype(BF16), preferred_element_type=F32)
        dests = [jnp.sum(ohs[k] * rank, axis=-1, keepdims=True) for k in range(TOP_K)]
        dest_ref[...] = jnp.concatenate(dests, axis=1).astype(jnp.int32)
        carry_ref[...] += tile_cnt


def _route(top_e, group):
    n_steps = G_TOK // ROUTE_TILE
    return pl.pallas_call(
        _route_kernel,
        grid=(2, n_steps),
        in_specs=[pl.BlockSpec((ROUTE_TILE, TOP_K), lambda ph, i: (group * n_steps + i, 0)),
                  pl.BlockSpec((ROUTE_TILE, ROUTE_TILE), lambda ph, i: (0, 0))],
        out_specs=[
            pl.BlockSpec((ROUTE_TILE, TOP_K), lambda ph, i: (ph * i, 0)),
            pl.BlockSpec((N_BLOCKS_PAD, LANES), lambda ph, i: (0, 0)),
        ],
        out_shape=[
            jax.ShapeDtypeStruct((G_TOK, TOP_K), jnp.int32),
            jax.ShapeDtypeStruct((N_BLOCKS_PAD, LANES), jnp.int32),
        ],
        scratch_shapes=[pltpu.VMEM((1, LANES), F32), pltpu.VMEM((1, LANES), F32)],
        compiler_params=pltpu.CompilerParams(dimension_semantics=("arbitrary", "arbitrary")),
        name="route",
    )(top_e, jnp.asarray(np.tril(np.ones((ROUTE_TILE, ROUTE_TILE), np.float32), -1), BF16))


def _sc_mesh():
    return plsc.VectorSubcoreMesh(core_axis_name="c", subcore_axis_name="s")


def _sc_worker():
    return lax.axis_index("s") * SC_CORES + lax.axis_index("c")


def _dispatch(f, dest_t, group):
    per_w = G_TOK // SC_WORKERS
    n_chunks = per_w // SC_SCATTER_ROWS

    @functools.partial(
        pl.kernel,
        out_type=jax.ShapeDtypeStruct((N_ROWS, PACKED), jnp.uint32),
        mesh=_sc_mesh(),
        scratch_types=[
            pltpu.VMEM((SC_SCATTER_ROWS,), jnp.int32),
            pltpu.VMEM((SC_SCATTER_ROWS, PACKED), jnp.uint32),
            pltpu.SemaphoreType.DMA,
        ],
        name="dispatch",
    )
    def k(f_hbm, i_hbm, o_hbm, idx_v, rows_v, sem):
        base = _sc_worker() * per_w

        @pl.loop(0, n_chunks)
        def _(j):
            t0 = pl.multiple_of(base + j * SC_SCATTER_ROWS, 8)
            pltpu.sync_copy(f_hbm.at[pl.ds(group * G_TOK + t0, SC_SCATTER_ROWS)], rows_v)
            for kk in range(TOP_K):
                pltpu.sync_copy(i_hbm.at[pl.ds(kk * G_TOK + t0, SC_SCATTER_ROWS)], idx_v)
                pltpu.async_copy(rows_v, o_hbm.at[idx_v], sem).wait()

    return k(f, dest_t.reshape(TOP_K * G_TOK))


def _combine_gather(yb, idx):
    n = idx.shape[0]
    assert n % (SC_WORKERS * SC_GATHER_ROWS) == 0
    per_w = n // SC_WORKERS
    n_chunks = per_w // SC_GATHER_ROWS

    @functools.partial(
        pl.kernel,
        out_type=jax.ShapeDtypeStruct((n, PACKED), jnp.uint32),
        mesh=_sc_mesh(),
        scratch_types=[
            pltpu.VMEM((SC_GATHER_ROWS,), jnp.int32),
            pltpu.VMEM((SC_GATHER_ROWS, PACKED), jnp.uint32),
            pltpu.SemaphoreType.DMA,
        ],
        name="combine_gather",
    )
    def k(y_hbm, i_hbm, o_hbm, idx_v, rows_v, sem):
        base = _sc_worker() * per_w

        @pl.loop(0, n_chunks)
        def _(j):
            r0 = pl.multiple_of(base + j * SC_GATHER_ROWS, 8)
            pltpu.sync_copy(i_hbm.at[pl.ds(r0, SC_GATHER_ROWS)], idx_v)
            pltpu.async_copy(y_hbm.at[idx_v], rows_v, sem).wait()
            pltpu.sync_copy(rows_v, o_hbm.at[pl.ds(r0, SC_GATHER_ROWS)])

    return k(yb, idx)


def _weight_copies(w1_hbm, w2_hbm, w1f, w2f, sem, idx):
    return (pltpu.make_async_copy(w1_hbm.at[idx], w1f, sem.at[0]),
            pltpu.make_async_copy(w2_hbm.at[idx], w2f, sem.at[1]))


def _expert_kernel(be_ref, x_ref, w1_hbm, b1_ref, w2_hbm, b2_ref, y_ref, w1f, w2f, w1b, w2b, sem, *, layer):
    copies = functools.partial(_weight_copies, w1_hbm, w2_hbm, w1f, w2f, sem)
    for a in range(EXPERT_STEP):
        j = pl.program_id(0) * EXPERT_STEP + a
        e = be_ref[j]
        prev = be_ref[jnp.maximum(j - 1, 0)]
        live = e < N_EXPERTS
        first = live & ((j == 0) | (e != prev))
        rows = slice(a * ROW_BLOCK, (a + 1) * ROW_BLOCK)

        if a == 0:
            @pl.when(first & (j == 0))
            def _():
                for c in copies(layer * N_EXPERTS + e):
                    c.start()

        @pl.when(first)
        def _():
            for c in copies(layer * N_EXPERTS + e):
                c.wait()
            w1b[...] = w1f[...].astype(BF16)
            w2b[...] = w2f[...].astype(BF16)
            nxt = lax.while_loop(
                lambda k: (k < N_BLOCKS) & (be_ref[jnp.minimum(k, N_BLOCKS - 1)] == e), lambda k: k + 1, j + 1)
            e_next = be_ref[jnp.minimum(nxt, N_BLOCKS - 1)]

            @pl.when((nxt < N_BLOCKS) & (e_next < N_EXPERTS))
            def _():
                for c in copies(layer * N_EXPERTS + e_next):
                    c.start()

        @pl.when(live)
        def _():
            idx = layer * N_EXPERTS + e
            x = jnp.concatenate(_unpack_rows(x_ref[rows, :]), axis=1).astype(BF16)
            gu = jnp.dot(x, w1b[...], preferred_element_type=F32) + b1_ref[idx]
            gate = jnp.minimum(gu[:, :D_EXPERT], SWIGLU_LIMIT)
            lin = jnp.clip(gu[:, D_EXPERT:], -SWIGLU_LIMIT, SWIGLU_LIMIT)
            act = gate * _sigmoid(SWIGLU_ALPHA * gate) * (lin + 1.0)
            y_ref[rows, :] = _pack_rows(
                jnp.dot(act.astype(BF16), w2b[...], preferred_element_type=F32) + b2_ref[idx])


def _experts(block_e, rows, w1, b1, w2, b2, layer):
    step_rows = EXPERT_STEP * ROW_BLOCK
    return pl.pallas_call(
        functools.partial(_expert_kernel, layer=layer),
        grid_spec=pltpu.PrefetchScalarGridSpec(
            num_scalar_prefetch=1,
            grid=(N_BLOCKS // EXPERT_STEP,),
            in_specs=[
                pl.BlockSpec((step_rows, PACKED), lambda j, be: (j, 0)),
                pl.BlockSpec(memory_space=pl.ANY),
                pl.BlockSpec((DEPTH * N_EXPERTS, 1, 2 * D_EXPERT), lambda j, be: (0, 0, 0)),
                pl.BlockSpec(memory_space=pl.ANY),
                pl.BlockSpec((DEPTH * N_EXPERTS, 1, D_MODEL), lambda j, be: (0, 0, 0)),
            ],
            out_specs=pl.BlockSpec((step_rows, PACKED), lambda j, be: (j, 0)),
            scratch_shapes=[
                pltpu.VMEM((D_MODEL, 2 * D_EXPERT), F32),
                pltpu.VMEM((D_EXPERT, D_MODEL), F32),
                pltpu.VMEM((D_MODEL, 2 * D_EXPERT), BF16),
                pltpu.VMEM((D_EXPERT, D_MODEL), BF16),
                pltpu.SemaphoreType.DMA((2,)),
            ],
        ),
        out_shape=jax.ShapeDtypeStruct((N_ROWS, PACKED), jnp.uint32),
        compiler_params=pltpu.CompilerParams(
            dimension_semantics=("arbitrary",), vmem_limit_bytes=VMEM_LIMIT),
        name="experts",
    )(block_e, rows, w1, b1, w2, b2)


def _final_kernel(x_ref, mod_ref, g_ref, p_ref, fg_ref, *rest, batch):
    o_ref = rest[-1]
    gate = mod_ref[5, pl.ds(batch, 1), :]
    o_ref[0] = _rms(x_ref[0] + gate * _routed_sum(g_ref, p_ref), fg_ref[...])


def _final(xa, mod, gathered, probs, final_g, batch, out=None):
    ctx_tiles = CTX_LEN // TOK_TILE
    tok = lambda i: (batch, i + ctx_tiles, 0)
    in_specs = [
        pl.BlockSpec((1, TOK_TILE, D_MODEL), tok),
        pl.BlockSpec((6, 8, D_MODEL), lambda i: (0, 0, 0)),
        pl.BlockSpec((TOP_K, 1, TOK_TILE, PACKED), lambda i: (0, 0, i + ctx_tiles, 0)),
        pl.BlockSpec((1, TOK_TILE, TOP_K), tok),
        pl.BlockSpec((1, D_MODEL), lambda i: (0, 0)),
    ]
    args = [xa, mod, gathered.reshape(TOP_K, 1, S_TOT, PACKED), probs, final_g]
    aliases = {}
    if out is not None:
        in_specs.append(pl.BlockSpec(memory_space=pl.ANY))
        args.append(out)
        aliases = {len(args) - 1: 0}
    return pl.pallas_call(
        functools.partial(_final_kernel, batch=batch),
        grid=(N_TILES - ctx_tiles,),
        in_specs=in_specs,
        out_specs=pl.BlockSpec((1, TOK_TILE, D_MODEL), lambda i: (batch, i, 0)),
        out_shape=jax.ShapeDtypeStruct((BATCH, SEQ, D_MODEL), F32),
        input_output_aliases=aliases,
        compiler_params=pltpu.CompilerParams(dimension_semantics=("parallel",)),
        name="final",
    )(*args)


def _split_bf16(w):
    bits = lax.bitcast_convert_type(w.astype(F32), jnp.uint32) & jnp.uint32(0xFFFF0000)
    hi = lax.bitcast_convert_type(bits, F32)
    return jnp.concatenate([hi.astype(BF16), (w - hi).astype(BF16)], axis=-1)


def _rope_tables():
    t = np.arange(SEQ)
    axis_dim = HEAD_DIM // 2
    inv_freq = (1.0 / (ROPE_BASE ** (np.arange(0, axis_dim, 2, dtype=np.float32) / axis_dim))).astype(np.float32)
    ang_r = (t // GRID_W).astype(np.float32)[:, None] * inv_freq[None]
    ang_c = (t % GRID_W).astype(np.float32)[:, None] * inv_freq[None]
    ang = np.concatenate([ang_r, ang_r, ang_c, ang_c] * (LANES // HEAD_DIM), axis=-1)
    cos = np.concatenate([np.ones((CTX_LEN, LANES), np.float32), np.cos(ang)], axis=0)
    sin = np.concatenate([np.zeros((CTX_LEN, LANES), np.float32), np.sin(ang)], axis=0)
    return jnp.asarray(cos, F32), jnp.asarray(sin, F32)


def kernel(x, c, ctx, c_ctx, w_ada, b_ada, norm1_g, norm2_g, w_in, ssm_lam_re, ssm_lam_im, ssm_log_dt, ssm_b_re, ssm_b_im, ssm_c_re, ssm_c_im, ssm_d, ssm_w_val, ssm_w_gate, attn_sink, attn_w_up, pool_w, pool_scale, w_out, router_w, router_b, exp_w1, exp_b1, exp_w2, exp_b2, final_g):
    assert x.shape == (BATCH, SEQ, D_MODEL) and ctx.shape == (BATCH, CTX_LEN, D_MODEL)
    cos_t, sin_t = _rope_tables()
    cv = jnp.concatenate([c, c_ctx[None], jnp.zeros((8 - BATCH - 1, D_MODEL), F32)], axis=0)
    mods = _ada(cv, w_ada, b_ada)
    xs = (ctx, x)
    ew1 = exp_w1.reshape(DEPTH * N_EXPERTS, D_MODEL, 2 * D_EXPERT)
    eb1 = exp_b1.reshape(DEPTH * N_EXPERTS, 1, 2 * D_EXPERT)
    ew2 = exp_w2.reshape(DEPTH * N_EXPERTS, D_EXPERT, D_MODEL)
    eb2 = exp_b2.reshape(DEPTH * N_EXPERTS, 1, D_MODEL)
    half_gate = jnp.where(jnp.arange(IN_WIDTH) >= OFF_GATE, 0.5, 1.0).astype(F32)
    w_in_b = (w_in * half_gate).astype(BF16)
    ssm_tables = _ssm_params(ssm_lam_re, ssm_lam_im, ssm_log_dt, ssm_b_re, ssm_b_im, ssm_c_re, ssm_c_im)
    proj = _inproj(xs, mods[0], norm1_g[0][None], w_in_b, 0, cos_t, sin_t)
    out = None
    for l in range(DEPTH):
        mod = mods[l]
        u, kv, q, pool_in = proj[:4]
        if len(proj) > 4:
            xs = (proj[4], proj[4])
        yf, yb = _ssm(u, *ssm_tables, l)
        o = _attn(attn_sink[l], q, kv)
        xa, f, top_e, top_p = _merge(
            xs, mod, norm1_g[l][None], w_in_b, l, yf, yb, u, ssm_d[l][None],
            (0.5 * ssm_w_val[l]).astype(BF16), (0.5 * ssm_w_gate[l]).astype(BF16), o, attn_w_up[l].astype(BF16),
            pool_in, pool_w[l].astype(BF16), pool_scale[l][None], (0.5 * w_out[l]).astype(BF16), norm2_g[l][None],
            _split_bf16(router_w[l]), router_b[l][None])
        proj = None
        for b in range(BATCH):
            dest, block_e = _route(top_e.reshape(N_TOK, TOP_K), b)
            dest_t = dest.T
            rows = _dispatch(f.reshape(N_TOK, PACKED), dest_t, b)
            yrows = _experts(block_e[:N_BLOCKS, 0], rows, ew1, eb1, ew2, eb2, l)
            gathered = _combine_gather(yrows, dest_t.reshape(-1))
            if l < DEPTH - 1:
                proj = _inproj((xa, xa), mods[l + 1], norm1_g[l + 1][None], w_in_b, l + 1, cos_t, sin_t,
                               (mod, gathered, top_p), b, proj)
            else:
                out = _final(xa, mod, gathered, top_p, final_g[None], b, out)
    return out
```

```python
import functools
import math

import jax
import jax.numpy as jnp
import numpy as np
from jax import lax
from jax.experimental import pallas as pl
from jax.experimental.pallas import tpu as pltpu
from jax.experimental.pallas import tpu_sc as plsc

F32 = jnp.float32
BF16 = jnp.bfloat16
HIGHEST = lax.Precision.HIGHEST

D_MODEL = 1024
BATCH = 2
SEQ = 8192
DEPTH = 2
GRID_W = 64
CTX_LEN = 256
NORM_EPS = 1e-6
SSM_WIDTH = 512
SSM_GROUP = 16
SSM_GROUPS = 32
SSM_STATE = 64
HEAD_DIM = 64
N_Q_HEADS = 8
N_KV_HEADS = 2
GQA_GROUP = 4
ATTN_WIDTH = 512
KV_WIDTH = 128
WINDOW = 128
ROPE_BASE = 10000.0
ATTN_SCALE = HEAD_DIM ** -0.5
POOL_WINDOWS = (2, 4, 8, 16)
POOL_WIDTH = 512
POOL_GROUP = 128
POOL_OUT_GROUP = 256
OFF_K = 512
OFF_V = 640
OFF_Q = 768
OFF_POOL = 1280
OFF_GATE = 1792
IN_WIDTH = OFF_GATE + 3 * D_MODEL
N_EXPERTS = 32
TOP_K = 4
D_EXPERT = 1024
SWIGLU_LIMIT = 7.0
SWIGLU_ALPHA = 1.702

S_TOT = CTX_LEN + SEQ
N_TOK = BATCH * S_TOT
G_TOK = S_TOT
TOK_TILE = 256
N_TILES = S_TOT // TOK_TILE
assert CTX_LEN == TOK_TILE
ATTN_BLOCK = 128
ATTN_STEP = 3
POOL_HALO = 8
ROUTE_TILE = 1056
ROW_BLOCK = 256
EXPERT_STEP = 4
N_ROWS = -(-(G_TOK * TOP_K + N_EXPERTS * (ROW_BLOCK - 1)) // (ROW_BLOCK * EXPERT_STEP)) * ROW_BLOCK * EXPERT_STEP
N_BLOCKS = N_ROWS // ROW_BLOCK
N_BLOCKS_PAD = -(-N_BLOCKS // 8) * 8
SC_CORES = 2
SC_WORKERS = SC_CORES * 16
PACKED = D_MODEL // 2
SC_SCATTER_ROWS = 88
SC_GATHER_ROWS = 88
assert G_TOK % (SC_WORKERS * SC_SCATTER_ROWS) == 0 and G_TOK % ROUTE_TILE == 0
VMEM_LIMIT = 56 * 1024 * 1024
LANES = 128
MASK_VALUE = -1e30


def _rms(x, g):
    return x * lax.rsqrt(jnp.mean(x * x, axis=-1, keepdims=True) + NORM_EPS) * g


def _sigmoid(x):
    return 0.5 * jnp.tanh(0.5 * x) + 0.5


def _pack_rows(x):
    half = x.shape[1] // 2
    lo = lax.bitcast_convert_type(x[:, :half].astype(BF16).astype(F32), jnp.uint32)
    hi = lax.bitcast_convert_type(x[:, half:].astype(BF16).astype(F32), jnp.uint32)
    return (lo >> 16) | hi


def _unpack_rows(p):
    lo = lax.bitcast_convert_type(p << 16, F32)
    hi = lax.bitcast_convert_type(p & jnp.uint32(0xFFFF0000), F32)
    return lo, hi


def _token_specs(xs, last=N_TILES - 1, im=lambda f: f):
    ctx_tiles = CTX_LEN // TOK_TILE
    off = 0 if xs[1].shape[1] == S_TOT else ctx_tiles
    return [pl.BlockSpec((1, TOK_TILE, D_MODEL), im(lambda b, i: (b, 0, 0))),
            pl.BlockSpec((1, TOK_TILE, D_MODEL), im(lambda b, i: (b, jnp.clip(i, ctx_tiles, last) - off, 0)))]


def _token_tile(i, xc_ref, xl_ref):
    return jnp.where(i == 0, xc_ref[0], xl_ref[0])


def _mod_row(i, b):
    return jnp.where(i == 0, 2, b)


def _ada_kernel(cv_ref, w_ref, b_ref, o_ref):
    cv = cv_ref[...]
    s = cv * jax.nn.sigmoid(cv)
    o_ref[0, 0] = jnp.dot(s, w_ref[0], preferred_element_type=F32, precision=HIGHEST) + b_ref[0, 0]


def _ada(cv, w_ada, b_ada):
    return pl.pallas_call(
        _ada_kernel,
        grid=(DEPTH, 6),
        in_specs=[
            pl.BlockSpec((8, D_MODEL), lambda l, j: (0, 0)),
            pl.BlockSpec((1, D_MODEL, D_MODEL), lambda l, j: (l, 0, j)),
            pl.BlockSpec((1, 1, 1, D_MODEL), lambda l, j: (l, j, 0, 0)),
        ],
        out_specs=pl.BlockSpec((1, 1, 8, D_MODEL), lambda l, j: (l, j, 0, 0)),
        out_shape=jax.ShapeDtypeStruct((DEPTH, 6, 8, D_MODEL), F32),
        name="ada",
    )(cv, w_ada, b_ada.reshape(DEPTH, 6, 1, D_MODEL))


def _routed_sum(g_ref, p_ref):
    p = p_ref[0]
    lo, hi = _unpack_rows(g_ref[0, 0])
    y_lo = p[:, 0:1] * lo
    y_hi = p[:, 0:1] * hi
    for k in range(1, TOP_K):
        lo, hi = _unpack_rows(g_ref[k, 0])
        y_lo = y_lo + p[:, k:k + 1] * lo
        y_hi = y_hi + p[:, k:k + 1] * hi
    return jnp.concatenate([y_lo, y_hi], axis=1)


def _rope(t, cos, sin):
    lane = lax.broadcasted_iota(jnp.int32, t.shape, 1)
    rot = jnp.where((lane % 32) < 16, -pltpu.roll(t, LANES - 16, 1), pltpu.roll(t, 16, 1))
    return t * cos + rot * sin


def _inproj_kernel(*refs, pending, batch):
    xc_ref, xl_ref, mod_ref, g_ref, w_ref, cos_ref, sin_ref = refs[:7]
    if pending:
        pmod_ref, rows_ref, probs_ref = refs[7:10]
        u_ref, kv_ref, q_ref, pool_ref, xo_ref = refs[-5:]
    else:
        u_ref, kv_ref, q_ref, pool_ref = refs[-4:]
    b, i = (pl.program_id(0), pl.program_id(1)) if batch is None else (batch, pl.program_id(0))
    row = _mod_row(i, b)
    shift = mod_ref[0, pl.ds(row, 1), :]
    scale = mod_ref[1, pl.ds(row, 1), :]
    x = _token_tile(i, xc_ref, xl_ref)
    if pending:
        x = x + pmod_ref[5, pl.ds(row, 1), :] * _routed_sum(rows_ref, probs_ref)
        xo_ref[0] = x
    h = _rms(x, g_ref[...]) * (1.0 + scale) + shift
    p = jnp.dot(h.astype(BF16), w_ref[0, :, :OFF_GATE], preferred_element_type=F32)
    cos = cos_ref[...]
    sin = sin_ref[...]
    u_ref[0] = p[:, :OFF_K]
    kv_ref[0, :, :KV_WIDTH] = _rope(p[:, OFF_K:OFF_V], cos, sin).astype(BF16)
    kv_ref[0, :, KV_WIDTH:] = p[:, OFF_V:OFF_Q].astype(BF16)
    for c in range(ATTN_WIDTH // LANES):
        qc = p[:, OFF_Q + c * LANES:OFF_Q + (c + 1) * LANES]
        q_ref[0, :, c * LANES:(c + 1) * LANES] = (_rope(qc, cos, sin) * ATTN_SCALE).astype(BF16)
    pool_ref[0] = p[:, OFF_POOL:OFF_GATE]


def _inproj(xs, mod, g1, w_in, layer, cos_t, sin_t, pending=None, batch=None, outs=None):
    im = (lambda f: f) if batch is None else (lambda f: lambda i: f(batch, i))
    tok = im(lambda b, i: (b, i, 0))
    in_specs = [
        *_token_specs(xs, im=im),
        pl.BlockSpec((6, 8, D_MODEL), im(lambda b, i: (0, 0, 0))),
        pl.BlockSpec((1, D_MODEL), im(lambda b, i: (0, 0))),
        pl.BlockSpec((1, D_MODEL, IN_WIDTH), im(lambda b, i: (layer, 0, 0))),
        pl.BlockSpec((TOK_TILE, LANES), im(lambda b, i: (i, 0))),
        pl.BlockSpec((TOK_TILE, LANES), im(lambda b, i: (i, 0))),
    ]
    out_specs = [
        pl.BlockSpec((1, TOK_TILE, SSM_WIDTH), tok),
        pl.BlockSpec((1, TOK_TILE, 2 * KV_WIDTH), tok),
        pl.BlockSpec((1, TOK_TILE, ATTN_WIDTH), tok),
        pl.BlockSpec((1, TOK_TILE, POOL_WIDTH), tok),
    ]
    out_shape = [
        jax.ShapeDtypeStruct((BATCH, S_TOT, SSM_WIDTH), F32),
        jax.ShapeDtypeStruct((BATCH, S_TOT, 2 * KV_WIDTH), BF16),
        jax.ShapeDtypeStruct((BATCH, S_TOT, ATTN_WIDTH), BF16),
        jax.ShapeDtypeStruct((BATCH, S_TOT, POOL_WIDTH), F32),
    ]
    args = [*xs, mod, g1, w_in, cos_t, sin_t]
    if pending is not None:
        pmod, gathered, probs = pending
        in_specs += [
            pl.BlockSpec((6, 8, D_MODEL), im(lambda b, i: (0, 0, 0))),
            pl.BlockSpec((TOP_K, 1, TOK_TILE, PACKED), im(lambda b, i: (0, 0, i, 0))),
            pl.BlockSpec((1, TOK_TILE, TOP_K), tok),
        ]
        out_specs.append(pl.BlockSpec((1, TOK_TILE, D_MODEL), tok))
        out_shape.append(jax.ShapeDtypeStruct((BATCH, S_TOT, D_MODEL), F32))
        args += [pmod, gathered.reshape(TOP_K, 1, S_TOT, PACKED), probs]
    aliases = {}
    if outs is not None:
        aliases = {len(args) + k: k for k in range(len(outs))}
        in_specs += [pl.BlockSpec(memory_space=pl.ANY)] * len(outs)
        args += list(outs)
    return pl.pallas_call(
        functools.partial(_inproj_kernel, pending=pending is not None, batch=batch),
        grid=(BATCH, N_TILES) if batch is None else (N_TILES,),
        in_specs=in_specs,
        out_specs=out_specs,
        out_shape=out_shape,
        input_output_aliases=aliases,
        compiler_params=pltpu.CompilerParams(
            dimension_semantics=("parallel",) * (2 if batch is None else 1), vmem_limit_bytes=VMEM_LIMIT),
        name="inproj",
    )(*args)


SSM_HALF_IN = SSM_WIDTH // 2
SSM_HALF_ST = SSM_GROUPS // 2 * SSM_STATE
N_STATE_ROWS = 8
N_LANE_TILES = SSM_HALF_ST // LANES
SCAN_STRIDE = TOK_TILE + 8


def _ssm_kernel(uf_ref, ub_ref, wb_ref, wc_ref, ar_ref, ai_ref, yf_ref, yb_ref, buf_f, buf_b, hf_ref, hb_ref):
    i = pl.program_id(0)
    T = TOK_TILE

    @pl.when(i == 0)
    def _():
        hf_ref[...] = jnp.zeros_like(hf_ref)
        hb_ref[...] = jnp.zeros_like(hb_ref)

    for d, (u_ref, buf) in enumerate(((uf_ref, buf_f), (ub_ref, buf_b))):
        for b in range(BATCH):
            for half in range(2):
                ub16 = u_ref[b, :, half * SSM_HALF_IN:(half + 1) * SSM_HALF_IN].astype(BF16)
                for reim in range(2):
                    k = reim * 4 + half * 2 + b
                    bu = jnp.dot(ub16, wb_ref[d, reim, half], preferred_element_type=F32)
                    for l in range(N_LANE_TILES):
                        buf[l, k * SCAN_STRIDE:k * SCAN_STRIDE + T, :] = bu[:, l * LANES:(l + 1) * LANES]

    def step(t, carry):
        hf, hb = carry
        tb = T - 1 - t
        nf, nb = [], []
        for l in range(N_LANE_TILES):
            sl = pl.ds(l * LANES, LANES)
            rows_f = pl.ds(t, N_STATE_ROWS, stride=SCAN_STRIDE)
            h = ar_ref[0, :, sl] * hf[l] + ai_ref[0, :, sl] * pltpu.roll(hf[l], 4, 0) + buf_f[l, rows_f, :]
            buf_f[l, rows_f, :] = h
            nf.append(h)
            rows_b = pl.ds(tb, N_STATE_ROWS, stride=SCAN_STRIDE)
            h = ar_ref[1, :, sl] * hb[l] + ai_ref[1, :, sl] * pltpu.roll(hb[l], 4, 0) + buf_b[l, rows_b, :]
            buf_b[l, rows_b, :] = h
            nb.append(h)
        return tuple(nf), tuple(nb)

    hf0 = tuple(hf_ref[:, l * LANES:(l + 1) * LANES] for l in range(N_LANE_TILES))
    hb0 = tuple(hb_ref[:, l * LANES:(l + 1) * LANES] for l in range(N_LANE_TILES))
    hf1, hb1 = lax.fori_loop(0, T, step, (hf0, hb0), unroll=2)
    for l in range(N_LANE_TILES):
        hf_ref[:, l * LANES:(l + 1) * LANES] = hf1[l]
        hb_ref[:, l * LANES:(l + 1) * LANES] = hb1[l]

    for d, (buf, y_ref) in enumerate(((buf_f, yf_ref), (buf_b, yb_ref))):
        for b in range(BATCH):
            for half in range(2):
                kre = half * 2 + b
                kim = 4 + half * 2 + b
                state = lambda k: jnp.concatenate(
                    [buf[l, k * SCAN_STRIDE:k * SCAN_STRIDE + T, :] for l in range(N_LANE_TILES)],
                    axis=1).astype(BF16)
                y = jnp.dot(state(kre), wc_ref[d, 0, half], preferred_element_type=F32)
                y = y + jnp.dot(state(kim), wc_ref[d, 1, half], preferred_element_type=F32)
                y_ref[b, :, half * SSM_HALF_IN:(half + 1) * SSM_HALF_IN] = y


def _ssm(u, wb, wc, ar, ai, layer):
    fwd = lambda i: (0, i, 0)
    bwd = lambda i: (0, jnp.where(i == 0, 0, N_TILES - i), 0)
    return pl.pallas_call(
        _ssm_kernel,
        grid=(N_TILES,),
        in_specs=[
            pl.BlockSpec((BATCH, TOK_TILE, SSM_WIDTH), fwd),
            pl.BlockSpec((BATCH, TOK_TILE, SSM_WIDTH), bwd),
            pl.BlockSpec((None, 2, 2, 2, SSM_HALF_IN, SSM_HALF_ST), lambda i: (layer, 0, 0, 0, 0, 0)),
            pl.BlockSpec((None, 2, 2, 2, SSM_HALF_ST, SSM_HALF_IN), lambda i: (layer, 0, 0, 0, 0, 0)),
            pl.BlockSpec((None, 2, N_STATE_ROWS, SSM_HALF_ST), lambda i: (layer, 0, 0, 0)),
            pl.BlockSpec((None, 2, N_STATE_ROWS, SSM_HALF_ST), lambda i: (layer, 0, 0, 0)),
        ],
        out_specs=[
            pl.BlockSpec((BATCH, TOK_TILE, SSM_WIDTH), fwd),
            pl.BlockSpec((BATCH, TOK_TILE, SSM_WIDTH), bwd),
        ],
        out_shape=[jax.ShapeDtypeStruct((BATCH, S_TOT, SSM_WIDTH), F32)] * 2,
        scratch_shapes=[
            pltpu.VMEM((N_LANE_TILES, N_STATE_ROWS * SCAN_STRIDE, LANES), F32),
            pltpu.VMEM((N_LANE_TILES, N_STATE_ROWS * SCAN_STRIDE, LANES), F32),
            pltpu.VMEM((N_STATE_ROWS, SSM_HALF_ST), F32),
            pltpu.VMEM((N_STATE_ROWS, SSM_HALF_ST), F32),
        ],
        compiler_params=pltpu.CompilerParams(
            dimension_semantics=("arbitrary",), vmem_limit_bytes=VMEM_LIMIT),
        name="ssm",
    )(u, u, wb, wc, ar, ai)


def _ssm_params(lam_re, lam_im, log_dt, b_re, b_im, c_re, c_im):
    lr = lam_re.astype(F32)
    li = lam_im.astype(F32)
    dt = jnp.exp(log_dt.astype(F32))[..., None]
    mag = jnp.exp(lr * dt)
    ang = li * dt
    a_re = mag * jnp.cos(ang)
    a_im = mag * jnp.sin(ang)
    den = lr * lr + li * li
    k_re = ((a_re - 1.0) * lr + a_im * li) / den
    k_im = (a_im * lr - (a_re - 1.0) * li) / den
    bb_re = k_re[..., None] * b_re - k_im[..., None] * b_im
    bb_im = k_re[..., None] * b_im + k_im[..., None] * b_re
    n_g = SSM_GROUPS // 2
    lead = lam_re.shape[:2]
    drive_mask = np.equal.outer(np.arange(SSM_HALF_IN) // SSM_GROUP, np.arange(SSM_HALF_ST) // SSM_STATE)
    repeat_st = np.tile(np.eye(SSM_STATE, dtype=np.float32), (1, n_g))
    repeat_in = np.tile(np.eye(SSM_GROUP, dtype=np.float32), (1, n_g))

    def block_diag(parts, rows_per_half, repeat, mask):
        rows = jnp.stack([jnp.swapaxes(p, -1, -2) for p in parts], axis=2)
        rows = rows.reshape(*lead, 2, 2, rows_per_half, repeat.shape[0])
        full = jnp.einsum('...rp,pc->...rc', rows, repeat, precision=HIGHEST)
        return (full * mask.astype(np.float32)).astype(BF16)

    wb = block_diag((bb_re, bb_im), SSM_HALF_IN, repeat_st, drive_mask)
    wc = block_diag((c_re.astype(F32), -c_im.astype(F32)), SSM_HALF_ST, repeat_in, drive_mask.T)

    def rows(a, sign_re):
        halves = a.reshape(*lead, 1, 2, 1, SSM_HALF_ST)
        signs = jnp.asarray([sign_re, 1.0], F32).reshape(2, 1, 1, 1)
        return jnp.broadcast_to(halves * signs, (*lead, 2, 2, BATCH, SSM_HALF_ST)).reshape(
            *lead, N_STATE_ROWS, SSM_HALF_ST)

    return wb, wc, rows(a_re, 1.0), rows(a_im, -1.0)


def _attn_block(n, q, kv_band, kv_ctx, sink_ref, band_bias):
    kv_all = jnp.concatenate([kv_band, kv_ctx], axis=0)
    nb = 3 * ATTN_BLOCK
    last = SEQ // ATTN_BLOCK - 1
    krow = lax.broadcasted_iota(jnp.int32, (nb, 1), 0)
    lo_r = jnp.where(n < 0, nb, jnp.where(n == 0, ATTN_BLOCK, 0))
    hi_r = jnp.where(n == last, 2 * ATTN_BLOCK, nb)
    bias = band_bias + jnp.where((krow >= lo_r) & (krow < hi_r), 0.0, MASK_VALUE)
    bias = jnp.concatenate([bias] * GQA_GROUP, axis=1)
    nt = (((1,), (1,)), ((), ()))
    outs = []
    for hk in range(N_KV_HEADS):
        k = kv_all[:, hk * HEAD_DIM:(hk + 1) * HEAD_DIM]
        v_t = kv_all[:, KV_WIDTH + hk * HEAD_DIM:KV_WIDTH + (hk + 1) * HEAD_DIM].astype(F32).T.astype(BF16)
        heads = range(hk * GQA_GROUP, (hk + 1) * GQA_GROUP)
        qs = jnp.concatenate([q[:, hq * HEAD_DIM:(hq + 1) * HEAD_DIM] for hq in heads], axis=0)
        sink = jnp.concatenate([jnp.full((1, ATTN_BLOCK), sink_ref[hq], F32) for hq in heads], axis=1)
        s_t = lax.dot_general(k, qs, nt, preferred_element_type=F32)
        s_w = s_t[:nb] + bias
        s_c = s_t[nb:]
        m = jnp.maximum(jnp.maximum(jnp.max(s_w, axis=0, keepdims=True),
                                    jnp.max(s_c, axis=0, keepdims=True)), sink)
        e_w = jnp.exp(s_w - m)
        e_c = jnp.exp(s_c - m)
        den = jnp.sum(e_w, axis=0, keepdims=True) + jnp.sum(e_c, axis=0, keepdims=True) + jnp.exp(sink - m)
        e = jnp.concatenate([e_w, e_c], axis=0).astype(BF16)
        o_t = jnp.dot(v_t, e, preferred_element_type=F32) / den
        outs.extend(o_t[:, g * ATTN_BLOCK:(g + 1) * ATTN_BLOCK].T for g in range(GQA_GROUP))
    return jnp.concatenate(outs, axis=1).astype(BF16)


def _attn_kernel(sink_ref, band_ref, q_ref, kc_ref, *rest):
    kv_refs, o_ref = rest[:-1], rest[-1]
    j0 = pl.program_id(1) * ATTN_STEP
    for a in range(ATTN_STEP):
        n = j0 + a - CTX_LEN // ATTN_BLOCK
        band = jnp.concatenate([kv_refs[a + d][0] for d in range(3)], axis=0)
        rows = slice(a * ATTN_BLOCK, (a + 1) * ATTN_BLOCK)
        o_ref[0, rows, :] = _attn_block(n, q_ref[0, rows, :], band, kc_ref[0], sink_ref, band_ref[...])


def _attn(sink, q, kv):
    nq = S_TOT // ATTN_BLOCK
    first = CTX_LEN // ATTN_BLOCK
    r = jnp.arange(ATTN_BLOCK)[None, :]
    c = jnp.arange(3 * ATTN_BLOCK)[:, None] - ATTN_BLOCK
    band_bias = jnp.where(jnp.abs(c - r) <= WINDOW, 0.0, MASK_VALUE).astype(F32)

    def kv_spec(d):
        return pl.BlockSpec((1, ATTN_BLOCK, 2 * KV_WIDTH),
                            lambda b, j: (b, jnp.clip(j * ATTN_STEP + d - 1, first, nq - 1), 0))

    return pl.pallas_call(
        _attn_kernel,
        grid=(BATCH, nq // ATTN_STEP),
        in_specs=[
            pl.BlockSpec(memory_space=pltpu.SMEM),
            pl.BlockSpec((3 * ATTN_BLOCK, ATTN_BLOCK), lambda b, j: (0, 0)),
            pl.BlockSpec((1, ATTN_STEP * ATTN_BLOCK, ATTN_WIDTH), lambda b, j: (b, j, 0)),
            pl.BlockSpec((1, CTX_LEN, 2 * KV_WIDTH), lambda b, j: (b, 0, 0)),
            *[kv_spec(d) for d in range(ATTN_STEP + 2)],
        ],
        out_specs=pl.BlockSpec((1, ATTN_STEP * ATTN_BLOCK, ATTN_WIDTH), lambda b, j: (b, j, 0)),
        out_shape=jax.ShapeDtypeStruct((BATCH, S_TOT, ATTN_WIDTH), BF16),
        compiler_params=pltpu.CompilerParams(dimension_semantics=("parallel", "parallel")),
        name="attn",
    )(sink, band_bias, q, kv, *[kv] * (ATTN_STEP + 2))


def _gelu_tanh(x):
    return 0.5 * x * (1.0 + jnp.tanh(math.sqrt(2.0 / math.pi) * (x + 0.044715 * (x * x * x))))


def _pool_means(ext, i):
    T = TOK_TILE
    n_ext = T + 2 * POOL_HALO
    seq_len = jnp.where(i == 0, CTX_LEN, SEQ)
    start = jnp.where(i == 0, 0, (i - 1) * T)
    r = start - POOL_HALO + lax.broadcasted_iota(jnp.int32, (n_ext, 1), 0)
    ext = jnp.where((r >= 0) & (r < seq_len), ext, 0.0)
    tl = start + lax.broadcasted_iota(jnp.int32, (T, 1), 0)
    back = lambda a, s: pltpu.roll(a, s, 0)
    fwd = lambda a, s: pltpu.roll(a, n_ext - s, 0)
    diffs = []
    for gi, w in enumerate(POOL_WINDOWS):
        e = ext[:, gi * POOL_GROUP:(gi + 1) * POOL_GROUP]
        lo = w // 2
        hi = w - 1 - lo
        s = e + back(e, 1)
        span = 2
        while span < w:
            s = s + back(s, span)
            span *= 2
        if w > 2:
            s = fwd(s, hi)
        cnt = (jnp.clip(tl + hi + 1, 0, seq_len) - jnp.clip(tl - lo, 0, seq_len)).astype(F32)
        mean = s[POOL_HALO:POOL_HALO + T] / cnt
        diffs.append((mean - e[POOL_HALO:POOL_HALO + T]).astype(BF16))
    return diffs


def _merge_kernel(xc_ref, xl_ref, mod_ref, g1_ref, w_ref, yf_ref, yb_ref, u_ref, dsk_ref, wval_ref, wgate_ref,
                  o_ref, wup_ref, pc_ref, pp_ref, pn_ref, pw_ref, ps_ref, wout_ref, g2_ref, rw_ref, rb_ref,
                  xo_ref, f_ref, te_ref, tp_ref, xprev_ref):
    b = pl.program_id(0)
    s = pl.program_id(1)

    @pl.when(s == 0)
    def _():
        xprev_ref[...] = jnp.zeros_like(xprev_ref)

    rowp = _mod_row(jnp.maximum(s - 1, 0), b)
    f = _rms(xprev_ref[...], g2_ref[...]) * (1.0 + mod_ref[4, pl.ds(rowp, 1), :]) + mod_ref[3, pl.ds(rowp, 1), :]
    f_ref[0] = _pack_rows(f)
    f_hi = f.astype(BF16)
    f_lo = (f - f_hi.astype(F32)).astype(BF16)
    r_hi = jnp.dot(f_hi, rw_ref[...], preferred_element_type=F32)
    r_lo = jnp.dot(f_lo, rw_ref[...], preferred_element_type=F32)
    logits = r_hi[:, :N_EXPERTS] + r_hi[:, N_EXPERTS:] + r_lo[:, :N_EXPERTS] + rb_ref[...]
    lane = lax.broadcasted_iota(jnp.int32, logits.shape, 1)
    vals, idxs = [], []
    for _ in range(TOP_K):
        mx = jnp.max(logits, axis=-1, keepdims=True)
        ix = jnp.min(jnp.where(logits == mx, lane, N_EXPERTS), axis=-1, keepdims=True)
        vals.append(mx)
        idxs.append(ix)
        logits = jnp.where(lane == ix, -jnp.inf, logits)
    ex = [jnp.exp(v - vals[0]) for v in vals]
    tot = ex[0] + ex[1] + ex[2] + ex[3]
    te_ref[0] = jnp.concatenate(idxs, axis=1)
    tp_ref[0] = jnp.concatenate(ex, axis=1) / tot

    i = jnp.minimum(s, N_TILES - 1)
    row = _mod_row(i, b)
    shift, scale, gate = (mod_ref[j, pl.ds(row, 1), :] for j in range(3))
    ext = jnp.concatenate([pp_ref[0], pc_ref[0], pn_ref[0]], axis=0)
    diffs = _pool_means(ext, i)
    x = _token_tile(i, xc_ref, xl_ref)
    h = (_rms(x, g1_ref[...]) * (1.0 + scale) + shift).astype(BF16)
    t = jnp.tanh(jnp.dot(h, w_ref[0, :, OFF_GATE:], preferred_element_type=F32))

    z = yf_ref[0] + yb_ref[0] + dsk_ref[...] * u_ref[0]
    z = _gelu_tanh(z).astype(BF16)
    val = jnp.dot(z, wval_ref[...], preferred_element_type=F32)
    y_ssm = val * jnp.tanh(jnp.dot(z, wgate_ref[...], preferred_element_type=F32)) + val
    y_attn = jnp.dot(o_ref[0], wup_ref[...], preferred_element_type=F32)
    y_pool = jnp.concatenate(
        [jnp.dot(diffs[gi], pw_ref[gi], preferred_element_type=F32) for gi in range(len(POOL_WINDOWS))],
        axis=1) * ps_ref[...]

    m = (t[:, :D_MODEL] * y_ssm + t[:, D_MODEL:2 * D_MODEL] * y_attn + t[:, 2 * D_MODEL:] * y_pool
         + (y_ssm + y_attn + y_pool))
    xo = x + gate * jnp.dot(m.astype(BF16), wout_ref[...], preferred_element_type=F32)
    xo_ref[0] = xo
    xprev_ref[...] = xo


def _merge(xs, mod, g1, w_in, layer, yf, yb, u, dsk, wval, wgate, o, wup, pool_in, pw, ps, wout, g2, rw, rb):
    last = N_TILES - 1
    tok = lambda b, s: (b, jnp.minimum(s, last), 0)
    prev = lambda b, s: (b, jnp.maximum(s - 1, 0), 0)
    const2 = lambda b, s: (0, 0)
    const3 = lambda b, s: (0, 0, 0)
    per_tile = TOK_TILE // POOL_HALO
    n_halo = S_TOT // POOL_HALO
    return pl.pallas_call(
        _merge_kernel,
        grid=(BATCH, N_TILES + 1),
        in_specs=[
            *_token_specs(xs, last),
            pl.BlockSpec((6, 8, D_MODEL), const3),
            pl.BlockSpec((1, D_MODEL), const2),
            pl.BlockSpec((1, D_MODEL, IN_WIDTH), lambda b, s: (layer, 0, 0)),
            pl.BlockSpec((1, TOK_TILE, SSM_WIDTH), tok),
            pl.BlockSpec((1, TOK_TILE, SSM_WIDTH), tok),
            pl.BlockSpec((1, TOK_TILE, SSM_WIDTH), tok),
            pl.BlockSpec((1, SSM_WIDTH), const2),
            pl.BlockSpec((SSM_WIDTH, D_MODEL), const2),
            pl.BlockSpec((SSM_WIDTH, D_MODEL), const2),
            pl.BlockSpec((1, TOK_TILE, ATTN_WIDTH), tok),
            pl.BlockSpec((ATTN_WIDTH, D_MODEL), const2),
            pl.BlockSpec((1, TOK_TILE, POOL_WIDTH), tok),
            pl.BlockSpec((1, POOL_HALO, POOL_WIDTH),
                         lambda b, s: (b, jnp.maximum(jnp.minimum(s, last) * per_tile - 1, 0), 0)),
            pl.BlockSpec((1, POOL_HALO, POOL_WIDTH),
                         lambda b, s: (b, jnp.minimum((jnp.minimum(s, last) + 1) * per_tile, n_halo - 1), 0)),
            pl.BlockSpec((len(POOL_WINDOWS), POOL_GROUP, POOL_OUT_GROUP), const3),
            pl.BlockSpec((1, D_MODEL), const2),
            pl.BlockSpec((D_MODEL, D_MODEL), const2),
            pl.BlockSpec((1, D_MODEL), const2),
            pl.BlockSpec((D_MODEL, 2 * N_EXPERTS), const2),
            pl.BlockSpec((1, N_EXPERTS), const2),
        ],
        out_specs=[
            pl.BlockSpec((1, TOK_TILE, D_MODEL), tok),
            pl.BlockSpec((1, TOK_TILE, PACKED), prev),
            pl.BlockSpec((1, TOK_TILE, TOP_K), prev),
            pl.BlockSpec((1, TOK_TILE, TOP_K), prev),
        ],
        out_shape=[
            jax.ShapeDtypeStruct((BATCH, S_TOT, D_MODEL), F32),
            jax.ShapeDtypeStruct((BATCH, S_TOT, PACKED), jnp.uint32),
            jax.ShapeDtypeStruct((BATCH, S_TOT, TOP_K), jnp.int32),
            jax.ShapeDtypeStruct((BATCH, S_TOT, TOP_K), F32),
        ],
        scratch_shapes=[pltpu.VMEM((TOK_TILE, D_MODEL), F32)],
        compiler_params=pltpu.CompilerParams(
            dimension_semantics=("parallel", "arbitrary"), vmem_limit_bytes=VMEM_LIMIT),
        name="merge",
    )(*xs, mod, g1, w_in, yf, yb, u, dsk, wval, wgate, o, wup, pool_in, pool_in, pool_in, pw, ps, wout, g2, rw, rb)


def _route_kernel(te_ref, below_ref, dest_ref, be_ref, cnt_ref, carry_ref):
    ph = pl.program_id(0)
    i = pl.program_id(1)
    te = te_ref[...]
    lane = lax.broadcasted_iota(jnp.int32, (ROUTE_TILE, LANES), 1)
    ohs = [(te[:, k:k + 1] == lane).astype(F32) for k in range(TOP_K)]
    oh_all = ohs[0] + ohs[1] + ohs[2] + ohs[3]
    tile_cnt = jnp.sum(oh_all, axis=0, keepdims=True)

    @pl.when((ph == 0) & (i == 0))
    def _():
        cnt_ref[...] = jnp.zeros_like(cnt_ref)

    @pl.when(ph == 0)
    def _():
        cnt_ref[...] += tile_cnt

    @pl.when((ph == 1) & (i == 0))
    def _():
        cnt = jnp.broadcast_to(cnt_ref[...], (8, LANES))
        padded = jnp.floor((cnt + (ROW_BLOCK - 1)) / ROW_BLOCK) * ROW_BLOCK
        l8 = lax.broadcasted_iota(jnp.int32, (8, LANES), 1)
        end = padded
        s = 1
        while s < N_EXPERTS:
            end = end + jnp.where(l8 >= s, pltpu.roll(end, s, 1), 0.0)
            s *= 2
        carry_ref[...] = (end - padded)[0:1]
        blk_row = (lax.broadcasted_iota(jnp.int32, (N_BLOCKS_PAD, LANES), 0) * ROW_BLOCK).astype(F32)
        lb = lax.broadcasted_iota(jnp.int32, (N_BLOCKS_PAD, LANES), 1)
        hit = jnp.where((lb < N_EXPERTS) & (jnp.broadcast_to(end[0:1], blk_row.shape) <= blk_row), 1.0, 0.0)
        be = jnp.sum(hit, axis=-1, keepdims=True)
        be_ref[...] = jnp.broadcast_to(be, (N_BLOCKS_PAD, LANES)).astype(jnp.int32)

    @pl.when(ph == 1)
    def _():
        rank = carry_ref[...] + jnp.dot(below_ref[...], oh_all.astype(BF16), preferred_element_type=F32)
        dests = [jnp.sum(ohs[k] * rank, axis=-1, keepdims=True) for k in range(TOP_K)]
        dest_ref[...] = jnp.concatenate(dests, axis=1).astype(jnp.int32)
        carry_ref[...] += tile_cnt


def _route(top_e, group):
    n_steps = G_TOK // ROUTE_TILE
    return pl.pallas_call(
        _route_kernel,
        grid=(2, n_steps),
        in_specs=[pl.BlockSpec((ROUTE_TILE, TOP_K), lambda ph, i: (group * n_steps + i, 0)),
                  pl.BlockSpec((ROUTE_TILE, ROUTE_TILE), lambda ph, i: (0, 0))],
        out_specs=[
            pl.BlockSpec((ROUTE_TILE, TOP_K), lambda ph, i: (ph * i, 0)),
            pl.BlockSpec((N_BLOCKS_PAD, LANES), lambda ph, i: (0, 0)),
        ],
        out_shape=[
            jax.ShapeDtypeStruct((G_TOK, TOP_K), jnp.int32),
            jax.ShapeDtypeStruct((N_BLOCKS_PAD, LANES), jnp.int32),
        ],
        scratch_shapes=[pltpu.VMEM((1, LANES), F32), pltpu.VMEM((1, LANES), F32)],
        compiler_params=pltpu.CompilerParams(dimension_semantics=("arbitrary", "arbitrary")),
        name="route",
    )(top_e, jnp.asarray(np.tril(np.ones((ROUTE_TILE, ROUTE_TILE), np.float32), -1), BF16))


def _sc_mesh():
    return plsc.VectorSubcoreMesh(core_axis_name="c", subcore_axis_name="s")


def _sc_worker():
    return lax.axis_index("s") * SC_CORES + lax.axis_index("c")


def _dispatch(f, dest_t, group):
    per_w = G_TOK // SC_WORKERS
    n_chunks = per_w // SC_SCATTER_ROWS

    @functools.partial(
        pl.kernel,
        out_type=jax.ShapeDtypeStruct((N_ROWS, PACKED), jnp.uint32),
        mesh=_sc_mesh(),
        scratch_types=[
            pltpu.VMEM((SC_SCATTER_ROWS,), jnp.int32),
            pltpu.VMEM((SC_SCATTER_ROWS, PACKED), jnp.uint32),
            pltpu.SemaphoreType.DMA,
        ],
        name="dispatch",
    )
    def k(f_hbm, i_hbm, o_hbm, idx_v, rows_v, sem):
        base = _sc_worker() * per_w

        @pl.loop(0, n_chunks)
        def _(j):
            t0 = pl.multiple_of(base + j * SC_SCATTER_ROWS, 8)
            pltpu.sync_copy(f_hbm.at[pl.ds(group * G_TOK + t0, SC_SCATTER_ROWS)], rows_v)
            for kk in range(TOP_K):
                pltpu.sync_copy(i_hbm.at[pl.ds(kk * G_TOK + t0, SC_SCATTER_ROWS)], idx_v)
                pltpu.async_copy(rows_v, o_hbm.at[idx_v], sem).wait()

    return k(f, dest_t.reshape(TOP_K * G_TOK))


def _combine_gather(yb, idx):
    n = idx.shape[0]
    assert n % (SC_WORKERS * SC_GATHER_ROWS) == 0
    per_w = n // SC_WORKERS
    n_chunks = per_w // SC_GATHER_ROWS

    @functools.partial(
        pl.kernel,
        out_type=jax.ShapeDtypeStruct((n, PACKED), jnp.uint32),
        mesh=_sc_mesh(),
        scratch_types=[
            pltpu.VMEM((SC_GATHER_ROWS,), jnp.int32),
            pltpu.VMEM((SC_GATHER_ROWS, PACKED), jnp.uint32),
            pltpu.SemaphoreType.DMA,
        ],
        name="combine_gather",
    )
    def k(y_hbm, i_hbm, o_hbm, idx_v, rows_v, sem):
        base = _sc_worker() * per_w

        @pl.loop(0, n_chunks)
        def _(j):
            r0 = pl.multiple_of(base + j * SC_GATHER_ROWS, 8)
            pltpu.sync_copy(i_hbm.at[pl.ds(r0, SC_GATHER_ROWS)], idx_v)
            pltpu.async_copy(y_hbm.at[idx_v], rows_v, sem).wait()
            pltpu.sync_copy(rows_v, o_hbm.at[pl.ds(r0, SC_GATHER_ROWS)])

    return k(yb, idx)


W1_COPIES = 4
W2_COPIES = 2


def _weight_copies(w1_hbm, w2_hbm, w1f, w2f, sem, idx):
    copies = []
    for hbm, buf, n, first in ((w1_hbm, w1f, W1_COPIES, 0), (w2_hbm, w2f, W2_COPIES, W1_COPIES)):
        rows = buf.shape[0] // n
        for c in range(n):
            part = pl.ds(c * rows, rows)
            copies.append(pltpu.make_async_copy(hbm.at[idx, part], buf.at[part], sem.at[first + c]))
    return copies


def _expert_kernel(be_ref, x_ref, w1_hbm, b1_ref, w2_hbm, b2_ref, y_ref, w1f, w2f, w1b, w2b, sem, *, layer):
    copies = functools.partial(_weight_copies, w1_hbm, w2_hbm, w1f, w2f, sem)
    for a in range(EXPERT_STEP):
        j = pl.program_id(0) * EXPERT_STEP + a
        e = be_ref[j]
        prev = be_ref[jnp.maximum(j - 1, 0)]
        live = e < N_EXPERTS
        first = live & ((j == 0) | (e != prev))
        rows = slice(a * ROW_BLOCK, (a + 1) * ROW_BLOCK)

        if a == 0:
            @pl.when(first & (j == 0))
            def _():
                for c in copies(layer * N_EXPERTS + e):
                    c.start()

        @pl.when(first)
        def _():
            for c in copies(layer * N_EXPERTS + e):
                c.wait()
            w1b[...] = w1f[...].astype(BF16)
            w2b[...] = w2f[...].astype(BF16)
            nxt = lax.while_loop(
                lambda k: (k < N_BLOCKS) & (be_ref[jnp.minimum(k, N_BLOCKS - 1)] == e), lambda k: k + 1, j + 1)
            e_next = be_ref[jnp.minimum(nxt, N_BLOCKS - 1)]

            @pl.when((nxt < N_BLOCKS) & (e_next < N_EXPERTS))
            def _():
                for c in copies(layer * N_EXPERTS + e_next):
                    c.start()

        @pl.when(live)
        def _():
            idx = layer * N_EXPERTS + e
            x = jnp.concatenate(_unpack_rows(x_ref[rows, :]), axis=1).astype(BF16)
            gu = jnp.dot(x, w1b[...], preferred_element_type=F32) + b1_ref[idx]
            gate = jnp.minimum(gu[:, :D_EXPERT], SWIGLU_LIMIT)
            lin = jnp.clip(gu[:, D_EXPERT:], -SWIGLU_LIMIT, SWIGLU_LIMIT)
            act = gate * _sigmoid(SWIGLU_ALPHA * gate) * (lin + 1.0)
            y_ref[rows, :] = _pack_rows(
                jnp.dot(act.astype(BF16), w2b[...], preferred_element_type=F32) + b2_ref[idx])


def _experts(block_e, rows, w1, b1, w2, b2, layer):
    step_rows = EXPERT_STEP * ROW_BLOCK
    return pl.pallas_call(
        functools.partial(_expert_kernel, layer=layer),
        grid_spec=pltpu.PrefetchScalarGridSpec(
            num_scalar_prefetch=1,
            grid=(N_BLOCKS // EXPERT_STEP,),
            in_specs=[
                pl.BlockSpec((step_rows, PACKED), lambda j, be: (j, 0)),
                pl.BlockSpec(memory_space=pl.ANY),
                pl.BlockSpec((DEPTH * N_EXPERTS, 1, 2 * D_EXPERT), lambda j, be: (0, 0, 0)),
                pl.BlockSpec(memory_space=pl.ANY),
                pl.BlockSpec((DEPTH * N_EXPERTS, 1, D_MODEL), lambda j, be: (0, 0, 0)),
            ],
            out_specs=pl.BlockSpec((step_rows, PACKED), lambda j, be: (j, 0)),
            scratch_shapes=[
                pltpu.VMEM((D_MODEL, 2 * D_EXPERT), F32),
                pltpu.VMEM((D_EXPERT, D_MODEL), F32),
                pltpu.VMEM((D_MODEL, 2 * D_EXPERT), BF16),
                pltpu.VMEM((D_EXPERT, D_MODEL), BF16),
                pltpu.SemaphoreType.DMA((W1_COPIES + W2_COPIES,)),
            ],
        ),
        out_shape=jax.ShapeDtypeStruct((N_ROWS, PACKED), jnp.uint32),
        compiler_params=pltpu.CompilerParams(
            dimension_semantics=("arbitrary",), vmem_limit_bytes=VMEM_LIMIT),
        name="experts",
    )(block_e, rows, w1, b1, w2, b2)


def _final_kernel(x_ref, mod_ref, g_ref, p_ref, fg_ref, *rest, batch):
    o_ref = rest[-1]
    gate = mod_ref[5, pl.ds(batch, 1), :]
    o_ref[0] = _rms(x_ref[0] + gate * _routed_sum(g_ref, p_ref), fg_ref[...])


def _final(xa, mod, gathered, probs, final_g, batch, out=None):
    ctx_tiles = CTX_LEN // TOK_TILE
    tok = lambda i: (batch, i + ctx_tiles, 0)
    in_specs = [
        pl.BlockSpec((1, TOK_TILE, D_MODEL), tok),
        pl.BlockSpec((6, 8, D_MODEL), lambda i: (0, 0, 0)),
        pl.BlockSpec((TOP_K, 1, TOK_TILE, PACKED), lambda i: (0, 0, i + ctx_tiles, 0)),
        pl.BlockSpec((1, TOK_TILE, TOP_K), tok),
        pl.BlockSpec((1, D_MODEL), lambda i: (0, 0)),
    ]
    args = [xa, mod, gathered.reshape(TOP_K, 1, S_TOT, PACKED), probs, final_g]
    aliases = {}
    if out is not None:
        in_specs.append(pl.BlockSpec(memory_space=pl.ANY))
        args.append(out)
        aliases = {len(args) - 1: 0}
    return pl.pallas_call(
        functools.partial(_final_kernel, batch=batch),
        grid=(N_TILES - ctx_tiles,),
        in_specs=in_specs,
        out_specs=pl.BlockSpec((1, TOK_TILE, D_MODEL), lambda i: (batch, i, 0)),
        out_shape=jax.ShapeDtypeStruct((BATCH, SEQ, D_MODEL), F32),
        input_output_aliases=aliases,
        compiler_params=pltpu.CompilerParams(dimension_semantics=("parallel",)),
        name="final",
    )(*args)


def _split_bf16(w):
    bits = lax.bitcast_convert_type(w.astype(F32), jnp.uint32) & jnp.uint32(0xFFFF0000)
    hi = lax.bitcast_convert_type(bits, F32)
    return jnp.concatenate([hi.astype(BF16), (w - hi).astype(BF16)], axis=-1)


def _rope_tables():
    t = np.arange(SEQ)
    axis_dim = HEAD_DIM // 2
    inv_freq = (1.0 / (ROPE_BASE ** (np.arange(0, axis_dim, 2, dtype=np.float32) / axis_dim))).astype(np.float32)
    ang_r = (t // GRID_W).astype(np.float32)[:, None] * inv_freq[None]
    ang_c = (t % GRID_W).astype(np.float32)[:, None] * inv_freq[None]
    ang = np.concatenate([ang_r, ang_r, ang_c, ang_c] * (LANES // HEAD_DIM), axis=-1)
    cos = np.concatenate([np.ones((CTX_LEN, LANES), np.float32), np.cos(ang)], axis=0)
    sin = np.concatenate([np.zeros((CTX_LEN, LANES), np.float32), np.sin(ang)], axis=0)
    return jnp.asarray(cos, F32), jnp.asarray(sin, F32)


def kernel(x, c, ctx, c_ctx, w_ada, b_ada, norm1_g, norm2_g, w_in, ssm_lam_re, ssm_lam_im, ssm_log_dt, ssm_b_re, ssm_b_im, ssm_c_re, ssm_c_im, ssm_d, ssm_w_val, ssm_w_gate, attn_sink, attn_w_up, pool_w, pool_scale, w_out, router_w, router_b, exp_w1, exp_b1, exp_w2, exp_b2, final_g):
    assert x.shape == (BATCH, SEQ, D_MODEL) and ctx.shape == (BATCH, CTX_LEN, D_MODEL)
    cos_t, sin_t = _rope_tables()
    cv = jnp.concatenate([c, c_ctx[None], jnp.zeros((8 - BATCH - 1, D_MODEL), F32)], axis=0)
    mods = _ada(cv, w_ada, b_ada)
    xs = (ctx, x)
    ew1 = exp_w1.reshape(DEPTH * N_EXPERTS, D_MODEL, 2 * D_EXPERT)
    eb1 = exp_b1.reshape(DEPTH * N_EXPERTS, 1, 2 * D_EXPERT)
    ew2 = exp_w2.reshape(DEPTH * N_EXPERTS, D_EXPERT, D_MODEL)
    eb2 = exp_b2.reshape(DEPTH * N_EXPERTS, 1, D_MODEL)
    half_gate = jnp.where(jnp.arange(IN_WIDTH) >= OFF_GATE, 0.5, 1.0).astype(F32)
    w_in_b = (w_in * half_gate).astype(BF16)
    ssm_tables = _ssm_params(ssm_lam_re, ssm_lam_im, ssm_log_dt, ssm_b_re, ssm_b_im, ssm_c_re, ssm_c_im)
    proj = _inproj(xs, mods[0], norm1_g[0][None], w_in_b, 0, cos_t, sin_t)
    out = None
    for l in range(DEPTH):
        mod = mods[l]
        u, kv, q, pool_in = proj[:4]
        if len(proj) > 4:
            xs = (proj[4], proj[4])
        yf, yb = _ssm(u, *ssm_tables, l)
        o = _attn(attn_sink[l], q, kv)
        xa, f, top_e, top_p = _merge(
            xs, mod, norm1_g[l][None], w_in_b, l, yf, yb, u, ssm_d[l][None],
            (0.5 * ssm_w_val[l]).astype(BF16), (0.5 * ssm_w_gate[l]).astype(BF16), o, attn_w_up[l].astype(BF16),
            pool_in, pool_w[l].astype(BF16), pool_scale[l][None], (0.5 * w_out[l]).astype(BF16), norm2_g[l][None],
            _split_bf16(router_w[l]), router_b[l][None])
        proj = None
        for b in range(BATCH):
            dest, block_e = _route(top_e.reshape(N_TOK, TOP_K), b)
            dest_t = dest.T
            rows = _dispatch(f.reshape(N_TOK, PACKED), dest_t, b)
            yrows = _experts(block_e[:N_BLOCKS, 0], rows, ew1, eb1, ew2, eb2, l)
            gathered = _combine_gather(yrows, dest_t.reshape(-1))
            if l < DEPTH - 1:
                proj = _inproj((xa, xa), mods[l + 1], norm1_g[l + 1][None], w_in_b, l + 1, cos_t, sin_t,
                               (mod, gathered, top_p), b, proj)
            else:
                out = _final(xa, mod, gathered, top_p, final_g[None], b, out)
    return out
```

```python
import functools
import math

import jax
import jax.numpy as jnp
import numpy as np
from jax import lax
from jax.experimental import pallas as pl
from jax.experimental.pallas import tpu as pltpu
from jax.experimental.pallas import tpu_sc as plsc

F32 = jnp.float32
BF16 = jnp.bfloat16
HIGHEST = lax.Precision.HIGHEST

D_MODEL = 1024
BATCH = 2
SEQ = 8192
DEPTH = 2
GRID_W = 64
CTX_LEN = 256
NORM_EPS = 1e-6
SSM_WIDTH = 512
SSM_GROUP = 16
SSM_GROUPS = 32
SSM_STATE = 64
HEAD_DIM = 64
N_Q_HEADS = 8
N_KV_HEADS = 2
GQA_GROUP = 4
ATTN_WIDTH = 512
KV_WIDTH = 128
WINDOW = 128
ROPE_BASE = 10000.0
ATTN_SCALE = HEAD_DIM ** -0.5
POOL_WINDOWS = (2, 4, 8, 16)
POOL_WIDTH = 512
POOL_GROUP = 128
POOL_OUT_GROUP = 256
OFF_K = 512
OFF_V = 640
OFF_Q = 768
OFF_POOL = 1280
OFF_GATE = 1792
IN_WIDTH = OFF_GATE + 3 * D_MODEL
N_EXPERTS = 32
TOP_K = 4
D_EXPERT = 1024
SWIGLU_LIMIT = 7.0
SWIGLU_ALPHA = 1.702

S_TOT = CTX_LEN + SEQ
N_TOK = BATCH * S_TOT
G_TOK = S_TOT
TOK_TILE = 256
N_TILES = S_TOT // TOK_TILE
assert CTX_LEN == TOK_TILE
ATTN_BLOCK = 128
ATTN_STEP = 3
POOL_HALO = 8
ROUTE_TILE = 1056
ROW_BLOCK = 256
EXPERT_STEP = 4
N_ROWS = -(-(G_TOK * TOP_K + N_EXPERTS * (ROW_BLOCK - 1)) // (ROW_BLOCK * EXPERT_STEP)) * ROW_BLOCK * EXPERT_STEP
N_BLOCKS = N_ROWS // ROW_BLOCK
N_BLOCKS_PAD = -(-N_BLOCKS // 8) * 8
SC_CORES = 2
SC_WORKERS = SC_CORES * 16
PACKED = D_MODEL // 2
SC_SCATTER_ROWS = 88
SC_GATHER_ROWS = 88
assert G_TOK % (SC_WORKERS * SC_SCATTER_ROWS) == 0 and G_TOK % ROUTE_TILE == 0
VMEM_LIMIT = 56 * 1024 * 1024
LANES = 128
MASK_VALUE = -1e30


def _rms(x, g):
    return x * lax.rsqrt(jnp.mean(x * x, axis=-1, keepdims=True) + NORM_EPS) * g


def _sigmoid(x):
    return 0.5 * jnp.tanh(0.5 * x) + 0.5


def _pack_rows(x):
    half = x.shape[1] // 2
    lo = lax.bitcast_convert_type(x[:, :half].astype(BF16).astype(F32), jnp.uint32)
    hi = lax.bitcast_convert_type(x[:, half:].astype(BF16).astype(F32), jnp.uint32)
    return (lo >> 16) | hi


def _unpack_rows(p):
    lo = lax.bitcast_convert_type(p << 16, F32)
    hi = lax.bitcast_convert_type(p & jnp.uint32(0xFFFF0000), F32)
    return lo, hi


def _token_specs(xs, last=N_TILES - 1, im=lambda f: f):
    ctx_tiles = CTX_LEN // TOK_TILE
    off = 0 if xs[1].shape[1] == S_TOT else ctx_tiles
    return [pl.BlockSpec((1, TOK_TILE, D_MODEL), im(lambda b, i: (b, 0, 0))),
            pl.BlockSpec((1, TOK_TILE, D_MODEL), im(lambda b, i: (b, jnp.clip(i, ctx_tiles, last) - off, 0)))]


def _token_tile(i, xc_ref, xl_ref):
    return jnp.where(i == 0, xc_ref[0], xl_ref[0])


def _mod_row(i, b):
    return jnp.where(i == 0, 2, b)


def _ada_kernel(cv_ref, w_ref, b_ref, o_ref):
    cv = cv_ref[...]
    s = cv * jax.nn.sigmoid(cv)
    o_ref[0, 0] = jnp.dot(s, w_ref[0], preferred_element_type=F32, precision=HIGHEST) + b_ref[0, 0]


def _ada(cv, w_ada, b_ada):
    return pl.pallas_call(
        _ada_kernel,
        grid=(DEPTH, 6),
        in_specs=[
            pl.BlockSpec((8, D_MODEL), lambda l, j: (0, 0)),
            pl.BlockSpec((1, D_MODEL, D_MODEL), lambda l, j: (l, 0, j)),
            pl.BlockSpec((1, 1, 1, D_MODEL), lambda l, j: (l, j, 0, 0)),
        ],
        out_specs=pl.BlockSpec((1, 1, 8, D_MODEL), lambda l, j: (l, j, 0, 0)),
        out_shape=jax.ShapeDtypeStruct((DEPTH, 6, 8, D_MODEL), F32),
        name="ada",
    )(cv, w_ada, b_ada.reshape(DEPTH, 6, 1, D_MODEL))


def _routed_sum(g_ref, p_ref):
    p = p_ref[0]
    lo, hi = _unpack_rows(g_ref[0, 0])
    y_lo = p[:, 0:1] * lo
    y_hi = p[:, 0:1] * hi
    for k in range(1, TOP_K):
        lo, hi = _unpack_rows(g_ref[k, 0])
        y_lo = y_lo + p[:, k:k + 1] * lo
        y_hi = y_hi + p[:, k:k + 1] * hi
    return jnp.concatenate([y_lo, y_hi], axis=1)


def _rope(t, cos, sin):
    lane = lax.broadcasted_iota(jnp.int32, t.shape, 1)
    rot = jnp.where((lane % 32) < 16, -pltpu.roll(t, LANES - 16, 1), pltpu.roll(t, 16, 1))
    return t * cos + rot * sin


def _inproj_kernel(*refs, pending, batch):
    xc_ref, xl_ref, mod_ref, g_ref, w_ref, cos_ref, sin_ref = refs[:7]
    if pending:
        pmod_ref, rows_ref, probs_ref = refs[7:10]
        u_ref, kv_ref, q_ref, pool_ref, xo_ref = refs[-5:]
    else:
        u_ref, kv_ref, q_ref, pool_ref = refs[-4:]
    b, i = (pl.program_id(0), pl.program_id(1)) if batch is None else (batch, pl.program_id(0))
    row = _mod_row(i, b)
    shift = mod_ref[0, pl.ds(row, 1), :]
    scale = mod_ref[1, pl.ds(row, 1), :]
    x = _token_tile(i, xc_ref, xl_ref)
    if pending:
        x = x + pmod_ref[5, pl.ds(row, 1), :] * _routed_sum(rows_ref, probs_ref)
        xo_ref[0] = x
    h = _rms(x, g_ref[...]) * (1.0 + scale) + shift
    p = jnp.dot(h.astype(BF16), w_ref[0, :, :OFF_GATE], preferred_element_type=F32)
    cos = cos_ref[...]
    sin = sin_ref[...]
    u_ref[0] = p[:, :OFF_K]
    kv_ref[0, :, :KV_WIDTH] = _rope(p[:, OFF_K:OFF_V], cos, sin).astype(BF16)
    kv_ref[0, :, KV_WIDTH:] = p[:, OFF_V:OFF_Q].astype(BF16)
    for c in range(ATTN_WIDTH // LANES):
        qc = p[:, OFF_Q + c * LANES:OFF_Q + (c + 1) * LANES]
        q_ref[0, :, c * LANES:(c + 1) * LANES] = (_rope(qc, cos, sin) * ATTN_SCALE).astype(BF16)
    pool_ref[0] = p[:, OFF_POOL:OFF_GATE]


def _inproj(xs, mod, g1, w_in, layer, cos_t, sin_t, pending=None, batch=None, outs=None):
    im = (lambda f: f) if batch is None else (lambda f: lambda i: f(batch, i))
    tok = im(lambda b, i: (b, i, 0))
    in_specs = [
        *_token_specs(xs, im=im),
        pl.BlockSpec((6, 8, D_MODEL), im(lambda b, i: (0, 0, 0))),
        pl.BlockSpec((1, D_MODEL), im(lambda b, i: (0, 0))),
        pl.BlockSpec((1, D_MODEL, IN_WIDTH), im(lambda b, i: (layer, 0, 0))),
        pl.BlockSpec((TOK_TILE, LANES), im(lambda b, i: (i, 0))),
        pl.BlockSpec((TOK_TILE, LANES), im(lambda b, i: (i, 0))),
    ]
    out_specs = [
        pl.BlockSpec((1, TOK_TILE, SSM_WIDTH), tok),
        pl.BlockSpec((1, TOK_TILE, 2 * KV_WIDTH), tok),
        pl.BlockSpec((1, TOK_TILE, ATTN_WIDTH), tok),
        pl.BlockSpec((1, TOK_TILE, POOL_WIDTH), tok),
    ]
    out_shape = [
        jax.ShapeDtypeStruct((BATCH, S_TOT, SSM_WIDTH), F32),
        jax.ShapeDtypeStruct((BATCH, S_TOT, 2 * KV_WIDTH), BF16),
        jax.ShapeDtypeStruct((BATCH, S_TOT, ATTN_WIDTH), BF16),
        jax.ShapeDtypeStruct((BATCH, S_TOT, POOL_WIDTH), F32),
    ]
    args = [*xs, mod, g1, w_in, cos_t, sin_t]
    if pending is not None:
        pmod, gathered, probs = pending
        in_specs += [
            pl.BlockSpec((6, 8, D_MODEL), im(lambda b, i: (0, 0, 0))),
            pl.BlockSpec((TOP_K, 1, TOK_TILE, PACKED), im(lambda b, i: (0, 0, i, 0))),
            pl.BlockSpec((1, TOK_TILE, TOP_K), tok),
        ]
        out_specs.append(pl.BlockSpec((1, TOK_TILE, D_MODEL), tok))
        out_shape.append(jax.ShapeDtypeStruct((BATCH, S_TOT, D_MODEL), F32))
        args += [pmod, gathered.reshape(TOP_K, 1, S_TOT, PACKED), probs]
    aliases = {}
    if outs is not None:
        aliases = {len(args) + k: k for k in range(len(outs))}
        in_specs += [pl.BlockSpec(memory_space=pl.ANY)] * len(outs)
        args += list(outs)
    return pl.pallas_call(
        functools.partial(_inproj_kernel, pending=pending is not None, batch=batch),
        grid=(BATCH, N_TILES) if batch is None else (N_TILES,),
        in_specs=in_specs,
        out_specs=out_specs,
        out_shape=out_shape,
        input_output_aliases=aliases,
        compiler_params=pltpu.CompilerParams(
            dimension_semantics=("parallel",) * (2 if batch is None else 1), vmem_limit_bytes=VMEM_LIMIT),
        name="inproj",
    )(*args)


SSM_HALF_IN = SSM_WIDTH // 2
SSM_HALF_ST = SSM_GROUPS // 2 * SSM_STATE
N_STATE_ROWS = 8
N_LANE_TILES = SSM_HALF_ST // LANES
SCAN_STRIDE = TOK_TILE + 8


def _ssm_kernel(uf_ref, ub_ref, wb_ref, wc_ref, ar_ref, ai_ref, yf_ref, yb_ref, buf_f, buf_b, hf_ref, hb_ref):
    i = pl.program_id(0)
    T = TOK_TILE

    @pl.when(i == 0)
    def _():
        hf_ref[...] = jnp.zeros_like(hf_ref)
        hb_ref[...] = jnp.zeros_like(hb_ref)

    for d, (u_ref, buf) in enumerate(((uf_ref, buf_f), (ub_ref, buf_b))):
        for b in range(BATCH):
            for half in range(2):
                ub16 = u_ref[b, :, half * SSM_HALF_IN:(half + 1) * SSM_HALF_IN].astype(BF16)
                for reim in range(2):
                    k = reim * 4 + half * 2 + b
                    bu = jnp.dot(ub16, wb_ref[d, reim, half], preferred_element_type=F32)
                    for l in range(N_LANE_TILES):
                        buf[l, k * SCAN_STRIDE:k * SCAN_STRIDE + T, :] = bu[:, l * LANES:(l + 1) * LANES]

    def step(t, carry):
        hf, hb = carry
        tb = T - 1 - t
        nf, nb = [], []
        for l in range(N_LANE_TILES):
            sl = pl.ds(l * LANES, LANES)
            rows_f = pl.ds(t, N_STATE_ROWS, stride=SCAN_STRIDE)
            h = ar_ref[0, :, sl] * hf[l] + ai_ref[0, :, sl] * pltpu.roll(hf[l], 4, 0) + buf_f[l, rows_f, :]
            buf_f[l, rows_f, :] = h
            nf.append(h)
            rows_b = pl.ds(tb, N_STATE_ROWS, stride=SCAN_STRIDE)
            h = ar_ref[1, :, sl] * hb[l] + ai_ref[1, :, sl] * pltpu.roll(hb[l], 4, 0) + buf_b[l, rows_b, :]
            buf_b[l, rows_b, :] = h
            nb.append(h)
        return tuple(nf), tuple(nb)

    hf0 = tuple(hf_ref[:, l * LANES:(l + 1) * LANES] for l in range(N_LANE_TILES))
    hb0 = tuple(hb_ref[:, l * LANES:(l + 1) * LANES] for l in range(N_LANE_TILES))
    hf1, hb1 = lax.fori_loop(0, T, step, (hf0, hb0), unroll=2)
    for l in range(N_LANE_TILES):
        hf_ref[:, l * LANES:(l + 1) * LANES] = hf1[l]
        hb_ref[:, l * LANES:(l + 1) * LANES] = hb1[l]

    for d, (buf, y_ref) in enumerate(((buf_f, yf_ref), (buf_b, yb_ref))):
        for b in range(BATCH):
            for half in range(2):
                kre = half * 2 + b
                kim = 4 + half * 2 + b
                state = lambda k: jnp.concatenate(
                    [buf[l, k * SCAN_STRIDE:k * SCAN_STRIDE + T, :] for l in range(N_LANE_TILES)],
                    axis=1).astype(BF16)
                y = jnp.dot(state(kre), wc_ref[d, 0, half], preferred_element_type=F32)
                y = y + jnp.dot(state(kim), wc_ref[d, 1, half], preferred_element_type=F32)
                y_ref[b, :, half * SSM_HALF_IN:(half + 1) * SSM_HALF_IN] = y


def _ssm(u, wb, wc, ar, ai, layer):
    fwd = lambda i: (0, i, 0)
    bwd = lambda i: (0, jnp.where(i == 0, 0, N_TILES - i), 0)
    return pl.pallas_call(
        _ssm_kernel,
        grid=(N_TILES,),
        in_specs=[
            pl.BlockSpec((BATCH, TOK_TILE, SSM_WIDTH), fwd),
            pl.BlockSpec((BATCH, TOK_TILE, SSM_WIDTH), bwd),
            pl.BlockSpec((None, 2, 2, 2, SSM_HALF_IN, SSM_HALF_ST), lambda i: (layer, 0, 0, 0, 0, 0)),
            pl.BlockSpec((None, 2, 2, 2, SSM_HALF_ST, SSM_HALF_IN), lambda i: (layer, 0, 0, 0, 0, 0)),
            pl.BlockSpec((None, 2, N_STATE_ROWS, SSM_HALF_ST), lambda i: (layer, 0, 0, 0)),
            pl.BlockSpec((None, 2, N_STATE_ROWS, SSM_HALF_ST), lambda i: (layer, 0, 0, 0)),
        ],
        out_specs=[
            pl.BlockSpec((BATCH, TOK_TILE, SSM_WIDTH), fwd),
            pl.BlockSpec((BATCH, TOK_TILE, SSM_WIDTH), bwd),
        ],
        out_shape=[jax.ShapeDtypeStruct((BATCH, S_TOT, SSM_WIDTH), F32)] * 2,
        scratch_shapes=[
            pltpu.VMEM((N_LANE_TILES, N_STATE_ROWS * SCAN_STRIDE, LANES), F32),
            pltpu.VMEM((N_LANE_TILES, N_STATE_ROWS * SCAN_STRIDE, LANES), F32),
            pltpu.VMEM((N_STATE_ROWS, SSM_HALF_ST), F32),
            pltpu.VMEM((N_STATE_ROWS, SSM_HALF_ST), F32),
        ],
        compiler_params=pltpu.CompilerParams(
            dimension_semantics=("arbitrary",), vmem_limit_bytes=VMEM_LIMIT),
        name="ssm",
    )(u, u, wb, wc, ar, ai)


def _ssm_params(lam_re, lam_im, log_dt, b_re, b_im, c_re, c_im):
    lr = lam_re.astype(F32)
    li = lam_im.astype(F32)
    dt = jnp.exp(log_dt.astype(F32))[..., None]
    mag = jnp.exp(lr * dt)
    ang = li * dt
    a_re = mag * jnp.cos(ang)
    a_im = mag * jnp.sin(ang)
    den = lr * lr + li * li
    k_re = ((a_re - 1.0) * lr + a_im * li) / den
    k_im = (a_im * lr - (a_re - 1.0) * li) / den
    bb_re = k_re[..., None] * b_re - k_im[..., None] * b_im
    bb_im = k_re[..., None] * b_im + k_im[..., None] * b_re
    n_g = SSM_GROUPS // 2
    lead = lam_re.shape[:2]
    drive_mask = np.equal.outer(np.arange(SSM_HALF_IN) // SSM_GROUP, np.arange(SSM_HALF_ST) // SSM_STATE)
    repeat_st = np.tile(np.eye(SSM_STATE, dtype=np.float32), (1, n_g))
    repeat_in = np.tile(np.eye(SSM_GROUP, dtype=np.float32), (1, n_g))

    def block_diag(parts, rows_per_half, repeat, mask):
        rows = jnp.stack([jnp.swapaxes(p, -1, -2) for p in parts], axis=2)
        rows = rows.reshape(*lead, 2, 2, rows_per_half, repeat.shape[0])
        full = jnp.einsum('...rp,pc->...rc', rows, repeat, precision=HIGHEST)
        return (full * mask.astype(np.float32)).astype(BF16)

    wb = block_diag((bb_re, bb_im), SSM_HALF_IN, repeat_st, drive_mask)
    wc = block_diag((c_re.astype(F32), -c_im.astype(F32)), SSM_HALF_ST, repeat_in, drive_mask.T)

    def rows(a, sign_re):
        halves = a.reshape(*lead, 1, 2, 1, SSM_HALF_ST)
        signs = jnp.asarray([sign_re, 1.0], F32).reshape(2, 1, 1, 1)
        return jnp.broadcast_to(halves * signs, (*lead, 2, 2, BATCH, SSM_HALF_ST)).reshape(
            *lead, N_STATE_ROWS, SSM_HALF_ST)

    return wb, wc, rows(a_re, 1.0), rows(a_im, -1.0)


def _attn_block(n, q, kv_band, kv_ctx, sink_ref, band_bias):
    kv_all = jnp.concatenate([kv_band, kv_ctx], axis=0)
    nb = 3 * ATTN_BLOCK
    last = SEQ // ATTN_BLOCK - 1
    krow = lax.broadcasted_iota(jnp.int32, (nb, 1), 0)
    lo_r = jnp.where(n < 0, nb, jnp.where(n == 0, ATTN_BLOCK, 0))
    hi_r = jnp.where(n == last, 2 * ATTN_BLOCK, nb)
    bias = band_bias + jnp.where((krow >= lo_r) & (krow < hi_r), 0.0, MASK_VALUE)
    bias = jnp.concatenate([bias] * GQA_GROUP, axis=1)
    nt = (((1,), (1,)), ((), ()))
    outs = []
    for hk in range(N_KV_HEADS):
        k = kv_all[:, hk * HEAD_DIM:(hk + 1) * HEAD_DIM]
        v_t = kv_all[:, KV_WIDTH + hk * HEAD_DIM:KV_WIDTH + (hk + 1) * HEAD_DIM].astype(F32).T.astype(BF16)
        heads = range(hk * GQA_GROUP, (hk + 1) * GQA_GROUP)
        qs = jnp.concatenate([q[:, hq * HEAD_DIM:(hq + 1) * HEAD_DIM] for hq in heads], axis=0)
        sink = jnp.concatenate([jnp.full((1, ATTN_BLOCK), sink_ref[hq], F32) for hq in heads], axis=1)
        s_t = lax.dot_general(k, qs, nt, preferred_element_type=F32)
        s_w = s_t[:nb] + bias
        s_c = s_t[nb:]
        m = jnp.maximum(jnp.maximum(jnp.max(s_w, axis=0, keepdims=True),
                                    jnp.max(s_c, axis=0, keepdims=True)), sink)
        e_w = jnp.exp(s_w - m)
        e_c = jnp.exp(s_c - m)
        den = jnp.sum(e_w, axis=0, keepdims=True) + jnp.sum(e_c, axis=0, keepdims=True) + jnp.exp(sink - m)
        e = jnp.concatenate([e_w, e_c], axis=0).astype(BF16)
        o_t = jnp.dot(v_t, e, preferred_element_type=F32) / den
        outs.extend(o_t[:, g * ATTN_BLOCK:(g + 1) * ATTN_BLOCK].T for g in range(GQA_GROUP))
    return jnp.concatenate(outs, axis=1).astype(BF16)


def _attn_kernel(sink_ref, band_ref, q_ref, kc_ref, *rest):
    kv_refs, o_ref = rest[:-1], rest[-1]
    j0 = pl.program_id(1) * ATTN_STEP
    for a in range(ATTN_STEP):
        n = j0 + a - CTX_LEN // ATTN_BLOCK
        band = jnp.concatenate([kv_refs[a + d][0] for d in range(3)], axis=0)
        rows = slice(a * ATTN_BLOCK, (a + 1) * ATTN_BLOCK)
        o_ref[0, rows, :] = _attn_block(n, q_ref[0, rows, :], band, kc_ref[0], sink_ref, band_ref[...])


def _attn(sink, q, kv):
    nq = S_TOT // ATTN_BLOCK
    first = CTX_LEN // ATTN_BLOCK
    r = jnp.arange(ATTN_BLOCK)[None, :]
    c = jnp.arange(3 * ATTN_BLOCK)[:, None] - ATTN_BLOCK
    band_bias = jnp.where(jnp.abs(c - r) <= WINDOW, 0.0, MASK_VALUE).astype(F32)

    def kv_spec(d):
        return pl.BlockSpec((1, ATTN_BLOCK, 2 * KV_WIDTH),
                            lambda b, j: (b, jnp.clip(j * ATTN_STEP + d - 1, first, nq - 1), 0))

    return pl.pallas_call(
        _attn_kernel,
        grid=(BATCH, nq // ATTN_STEP),
        in_specs=[
            pl.BlockSpec(memory_space=pltpu.SMEM),
            pl.BlockSpec((3 * ATTN_BLOCK, ATTN_BLOCK), lambda b, j: (0, 0)),
            pl.BlockSpec((1, ATTN_STEP * ATTN_BLOCK, ATTN_WIDTH), lambda b, j: (b, j, 0)),
            pl.BlockSpec((1, CTX_LEN, 2 * KV_WIDTH), lambda b, j: (b, 0, 0)),
            *[kv_spec(d) for d in range(ATTN_STEP + 2)],
        ],
        out_specs=pl.BlockSpec((1, ATTN_STEP * ATTN_BLOCK, ATTN_WIDTH), lambda b, j: (b, j, 0)),
        out_shape=jax.ShapeDtypeStruct((BATCH, S_TOT, ATTN_WIDTH), BF16),
        compiler_params=pltpu.CompilerParams(dimension_semantics=("parallel", "parallel")),
        name="attn",
    )(sink, band_bias, q, kv, *[kv] * (ATTN_STEP + 2))


def _gelu_tanh(x):
    return 0.5 * x * (1.0 + jnp.tanh(math.sqrt(2.0 / math.pi) * (x + 0.044715 * (x * x * x))))


def _pool_means(ext, i):
    T = TOK_TILE
    n_ext = T + 2 * POOL_HALO
    seq_len = jnp.where(i == 0, CTX_LEN, SEQ)
    start = jnp.where(i == 0, 0, (i - 1) * T)
    r = start - POOL_HALO + lax.broadcasted_iota(jnp.int32, (n_ext, 1), 0)
    ext = jnp.where((r >= 0) & (r < seq_len), ext, 0.0)
    tl = start + lax.broadcasted_iota(jnp.int32, (T, 1), 0)
    back = lambda a, s: pltpu.roll(a, s, 0)
    fwd = lambda a, s: pltpu.roll(a, n_ext - s, 0)
    diffs = []
    for gi, w in enumerate(POOL_WINDOWS):
        e = ext[:, gi * POOL_GROUP:(gi + 1) * POOL_GROUP]
        lo = w // 2
        hi = w - 1 - lo
        s = e + back(e, 1)
        span = 2
        while span < w:
            s = s + back(s, span)
            span *= 2
        if w > 2:
            s = fwd(s, hi)
        cnt = (jnp.clip(tl + hi + 1, 0, seq_len) - jnp.clip(tl - lo, 0, seq_len)).astype(F32)
        mean = s[POOL_HALO:POOL_HALO + T] / cnt
        diffs.append((mean - e[POOL_HALO:POOL_HALO + T]).astype(BF16))
    return diffs


def _merge_kernel(xc_ref, xl_ref, mod_ref, g1_ref, w_ref, yf_ref, yb_ref, u_ref, dsk_ref, wval_ref, wgate_ref,
                  o_ref, wup_ref, pc_ref, pp_ref, pn_ref, pw_ref, ps_ref, wout_ref, g2_ref, rw_ref, rb_ref,
                  xo_ref, f_ref, te_ref, tp_ref, cnt_ref, xprev_ref):
    b = pl.program_id(0)
    s = pl.program_id(1)

    @pl.when(s == 0)
    def _():
        xprev_ref[...] = jnp.zeros_like(xprev_ref)

    rowp = _mod_row(jnp.maximum(s - 1, 0), b)
    f = _rms(xprev_ref[...], g2_ref[...]) * (1.0 + mod_ref[4, pl.ds(rowp, 1), :]) + mod_ref[3, pl.ds(rowp, 1), :]
    f_ref[0] = _pack_rows(f)
    f_hi = f.astype(BF16)
    f_lo = (f - f_hi.astype(F32)).astype(BF16)
    r_hi = jnp.dot(f_hi, rw_ref[...], preferred_element_type=F32)
    r_lo = jnp.dot(f_lo, rw_ref[...], preferred_element_type=F32)
    logits = r_hi[:, :N_EXPERTS] + r_hi[:, N_EXPERTS:] + r_lo[:, :N_EXPERTS] + rb_ref[...]
    lane = lax.broadcasted_iota(jnp.int32, logits.shape, 1)
    vals, idxs = [], []
    for _ in range(TOP_K):
        mx = jnp.max(logits, axis=-1, keepdims=True)
        ix = jnp.min(jnp.where(logits == mx, lane, N_EXPERTS), axis=-1, keepdims=True)
        vals.append(mx)
        idxs.append(ix)
        logits = jnp.where(lane == ix, -jnp.inf, logits)
    ex = [jnp.exp(v - vals[0]) for v in vals]
    tot = ex[0] + ex[1] + ex[2] + ex[3]
    te_ref[0] = jnp.concatenate(idxs, axis=1)
    tp_ref[0] = jnp.concatenate(ex, axis=1) / tot
    chosen = sum((lane == ix).astype(F32) for ix in idxs)
    cnt_ref[0, 0] = jnp.sum(chosen, axis=0, keepdims=True)

    i = jnp.minimum(s, N_TILES - 1)
    row = _mod_row(i, b)
    shift, scale, gate = (mod_ref[j, pl.ds(row, 1), :] for j in range(3))
    ext = jnp.concatenate([pp_ref[0], pc_ref[0], pn_ref[0]], axis=0)
    diffs = _pool_means(ext, i)
    x = _token_tile(i, xc_ref, xl_ref)
    h = (_rms(x, g1_ref[...]) * (1.0 + scale) + shift).astype(BF16)
    t = jnp.tanh(jnp.dot(h, w_ref[0, :, OFF_GATE:], preferred_element_type=F32))

    z = yf_ref[0] + yb_ref[0] + dsk_ref[...] * u_ref[0]
    z = _gelu_tanh(z).astype(BF16)
    val = jnp.dot(z, wval_ref[...], preferred_element_type=F32)
    y_ssm = val * jnp.tanh(jnp.dot(z, wgate_ref[...], preferred_element_type=F32)) + val
    y_attn = jnp.dot(o_ref[0], wup_ref[...], preferred_element_type=F32)
    y_pool = jnp.concatenate(
        [jnp.dot(diffs[gi], pw_ref[gi], preferred_element_type=F32) for gi in range(len(POOL_WINDOWS))],
        axis=1) * ps_ref[...]

    m = (t[:, :D_MODEL] * y_ssm + t[:, D_MODEL:2 * D_MODEL] * y_attn + t[:, 2 * D_MODEL:] * y_pool
         + (y_ssm + y_attn + y_pool))
    xo = x + gate * jnp.dot(m.astype(BF16), wout_ref[...], preferred_element_type=F32)
    xo_ref[0] = xo
    xprev_ref[...] = xo


def _merge(xs, mod, g1, w_in, layer, yf, yb, u, dsk, wval, wgate, o, wup, pool_in, pw, ps, wout, g2, rw, rb):
    last = N_TILES - 1
    tok = lambda b, s: (b, jnp.minimum(s, last), 0)
    prev = lambda b, s: (b, jnp.maximum(s - 1, 0), 0)
    const2 = lambda b, s: (0, 0)
    const3 = lambda b, s: (0, 0, 0)
    per_tile = TOK_TILE // POOL_HALO
    n_halo = S_TOT // POOL_HALO
    return pl.pallas_call(
        _merge_kernel,
        grid=(BATCH, N_TILES + 1),
        in_specs=[
            *_token_specs(xs, last),
            pl.BlockSpec((6, 8, D_MODEL), const3),
            pl.BlockSpec((1, D_MODEL), const2),
            pl.BlockSpec((1, D_MODEL, IN_WIDTH), lambda b, s: (layer, 0, 0)),
            pl.BlockSpec((1, TOK_TILE, SSM_WIDTH), tok),
            pl.BlockSpec((1, TOK_TILE, SSM_WIDTH), tok),
            pl.BlockSpec((1, TOK_TILE, SSM_WIDTH), tok),
            pl.BlockSpec((1, SSM_WIDTH), const2),
            pl.BlockSpec((SSM_WIDTH, D_MODEL), const2),
            pl.BlockSpec((SSM_WIDTH, D_MODEL), const2),
            pl.BlockSpec((1, TOK_TILE, ATTN_WIDTH), tok),
            pl.BlockSpec((ATTN_WIDTH, D_MODEL), const2),
            pl.BlockSpec((1, TOK_TILE, POOL_WIDTH), tok),
            pl.BlockSpec((1, POOL_HALO, POOL_WIDTH),
                         lambda b, s: (b, jnp.maximum(jnp.minimum(s, last) * per_tile - 1, 0), 0)),
            pl.BlockSpec((1, POOL_HALO, POOL_WIDTH),
                         lambda b, s: (b, jnp.minimum((jnp.minimum(s, last) + 1) * per_tile, n_halo - 1), 0)),
            pl.BlockSpec((len(POOL_WINDOWS), POOL_GROUP, POOL_OUT_GROUP), const3),
            pl.BlockSpec((1, D_MODEL), const2),
            pl.BlockSpec((D_MODEL, D_MODEL), const2),
            pl.BlockSpec((1, D_MODEL), const2),
            pl.BlockSpec((D_MODEL, 2 * N_EXPERTS), const2),
            pl.BlockSpec((1, N_EXPERTS), const2),
        ],
        out_specs=[
            pl.BlockSpec((1, TOK_TILE, D_MODEL), tok),
            pl.BlockSpec((1, TOK_TILE, PACKED), prev),
            pl.BlockSpec((1, TOK_TILE, TOP_K), prev),
            pl.BlockSpec((1, TOK_TILE, TOP_K), prev),
            pl.BlockSpec((1, 1, 1, N_EXPERTS), lambda b, s: (b, jnp.maximum(s - 1, 0), 0, 0)),
        ],
        out_shape=[
            jax.ShapeDtypeStruct((BATCH, S_TOT, D_MODEL), F32),
            jax.ShapeDtypeStruct((BATCH, S_TOT, PACKED), jnp.uint32),
            jax.ShapeDtypeStruct((BATCH, S_TOT, TOP_K), jnp.int32),
            jax.ShapeDtypeStruct((BATCH, S_TOT, TOP_K), F32),
            jax.ShapeDtypeStruct((BATCH, N_TILES, 1, N_EXPERTS), F32),
        ],
        scratch_shapes=[pltpu.VMEM((TOK_TILE, D_MODEL), F32)],
        compiler_params=pltpu.CompilerParams(
            dimension_semantics=("parallel", "arbitrary"), vmem_limit_bytes=VMEM_LIMIT),
        name="merge",
    )(*xs, mod, g1, w_in, yf, yb, u, dsk, wval, wgate, o, wup, pool_in, pool_in, pool_in, pw, ps, wout, g2, rw, rb)


def _route_kernel(te_ref, below_ref, cnt_ref, dest_ref, be_ref, carry_ref):
    i = pl.program_id(0)
    te = te_ref[...]
    lane = lax.broadcasted_iota(jnp.int32, (ROUTE_TILE, LANES), 1)
    ohs = [(te[:, k:k + 1] == lane).astype(F32) for k in range(TOP_K)]
    oh_all = ohs[0] + ohs[1] + ohs[2] + ohs[3]

    @pl.when(i == 0)
    def _():
        cnt = jnp.broadcast_to(cnt_ref[...], (8, LANES))
        padded = jnp.floor((cnt + (ROW_BLOCK - 1)) / ROW_BLOCK) * ROW_BLOCK
        l8 = lax.broadcasted_iota(jnp.int32, (8, LANES), 1)
        end = padded
        s = 1
        while s < N_EXPERTS:
            end = end + jnp.where(l8 >= s, pltpu.roll(end, s, 1), 0.0)
            s *= 2
        carry_ref[...] = (end - padded)[0:1]
        blk_row = (lax.broadcasted_iota(jnp.int32, (N_BLOCKS_PAD, LANES), 0) * ROW_BLOCK).astype(F32)
        lb = lax.broadcasted_iota(jnp.int32, (N_BLOCKS_PAD, LANES), 1)
        hit = jnp.where((lb < N_EXPERTS) & (jnp.broadcast_to(end[0:1], blk_row.shape) <= blk_row), 1.0, 0.0)
        be = jnp.sum(hit, axis=-1, keepdims=True)
        be_ref[...] = jnp.broadcast_to(be, (N_BLOCKS_PAD, LANES)).astype(jnp.int32)

    rank = carry_ref[...] + jnp.dot(below_ref[...], oh_all.astype(BF16), preferred_element_type=F32)
    dests = [jnp.sum(ohs[k] * rank, axis=-1, keepdims=True) for k in range(TOP_K)]
    dest_ref[...] = jnp.concatenate(dests, axis=1).astype(jnp.int32)
    carry_ref[...] += jnp.sum(oh_all, axis=0, keepdims=True)


def _route(top_e, counts, group):
    n_steps = G_TOK // ROUTE_TILE
    return pl.pallas_call(
        _route_kernel,
        grid=(n_steps,),
        in_specs=[pl.BlockSpec((ROUTE_TILE, TOP_K), lambda i: (group * n_steps + i, 0)),
                  pl.BlockSpec((ROUTE_TILE, ROUTE_TILE), lambda i: (0, 0)),
                  pl.BlockSpec((1, LANES), lambda i: (0, 0))],
        out_specs=[
            pl.BlockSpec((ROUTE_TILE, TOP_K), lambda i: (i, 0)),
            pl.BlockSpec((N_BLOCKS_PAD, LANES), lambda i: (0, 0)),
        ],
        out_shape=[
            jax.ShapeDtypeStruct((G_TOK, TOP_K), jnp.int32),
            jax.ShapeDtypeStruct((N_BLOCKS_PAD, LANES), jnp.int32),
        ],
        scratch_shapes=[pltpu.VMEM((1, LANES), F32)],
        compiler_params=pltpu.CompilerParams(dimension_semantics=("arbitrary",)),
        name="route",
    )(top_e, jnp.asarray(np.tril(np.ones((ROUTE_TILE, ROUTE_TILE), np.float32), -1), BF16), counts)


def _sc_mesh():
    return plsc.VectorSubcoreMesh(core_axis_name="c", subcore_axis_name="s")


def _sc_worker():
    return lax.axis_index("s") * SC_CORES + lax.axis_index("c")


def _dispatch(f, dest_t, group):
    per_w = G_TOK // SC_WORKERS
    n_chunks = per_w // SC_SCATTER_ROWS

    @functools.partial(
        pl.kernel,
        out_type=jax.ShapeDtypeStruct((N_ROWS, PACKED), jnp.uint32),
        mesh=_sc_mesh(),
        scratch_types=[
            pltpu.VMEM((SC_SCATTER_ROWS,), jnp.int32),
            pltpu.VMEM((SC_SCATTER_ROWS, PACKED), jnp.uint32),
            pltpu.SemaphoreType.DMA,
        ],
        name="dispatch",
    )
    def k(f_hbm, i_hbm, o_hbm, idx_v, rows_v, sem):
        base = _sc_worker() * per_w

        @pl.loop(0, n_chunks)
        def _(j):
            t0 = pl.multiple_of(base + j * SC_SCATTER_ROWS, 8)
            pltpu.sync_copy(f_hbm.at[pl.ds(group * G_TOK + t0, SC_SCATTER_ROWS)], rows_v)
            for kk in range(TOP_K):
                pltpu.sync_copy(i_hbm.at[pl.ds(kk * G_TOK + t0, SC_SCATTER_ROWS)], idx_v)
                pltpu.async_copy(rows_v, o_hbm.at[idx_v], sem).wait()

    return k(f, dest_t.reshape(TOP_K * G_TOK))


def _combine_gather(yb, idx):
    n = idx.shape[0]
    assert n % (SC_WORKERS * SC_GATHER_ROWS) == 0
    per_w = n // SC_WORKERS
    n_chunks = per_w // SC_GATHER_ROWS

    @functools.partial(
        pl.kernel,
        out_type=jax.ShapeDtypeStruct((n, PACKED), jnp.uint32),
        mesh=_sc_mesh(),
        scratch_types=[
            pltpu.VMEM((SC_GATHER_ROWS,), jnp.int32),
            pltpu.VMEM((SC_GATHER_ROWS, PACKED), jnp.uint32),
            pltpu.SemaphoreType.DMA,
        ],
        name="combine_gather",
    )
    def k(y_hbm, i_hbm, o_hbm, idx_v, rows_v, sem):
        base = _sc_worker() * per_w

        @pl.loop(0, n_chunks)
        def _(j):
            r0 = pl.multiple_of(base + j * SC_GATHER_ROWS, 8)
            pltpu.sync_copy(i_hbm.at[pl.ds(r0, SC_GATHER_ROWS)], idx_v)
            pltpu.async_copy(y_hbm.at[idx_v], rows_v, sem).wait()
            pltpu.sync_copy(rows_v, o_hbm.at[pl.ds(r0, SC_GATHER_ROWS)])

    return k(yb, idx)


def _weight_copies(w1_hbm, w2_hbm, w1f, w2f, sem, idx):
    return (pltpu.make_async_copy(w1_hbm.at[idx], w1f, sem.at[0]),
            pltpu.make_async_copy(w2_hbm.at[idx], w2f, sem.at[1]))


def _expert_kernel(be_ref, x_ref, w1_hbm, b1_ref, w2_hbm, b2_ref, y_ref, w1f, w2f, w1b, w2b, sem, *, layer):
    copies = functools.partial(_weight_copies, w1_hbm, w2_hbm, w1f, w2f, sem)
    for a in range(EXPERT_STEP):
        j = pl.program_id(0) * EXPERT_STEP + a
        e = be_ref[j]
        prev = be_ref[jnp.maximum(j - 1, 0)]
        live = e < N_EXPERTS
        first = live & ((j == 0) | (e != prev))
        rows = slice(a * ROW_BLOCK, (a + 1) * ROW_BLOCK)

        if a == 0:
            @pl.when(first & (j == 0))
            def _():
                for c in copies(layer * N_EXPERTS + e):
                    c.start()

        @pl.when(first)
        def _():
            for c in copies(layer * N_EXPERTS + e):
                c.wait()
            w1b[...] = w1f[...].astype(BF16)
            w2b[...] = w2f[...].astype(BF16)
            nxt = lax.while_loop(
                lambda k: (k < N_BLOCKS) & (be_ref[jnp.minimum(k, N_BLOCKS - 1)] == e), lambda k: k + 1, j + 1)
            e_next = be_ref[jnp.minimum(nxt, N_BLOCKS - 1)]

            @pl.when((nxt < N_BLOCKS) & (e_next < N_EXPERTS))
            def _():
                for c in copies(layer * N_EXPERTS + e_next):
                    c.start()

        @pl.when(live)
        def _():
            idx = layer * N_EXPERTS + e
            x = jnp.concatenate(_unpack_rows(x_ref[rows, :]), axis=1).astype(BF16)
            gu = jnp.dot(x, w1b[...], preferred_element_type=F32) + b1_ref[idx]
            gate = jnp.minimum(gu[:, :D_EXPERT], SWIGLU_LIMIT)
            lin = jnp.clip(gu[:, D_EXPERT:], -SWIGLU_LIMIT, SWIGLU_LIMIT)
            act = gate * _sigmoid(SWIGLU_ALPHA * gate) * (lin + 1.0)
            y_ref[rows, :] = _pack_rows(
                jnp.dot(act.astype(BF16), w2b[...], preferred_element_type=F32) + b2_ref[idx])


def _experts(block_e, rows, w1, b1, w2, b2, layer):
    step_rows = EXPERT_STEP * ROW_BLOCK
    return pl.pallas_call(
        functools.partial(_expert_kernel, layer=layer),
        grid_spec=pltpu.PrefetchScalarGridSpec(
            num_scalar_prefetch=1,
            grid=(N_BLOCKS // EXPERT_STEP,),
            in_specs=[
                pl.BlockSpec((step_rows, PACKED), lambda j, be: (j, 0)),
                pl.BlockSpec(memory_space=pl.ANY),
                pl.BlockSpec((DEPTH * N_EXPERTS, 1, 2 * D_EXPERT), lambda j, be: (0, 0, 0)),
                pl.BlockSpec(memory_space=pl.ANY),
                pl.BlockSpec((DEPTH * N_EXPERTS, 1, D_MODEL), lambda j, be: (0, 0, 0)),
            ],
            out_specs=pl.BlockSpec((step_rows, PACKED), lambda j, be: (j, 0)),
            scratch_shapes=[
                pltpu.VMEM((D_MODEL, 2 * D_EXPERT), F32),
                pltpu.VMEM((D_EXPERT, D_MODEL), F32),
                pltpu.VMEM((D_MODEL, 2 * D_EXPERT), BF16),
                pltpu.VMEM((D_EXPERT, D_MODEL), BF16),
                pltpu.SemaphoreType.DMA((2,)),
            ],
        ),
        out_shape=jax.ShapeDtypeStruct((N_ROWS, PACKED), jnp.uint32),
        compiler_params=pltpu.CompilerParams(
            dimension_semantics=("arbitrary",), vmem_limit_bytes=VMEM_LIMIT),
        name="experts",
    )(block_e, rows, w1, b1, w2, b2)


def _final_kernel(x_ref, mod_ref, g_ref, p_ref, fg_ref, *rest, batch):
    o_ref = rest[-1]
    gate = mod_ref[5, pl.ds(batch, 1), :]
    o_ref[0] = _rms(x_ref[0] + gate * _routed_sum(g_ref, p_ref), fg_ref[...])


def _final(xa, mod, gathered, probs, final_g, batch, out=None):
    ctx_tiles = CTX_LEN // TOK_TILE
    tok = lambda i: (batch, i + ctx_tiles, 0)
    in_specs = [
        pl.BlockSpec((1, TOK_TILE, D_MODEL), tok),
        pl.BlockSpec((6, 8, D_MODEL), lambda i: (0, 0, 0)),
        pl.BlockSpec((TOP_K, 1, TOK_TILE, PACKED), lambda i: (0, 0, i + ctx_tiles, 0)),
        pl.BlockSpec((1, TOK_TILE, TOP_K), tok),
        pl.BlockSpec((1, D_MODEL), lambda i: (0, 0)),
    ]
    args = [xa, mod, gathered.reshape(TOP_K, 1, S_TOT, PACKED), probs, final_g]
    aliases = {}
    if out is not None:
        in_specs.append(pl.BlockSpec(memory_space=pl.ANY))
        args.append(out)
        aliases = {len(args) - 1: 0}
    return pl.pallas_call(
        functools.partial(_final_kernel, batch=batch),
        grid=(N_TILES - ctx_tiles,),
        in_specs=in_specs,
        out_specs=pl.BlockSpec((1, TOK_TILE, D_MODEL), lambda i: (batch, i, 0)),
        out_shape=jax.ShapeDtypeStruct((BATCH, SEQ, D_MODEL), F32),
        input_output_aliases=aliases,
        compiler_params=pltpu.CompilerParams(dimension_semantics=("parallel",)),
        name="final",
    )(*args)


def _split_bf16(w):
    bits = lax.bitcast_convert_type(w.astype(F32), jnp.uint32) & jnp.uint32(0xFFFF0000)
    hi = lax.bitcast_convert_type(bits, F32)
    return jnp.concatenate([hi.astype(BF16), (w - hi).astype(BF16)], axis=-1)


def _rope_tables():
    t = np.arange(SEQ)
    axis_dim = HEAD_DIM // 2
    inv_freq = (1.0 / (ROPE_BASE ** (np.arange(0, axis_dim, 2, dtype=np.float32) / axis_dim))).astype(np.float32)
    ang_r = (t // GRID_W).astype(np.float32)[:, None] * inv_freq[None]
    ang_c = (t % GRID_W).astype(np.float32)[:, None] * inv_freq[None]
    ang = np.concatenate([ang_r, ang_r, ang_c, ang_c] * (LANES // HEAD_DIM), axis=-1)
    cos = np.concatenate([np.ones((CTX_LEN, LANES), np.float32), np.cos(ang)], axis=0)
    sin = np.concatenate([np.zeros((CTX_LEN, LANES), np.float32), np.sin(ang)], axis=0)
    return jnp.asarray(cos, F32), jnp.asarray(sin, F32)


def kernel(x, c, ctx, c_ctx, w_ada, b_ada, norm1_g, norm2_g, w_in, ssm_lam_re, ssm_lam_im, ssm_log_dt, ssm_b_re, ssm_b_im, ssm_c_re, ssm_c_im, ssm_d, ssm_w_val, ssm_w_gate, attn_sink, attn_w_up, pool_w, pool_scale, w_out, router_w, router_b, exp_w1, exp_b1, exp_w2, exp_b2, final_g):
    assert x.shape == (BATCH, SEQ, D_MODEL) and ctx.shape == (BATCH, CTX_LEN, D_MODEL)
    cos_t, sin_t = _rope_tables()
    cv = jnp.concatenate([c, c_ctx[None], jnp.zeros((8 - BATCH - 1, D_MODEL), F32)], axis=0)
    mods = _ada(cv, w_ada, b_ada)
    xs = (ctx, x)
    ew1 = exp_w1.reshape(DEPTH * N_EXPERTS, D_MODEL, 2 * D_EXPERT)
    eb1 = exp_b1.reshape(DEPTH * N_EXPERTS, 1, 2 * D_EXPERT)
    ew2 = exp_w2.reshape(DEPTH * N_EXPERTS, D_EXPERT, D_MODEL)
    eb2 = exp_b2.reshape(DEPTH * N_EXPERTS, 1, D_MODEL)
    half_gate = jnp.where(jnp.arange(IN_WIDTH) >= OFF_GATE, 0.5, 1.0).astype(F32)
    w_in_b = (w_in * half_gate).astype(BF16)
    ssm_tables = _ssm_params(ssm_lam_re, ssm_lam_im, ssm_log_dt, ssm_b_re, ssm_b_im, ssm_c_re, ssm_c_im)
    proj = _inproj(xs, mods[0], norm1_g[0][None], w_in_b, 0, cos_t, sin_t)
    out = None
    for l in range(DEPTH):
        mod = mods[l]
        u, kv, q, pool_in = proj[:4]
        if len(proj) > 4:
            xs = (proj[4], proj[4])
        yf, yb = _ssm(u, *ssm_tables, l)
        o = _attn(attn_sink[l], q, kv)
        xa, f, top_e, top_p, tile_cnt = _merge(
            xs, mod, norm1_g[l][None], w_in_b, l, yf, yb, u, ssm_d[l][None],
            (0.5 * ssm_w_val[l]).astype(BF16), (0.5 * ssm_w_gate[l]).astype(BF16), o, attn_w_up[l].astype(BF16),
            pool_in, pool_w[l].astype(BF16), pool_scale[l][None], (0.5 * w_out[l]).astype(BF16), norm2_g[l][None],
            _split_bf16(router_w[l]), router_b[l][None])
        proj = None
        for b in range(BATCH):
            counts = jnp.pad(tile_cnt[b, :, 0, :].sum(axis=0)[None], ((0, 0), (0, LANES - N_EXPERTS)))
            dest, block_e = _route(top_e.reshape(N_TOK, TOP_K), counts, b)
            dest_t = dest.T
            rows = _dispatch(f.reshape(N_TOK, PACKED), dest_t, b)
            yrows = _experts(block_e[:N_BLOCKS, 0], rows, ew1, eb1, ew2, eb2, l)
            gathered = _combine_gather(yrows, dest_t.reshape(-1))
            if l < DEPTH - 1:
                proj = _inproj((xa, xa), mods[l + 1], norm1_g[l + 1][None], w_in_b, l + 1, cos_t, sin_t,
                               (mod, gathered, top_p), b, proj)
            else:
                out = _final(xa, mod, gathered, top_p, final_g[None], b, out)
    return out
```

```python
import functools
import math

import jax
import jax.numpy as jnp
import numpy as np
from jax import lax
from jax.experimental import pallas as pl
from jax.experimental.pallas import tpu as pltpu
from jax.experimental.pallas import tpu_sc as plsc

F32 = jnp.float32
BF16 = jnp.bfloat16
HIGHEST = lax.Precision.HIGHEST

D_MODEL = 1024
BATCH = 2
SEQ = 8192
DEPTH = 2
GRID_W = 64
CTX_LEN = 256
NORM_EPS = 1e-6
SSM_WIDTH = 512
SSM_GROUP = 16
SSM_GROUPS = 32
SSM_STATE = 64
HEAD_DIM = 64
N_Q_HEADS = 8
N_KV_HEADS = 2
GQA_GROUP = 4
ATTN_WIDTH = 512
KV_WIDTH = 128
WINDOW = 128
ROPE_BASE = 10000.0
ATTN_SCALE = HEAD_DIM ** -0.5
POOL_WINDOWS = (2, 4, 8, 16)
POOL_WIDTH = 512
POOL_GROUP = 128
POOL_OUT_GROUP = 256
OFF_K = 512
OFF_V = 640
OFF_Q = 768
OFF_POOL = 1280
OFF_GATE = 1792
IN_WIDTH = OFF_GATE + 3 * D_MODEL
N_EXPERTS = 32
TOP_K = 4
D_EXPERT = 1024
SWIGLU_LIMIT = 7.0
SWIGLU_ALPHA = 1.702

S_TOT = CTX_LEN + SEQ
N_TOK = BATCH * S_TOT
G_TOK = S_TOT
TOK_TILE = 256
N_TILES = S_TOT // TOK_TILE
assert CTX_LEN == TOK_TILE
ATTN_BLOCK = 128
ATTN_STEP = 6
POOL_HALO = 8
ROUTE_TILE = 1056
ROW_BLOCK = 256
EXPERT_STEP = 4
N_ROWS = -(-(G_TOK * TOP_K + N_EXPERTS * (ROW_BLOCK - 1)) // (ROW_BLOCK * EXPERT_STEP)) * ROW_BLOCK * EXPERT_STEP
N_BLOCKS = N_ROWS // ROW_BLOCK
N_BLOCKS_PAD = -(-N_BLOCKS // 8) * 8
SC_CORES = 2
SC_WORKERS = SC_CORES * 16
PACKED = D_MODEL // 2
SC_SCATTER_ROWS = 88
SC_GATHER_ROWS = 88
assert G_TOK % (SC_WORKERS * SC_SCATTER_ROWS) == 0 and G_TOK % ROUTE_TILE == 0
VMEM_LIMIT = 56 * 1024 * 1024
LANES = 128
MASK_VALUE = -1e30


def _rms(x, g):
    return x * lax.rsqrt(jnp.mean(x * x, axis=-1, keepdims=True) + NORM_EPS) * g


def _sigmoid(x):
    return 0.5 * jnp.tanh(0.5 * x) + 0.5


def _pack_rows(x):
    half = x.shape[1] // 2
    lo = lax.bitcast_convert_type(x[:, :half].astype(BF16).astype(F32), jnp.uint32)
    hi = lax.bitcast_convert_type(x[:, half:].astype(BF16).astype(F32), jnp.uint32)
    return (lo >> 16) | hi


def _unpack_rows(p):
    lo = lax.bitcast_convert_type(p << 16, F32)
    hi = lax.bitcast_convert_type(p & jnp.uint32(0xFFFF0000), F32)
    return lo, hi


def _token_specs(xs, last=N_TILES - 1, im=lambda f: f):
    ctx_tiles = CTX_LEN // TOK_TILE
    off = 0 if xs[1].shape[1] == S_TOT else ctx_tiles
    return [pl.BlockSpec((1, TOK_TILE, D_MODEL), im(lambda b, i: (b, 0, 0))),
            pl.BlockSpec((1, TOK_TILE, D_MODEL), im(lambda b, i: (b, jnp.clip(i, ctx_tiles, last) - off, 0)))]


def _token_tile(i, xc_ref, xl_ref):
    return jnp.where(i == 0, xc_ref[0], xl_ref[0])


def _mod_row(i, b):
    return jnp.where(i == 0, 2, b)


def _ada_kernel(cv_ref, w_ref, b_ref, o_ref):
    cv = cv_ref[...]
    s = cv * jax.nn.sigmoid(cv)
    o_ref[0, 0] = jnp.dot(s, w_ref[0], preferred_element_type=F32, precision=HIGHEST) + b_ref[0, 0]


def _ada(cv, w_ada, b_ada):
    return pl.pallas_call(
        _ada_kernel,
        grid=(DEPTH, 6),
        in_specs=[
            pl.BlockSpec((8, D_MODEL), lambda l, j: (0, 0)),
            pl.BlockSpec((1, D_MODEL, D_MODEL), lambda l, j: (l, 0, j)),
            pl.BlockSpec((1, 1, 1, D_MODEL), lambda l, j: (l, j, 0, 0)),
        ],
        out_specs=pl.BlockSpec((1, 1, 8, D_MODEL), lambda l, j: (l, j, 0, 0)),
        out_shape=jax.ShapeDtypeStruct((DEPTH, 6, 8, D_MODEL), F32),
        name="ada",
    )(cv, w_ada, b_ada.reshape(DEPTH, 6, 1, D_MODEL))


def _routed_sum(g_ref, p_ref):
    p = p_ref[0]
    lo, hi = _unpack_rows(g_ref[0, 0])
    y_lo = p[:, 0:1] * lo
    y_hi = p[:, 0:1] * hi
    for k in range(1, TOP_K):
        lo, hi = _unpack_rows(g_ref[k, 0])
        y_lo = y_lo + p[:, k:k + 1] * lo
        y_hi = y_hi + p[:, k:k + 1] * hi
    return jnp.concatenate([y_lo, y_hi], axis=1)


def _rope(t, cos, sin):
    lane = lax.broadcasted_iota(jnp.int32, t.shape, 1)
    rot = jnp.where((lane % 32) < 16, -pltpu.roll(t, LANES - 16, 1), pltpu.roll(t, 16, 1))
    return t * cos + rot * sin


def _inproj_kernel(*refs, pending, batch):
    xc_ref, xl_ref, mod_ref, g_ref, w_ref, cos_ref, sin_ref = refs[:7]
    if pending:
        pmod_ref, rows_ref, probs_ref = refs[7:10]
        u_ref, kv_ref, q_ref, pool_ref, xo_ref = refs[-5:]
    else:
        u_ref, kv_ref, q_ref, pool_ref = refs[-4:]
    b, i = (pl.program_id(0), pl.program_id(1)) if batch is None else (batch, pl.program_id(0))
    row = _mod_row(i, b)
    shift = mod_ref[0, pl.ds(row, 1), :]
    scale = mod_ref[1, pl.ds(row, 1), :]
    x = _token_tile(i, xc_ref, xl_ref)
    if pending:
        x = x + pmod_ref[5, pl.ds(row, 1), :] * _routed_sum(rows_ref, probs_ref)
        xo_ref[0] = x
    h = _rms(x, g_ref[...]) * (1.0 + scale) + shift
    p = jnp.dot(h.astype(BF16), w_ref[0, :, :OFF_GATE], preferred_element_type=F32)
    cos = cos_ref[...]
    sin = sin_ref[...]
    u_ref[0] = p[:, :OFF_K]
    kv_ref[0, :, :KV_WIDTH] = _rope(p[:, OFF_K:OFF_V], cos, sin).astype(BF16)
    kv_ref[0, :, KV_WIDTH:] = p[:, OFF_V:OFF_Q].astype(BF16)
    for c in range(ATTN_WIDTH // LANES):
        qc = p[:, OFF_Q + c * LANES:OFF_Q + (c + 1) * LANES]
        q_ref[0, :, c * LANES:(c + 1) * LANES] = (_rope(qc, cos, sin) * ATTN_SCALE).astype(BF16)
    pool_ref[0] = p[:, OFF_POOL:OFF_GATE]


def _inproj(xs, mod, g1, w_in, layer, cos_t, sin_t, pending=None, batch=None, outs=None):
    im = (lambda f: f) if batch is None else (lambda f: lambda i: f(batch, i))
    tok = im(lambda b, i: (b, i, 0))
    in_specs = [
        *_token_specs(xs, im=im),
        pl.BlockSpec((6, 8, D_MODEL), im(lambda b, i: (0, 0, 0))),
        pl.BlockSpec((1, D_MODEL), im(lambda b, i: (0, 0))),
        pl.BlockSpec((1, D_MODEL, IN_WIDTH), im(lambda b, i: (layer, 0, 0))),
        pl.BlockSpec((TOK_TILE, LANES), im(lambda b, i: (i, 0))),
        pl.BlockSpec((TOK_TILE, LANES), im(lambda b, i: (i, 0))),
    ]
    out_specs = [
        pl.BlockSpec((1, TOK_TILE, SSM_WIDTH), tok),
        pl.BlockSpec((1, TOK_TILE, 2 * KV_WIDTH), tok),
        pl.BlockSpec((1, TOK_TILE, ATTN_WIDTH), tok),
        pl.BlockSpec((1, TOK_TILE, POOL_WIDTH), tok),
    ]
    out_shape = [
        jax.ShapeDtypeStruct((BATCH, S_TOT, SSM_WIDTH), F32),
        jax.ShapeDtypeStruct((BATCH, S_TOT, 2 * KV_WIDTH), BF16),
        jax.ShapeDtypeStruct((BATCH, S_TOT, ATTN_WIDTH), BF16),
        jax.ShapeDtypeStruct((BATCH, S_TOT, POOL_WIDTH), F32),
    ]
    args = [*xs, mod, g1, w_in, cos_t, sin_t]
    if pending is not None:
        pmod, gathered, probs = pending
        in_specs += [
            pl.BlockSpec((6, 8, D_MODEL), im(lambda b, i: (0, 0, 0))),
            pl.BlockSpec((TOP_K, 1, TOK_TILE, PACKED), im(lambda b, i: (0, 0, i, 0))),
            pl.BlockSpec((1, TOK_TILE, TOP_K), tok),
        ]
        out_specs.append(pl.BlockSpec((1, TOK_TILE, D_MODEL), tok))
        out_shape.append(jax.ShapeDtypeStruct((BATCH, S_TOT, D_MODEL), F32))
        args += [pmod, gathered.reshape(TOP_K, 1, S_TOT, PACKED), probs]
    aliases = {}
    if outs is not None:
        aliases = {len(args) + k: k for k in range(len(outs))}
        in_specs += [pl.BlockSpec(memory_space=pl.ANY)] * len(outs)
        args += list(outs)
    return pl.pallas_call(
        functools.partial(_inproj_kernel, pending=pending is not None, batch=batch),
        grid=(BATCH, N_TILES) if batch is None else (N_TILES,),
        in_specs=in_specs,
        out_specs=out_specs,
        out_shape=out_shape,
        input_output_aliases=aliases,
        compiler_params=pltpu.CompilerParams(
            dimension_semantics=("parallel",) * (2 if batch is None else 1), vmem_limit_bytes=VMEM_LIMIT),
        name="inproj",
    )(*args)


SSM_HALF_IN = SSM_WIDTH // 2
SSM_HALF_ST = SSM_GROUPS // 2 * SSM_STATE
N_STATE_ROWS = 8
N_LANE_TILES = SSM_HALF_ST // LANES
SCAN_STRIDE = TOK_TILE + 8


def _ssm_kernel(uf_ref, ub_ref, wb_ref, wc_ref, ar_ref, ai_ref, yf_ref, yb_ref, buf_f, buf_b, hf_ref, hb_ref):
    i = pl.program_id(0)
    T = TOK_TILE

    @pl.when(i == 0)
    def _():
        hf_ref[...] = jnp.zeros_like(hf_ref)
        hb_ref[...] = jnp.zeros_like(hb_ref)

    for d, (u_ref, buf) in enumerate(((uf_ref, buf_f), (ub_ref, buf_b))):
        for b in range(BATCH):
            for half in range(2):
                ub16 = u_ref[b, :, half * SSM_HALF_IN:(half + 1) * SSM_HALF_IN].astype(BF16)
                for reim in range(2):
                    k = reim * 4 + half * 2 + b
                    bu = jnp.dot(ub16, wb_ref[d, reim, half], preferred_element_type=F32)
                    for l in range(N_LANE_TILES):
                        buf[l, k * SCAN_STRIDE:k * SCAN_STRIDE + T, :] = bu[:, l * LANES:(l + 1) * LANES]

    def step(t, carry):
        hf, hb = carry
        tb = T - 1 - t
        nf, nb = [], []
        for l in range(N_LANE_TILES):
            sl = pl.ds(l * LANES, LANES)
            rows_f = pl.ds(t, N_STATE_ROWS, stride=SCAN_STRIDE)
            h = ar_ref[0, :, sl] * hf[l] + ai_ref[0, :, sl] * pltpu.roll(hf[l], 4, 0) + buf_f[l, rows_f, :]
            buf_f[l, rows_f, :] = h
            nf.append(h)
            rows_b = pl.ds(tb, N_STATE_ROWS, stride=SCAN_STRIDE)
            h = ar_ref[1, :, sl] * hb[l] + ai_ref[1, :, sl] * pltpu.roll(hb[l], 4, 0) + buf_b[l, rows_b, :]
            buf_b[l, rows_b, :] = h
            nb.append(h)
        return tuple(nf), tuple(nb)

    hf0 = tuple(hf_ref[:, l * LANES:(l + 1) * LANES] for l in range(N_LANE_TILES))
    hb0 = tuple(hb_ref[:, l * LANES:(l + 1) * LANES] for l in range(N_LANE_TILES))
    hf1, hb1 = lax.fori_loop(0, T, step, (hf0, hb0), unroll=2)
    for l in range(N_LANE_TILES):
        hf_ref[:, l * LANES:(l + 1) * LANES] = hf1[l]
        hb_ref[:, l * LANES:(l + 1) * LANES] = hb1[l]

    for d, (buf, y_ref) in enumerate(((buf_f, yf_ref), (buf_b, yb_ref))):
        for b in range(BATCH):
            for half in range(2):
                kre = half * 2 + b
                kim = 4 + half * 2 + b
                state = lambda k: jnp.concatenate(
                    [buf[l, k * SCAN_STRIDE:k * SCAN_STRIDE + T, :] for l in range(N_LANE_TILES)],
                    axis=1).astype(BF16)
                y = jnp.dot(state(kre), wc_ref[d, 0, half], preferred_element_type=F32)
                y = y + jnp.dot(state(kim), wc_ref[d, 1, half], preferred_element_type=F32)
                y_ref[b, :, half * SSM_HALF_IN:(half + 1) * SSM_HALF_IN] = y


def _ssm(u, wb, wc, ar, ai, layer):
    fwd = lambda i: (0, i, 0)
    bwd = lambda i: (0, jnp.where(i == 0, 0, N_TILES - i), 0)
    return pl.pallas_call(
        _ssm_kernel,
        grid=(N_TILES,),
        in_specs=[
            pl.BlockSpec((BATCH, TOK_TILE, SSM_WIDTH), fwd),
            pl.BlockSpec((BATCH, TOK_TILE, SSM_WIDTH), bwd),
            pl.BlockSpec((None, 2, 2, 2, SSM_HALF_IN, SSM_HALF_ST), lambda i: (layer, 0, 0, 0, 0, 0)),
            pl.BlockSpec((None, 2, 2, 2, SSM_HALF_ST, SSM_HALF_IN), lambda i: (layer, 0, 0, 0, 0, 0)),
            pl.BlockSpec((None, 2, N_STATE_ROWS, SSM_HALF_ST), lambda i: (layer, 0, 0, 0)),
            pl.BlockSpec((None, 2, N_STATE_ROWS, SSM_HALF_ST), lambda i: (layer, 0, 0, 0)),
        ],
        out_specs=[
            pl.BlockSpec((BATCH, TOK_TILE, SSM_WIDTH), fwd),
            pl.BlockSpec((BATCH, TOK_TILE, SSM_WIDTH), bwd),
        ],
        out_shape=[jax.ShapeDtypeStruct((BATCH, S_TOT, SSM_WIDTH), F32)] * 2,
        scratch_shapes=[
            pltpu.VMEM((N_LANE_TILES, N_STATE_ROWS * SCAN_STRIDE, LANES), F32),
            pltpu.VMEM((N_LANE_TILES, N_STATE_ROWS * SCAN_STRIDE, LANES), F32),
            pltpu.VMEM((N_STATE_ROWS, SSM_HALF_ST), F32),
            pltpu.VMEM((N_STATE_ROWS, SSM_HALF_ST), F32),
        ],
        compiler_params=pltpu.CompilerParams(
            dimension_semantics=("arbitrary",), vmem_limit_bytes=VMEM_LIMIT),
        name="ssm",
    )(u, u, wb, wc, ar, ai)


def _ssm_params(lam_re, lam_im, log_dt, b_re, b_im, c_re, c_im):
    lr = lam_re.astype(F32)
    li = lam_im.astype(F32)
    dt = jnp.exp(log_dt.astype(F32))[..., None]
    mag = jnp.exp(lr * dt)
    ang = li * dt
    a_re = mag * jnp.cos(ang)
    a_im = mag * jnp.sin(ang)
    den = lr * lr + li * li
    k_re = ((a_re - 1.0) * lr + a_im * li) / den
    k_im = (a_im * lr - (a_re - 1.0) * li) / den
    bb_re = k_re[..., None] * b_re - k_im[..., None] * b_im
    bb_im = k_re[..., None] * b_im + k_im[..., None] * b_re
    n_g = SSM_GROUPS // 2
    lead = lam_re.shape[:2]
    drive_mask = np.equal.outer(np.arange(SSM_HALF_IN) // SSM_GROUP, np.arange(SSM_HALF_ST) // SSM_STATE)
    repeat_st = np.tile(np.eye(SSM_STATE, dtype=np.float32), (1, n_g))
    repeat_in = np.tile(np.eye(SSM_GROUP, dtype=np.float32), (1, n_g))

    def block_diag(parts, rows_per_half, repeat, mask):
        rows = jnp.stack([jnp.swapaxes(p, -1, -2) for p in parts], axis=2)
        rows = rows.reshape(*lead, 2, 2, rows_per_half, repeat.shape[0])
        full = jnp.einsum('...rp,pc->...rc', rows, repeat, precision=HIGHEST)
        return (full * mask.astype(np.float32)).astype(BF16)

    wb = block_diag((bb_re, bb_im), SSM_HALF_IN, repeat_st, drive_mask)
    wc = block_diag((c_re.astype(F32), -c_im.astype(F32)), SSM_HALF_ST, repeat_in, drive_mask.T)

    def rows(a, sign_re):
        halves = a.reshape(*lead, 1, 2, 1, SSM_HALF_ST)
        signs = jnp.asarray([sign_re, 1.0], F32).reshape(2, 1, 1, 1)
        return jnp.broadcast_to(halves * signs, (*lead, 2, 2, BATCH, SSM_HALF_ST)).reshape(
            *lead, N_STATE_ROWS, SSM_HALF_ST)

    return wb, wc, rows(a_re, 1.0), rows(a_im, -1.0)


def _attn_block(n, q, kv_band, kv_ctx, sink_ref, band_bias):
    kv_all = jnp.concatenate([kv_band, kv_ctx], axis=0)
    nb = 3 * ATTN_BLOCK
    last = SEQ // ATTN_BLOCK - 1
    krow = lax.broadcasted_iota(jnp.int32, (nb, 1), 0)
    lo_r = jnp.where(n < 0, nb, jnp.where(n == 0, ATTN_BLOCK, 0))
    hi_r = jnp.where(n == last, 2 * ATTN_BLOCK, nb)
    bias = band_bias + jnp.where((krow >= lo_r) & (krow < hi_r), 0.0, MASK_VALUE)
    bias = jnp.concatenate([bias] * GQA_GROUP, axis=1)
    nt = (((1,), (1,)), ((), ()))
    outs = []
    for hk in range(N_KV_HEADS):
        k = kv_all[:, hk * HEAD_DIM:(hk + 1) * HEAD_DIM]
        v_t = kv_all[:, KV_WIDTH + hk * HEAD_DIM:KV_WIDTH + (hk + 1) * HEAD_DIM].astype(F32).T.astype(BF16)
        heads = range(hk * GQA_GROUP, (hk + 1) * GQA_GROUP)
        qs = jnp.concatenate([q[:, hq * HEAD_DIM:(hq + 1) * HEAD_DIM] for hq in heads], axis=0)
        sink = jnp.concatenate([jnp.full((1, ATTN_BLOCK), sink_ref[hq], F32) for hq in heads], axis=1)
        s_t = lax.dot_general(k, qs, nt, preferred_element_type=F32)
        s_w = s_t[:nb] + bias
        s_c = s_t[nb:]
        m = jnp.maximum(jnp.maximum(jnp.max(s_w, axis=0, keepdims=True),
                                    jnp.max(s_c, axis=0, keepdims=True)), sink)
        e_w = jnp.exp(s_w - m)
        e_c = jnp.exp(s_c - m)
        den = jnp.sum(e_w, axis=0, keepdims=True) + jnp.sum(e_c, axis=0, keepdims=True) + jnp.exp(sink - m)
        e = jnp.concatenate([e_w, e_c], axis=0).astype(BF16)
        o_t = jnp.dot(v_t, e, preferred_element_type=F32) / den
        outs.extend(o_t[:, g * ATTN_BLOCK:(g + 1) * ATTN_BLOCK].T for g in range(GQA_GROUP))
    return jnp.concatenate(outs, axis=1).astype(BF16)


def _attn_kernel(sink_ref, band_ref, q_ref, kc_ref, *rest):
    kv_refs, o_ref = rest[:-1], rest[-1]
    j0 = pl.program_id(1) * ATTN_STEP
    for a in range(ATTN_STEP):
        n = j0 + a - CTX_LEN // ATTN_BLOCK
        band = jnp.concatenate([kv_refs[a + d][0] for d in range(3)], axis=0)
        rows = slice(a * ATTN_BLOCK, (a + 1) * ATTN_BLOCK)
        o_ref[0, rows, :] = _attn_block(n, q_ref[0, rows, :], band, kc_ref[0], sink_ref, band_ref[...])


def _attn(sink, q, kv):
    nq = S_TOT // ATTN_BLOCK
    first = CTX_LEN // ATTN_BLOCK
    r = jnp.arange(ATTN_BLOCK)[None, :]
    c = jnp.arange(3 * ATTN_BLOCK)[:, None] - ATTN_BLOCK
    band_bias = jnp.where(jnp.abs(c - r) <= WINDOW, 0.0, MASK_VALUE).astype(F32)

    def kv_spec(d):
        return pl.BlockSpec((1, ATTN_BLOCK, 2 * KV_WIDTH),
                            lambda b, j: (b, jnp.clip(j * ATTN_STEP + d - 1, first, nq - 1), 0))

    return pl.pallas_call(
        _attn_kernel,
        grid=(BATCH, nq // ATTN_STEP),
        in_specs=[
            pl.BlockSpec(memory_space=pltpu.SMEM),
            pl.BlockSpec((3 * ATTN_BLOCK, ATTN_BLOCK), lambda b, j: (0, 0)),
            pl.BlockSpec((1, ATTN_STEP * ATTN_BLOCK, ATTN_WIDTH), lambda b, j: (b, j, 0)),
            pl.BlockSpec((1, CTX_LEN, 2 * KV_WIDTH), lambda b, j: (b, 0, 0)),
            *[kv_spec(d) for d in range(ATTN_STEP + 2)],
        ],
        out_specs=pl.BlockSpec((1, ATTN_STEP * ATTN_BLOCK, ATTN_WIDTH), lambda b, j: (b, j, 0)),
        out_shape=jax.ShapeDtypeStruct((BATCH, S_TOT, ATTN_WIDTH), BF16),
        compiler_params=pltpu.CompilerParams(dimension_semantics=("parallel", "parallel")),
        name="attn",
    )(sink, band_bias, q, kv, *[kv] * (ATTN_STEP + 2))


def _gelu_tanh(x):
    return 0.5 * x * (1.0 + jnp.tanh(math.sqrt(2.0 / math.pi) * (x + 0.044715 * (x * x * x))))


def _pool_means(ext, i):
    T = TOK_TILE
    n_ext = T + 2 * POOL_HALO
    seq_len = jnp.where(i == 0, CTX_LEN, SEQ)
    start = jnp.where(i == 0, 0, (i - 1) * T)
    r = start - POOL_HALO + lax.broadcasted_iota(jnp.int32, (n_ext, 1), 0)
    ext = jnp.where((r >= 0) & (r < seq_len), ext, 0.0)
    tl = start + lax.broadcasted_iota(jnp.int32, (T, 1), 0)
    back = lambda a, s: pltpu.roll(a, s, 0)
    fwd = lambda a, s: pltpu.roll(a, n_ext - s, 0)
    diffs = []
    for gi, w in enumerate(POOL_WINDOWS):
        e = ext[:, gi * POOL_GROUP:(gi + 1) * POOL_GROUP]
        lo = w // 2
        hi = w - 1 - lo
        s = e + back(e, 1)
        span = 2
        while span < w:
            s = s + back(s, span)
            span *= 2
        if w > 2:
            s = fwd(s, hi)
        cnt = (jnp.clip(tl + hi + 1, 0, seq_len) - jnp.clip(tl - lo, 0, seq_len)).astype(F32)
        mean = s[POOL_HALO:POOL_HALO + T] / cnt
        diffs.append((mean - e[POOL_HALO:POOL_HALO + T]).astype(BF16))
    return diffs


def _merge_kernel(xc_ref, xl_ref, mod_ref, g1_ref, w_ref, yf_ref, yb_ref, u_ref, dsk_ref, wval_ref, wgate_ref,
                  o_ref, wup_ref, pc_ref, pp_ref, pn_ref, pw_ref, ps_ref, wout_ref, g2_ref, rw_ref, rb_ref,
                  xo_ref, f_ref, te_ref, tp_ref, cnt_ref, xprev_ref):
    b = pl.program_id(0)
    s = pl.program_id(1)

    @pl.when(s == 0)
    def _():
        xprev_ref[...] = jnp.zeros_like(xprev_ref)

    rowp = _mod_row(jnp.maximum(s - 1, 0), b)
    f = _rms(xprev_ref[...], g2_ref[...]) * (1.0 + mod_ref[4, pl.ds(rowp, 1), :]) + mod_ref[3, pl.ds(rowp, 1), :]
    f_ref[0] = _pack_rows(f)
    f_hi = f.astype(BF16)
    f_lo = (f - f_hi.astype(F32)).astype(BF16)
    r_hi = jnp.dot(f_hi, rw_ref[...], preferred_element_type=F32)
    r_lo = jnp.dot(f_lo, rw_ref[...], preferred_element_type=F32)
    logits = r_hi[:, :N_EXPERTS] + r_hi[:, N_EXPERTS:] + r_lo[:, :N_EXPERTS] + rb_ref[...]
    lane = lax.broadcasted_iota(jnp.int32, logits.shape, 1)
    vals, idxs = [], []
    for _ in range(TOP_K):
        mx = jnp.max(logits, axis=-1, keepdims=True)
        ix = jnp.min(jnp.where(logits == mx, lane, N_EXPERTS), axis=-1, keepdims=True)
        vals.append(mx)
        idxs.append(ix)
        logits = jnp.where(lane == ix, -jnp.inf, logits)
    ex = [jnp.exp(v - vals[0]) for v in vals]
    tot = ex[0] + ex[1] + ex[2] + ex[3]
    te_ref[0] = jnp.concatenate(idxs, axis=1)
    tp_ref[0] = jnp.concatenate(ex, axis=1) / tot
    chosen = sum((lane == ix).astype(F32) for ix in idxs)
    cnt_ref[0, 0] = jnp.sum(chosen, axis=0, keepdims=True)

    i = jnp.minimum(s, N_TILES - 1)
    row = _mod_row(i, b)
    shift, scale, gate = (mod_ref[j, pl.ds(row, 1), :] for j in range(3))
    ext = jnp.concatenate([pp_ref[0], pc_ref[0], pn_ref[0]], axis=0)
    diffs = _pool_means(ext, i)
    x = _token_tile(i, xc_ref, xl_ref)
    h = (_rms(x, g1_ref[...]) * (1.0 + scale) + shift).astype(BF16)
    t = jnp.tanh(jnp.dot(h, w_ref[0, :, OFF_GATE:], preferred_element_type=F32))

    z = yf_ref[0] + yb_ref[0] + dsk_ref[...] * u_ref[0]
    z = _gelu_tanh(z).astype(BF16)
    val = jnp.dot(z, wval_ref[...], preferred_element_type=F32)
    y_ssm = val * jnp.tanh(jnp.dot(z, wgate_ref[...], preferred_element_type=F32)) + val
    y_attn = jnp.dot(o_ref[0], wup_ref[...], preferred_element_type=F32)
    y_pool = jnp.concatenate(
        [jnp.dot(diffs[gi], pw_ref[gi], preferred_element_type=F32) for gi in range(len(POOL_WINDOWS))],
        axis=1) * ps_ref[...]

    m = (t[:, :D_MODEL] * y_ssm + t[:, D_MODEL:2 * D_MODEL] * y_attn + t[:, 2 * D_MODEL:] * y_pool
         + (y_ssm + y_attn + y_pool))
    xo = x + gate * jnp.dot(m.astype(BF16), wout_ref[...], preferred_element_type=F32)
    xo_ref[0] = xo
    xprev_ref[...] = xo


def _merge(xs, mod, g1, w_in, layer, yf, yb, u, dsk, wval, wgate, o, wup, pool_in, pw, ps, wout, g2, rw, rb):
    last = N_TILES - 1
    tok = lambda b, s: (b, jnp.minimum(s, last), 0)
    prev = lambda b, s: (b, jnp.maximum(s - 1, 0), 0)
    const2 = lambda b, s: (0, 0)
    const3 = lambda b, s: (0, 0, 0)
    per_tile = TOK_TILE // POOL_HALO
    n_halo = S_TOT // POOL_HALO
    return pl.pallas_call(
        _merge_kernel,
        grid=(BATCH, N_TILES + 1),
        in_specs=[
            *_token_specs(xs, last),
            pl.BlockSpec((6, 8, D_MODEL), const3),
            pl.BlockSpec((1, D_MODEL), const2),
            pl.BlockSpec((1, D_MODEL, IN_WIDTH), lambda b, s: (layer, 0, 0)),
            pl.BlockSpec((1, TOK_TILE, SSM_WIDTH), tok),
            pl.BlockSpec((1, TOK_TILE, SSM_WIDTH), tok),
            pl.BlockSpec((1, TOK_TILE, SSM_WIDTH), tok),
            pl.BlockSpec((1, SSM_WIDTH), const2),
            pl.BlockSpec((SSM_WIDTH, D_MODEL), const2),
            pl.BlockSpec((SSM_WIDTH, D_MODEL), const2),
            pl.BlockSpec((1, TOK_TILE, ATTN_WIDTH), tok),
            pl.BlockSpec((ATTN_WIDTH, D_MODEL), const2),
            pl.BlockSpec((1, TOK_TILE, POOL_WIDTH), tok),
            pl.BlockSpec((1, POOL_HALO, POOL_WIDTH),
                         lambda b, s: (b, jnp.maximum(jnp.minimum(s, last) * per_tile - 1, 0), 0)),
            pl.BlockSpec((1, POOL_HALO, POOL_WIDTH),
                         lambda b, s: (b, jnp.minimum((jnp.minimum(s, last) + 1) * per_tile, n_halo - 1), 0)),
            pl.BlockSpec((len(POOL_WINDOWS), POOL_GROUP, POOL_OUT_GROUP), const3),
            pl.BlockSpec((1, D_MODEL), const2),
            pl.BlockSpec((D_MODEL, D_MODEL), const2),
            pl.BlockSpec((1, D_MODEL), const2),
            pl.BlockSpec((D_MODEL, 2 * N_EXPERTS), const2),
            pl.BlockSpec((1, N_EXPERTS), const2),
        ],
        out_specs=[
            pl.BlockSpec((1, TOK_TILE, D_MODEL), tok),
            pl.BlockSpec((1, TOK_TILE, PACKED), prev),
            pl.BlockSpec((1, TOK_TILE, TOP_K), prev),
            pl.BlockSpec((1, TOK_TILE, TOP_K), prev),
            pl.BlockSpec((1, 1, 1, N_EXPERTS), lambda b, s: (b, jnp.maximum(s - 1, 0), 0, 0)),
        ],
        out_shape=[
            jax.ShapeDtypeStruct((BATCH, S_TOT, D_MODEL), F32),
            jax.ShapeDtypeStruct((BATCH, S_TOT, PACKED), jnp.uint32),
            jax.ShapeDtypeStruct((BATCH, S_TOT, TOP_K), jnp.int32),
            jax.ShapeDtypeStruct((BATCH, S_TOT, TOP_K), F32),
            jax.ShapeDtypeStruct((BATCH, N_TILES, 1, N_EXPERTS), F32),
        ],
        scratch_shapes=[pltpu.VMEM((TOK_TILE, D_MODEL), F32)],
        compiler_params=pltpu.CompilerParams(
            dimension_semantics=("parallel", "arbitrary"), vmem_limit_bytes=VMEM_LIMIT),
        name="merge",
    )(*xs, mod, g1, w_in, yf, yb, u, dsk, wval, wgate, o, wup, pool_in, pool_in, pool_in, pw, ps, wout, g2, rw, rb)


def _route_kernel(te_ref, below_ref, cnt_ref, dest_ref, be_ref, carry_ref):
    i = pl.program_id(0)
    te = te_ref[...]
    lane = lax.broadcasted_iota(jnp.int32, (ROUTE_TILE, LANES), 1)
    ohs = [(te[:, k:k + 1] == lane).astype(F32) for k in range(TOP_K)]
    oh_all = ohs[0] + ohs[1] + ohs[2] + ohs[3]

    @pl.when(i == 0)
    def _():
        cnt = jnp.broadcast_to(cnt_ref[...], (8, LANES))
        padded = jnp.floor((cnt + (ROW_BLOCK - 1)) / ROW_BLOCK) * ROW_BLOCK
        l8 = lax.broadcasted_iota(jnp.int32, (8, LANES), 1)
        end = padded
        s = 1
        while s < N_EXPERTS:
            end = end + jnp.where(l8 >= s, pltpu.roll(end, s, 1), 0.0)
            s *= 2
        carry_ref[...] = (end - padded)[0:1]
        blk_row = (lax.broadcasted_iota(jnp.int32, (N_BLOCKS_PAD, LANES), 0) * ROW_BLOCK).astype(F32)
        lb = lax.broadcasted_iota(jnp.int32, (N_BLOCKS_PAD, LANES), 1)
        hit = jnp.where((lb < N_EXPERTS) & (jnp.broadcast_to(end[0:1], blk_row.shape) <= blk_row), 1.0, 0.0)
        be = jnp.sum(hit, axis=-1, keepdims=True)
        be_ref[...] = jnp.broadcast_to(be, (N_BLOCKS_PAD, LANES)).astype(jnp.int32)

    rank = carry_ref[...] + jnp.dot(below_ref[...], oh_all.astype(BF16), preferred_element_type=F32)
    dests = [jnp.sum(ohs[k] * rank, axis=-1, keepdims=True) for k in range(TOP_K)]
    dest_ref[...] = jnp.concatenate(dests, axis=1).astype(jnp.int32)
    carry_ref[...] += jnp.sum(oh_all, axis=0, keepdims=True)


def _route(top_e, counts, group):
    n_steps = G_TOK // ROUTE_TILE
    return pl.pallas_call(
        _route_kernel,
        grid=(n_steps,),
        in_specs=[pl.BlockSpec((ROUTE_TILE, TOP_K), lambda i: (group * n_steps + i, 0)),
                  pl.BlockSpec((ROUTE_TILE, ROUTE_TILE), lambda i: (0, 0)),
                  pl.BlockSpec((1, LANES), lambda i: (0, 0))],
        out_specs=[
            pl.BlockSpec((ROUTE_TILE, TOP_K), lambda i: (i, 0)),
            pl.BlockSpec((N_BLOCKS_PAD, LANES), lambda i: (0, 0)),
        ],
        out_shape=[
            jax.ShapeDtypeStruct((G_TOK, TOP_K), jnp.int32),
            jax.ShapeDtypeStruct((N_BLOCKS_PAD, LANES), jnp.int32),
        ],
        scratch_shapes=[pltpu.VMEM((1, LANES), F32)],
        compiler_params=pltpu.CompilerParams(dimension_semantics=("arbitrary",)),
        name="route",
    )(top_e, jnp.asarray(np.tril(np.ones((ROUTE_TILE, ROUTE_TILE), np.float32), -1), BF16), counts)


def _sc_mesh():
    return plsc.VectorSubcoreMesh(core_axis_name="c", subcore_axis_name="s")


def _sc_worker():
    return lax.axis_index("s") * SC_CORES + lax.axis_index("c")


def _dispatch(f, dest_t, group):
    per_w = G_TOK // SC_WORKERS
    n_chunks = per_w // SC_SCATTER_ROWS

    @functools.partial(
        pl.kernel,
        out_type=jax.ShapeDtypeStruct((N_ROWS, PACKED), jnp.uint32),
        mesh=_sc_mesh(),
        scratch_types=[
            pltpu.VMEM((SC_SCATTER_ROWS,), jnp.int32),
            pltpu.VMEM((SC_SCATTER_ROWS, PACKED), jnp.uint32),
            pltpu.SemaphoreType.DMA,
        ],
        name="dispatch",
    )
    def k(f_hbm, i_hbm, o_hbm, idx_v, rows_v, sem):
        base = _sc_worker() * per_w

        @pl.loop(0, n_chunks)
        def _(j):
            t0 = pl.multiple_of(base + j * SC_SCATTER_ROWS, 8)
            pltpu.sync_copy(f_hbm.at[pl.ds(group * G_TOK + t0, SC_SCATTER_ROWS)], rows_v)
            for kk in range(TOP_K):
                pltpu.sync_copy(i_hbm.at[pl.ds(kk * G_TOK + t0, SC_SCATTER_ROWS)], idx_v)
                pltpu.async_copy(rows_v, o_hbm.at[idx_v], sem).wait()

    return k(f, dest_t.reshape(TOP_K * G_TOK))


def _combine_gather(yb, idx):
    n = idx.shape[0]
    assert n % (SC_WORKERS * SC_GATHER_ROWS) == 0
    per_w = n // SC_WORKERS
    n_chunks = per_w // SC_GATHER_ROWS

    @functools.partial(
        pl.kernel,
        out_type=jax.ShapeDtypeStruct((n, PACKED), jnp.uint32),
        mesh=_sc_mesh(),
        scratch_types=[
            pltpu.VMEM((SC_GATHER_ROWS,), jnp.int32),
            pltpu.VMEM((SC_GATHER_ROWS, PACKED), jnp.uint32),
            pltpu.SemaphoreType.DMA,
        ],
        name="combine_gather",
    )
    def k(y_hbm, i_hbm, o_hbm, idx_v, rows_v, sem):
        base = _sc_worker() * per_w

        @pl.loop(0, n_chunks)
        def _(j):
            r0 = pl.multiple_of(base + j * SC_GATHER_ROWS, 8)
            pltpu.sync_copy(i_hbm.at[pl.ds(r0, SC_GATHER_ROWS)], idx_v)
            pltpu.async_copy(y_hbm.at[idx_v], rows_v, sem).wait()
            pltpu.sync_copy(rows_v, o_hbm.at[pl.ds(r0, SC_GATHER_ROWS)])

    return k(yb, idx)


def _weight_copies(w1_hbm, w2_hbm, w1f, w2f, sem, idx):
    return (pltpu.make_async_copy(w1_hbm.at[idx], w1f, sem.at[0]),
            pltpu.make_async_copy(w2_hbm.at[idx], w2f, sem.at[1]))


def _expert_kernel(be_ref, x_ref, w1_hbm, b1_ref, w2_hbm, b2_ref, y_ref, w1f, w2f, w1b, w2b, sem, *, layer):
    copies = functools.partial(_weight_copies, w1_hbm, w2_hbm, w1f, w2f, sem)
    for a in range(EXPERT_STEP):
        j = pl.program_id(0) * EXPERT_STEP + a
        e = be_ref[j]
        prev = be_ref[jnp.maximum(j - 1, 0)]
        live = e < N_EXPERTS
        first = live & ((j == 0) | (e != prev))
        rows = slice(a * ROW_BLOCK, (a + 1) * ROW_BLOCK)

        if a == 0:
            @pl.when(first & (j == 0))
            def _():
                for c in copies(layer * N_EXPERTS + e):
                    c.start()

        @pl.when(first)
        def _():
            for c in copies(layer * N_EXPERTS + e):
                c.wait()
            w1b[...] = w1f[...].astype(BF16)
            w2b[...] = w2f[...].astype(BF16)
            nxt = lax.while_loop(
                lambda k: (k < N_BLOCKS) & (be_ref[jnp.minimum(k, N_BLOCKS - 1)] == e), lambda k: k + 1, j + 1)
            e_next = be_ref[jnp.minimum(nxt, N_BLOCKS - 1)]

            @pl.when((nxt < N_BLOCKS) & (e_next < N_EXPERTS))
            def _():
                for c in copies(layer * N_EXPERTS + e_next):
                    c.start()

        @pl.when(live)
        def _():
            idx = layer * N_EXPERTS + e
            x = jnp.concatenate(_unpack_rows(x_ref[rows, :]), axis=1).astype(BF16)
            gu = jnp.dot(x, w1b[...], preferred_element_type=F32) + b1_ref[idx]
            gate = jnp.minimum(gu[:, :D_EXPERT], SWIGLU_LIMIT)
            lin = jnp.clip(gu[:, D_EXPERT:], -SWIGLU_LIMIT, SWIGLU_LIMIT)
            act = gate * _sigmoid(SWIGLU_ALPHA * gate) * (lin + 1.0)
            y_ref[rows, :] = _pack_rows(
                jnp.dot(act.astype(BF16), w2b[...], preferred_element_type=F32) + b2_ref[idx])


def _experts(block_e, rows, w1, b1, w2, b2, layer):
    step_rows = EXPERT_STEP * ROW_BLOCK
    return pl.pallas_call(
        functools.partial(_expert_kernel, layer=layer),
        grid_spec=pltpu.PrefetchScalarGridSpec(
            num_scalar_prefetch=1,
            grid=(N_BLOCKS // EXPERT_STEP,),
            in_specs=[
                pl.BlockSpec((step_rows, PACKED), lambda j, be: (j, 0)),
                pl.BlockSpec(memory_space=pl.ANY),
                pl.BlockSpec((DEPTH * N_EXPERTS, 1, 2 * D_EXPERT), lambda j, be: (0, 0, 0)),
                pl.BlockSpec(memory_space=pl.ANY),
                pl.BlockSpec((DEPTH * N_EXPERTS, 1, D_MODEL), lambda j, be: (0, 0, 0)),
            ],
            out_specs=pl.BlockSpec((step_rows, PACKED), lambda j, be: (j, 0)),
            scratch_shapes=[
                pltpu.VMEM((D_MODEL, 2 * D_EXPERT), F32),
                pltpu.VMEM((D_EXPERT, D_MODEL), F32),
                pltpu.VMEM((D_MODEL, 2 * D_EXPERT), BF16),
                pltpu.VMEM((D_EXPERT, D_MODEL), BF16),
                pltpu.SemaphoreType.DMA((2,)),
            ],
        ),
        out_shape=jax.ShapeDtypeStruct((N_ROWS, PACKED), jnp.uint32),
        compiler_params=pltpu.CompilerParams(
            dimension_semantics=("arbitrary",), vmem_limit_bytes=VMEM_LIMIT),
        name="experts",
    )(block_e, rows, w1, b1, w2, b2)


def _final_kernel(x_ref, mod_ref, g_ref, p_ref, fg_ref, *rest, batch):
    o_ref = rest[-1]
    gate = mod_ref[5, pl.ds(batch, 1), :]
    o_ref[0] = _rms(x_ref[0] + gate * _routed_sum(g_ref, p_ref), fg_ref[...])


def _final(xa, mod, gathered, probs, final_g, batch, out=None):
    ctx_tiles = CTX_LEN // TOK_TILE
    tok = lambda i: (batch, i + ctx_tiles, 0)
    in_specs = [
        pl.BlockSpec((1, TOK_TILE, D_MODEL), tok),
        pl.BlockSpec((6, 8, D_MODEL), lambda i: (0, 0, 0)),
        pl.BlockSpec((TOP_K, 1, TOK_TILE, PACKED), lambda i: (0, 0, i + ctx_tiles, 0)),
        pl.BlockSpec((1, TOK_TILE, TOP_K), tok),
        pl.BlockSpec((1, D_MODEL), lambda i: (0, 0)),
    ]
    args = [xa, mod, gathered.reshape(TOP_K, 1, S_TOT, PACKED), probs, final_g]
    aliases = {}
    if out is not None:
        in_specs.append(pl.BlockSpec(memory_space=pl.ANY))
        args.append(out)
        aliases = {len(args) - 1: 0}
    return pl.pallas_call(
        functools.partial(_final_kernel, batch=batch),
        grid=(N_TILES - ctx_tiles,),
        in_specs=in_specs,
        out_specs=pl.BlockSpec((1, TOK_TILE, D_MODEL), lambda i: (batch, i, 0)),
        out_shape=jax.ShapeDtypeStruct((BATCH, SEQ, D_MODEL), F32),
        input_output_aliases=aliases,
        compiler_params=pltpu.CompilerParams(dimension_semantics=("parallel",)),
        name="final",
    )(*args)


def _split_bf16(w):
    bits = lax.bitcast_convert_type(w.astype(F32), jnp.uint32) & jnp.uint32(0xFFFF0000)
    hi = lax.bitcast_convert_type(bits, F32)
    return jnp.concatenate([hi.astype(BF16), (w - hi).astype(BF16)], axis=-1)


def _rope_tables():
    t = np.arange(SEQ)
    axis_dim = HEAD_DIM // 2
    inv_freq = (1.0 / (ROPE_BASE ** (np.arange(0, axis_dim, 2, dtype=np.float32) / axis_dim))).astype(np.float32)
    ang_r = (t // GRID_W).astype(np.float32)[:, None] * inv_freq[None]
    ang_c = (t % GRID_W).astype(np.float32)[:, None] * inv_freq[None]
    ang = np.concatenate([ang_r, ang_r, ang_c, ang_c] * (LANES // HEAD_DIM), axis=-1)
    cos = np.concatenate([np.ones((CTX_LEN, LANES), np.float32), np.cos(ang)], axis=0)
    sin = np.concatenate([np.zeros((CTX_LEN, LANES), np.float32), np.sin(ang)], axis=0)
    return jnp.asarray(cos, F32), jnp.asarray(sin, F32)


def kernel(x, c, ctx, c_ctx, w_ada, b_ada, norm1_g, norm2_g, w_in, ssm_lam_re, ssm_lam_im, ssm_log_dt, ssm_b_re, ssm_b_im, ssm_c_re, ssm_c_im, ssm_d, ssm_w_val, ssm_w_gate, attn_sink, attn_w_up, pool_w, pool_scale, w_out, router_w, router_b, exp_w1, exp_b1, exp_w2, exp_b2, final_g):
    assert x.shape == (BATCH, SEQ, D_MODEL) and ctx.shape == (BATCH, CTX_LEN, D_MODEL)
    cos_t, sin_t = _rope_tables()
    cv = jnp.concatenate([c, c_ctx[None], jnp.zeros((8 - BATCH - 1, D_MODEL), F32)], axis=0)
    mods = _ada(cv, w_ada, b_ada)
    xs = (ctx, x)
    ew1 = exp_w1.reshape(DEPTH * N_EXPERTS, D_MODEL, 2 * D_EXPERT)
    eb1 = exp_b1.reshape(DEPTH * N_EXPERTS, 1, 2 * D_EXPERT)
    ew2 = exp_w2.reshape(DEPTH * N_EXPERTS, D_EXPERT, D_MODEL)
    eb2 = exp_b2.reshape(DEPTH * N_EXPERTS, 1, D_MODEL)
    half_gate = jnp.where(jnp.arange(IN_WIDTH) >= OFF_GATE, 0.5, 1.0).astype(F32)
    w_in_b = (w_in * half_gate).astype(BF16)
    ssm_tables = _ssm_params(ssm_lam_re, ssm_lam_im, ssm_log_dt, ssm_b_re, ssm_b_im, ssm_c_re, ssm_c_im)
    proj = _inproj(xs, mods[0], norm1_g[0][None], w_in_b, 0, cos_t, sin_t)
    out = None
    for l in range(DEPTH):
        mod = mods[l]
        u, kv, q, pool_in = proj[:4]
        if len(proj) > 4:
            xs = (proj[4], proj[4])
        yf, yb = _ssm(u, *ssm_tables, l)
        o = _attn(attn_sink[l], q, kv)
        xa, f, top_e, top_p, tile_cnt = _merge(
            xs, mod, norm1_g[l][None], w_in_b, l, yf, yb, u, ssm_d[l][None],
            (0.5 * ssm_w_val[l]).astype(BF16), (0.5 * ssm_w_gate[l]).astype(BF16), o, attn_w_up[l].astype(BF16),
            pool_in, pool_w[l].astype(BF16), pool_scale[l][None], (0.5 * w_out[l]).astype(BF16), norm2_g[l][None],
            _split_bf16(router_w[l]), router_b[l][None])
        proj = None
        for b in range(BATCH):
            counts = jnp.pad(tile_cnt[b, :, 0, :].sum(axis=0)[None], ((0, 0), (0, LANES - N_EXPERTS)))
            dest, block_e = _route(top_e.reshape(N_TOK, TOP_K), counts, b)
            dest_t = dest.T
            rows = _dispatch(f.reshape(N_TOK, PACKED), dest_t, b)
            yrows = _experts(block_e[:N_BLOCKS, 0], rows, ew1, eb1, ew2, eb2, l)
            gathered = _combine_gather(yrows, dest_t.reshape(-1))
            if l < DEPTH - 1:
                proj = _inproj((xa, xa), mods[l + 1], norm1_g[l + 1][None], w_in_b, l + 1, cos_t, sin_t,
                               (mod, gathered, top_p), b, proj)
            else:
                out = _final(xa, mod, gathered, top_p, final_g[None], b, out)
    return out
```

```python
import functools
import math

import jax
import jax.numpy as jnp
import numpy as np
from jax import lax
from jax.experimental import pallas as pl
from jax.experimental.pallas import tpu as pltpu
from jax.experimental.pallas import tpu_sc as plsc

F32 = jnp.float32
BF16 = jnp.bfloat16
HIGHEST = lax.Precision.HIGHEST

D_MODEL = 1024
BATCH = 2
SEQ = 8192
DEPTH = 2
GRID_W = 64
CTX_LEN = 256
NORM_EPS = 1e-6
SSM_WIDTH = 512
SSM_GROUP = 16
SSM_GROUPS = 32
SSM_STATE = 64
HEAD_DIM = 64
N_Q_HEADS = 8
N_KV_HEADS = 2
GQA_GROUP = 4
ATTN_WIDTH = 512
KV_WIDTH = 128
WINDOW = 128
ROPE_BASE = 10000.0
ATTN_SCALE = HEAD_DIM ** -0.5
POOL_WINDOWS = (2, 4, 8, 16)
POOL_WIDTH = 512
POOL_GROUP = 128
POOL_OUT_GROUP = 256
OFF_K = 512
OFF_V = 640
OFF_Q = 768
OFF_POOL = 1280
OFF_GATE = 1792
IN_WIDTH = OFF_GATE + 3 * D_MODEL
N_EXPERTS = 32
TOP_K = 4
D_EXPERT = 1024
SWIGLU_LIMIT = 7.0
SWIGLU_ALPHA = 1.702

S_TOT = CTX_LEN + SEQ
N_TOK = BATCH * S_TOT
G_TOK = S_TOT
TOK_TILE = 256
N_TILES = S_TOT // TOK_TILE
assert CTX_LEN == TOK_TILE
ATTN_BLOCK = 128
ATTN_STEP = 6
POOL_HALO = 8
ROUTE_TILE = 1056
ROW_BLOCK = 256
EXPERT_STEP = 4
N_ROWS = -(-(G_TOK * TOP_K + N_EXPERTS * (ROW_BLOCK - 1)) // (ROW_BLOCK * EXPERT_STEP)) * ROW_BLOCK * EXPERT_STEP
N_BLOCKS = N_ROWS // ROW_BLOCK
N_BLOCKS_PAD = -(-N_BLOCKS // 8) * 8
SC_CORES = 2
SC_WORKERS = SC_CORES * 16
PACKED = D_MODEL // 2
SC_SCATTER_ROWS = 88
SC_GATHER_ROWS = 88
assert G_TOK % (SC_WORKERS * SC_SCATTER_ROWS) == 0 and G_TOK % ROUTE_TILE == 0
VMEM_LIMIT = 56 * 1024 * 1024
LANES = 128
MASK_VALUE = -1e30


def _rms(x, g):
    return x * lax.rsqrt(jnp.mean(x * x, axis=-1, keepdims=True) + NORM_EPS) * g


def _sigmoid(x):
    return 0.5 * jnp.tanh(0.5 * x) + 0.5


def _pack_rows(x):
    half = x.shape[1] // 2
    lo = lax.bitcast_convert_type(x[:, :half].astype(BF16).astype(F32), jnp.uint32)
    hi = lax.bitcast_convert_type(x[:, half:].astype(BF16).astype(F32), jnp.uint32)
    return (lo >> 16) | hi


def _unpack_rows(p):
    lo = lax.bitcast_convert_type(p << 16, F32)
    hi = lax.bitcast_convert_type(p & jnp.uint32(0xFFFF0000), F32)
    return lo, hi


def _token_specs(xs, last=N_TILES - 1, im=lambda f: f):
    ctx_tiles = CTX_LEN // TOK_TILE
    off = 0 if xs[1].shape[1] == S_TOT else ctx_tiles
    return [pl.BlockSpec((1, TOK_TILE, D_MODEL), im(lambda b, i: (b, 0, 0))),
            pl.BlockSpec((1, TOK_TILE, D_MODEL), im(lambda b, i: (b, jnp.clip(i, ctx_tiles, last) - off, 0)))]


def _token_tile(i, xc_ref, xl_ref):
    return jnp.where(i == 0, xc_ref[0], xl_ref[0])


def _mod_row(i, b):
    return jnp.where(i == 0, 2, b)


def _ada_kernel(cv_ref, w_ref, b_ref, o_ref):
    cv = cv_ref[...]
    s = cv * jax.nn.sigmoid(cv)
    o_ref[0, 0] = jnp.dot(s, w_ref[0], preferred_element_type=F32, precision=HIGHEST) + b_ref[0, 0]


def _ada(cv, w_ada, b_ada):
    return pl.pallas_call(
        _ada_kernel,
        grid=(DEPTH, 6),
        in_specs=[
            pl.BlockSpec((8, D_MODEL), lambda l, j: (0, 0)),
            pl.BlockSpec((1, D_MODEL, D_MODEL), lambda l, j: (l, 0, j)),
            pl.BlockSpec((1, 1, 1, D_MODEL), lambda l, j: (l, j, 0, 0)),
        ],
        out_specs=pl.BlockSpec((1, 1, 8, D_MODEL), lambda l, j: (l, j, 0, 0)),
        out_shape=jax.ShapeDtypeStruct((DEPTH, 6, 8, D_MODEL), F32),
        name="ada",
    )(cv, w_ada, b_ada.reshape(DEPTH, 6, 1, D_MODEL))


def _routed_sum(g_ref, p_ref):
    p = p_ref[0]
    lo, hi = _unpack_rows(g_ref[0, 0])
    y_lo = p[:, 0:1] * lo
    y_hi = p[:, 0:1] * hi
    for k in range(1, TOP_K):
        lo, hi = _unpack_rows(g_ref[k, 0])
        y_lo = y_lo + p[:, k:k + 1] * lo
        y_hi = y_hi + p[:, k:k + 1] * hi
    return jnp.concatenate([y_lo, y_hi], axis=1)


def _rope(t, cos, sin):
    lane = lax.broadcasted_iota(jnp.int32, t.shape, 1)
    rot = jnp.where((lane % 32) < 16, -pltpu.roll(t, LANES - 16, 1), pltpu.roll(t, 16, 1))
    return t * cos + rot * sin


def _inproj_kernel(*refs, pending, batch):
    xc_ref, xl_ref, mod_ref, g_ref, w_ref, cos_ref, sin_ref = refs[:7]
    if pending:
        pmod_ref, rows_ref, probs_ref = refs[7:10]
        u_ref, kv_ref, q_ref, pool_ref, xo_ref = refs[-5:]
    else:
        u_ref, kv_ref, q_ref, pool_ref = refs[-4:]
    b, i = (pl.program_id(0), pl.program_id(1)) if batch is None else (batch, pl.program_id(0))
    row = _mod_row(i, b)
    shift = mod_ref[0, pl.ds(row, 1), :]
    scale = mod_ref[1, pl.ds(row, 1), :]
    x = _token_tile(i, xc_ref, xl_ref)
    if pending:
        x = x + pmod_ref[5, pl.ds(row, 1), :] * _routed_sum(rows_ref, probs_ref)
        xo_ref[0] = x
    h = _rms(x, g_ref[...]) * (1.0 + scale) + shift
    p = jnp.dot(h.astype(BF16), w_ref[0, :, :OFF_GATE], preferred_element_type=F32)
    cos = cos_ref[...]
    sin = sin_ref[...]
    u_ref[0] = p[:, :OFF_K]
    kv_ref[0, :, :KV_WIDTH] = _rope(p[:, OFF_K:OFF_V], cos, sin).astype(BF16)
    kv_ref[0, :, KV_WIDTH:] = p[:, OFF_V:OFF_Q].astype(BF16)
    for c in range(ATTN_WIDTH // LANES):
        qc = p[:, OFF_Q + c * LANES:OFF_Q + (c + 1) * LANES]
        q_ref[0, :, c * LANES:(c + 1) * LANES] = (_rope(qc, cos, sin) * ATTN_SCALE).astype(BF16)
    pool_ref[0] = p[:, OFF_POOL:OFF_GATE]


def _inproj(xs, mod, g1, w_in, layer, cos_t, sin_t, pending=None, batch=None, outs=None):
    im = (lambda f: f) if batch is None else (lambda f: lambda i: f(batch, i))
    tok = im(lambda b, i: (b, i, 0))
    in_specs = [
        *_token_specs(xs, im=im),
        pl.BlockSpec((6, 8, D_MODEL), im(lambda b, i: (0, 0, 0))),
        pl.BlockSpec((1, D_MODEL), im(lambda b, i: (0, 0))),
        pl.BlockSpec((1, D_MODEL, IN_WIDTH), im(lambda b, i: (layer, 0, 0))),
        pl.BlockSpec((TOK_TILE, LANES), im(lambda b, i: (i, 0))),
        pl.BlockSpec((TOK_TILE, LANES), im(lambda b, i: (i, 0))),
    ]
    out_specs = [
        pl.BlockSpec((1, TOK_TILE, SSM_WIDTH), tok),
        pl.BlockSpec((1, TOK_TILE, 2 * KV_WIDTH), tok),
        pl.BlockSpec((1, TOK_TILE, ATTN_WIDTH), tok),
        pl.BlockSpec((1, TOK_TILE, POOL_WIDTH), tok),
    ]
    out_shape = [
        jax.ShapeDtypeStruct((BATCH, S_TOT, SSM_WIDTH), F32),
        jax.ShapeDtypeStruct((BATCH, S_TOT, 2 * KV_WIDTH), BF16),
        jax.ShapeDtypeStruct((BATCH, S_TOT, ATTN_WIDTH), BF16),
        jax.ShapeDtypeStruct((BATCH, S_TOT, POOL_WIDTH), F32),
    ]
    args = [*xs, mod, g1, w_in, cos_t, sin_t]
    if pending is not None:
        pmod, gathered, probs = pending
        in_specs += [
            pl.BlockSpec((6, 8, D_MODEL), im(lambda b, i: (0, 0, 0))),
            pl.BlockSpec((TOP_K, 1, TOK_TILE, PACKED), im(lambda b, i: (0, 0, i, 0))),
            pl.BlockSpec((1, TOK_TILE, TOP_K), tok),
        ]
        out_specs.append(pl.BlockSpec((1, TOK_TILE, D_MODEL), tok))
        out_shape.append(jax.ShapeDtypeStruct((BATCH, S_TOT, D_MODEL), F32))
        args += [pmod, gathered.reshape(TOP_K, 1, S_TOT, PACKED), probs]
    aliases = {}
    if outs is not None:
        aliases = {len(args) + k: k for k in range(len(outs))}
        in_specs += [pl.BlockSpec(memory_space=pl.ANY)] * len(outs)
        args += list(outs)
    return pl.pallas_call(
        functools.partial(_inproj_kernel, pending=pending is not None, batch=batch),
        grid=(BATCH, N_TILES) if batch is None else (N_TILES,),
        in_specs=in_specs,
        out_specs=out_specs,
        out_shape=out_shape,
        input_output_aliases=aliases,
        compiler_params=pltpu.CompilerParams(
            dimension_semantics=("parallel",) * (2 if batch is None else 1), vmem_limit_bytes=VMEM_LIMIT),
        name="inproj",
    )(*args)


SSM_HALF_IN = SSM_WIDTH // 2
SSM_HALF_ST = SSM_GROUPS // 2 * SSM_STATE
N_STATE_ROWS = 8
N_LANE_TILES = SSM_HALF_ST // LANES
SCAN_STRIDE = TOK_TILE + 8


def _ssm_kernel(uf_ref, ub_ref, wb_ref, wc_ref, ar_ref, ai_ref, yf_ref, yb_ref, buf_f, buf_b, hf_ref, hb_ref):
    i = pl.program_id(0)
    T = TOK_TILE

    @pl.when(i == 0)
    def _():
        hf_ref[...] = jnp.zeros_like(hf_ref)
        hb_ref[...] = jnp.zeros_like(hb_ref)

    for d, (u_ref, buf) in enumerate(((uf_ref, buf_f), (ub_ref, buf_b))):
        for b in range(BATCH):
            for half in range(2):
                ub16 = u_ref[b, :, half * SSM_HALF_IN:(half + 1) * SSM_HALF_IN].astype(BF16)
                for reim in range(2):
                    k = reim * 4 + half * 2 + b
                    bu = jnp.dot(ub16, wb_ref[d, reim, half], preferred_element_type=F32)
                    for l in range(N_LANE_TILES):
                        buf[l, k * SCAN_STRIDE:k * SCAN_STRIDE + T, :] = bu[:, l * LANES:(l + 1) * LANES]

    def step(t, carry):
        hf, hb = carry
        tb = T - 1 - t
        nf, nb = [], []
        for l in range(N_LANE_TILES):
            sl = pl.ds(l * LANES, LANES)
            rows_f = pl.ds(t, N_STATE_ROWS, stride=SCAN_STRIDE)
            h = ar_ref[0, :, sl] * hf[l] + ai_ref[0, :, sl] * pltpu.roll(hf[l], 4, 0) + buf_f[l, rows_f, :]
            buf_f[l, rows_f, :] = h
            nf.append(h)
            rows_b = pl.ds(tb, N_STATE_ROWS, stride=SCAN_STRIDE)
            h = ar_ref[1, :, sl] * hb[l] + ai_ref[1, :, sl] * pltpu.roll(hb[l], 4, 0) + buf_b[l, rows_b, :]
            buf_b[l, rows_b, :] = h
            nb.append(h)
        return tuple(nf), tuple(nb)

    hf0 = tuple(hf_ref[:, l * LANES:(l + 1) * LANES] for l in range(N_LANE_TILES))
    hb0 = tuple(hb_ref[:, l * LANES:(l + 1) * LANES] for l in range(N_LANE_TILES))
    hf1, hb1 = lax.fori_loop(0, T, step, (hf0, hb0), unroll=True)
    for l in range(N_LANE_TILES):
        hf_ref[:, l * LANES:(l + 1) * LANES] = hf1[l]
        hb_ref[:, l * LANES:(l + 1) * LANES] = hb1[l]

    for d, (buf, y_ref) in enumerate(((buf_f, yf_ref), (buf_b, yb_ref))):
        for b in range(BATCH):
            for half in range(2):
                kre = half * 2 + b
                kim = 4 + half * 2 + b
                state = lambda k: jnp.concatenate(
                    [buf[l, k * SCAN_STRIDE:k * SCAN_STRIDE + T, :] for l in range(N_LANE_TILES)],
                    axis=1).astype(BF16)
                y = jnp.dot(state(kre), wc_ref[d, 0, half], preferred_element_type=F32)
                y = y + jnp.dot(state(kim), wc_ref[d, 1, half], preferred_element_type=F32)
                y_ref[b, :, half * SSM_HALF_IN:(half + 1) * SSM_HALF_IN] = y


def _ssm(u, wb, wc, ar, ai, layer):
    fwd = lambda i: (0, i, 0)
    bwd = lambda i: (0, jnp.where(i == 0, 0, N_TILES - i), 0)
    return pl.pallas_call(
        _ssm_kernel,
        grid=(N_TILES,),
        in_specs=[
            pl.BlockSpec((BATCH, TOK_TILE, SSM_WIDTH), fwd),
            pl.BlockSpec((BATCH, TOK_TILE, SSM_WIDTH), bwd),
            pl.BlockSpec((None, 2, 2, 2, SSM_HALF_IN, SSM_HALF_ST), lambda i: (layer, 0, 0, 0, 0, 0)),
            pl.BlockSpec((None, 2, 2, 2, SSM_HALF_ST, SSM_HALF_IN), lambda i: (layer, 0, 0, 0, 0, 0)),
            pl.BlockSpec((None, 2, N_STATE_ROWS, SSM_HALF_ST), lambda i: (layer, 0, 0, 0)),
            pl.BlockSpec((None, 2, N_STATE_ROWS, SSM_HALF_ST), lambda i: (layer, 0, 0, 0)),
        ],
        out_specs=[
            pl.BlockSpec((BATCH, TOK_TILE, SSM_WIDTH), fwd),
            pl.BlockSpec((BATCH, TOK_TILE, SSM_WIDTH), bwd),
        ],
        out_shape=[jax.ShapeDtypeStruct((BATCH, S_TOT, SSM_WIDTH), F32)] * 2,
        scratch_shapes=[
            pltpu.VMEM((N_LANE_TILES, N_STATE_ROWS * SCAN_STRIDE, LANES), F32),
            pltpu.VMEM((N_LANE_TILES, N_STATE_ROWS * SCAN_STRIDE, LANES), F32),
            pltpu.VMEM((N_STATE_ROWS, SSM_HALF_ST), F32),
            pltpu.VMEM((N_STATE_ROWS, SSM_HALF_ST), F32),
        ],
        compiler_params=pltpu.CompilerParams(
            dimension_semantics=("arbitrary",), vmem_limit_bytes=VMEM_LIMIT),
        name="ssm",
    )(u, u, wb, wc, ar, ai)


def _ssm_params(lam_re, lam_im, log_dt, b_re, b_im, c_re, c_im):
    lr = lam_re.astype(F32)
    li = lam_im.astype(F32)
    dt = jnp.exp(log_dt.astype(F32))[..., None]
    mag = jnp.exp(lr * dt)
    ang = li * dt
    a_re = mag * jnp.cos(ang)
    a_im = mag * jnp.sin(ang)
    den = lr * lr + li * li
    k_re = ((a_re - 1.0) * lr + a_im * li) / den
    k_im = (a_im * lr - (a_re - 1.0) * li) / den
    bb_re = k_re[..., None] * b_re - k_im[..., None] * b_im
    bb_im = k_re[..., None] * b_im + k_im[..., None] * b_re
    n_g = SSM_GROUPS // 2
    lead = lam_re.shape[:2]
    drive_mask = np.equal.outer(np.arange(SSM_HALF_IN) // SSM_GROUP, np.arange(SSM_HALF_ST) // SSM_STATE)
    repeat_st = np.tile(np.eye(SSM_STATE, dtype=np.float32), (1, n_g))
    repeat_in = np.tile(np.eye(SSM_GROUP, dtype=np.float32), (1, n_g))

    def block_diag(parts, rows_per_half, repeat, mask):
        rows = jnp.stack([jnp.swapaxes(p, -1, -2) for p in parts], axis=2)
        rows = rows.reshape(*lead, 2, 2, rows_per_half, repeat.shape[0])
        full = jnp.einsum('...rp,pc->...rc', rows, repeat, precision=HIGHEST)
        return (full * mask.astype(np.float32)).astype(BF16)

    wb = block_diag((bb_re, bb_im), SSM_HALF_IN, repeat_st, drive_mask)
    wc = block_diag((c_re.astype(F32), -c_im.astype(F32)), SSM_HALF_ST, repeat_in, drive_mask.T)

    def rows(a, sign_re):
        halves = a.reshape(*lead, 1, 2, 1, SSM_HALF_ST)
        signs = jnp.asarray([sign_re, 1.0], F32).reshape(2, 1, 1, 1)
        return jnp.broadcast_to(halves * signs, (*lead, 2, 2, BATCH, SSM_HALF_ST)).reshape(
            *lead, N_STATE_ROWS, SSM_HALF_ST)

    return wb, wc, rows(a_re, 1.0), rows(a_im, -1.0)


def _attn_block(n, q, kv_band, kv_ctx, sink_ref, band_bias):
    kv_all = jnp.concatenate([kv_band, kv_ctx], axis=0)
    nb = 3 * ATTN_BLOCK
    last = SEQ // ATTN_BLOCK - 1
    krow = lax.broadcasted_iota(jnp.int32, (nb, 1), 0)
    lo_r = jnp.where(n < 0, nb, jnp.where(n == 0, ATTN_BLOCK, 0))
    hi_r = jnp.where(n == last, 2 * ATTN_BLOCK, nb)
    bias = band_bias + jnp.where((krow >= lo_r) & (krow < hi_r), 0.0, MASK_VALUE)
    bias = jnp.concatenate([bias] * GQA_GROUP, axis=1)
    nt = (((1,), (1,)), ((), ()))
    outs = []
    for hk in range(N_KV_HEADS):
        k = kv_all[:, hk * HEAD_DIM:(hk + 1) * HEAD_DIM]
        v_t = kv_all[:, KV_WIDTH + hk * HEAD_DIM:KV_WIDTH + (hk + 1) * HEAD_DIM].astype(F32).T.astype(BF16)
        heads = range(hk * GQA_GROUP, (hk + 1) * GQA_GROUP)
        qs = jnp.concatenate([q[:, hq * HEAD_DIM:(hq + 1) * HEAD_DIM] for hq in heads], axis=0)
        sink = jnp.concatenate([jnp.full((1, ATTN_BLOCK), sink_ref[hq], F32) for hq in heads], axis=1)
        s_t = lax.dot_general(k, qs, nt, preferred_element_type=F32)
        s_w = s_t[:nb] + bias
        s_c = s_t[nb:]
        m = jnp.maximum(jnp.maximum(jnp.max(s_w, axis=0, keepdims=True),
                                    jnp.max(s_c, axis=0, keepdims=True)), sink)
        e_w = jnp.exp(s_w - m)
        e_c = jnp.exp(s_c - m)
        den = jnp.sum(e_w, axis=0, keepdims=True) + jnp.sum(e_c, axis=0, keepdims=True) + jnp.exp(sink - m)
        e = jnp.concatenate([e_w, e_c], axis=0).astype(BF16)
        o_t = jnp.dot(v_t, e, preferred_element_type=F32) / den
        outs.extend(o_t[:, g * ATTN_BLOCK:(g + 1) * ATTN_BLOCK].T for g in range(GQA_GROUP))
    return jnp.concatenate(outs, axis=1).astype(BF16)


def _attn_kernel(sink_ref, band_ref, q_ref, kc_ref, *rest):
    kv_refs, o_ref = rest[:-1], rest[-1]
    j0 = pl.program_id(1) * ATTN_STEP
    for a in range(ATTN_STEP):
        n = j0 + a - CTX_LEN // ATTN_BLOCK
        band = jnp.concatenate([kv_refs[a + d][0] for d in range(3)], axis=0)
        rows = slice(a * ATTN_BLOCK, (a + 1) * ATTN_BLOCK)
        o_ref[0, rows, :] = _attn_block(n, q_ref[0, rows, :], band, kc_ref[0], sink_ref, band_ref[...])


def _attn(sink, q, kv):
    nq = S_TOT // ATTN_BLOCK
    first = CTX_LEN // ATTN_BLOCK
    r = jnp.arange(ATTN_BLOCK)[None, :]
    c = jnp.arange(3 * ATTN_BLOCK)[:, None] - ATTN_BLOCK
    band_bias = jnp.where(jnp.abs(c - r) <= WINDOW, 0.0, MASK_VALUE).astype(F32)

    def kv_spec(d):
        return pl.BlockSpec((1, ATTN_BLOCK, 2 * KV_WIDTH),
                            lambda b, j: (b, jnp.clip(j * ATTN_STEP + d - 1, first, nq - 1), 0))

    return pl.pallas_call(
        _attn_kernel,
        grid=(BATCH, nq // ATTN_STEP),
        in_specs=[
            pl.BlockSpec(memory_space=pltpu.SMEM),
            pl.BlockSpec((3 * ATTN_BLOCK, ATTN_BLOCK), lambda b, j: (0, 0)),
            pl.BlockSpec((1, ATTN_STEP * ATTN_BLOCK, ATTN_WIDTH), lambda b, j: (b, j, 0)),
            pl.BlockSpec((1, CTX_LEN, 2 * KV_WIDTH), lambda b, j: (b, 0, 0)),
            *[kv_spec(d) for d in range(ATTN_STEP + 2)],
        ],
        out_specs=pl.BlockSpec((1, ATTN_STEP * ATTN_BLOCK, ATTN_WIDTH), lambda b, j: (b, j, 0)),
        out_shape=jax.ShapeDtypeStruct((BATCH, S_TOT, ATTN_WIDTH), BF16),
        compiler_params=pltpu.CompilerParams(dimension_semantics=("parallel", "parallel")),
        name="attn",
    )(sink, band_bias, q, kv, *[kv] * (ATTN_STEP + 2))


def _gelu_tanh(x):
    return 0.5 * x * (1.0 + jnp.tanh(math.sqrt(2.0 / math.pi) * (x + 0.044715 * (x * x * x))))


def _pool_means(ext, i):
    T = TOK_TILE
    n_ext = T + 2 * POOL_HALO
    seq_len = jnp.where(i == 0, CTX_LEN, SEQ)
    start = jnp.where(i == 0, 0, (i - 1) * T)
    r = start - POOL_HALO + lax.broadcasted_iota(jnp.int32, (n_ext, 1), 0)
    ext = jnp.where((r >= 0) & (r < seq_len), ext, 0.0)
    tl = start + lax.broadcasted_iota(jnp.int32, (T, 1), 0)
    back = lambda a, s: pltpu.roll(a, s, 0)
    fwd = lambda a, s: pltpu.roll(a, n_ext - s, 0)
    diffs = []
    for gi, w in enumerate(POOL_WINDOWS):
        e = ext[:, gi * POOL_GROUP:(gi + 1) * POOL_GROUP]
        lo = w // 2
        hi = w - 1 - lo
        s = e + back(e, 1)
        span = 2
        while span < w:
            s = s + back(s, span)
            span *= 2
        if w > 2:
            s = fwd(s, hi)
        cnt = (jnp.clip(tl + hi + 1, 0, seq_len) - jnp.clip(tl - lo, 0, seq_len)).astype(F32)
        mean = s[POOL_HALO:POOL_HALO + T] / cnt
        diffs.append((mean - e[POOL_HALO:POOL_HALO + T]).astype(BF16))
    return diffs


def _merge_kernel(xc_ref, xl_ref, mod_ref, g1_ref, w_ref, yf_ref, yb_ref, u_ref, dsk_ref, wval_ref, wgate_ref,
                  o_ref, wup_ref, pc_ref, pp_ref, pn_ref, pw_ref, ps_ref, wout_ref, g2_ref, rw_ref, rb_ref,
                  xo_ref, f_ref, te_ref, tp_ref, cnt_ref, xprev_ref):
    b = pl.program_id(0)
    s = pl.program_id(1)

    @pl.when(s == 0)
    def _():
        xprev_ref[...] = jnp.zeros_like(xprev_ref)

    rowp = _mod_row(jnp.maximum(s - 1, 0), b)
    f = _rms(xprev_ref[...], g2_ref[...]) * (1.0 + mod_ref[4, pl.ds(rowp, 1), :]) + mod_ref[3, pl.ds(rowp, 1), :]
    f_ref[0] = _pack_rows(f)
    f_hi = f.astype(BF16)
    f_lo = (f - f_hi.astype(F32)).astype(BF16)
    r_hi = jnp.dot(f_hi, rw_ref[...], preferred_element_type=F32)
    r_lo = jnp.dot(f_lo, rw_ref[...], preferred_element_type=F32)
    logits = r_hi[:, :N_EXPERTS] + r_hi[:, N_EXPERTS:] + r_lo[:, :N_EXPERTS] + rb_ref[...]
    lane = lax.broadcasted_iota(jnp.int32, logits.shape, 1)
    vals, idxs = [], []
    for _ in range(TOP_K):
        mx = jnp.max(logits, axis=-1, keepdims=True)
        ix = jnp.min(jnp.where(logits == mx, lane, N_EXPERTS), axis=-1, keepdims=True)
        vals.append(mx)
        idxs.append(ix)
        logits = jnp.where(lane == ix, -jnp.inf, logits)
    ex = [jnp.exp(v - vals[0]) for v in vals]
    tot = ex[0] + ex[1] + ex[2] + ex[3]
    te_ref[0] = jnp.concatenate(idxs, axis=1)
    tp_ref[0] = jnp.concatenate(ex, axis=1) / tot
    chosen = sum((lane == ix).astype(F32) for ix in idxs)
    cnt_ref[0, 0] = jnp.sum(chosen, axis=0, keepdims=True)

    i = jnp.minimum(s, N_TILES - 1)
    row = _mod_row(i, b)
    shift, scale, gate = (mod_ref[j, pl.ds(row, 1), :] for j in range(3))
    ext = jnp.concatenate([pp_ref[0], pc_ref[0], pn_ref[0]], axis=0)
    diffs = _pool_means(ext, i)
    x = _token_tile(i, xc_ref, xl_ref)
    h = (_rms(x, g1_ref[...]) * (1.0 + scale) + shift).astype(BF16)
    t = jnp.tanh(jnp.dot(h, w_ref[0, :, OFF_GATE:], preferred_element_type=F32))

    z = yf_ref[0] + yb_ref[0] + dsk_ref[...] * u_ref[0]
    z = _gelu_tanh(z).astype(BF16)
    val = jnp.dot(z, wval_ref[...], preferred_element_type=F32)
    y_ssm = val * jnp.tanh(jnp.dot(z, wgate_ref[...], preferred_element_type=F32)) + val
    y_attn = jnp.dot(o_ref[0], wup_ref[...], preferred_element_type=F32)
    y_pool = jnp.concatenate(
        [jnp.dot(diffs[gi], pw_ref[gi], preferred_element_type=F32) for gi in range(len(POOL_WINDOWS))],
        axis=1) * ps_ref[...]

    m = (t[:, :D_MODEL] * y_ssm + t[:, D_MODEL:2 * D_MODEL] * y_attn + t[:, 2 * D_MODEL:] * y_pool
         + (y_ssm + y_attn + y_pool))
    xo = x + gate * jnp.dot(m.astype(BF16), wout_ref[...], preferred_element_type=F32)
    xo_ref[0] = xo
    xprev_ref[...] = xo


def _merge(xs, mod, g1, w_in, layer, yf, yb, u, dsk, wval, wgate, o, wup, pool_in, pw, ps, wout, g2, rw, rb):
    last = N_TILES - 1
    tok = lambda b, s: (b, jnp.minimum(s, last), 0)
    prev = lambda b, s: (b, jnp.maximum(s - 1, 0), 0)
    const2 = lambda b, s: (0, 0)
    const3 = lambda b, s: (0, 0, 0)
    per_tile = TOK_TILE // POOL_HALO
    n_halo = S_TOT // POOL_HALO
    return pl.pallas_call(
        _merge_kernel,
        grid=(BATCH, N_TILES + 1),
        in_specs=[
            *_token_specs(xs, last),
            pl.BlockSpec((6, 8, D_MODEL), const3),
            pl.BlockSpec((1, D_MODEL), const2),
            pl.BlockSpec((1, D_MODEL, IN_WIDTH), lambda b, s: (layer, 0, 0)),
            pl.BlockSpec((1, TOK_TILE, SSM_WIDTH), tok),
            pl.BlockSpec((1, TOK_TILE, SSM_WIDTH), tok),
            pl.BlockSpec((1, TOK_TILE, SSM_WIDTH), tok),
            pl.BlockSpec((1, SSM_WIDTH), const2),
            pl.BlockSpec((SSM_WIDTH, D_MODEL), const2),
            pl.BlockSpec((SSM_WIDTH, D_MODEL), const2),
            pl.BlockSpec((1, TOK_TILE, ATTN_WIDTH), tok),
            pl.BlockSpec((ATTN_WIDTH, D_MODEL), const2),
            pl.BlockSpec((1, TOK_TILE, POOL_WIDTH), tok),
            pl.BlockSpec((1, POOL_HALO, POOL_WIDTH),
                         lambda b, s: (b, jnp.maximum(jnp.minimum(s, last) * per_tile - 1, 0), 0)),
            pl.BlockSpec((1, POOL_HALO, POOL_WIDTH),
                         lambda b, s: (b, jnp.minimum((jnp.minimum(s, last) + 1) * per_tile, n_halo - 1), 0)),
            pl.BlockSpec((len(POOL_WINDOWS), POOL_GROUP, POOL_OUT_GROUP), const3),
            pl.BlockSpec((1, D_MODEL), const2),
            pl.BlockSpec((D_MODEL, D_MODEL), const2),
            pl.BlockSpec((1, D_MODEL), const2),
            pl.BlockSpec((D_MODEL, 2 * N_EXPERTS), const2),
            pl.BlockSpec((1, N_EXPERTS), const2),
        ],
        out_specs=[
            pl.BlockSpec((1, TOK_TILE, D_MODEL), tok),
            pl.BlockSpec((1, TOK_TILE, PACKED), prev),
            pl.BlockSpec((1, TOK_TILE, TOP_K), prev),
            pl.BlockSpec((1, TOK_TILE, TOP_K), prev),
            pl.BlockSpec((1, 1, 1, N_EXPERTS), lambda b, s: (b, jnp.maximum(s - 1, 0), 0, 0)),
        ],
        out_shape=[
            jax.ShapeDtypeStruct((BATCH, S_TOT, D_MODEL), F32),
            jax.ShapeDtypeStruct((BATCH, S_TOT, PACKED), jnp.uint32),
            jax.ShapeDtypeStruct((BATCH, S_TOT, TOP_K), jnp.int32),
            jax.ShapeDtypeStruct((BATCH, S_TOT, TOP_K), F32),
            jax.ShapeDtypeStruct((BATCH, N_TILES, 1, N_EXPERTS), F32),
        ],
        scratch_shapes=[pltpu.VMEM((TOK_TILE, D_MODEL), F32)],
        compiler_params=pltpu.CompilerParams(
            dimension_semantics=("parallel", "arbitrary"), vmem_limit_bytes=VMEM_LIMIT),
        name="merge",
    )(*xs, mod, g1, w_in, yf, yb, u, dsk, wval, wgate, o, wup, pool_in, pool_in, pool_in, pw, ps, wout, g2, rw, rb)


def _route_kernel(te_ref, below_ref, cnt_ref, dest_ref, be_ref, carry_ref):
    i = pl.program_id(0)
    te = te_ref[...]
    lane = lax.broadcasted_iota(jnp.int32, (ROUTE_TILE, LANES), 1)
    ohs = [(te[:, k:k + 1] == lane).astype(F32) for k in range(TOP_K)]
    oh_all = ohs[0] + ohs[1] + ohs[2] + ohs[3]

    @pl.when(i == 0)
    def _():
        cnt = jnp.broadcast_to(cnt_ref[...], (8, LANES))
        padded = jnp.floor((cnt + (ROW_BLOCK - 1)) / ROW_BLOCK) * ROW_BLOCK
        l8 = lax.broadcasted_iota(jnp.int32, (8, LANES), 1)
        end = padded
        s = 1
        while s < N_EXPERTS:
            end = end + jnp.where(l8 >= s, pltpu.roll(end, s, 1), 0.0)
            s *= 2
        carry_ref[...] = (end - padded)[0:1]
        blk_row = (lax.broadcasted_iota(jnp.int32, (N_BLOCKS_PAD, LANES), 0) * ROW_BLOCK).astype(F32)
        lb = lax.broadcasted_iota(jnp.int32, (N_BLOCKS_PAD, LANES), 1)
        hit = jnp.where((lb < N_EXPERTS) & (jnp.broadcast_to(end[0:1], blk_row.shape) <= blk_row), 1.0, 0.0)
        be = jnp.sum(hit, axis=-1, keepdims=True)
        be_ref[...] = jnp.broadcast_to(be, (N_BLOCKS_PAD, LANES)).astype(jnp.int32)

    rank = carry_ref[...] + jnp.dot(below_ref[...], oh_all.astype(BF16), preferred_element_type=F32)
    dests = [jnp.sum(ohs[k] * rank, axis=-1, keepdims=True) for k in range(TOP_K)]
    dest_ref[...] = jnp.concatenate(dests, axis=1).astype(jnp.int32)
    carry_ref[...] += jnp.sum(oh_all, axis=0, keepdims=True)


def _route(top_e, counts, group):
    n_steps = G_TOK // ROUTE_TILE
    return pl.pallas_call(
        _route_kernel,
        grid=(n_steps,),
        in_specs=[pl.BlockSpec((ROUTE_TILE, TOP_K), lambda i: (group * n_steps + i, 0)),
                  pl.BlockSpec((ROUTE_TILE, ROUTE_TILE), lambda i: (0, 0)),
                  pl.BlockSpec((1, LANES), lambda i: (0, 0))],
        out_specs=[
            pl.BlockSpec((ROUTE_TILE, TOP_K), lambda i: (i, 0)),
            pl.BlockSpec((N_BLOCKS_PAD, LANES), lambda i: (0, 0)),
        ],
        out_shape=[
            jax.ShapeDtypeStruct((G_TOK, TOP_K), jnp.int32),
            jax.ShapeDtypeStruct((N_BLOCKS_PAD, LANES), jnp.int32),
        ],
        scratch_shapes=[pltpu.VMEM((1, LANES), F32)],
        compiler_params=pltpu.CompilerParams(dimension_semantics=("arbitrary",)),
        name="route",
    )(top_e, jnp.asarray(np.tril(np.ones((ROUTE_TILE, ROUTE_TILE), np.float32), -1), BF16), counts)


def _sc_mesh():
    return plsc.VectorSubcoreMesh(core_axis_name="c", subcore_axis_name="s")


def _sc_worker():
    return lax.axis_index("s") * SC_CORES + lax.axis_index("c")


def _dispatch(f, dest_t, group):
    per_w = G_TOK // SC_WORKERS
    n_chunks = per_w // SC_SCATTER_ROWS

    @functools.partial(
        pl.kernel,
        out_type=jax.ShapeDtypeStruct((N_ROWS, PACKED), jnp.uint32),
        mesh=_sc_mesh(),
        scratch_types=[
            pltpu.VMEM((SC_SCATTER_ROWS,), jnp.int32),
            pltpu.VMEM((SC_SCATTER_ROWS, PACKED), jnp.uint32),
            pltpu.SemaphoreType.DMA,
        ],
        name="dispatch",
    )
    def k(f_hbm, i_hbm, o_hbm, idx_v, rows_v, sem):
        base = _sc_worker() * per_w

        @pl.loop(0, n_chunks)
        def _(j):
            t0 = pl.multiple_of(base + j * SC_SCATTER_ROWS, 8)
            pltpu.sync_copy(f_hbm.at[pl.ds(group * G_TOK + t0, SC_SCATTER_ROWS)], rows_v)
            for kk in range(TOP_K):
                pltpu.sync_copy(i_hbm.at[pl.ds(kk * G_TOK + t0, SC_SCATTER_ROWS)], idx_v)
                pltpu.async_copy(rows_v, o_hbm.at[idx_v], sem).wait()

    return k(f, dest_t.reshape(TOP_K * G_TOK))


def _combine_gather(yb, idx):
    n = idx.shape[0]
    assert n % (SC_WORKERS * SC_GATHER_ROWS) == 0
    per_w = n // SC_WORKERS
    n_chunks = per_w // SC_GATHER_ROWS

    @functools.partial(
        pl.kernel,
        out_type=jax.ShapeDtypeStruct((n, PACKED), jnp.uint32),
        mesh=_sc_mesh(),
        scratch_types=[
            pltpu.VMEM((SC_GATHER_ROWS,), jnp.int32),
            pltpu.VMEM((SC_GATHER_ROWS, PACKED), jnp.uint32),
            pltpu.SemaphoreType.DMA,
        ],
        name="combine_gather",
    )
    def k(y_hbm, i_hbm, o_hbm, idx_v, rows_v, sem):
        base = _sc_worker() * per_w

        @pl.loop(0, n_chunks)
        def _(j):
            r0 = pl.multiple_of(base + j * SC_GATHER_ROWS, 8)
            pltpu.sync_copy(i_hbm.at[pl.ds(r0, SC_GATHER_ROWS)], idx_v)
            pltpu.async_copy(y_hbm.at[idx_v], rows_v, sem).wait()
            pltpu.sync_copy(rows_v, o_hbm.at[pl.ds(r0, SC_GATHER_ROWS)])

    return k(yb, idx)


def _weight_copies(w1_hbm, w2_hbm, w1f, w2f, sem, idx):
    return (pltpu.make_async_copy(w1_hbm.at[idx], w1f, sem.at[0]),
            pltpu.make_async_copy(w2_hbm.at[idx], w2f, sem.at[1]))


def _expert_kernel(be_ref, x_ref, w1_hbm, b1_ref, w2_hbm, b2_ref, y_ref, w1f, w2f, w1b, w2b, sem, *, layer):
    copies = functools.partial(_weight_copies, w1_hbm, w2_hbm, w1f, w2f, sem)
    for a in range(EXPERT_STEP):
        j = pl.program_id(0) * EXPERT_STEP + a
        e = be_ref[j]
        prev = be_ref[jnp.maximum(j - 1, 0)]
        live = e < N_EXPERTS
        first = live & ((j == 0) | (e != prev))
        rows = slice(a * ROW_BLOCK, (a + 1) * ROW_BLOCK)

        if a == 0:
            @pl.when(first & (j == 0))
            def _():
                for c in copies(layer * N_EXPERTS + e):
                    c.start()

        @pl.when(first)
        def _():
            for c in copies(layer * N_EXPERTS + e):
                c.wait()
            w1b[...] = w1f[...].astype(BF16)
            w2b[...] = w2f[...].astype(BF16)
            nxt = lax.while_loop(
                lambda k: (k < N_BLOCKS) & (be_ref[jnp.minimum(k, N_BLOCKS - 1)] == e), lambda k: k + 1, j + 1)
            e_next = be_ref[jnp.minimum(nxt, N_BLOCKS - 1)]

            @pl.when((nxt < N_BLOCKS) & (e_next < N_EXPERTS))
            def _():
                for c in copies(layer * N_EXPERTS + e_next):
                    c.start()

        @pl.when(live)
        def _():
            idx = layer * N_EXPERTS + e
            x = jnp.concatenate(_unpack_rows(x_ref[rows, :]), axis=1).astype(BF16)
            gu = jnp.dot(x, w1b[...], preferred_element_type=F32) + b1_ref[idx]
            gate = jnp.minimum(gu[:, :D_EXPERT], SWIGLU_LIMIT)
            lin = jnp.clip(gu[:, D_EXPERT:], -SWIGLU_LIMIT, SWIGLU_LIMIT)
            act = gate * _sigmoid(SWIGLU_ALPHA * gate) * (lin + 1.0)
            y_ref[rows, :] = _pack_rows(
                jnp.dot(act.astype(BF16), w2b[...], preferred_element_type=F32) + b2_ref[idx])


def _experts(block_e, rows, w1, b1, w2, b2, layer):
    step_rows = EXPERT_STEP * ROW_BLOCK
    return pl.pallas_call(
        functools.partial(_expert_kernel, layer=layer),
        grid_spec=pltpu.PrefetchScalarGridSpec(
            num_scalar_prefetch=1,
            grid=(N_BLOCKS // EXPERT_STEP,),
            in_specs=[
                pl.BlockSpec((step_rows, PACKED), lambda j, be: (j, 0)),
                pl.BlockSpec(memory_space=pl.ANY),
                pl.BlockSpec((DEPTH * N_EXPERTS, 1, 2 * D_EXPERT), lambda j, be: (0, 0, 0)),
                pl.BlockSpec(memory_space=pl.ANY),
                pl.BlockSpec((DEPTH * N_EXPERTS, 1, D_MODEL), lambda j, be: (0, 0, 0)),
            ],
            out_specs=pl.BlockSpec((step_rows, PACKED), lambda j, be: (j, 0)),
            scratch_shapes=[
                pltpu.VMEM((D_MODEL, 2 * D_EXPERT), F32),
                pltpu.VMEM((D_EXPERT, D_MODEL), F32),
                pltpu.VMEM((D_MODEL, 2 * D_EXPERT), BF16),
                pltpu.VMEM((D_EXPERT, D_MODEL), BF16),
                pltpu.SemaphoreType.DMA((2,)),
            ],
        ),
        out_shape=jax.ShapeDtypeStruct((N_ROWS, PACKED), jnp.uint32),
        compiler_params=pltpu.CompilerParams(
            dimension_semantics=("arbitrary",), vmem_limit_bytes=VMEM_LIMIT),
        name="experts",
    )(block_e, rows, w1, b1, w2, b2)


def _final_kernel(x_ref, mod_ref, g_ref, p_ref, fg_ref, *rest, batch):
    o_ref = rest[-1]
    gate = mod_ref[5, pl.ds(batch, 1), :]
    o_ref[0] = _rms(x_ref[0] + gate * _routed_sum(g_ref, p_ref), fg_ref[...])


def _final(xa, mod, gathered, probs, final_g, batch, out=None):
    ctx_tiles = CTX_LEN // TOK_TILE
    tok = lambda i: (batch, i + ctx_tiles, 0)
    in_specs = [
        pl.BlockSpec((1, TOK_TILE, D_MODEL), tok),
        pl.BlockSpec((6, 8, D_MODEL), lambda i: (0, 0, 0)),
        pl.BlockSpec((TOP_K, 1, TOK_TILE, PACKED), lambda i: (0, 0, i + ctx_tiles, 0)),
        pl.BlockSpec((1, TOK_TILE, TOP_K), tok),
        pl.BlockSpec((1, D_MODEL), lambda i: (0, 0)),
    ]
    args = [xa, mod, gathered.reshape(TOP_K, 1, S_TOT, PACKED), probs, final_g]
    aliases = {}
    if out is not None:
        in_specs.append(pl.BlockSpec(memory_space=pl.ANY))
        args.append(out)
        aliases = {len(args) - 1: 0}
    return pl.pallas_call(
        functools.partial(_final_kernel, batch=batch),
        grid=(N_TILES - ctx_tiles,),
        in_specs=in_specs,
        out_specs=pl.BlockSpec((1, TOK_TILE, D_MODEL), lambda i: (batch, i, 0)),
        out_shape=jax.ShapeDtypeStruct((BATCH, SEQ, D_MODEL), F32),
        input_output_aliases=aliases,
        compiler_params=pltpu.CompilerParams(dimension_semantics=("parallel",)),
        name="final",
    )(*args)


def _split_bf16(w):
    bits = lax.bitcast_convert_type(w.astype(F32), jnp.uint32) & jnp.uint32(0xFFFF0000)
    hi = lax.bitcast_convert_type(bits, F32)
    return jnp.concatenate([hi.astype(BF16), (w - hi).astype(BF16)], axis=-1)


def _rope_tables():
    t = np.arange(SEQ)
    axis_dim = HEAD_DIM // 2
    inv_freq = (1.0 / (ROPE_BASE ** (np.arange(0, axis_dim, 2, dtype=np.float32) / axis_dim))).astype(np.float32)
    ang_r = (t // GRID_W).astype(np.float32)[:, None] * inv_freq[None]
    ang_c = (t % GRID_W).astype(np.float32)[:, None] * inv_freq[None]
    ang = np.concatenate([ang_r, ang_r, ang_c, ang_c] * (LANES // HEAD_DIM), axis=-1)
    cos = np.concatenate([np.ones((CTX_LEN, LANES), np.float32), np.cos(ang)], axis=0)
    sin = np.concatenate([np.zeros((CTX_LEN, LANES), np.float32), np.sin(ang)], axis=0)
    return jnp.asarray(cos, F32), jnp.asarray(sin, F32)


def kernel(x, c, ctx, c_ctx, w_ada, b_ada, norm1_g, norm2_g, w_in, ssm_lam_re, ssm_lam_im, ssm_log_dt, ssm_b_re, ssm_b_im, ssm_c_re, ssm_c_im, ssm_d, ssm_w_val, ssm_w_gate, attn_sink, attn_w_up, pool_w, pool_scale, w_out, router_w, router_b, exp_w1, exp_b1, exp_w2, exp_b2, final_g):
    assert x.shape == (BATCH, SEQ, D_MODEL) and ctx.shape == (BATCH, CTX_LEN, D_MODEL)
    cos_t, sin_t = _rope_tables()
    cv = jnp.concatenate([c, c_ctx[None], jnp.zeros((8 - BATCH - 1, D_MODEL), F32)], axis=0)
    mods = _ada(cv, w_ada, b_ada)
    xs = (ctx, x)
    ew1 = exp_w1.reshape(DEPTH * N_EXPERTS, D_MODEL, 2 * D_EXPERT)
    eb1 = exp_b1.reshape(DEPTH * N_EXPERTS, 1, 2 * D_EXPERT)
    ew2 = exp_w2.reshape(DEPTH * N_EXPERTS, D_EXPERT, D_MODEL)
    eb2 = exp_b2.reshape(DEPTH * N_EXPERTS, 1, D_MODEL)
    half_gate = jnp.where(jnp.arange(IN_WIDTH) >= OFF_GATE, 0.5, 1.0).astype(F32)
    w_in_b = (w_in * half_gate).astype(BF16)
    ssm_tables = _ssm_params(ssm_lam_re, ssm_lam_im, ssm_log_dt, ssm_b_re, ssm_b_im, ssm_c_re, ssm_c_im)
    proj = _inproj(xs, mods[0], norm1_g[0][None], w_in_b, 0, cos_t, sin_t)
    out = None
    for l in range(DEPTH):
        mod = mods[l]
        u, kv, q, pool_in = proj[:4]
        if len(proj) > 4:
            xs = (proj[4], proj[4])
        yf, yb = _ssm(u, *ssm_tables, l)
        o = _attn(attn_sink[l], q, kv)
        xa, f, top_e, top_p, tile_cnt = _merge(
            xs, mod, norm1_g[l][None], w_in_b, l, yf, yb, u, ssm_d[l][None],
            (0.5 * ssm_w_val[l]).astype(BF16), (0.5 * ssm_w_gate[l]).astype(BF16), o, attn_w_up[l].astype(BF16),
            pool_in, pool_w[l].astype(BF16), pool_scale[l][None], (0.5 * w_out[l]).astype(BF16), norm2_g[l][None],
            _split_bf16(router_w[l]), router_b[l][None])
        proj = None
        for b in range(BATCH):
            counts = jnp.pad(tile_cnt[b, :, 0, :].sum(axis=0)[None], ((0, 0), (0, LANES - N_EXPERTS)))
            dest, block_e = _route(top_e.reshape(N_TOK, TOP_K), counts, b)
            dest_t = dest.T
            rows = _dispatch(f.reshape(N_TOK, PACKED), dest_t, b)
            yrows = _experts(block_e[:N_BLOCKS, 0], rows, ew1, eb1, ew2, eb2, l)
            gathered = _combine_gather(yrows, dest_t.reshape(-1))
            if l < DEPTH - 1:
                proj = _inproj((xa, xa), mods[l + 1], norm1_g[l + 1][None], w_in_b, l + 1, cos_t, sin_t,
                               (mod, gathered, top_p), b, proj)
            else:
                out = _final(xa, mod, gathered, top_p, final_g[None], b, out)
    return out
```

```python
import functools
import math

import jax
import jax.numpy as jnp
import numpy as np
from jax import lax
from jax.experimental import pallas as pl
from jax.experimental.pallas import tpu as pltpu
from jax.experimental.pallas import tpu_sc as plsc

F32 = jnp.float32
BF16 = jnp.bfloat16
HIGHEST = lax.Precision.HIGHEST

D_MODEL = 1024
BATCH = 2
SEQ = 8192
DEPTH = 2
GRID_W = 64
CTX_LEN = 256
NORM_EPS = 1e-6
SSM_WIDTH = 512
SSM_GROUP = 16
SSM_GROUPS = 32
SSM_STATE = 64
HEAD_DIM = 64
N_Q_HEADS = 8
N_KV_HEADS = 2
GQA_GROUP = 4
ATTN_WIDTH = 512
KV_WIDTH = 128
WINDOW = 128
ROPE_BASE = 10000.0
ATTN_SCALE = HEAD_DIM ** -0.5
POOL_WINDOWS = (2, 4, 8, 16)
POOL_WIDTH = 512
POOL_GROUP = 128
POOL_OUT_GROUP = 256
OFF_K = 512
OFF_V = 640
OFF_Q = 768
OFF_POOL = 1280
OFF_GATE = 1792
IN_WIDTH = OFF_GATE + 3 * D_MODEL
N_EXPERTS = 32
TOP_K = 4
D_EXPERT = 1024
SWIGLU_LIMIT = 7.0
SWIGLU_ALPHA = 1.702

S_TOT = CTX_LEN + SEQ
N_TOK = BATCH * S_TOT
G_TOK = S_TOT
TOK_TILE = 256
N_TILES = S_TOT // TOK_TILE
assert CTX_LEN == TOK_TILE
ATTN_BLOCK = 128
ATTN_STEP = 6
POOL_HALO = 8
ROUTE_TILE = 1056
ROW_BLOCK = 256
EXPERT_STEP = 4
N_ROWS = -(-(G_TOK * TOP_K + N_EXPERTS * (ROW_BLOCK - 1)) // (ROW_BLOCK * EXPERT_STEP)) * ROW_BLOCK * EXPERT_STEP
N_BLOCKS = N_ROWS // ROW_BLOCK
N_BLOCKS_PAD = -(-N_BLOCKS // 8) * 8
SC_CORES = 2
SC_WORKERS = SC_CORES * 16
PACKED = D_MODEL // 2
SC_SCATTER_ROWS = 88
SC_GATHER_ROWS = 88
assert G_TOK % (SC_WORKERS * SC_SCATTER_ROWS) == 0 and G_TOK % ROUTE_TILE == 0
VMEM_LIMIT = 56 * 1024 * 1024
LANES = 128
MASK_VALUE = -1e30


def _rms(x, g):
    return x * lax.rsqrt(jnp.mean(x * x, axis=-1, keepdims=True) + NORM_EPS) * g


def _sigmoid(x):
    return 0.5 * jnp.tanh(0.5 * x) + 0.5


def _pack_rows(x):
    half = x.shape[1] // 2
    lo = lax.bitcast_convert_type(x[:, :half].astype(BF16).astype(F32), jnp.uint32)
    hi = lax.bitcast_convert_type(x[:, half:].astype(BF16).astype(F32), jnp.uint32)
    return (lo >> 16) | hi


def _unpack_rows(p):
    lo = lax.bitcast_convert_type(p << 16, F32)
    hi = lax.bitcast_convert_type(p & jnp.uint32(0xFFFF0000), F32)
    return lo, hi


def _token_specs(xs, last=N_TILES - 1, im=lambda f: f):
    ctx_tiles = CTX_LEN // TOK_TILE
    off = 0 if xs[1].shape[1] == S_TOT else ctx_tiles
    return [pl.BlockSpec((1, TOK_TILE, D_MODEL), im(lambda b, i: (b, 0, 0))),
            pl.BlockSpec((1, TOK_TILE, D_MODEL), im(lambda b, i: (b, jnp.clip(i, ctx_tiles, last) - off, 0)))]


def _token_tile(i, xc_ref, xl_ref):
    return jnp.where(i == 0, xc_ref[0], xl_ref[0])


def _mod_row(i, b):
    return jnp.where(i == 0, 2, b)


def _ada_kernel(cv_ref, w_ref, b_ref, o_ref):
    cv = cv_ref[...]
    s = cv * jax.nn.sigmoid(cv)
    o_ref[0, 0] = jnp.dot(s, w_ref[0], preferred_element_type=F32, precision=HIGHEST) + b_ref[0, 0]


def _ada(cv, w_ada, b_ada):
    return pl.pallas_call(
        _ada_kernel,
        grid=(DEPTH, 6),
        in_specs=[
            pl.BlockSpec((8, D_MODEL), lambda l, j: (0, 0)),
            pl.BlockSpec((1, D_MODEL, D_MODEL), lambda l, j: (l, 0, j)),
            pl.BlockSpec((1, 1, 1, D_MODEL), lambda l, j: (l, j, 0, 0)),
        ],
        out_specs=pl.BlockSpec((1, 1, 8, D_MODEL), lambda l, j: (l, j, 0, 0)),
        out_shape=jax.ShapeDtypeStruct((DEPTH, 6, 8, D_MODEL), F32),
        name="ada",
    )(cv, w_ada, b_ada.reshape(DEPTH, 6, 1, D_MODEL))


def _routed_sum(g_ref, p_ref):
    p = p_ref[0]
    lo, hi = _unpack_rows(g_ref[0, 0])
    y_lo = p[:, 0:1] * lo
    y_hi = p[:, 0:1] * hi
    for k in range(1, TOP_K):
        lo, hi = _unpack_rows(g_ref[k, 0])
        y_lo = y_lo + p[:, k:k + 1] * lo
        y_hi = y_hi + p[:, k:k + 1] * hi
    return jnp.concatenate([y_lo, y_hi], axis=1)


def _rope(t, cos, sin):
    lane = lax.broadcasted_iota(jnp.int32, t.shape, 1)
    rot = jnp.where((lane % 32) < 16, -pltpu.roll(t, LANES - 16, 1), pltpu.roll(t, 16, 1))
    return t * cos + rot * sin


def _inproj_kernel(*refs, pending, batch):
    xc_ref, xl_ref, mod_ref, g_ref, w_ref, cos_ref, sin_ref = refs[:7]
    if pending:
        pmod_ref, rows_ref, probs_ref = refs[7:10]
        u_ref, kv_ref, q_ref, pool_ref, xo_ref = refs[-5:]
    else:
        u_ref, kv_ref, q_ref, pool_ref = refs[-4:]
    b, i = (pl.program_id(0), pl.program_id(1)) if batch is None else (batch, pl.program_id(0))
    row = _mod_row(i, b)
    shift = mod_ref[0, pl.ds(row, 1), :]
    scale = mod_ref[1, pl.ds(row, 1), :]
    x = _token_tile(i, xc_ref, xl_ref)
    if pending:
        x = x + pmod_ref[5, pl.ds(row, 1), :] * _routed_sum(rows_ref, probs_ref)
        xo_ref[0] = x
    h = _rms(x, g_ref[...]) * (1.0 + scale) + shift
    p = jnp.dot(h.astype(BF16), w_ref[0, :, :OFF_GATE], preferred_element_type=F32)
    cos = cos_ref[...]
    sin = sin_ref[...]
    u_ref[0] = p[:, :OFF_K]
    kv_ref[0, :, :KV_WIDTH] = _rope(p[:, OFF_K:OFF_V], cos, sin).astype(BF16)
    kv_ref[0, :, KV_WIDTH:] = p[:, OFF_V:OFF_Q].astype(BF16)
    for c in range(ATTN_WIDTH // LANES):
        qc = p[:, OFF_Q + c * LANES:OFF_Q + (c + 1) * LANES]
        q_ref[0, :, c * LANES:(c + 1) * LANES] = (_rope(qc, cos, sin) * ATTN_SCALE).astype(BF16)
    pool_ref[0] = p[:, OFF_POOL:OFF_GATE]


def _inproj(xs, mod, g1, w_in, layer, cos_t, sin_t, pending=None, batch=None, outs=None):
    im = (lambda f: f) if batch is None else (lambda f: lambda i: f(batch, i))
    tok = im(lambda b, i: (b, i, 0))
    in_specs = [
        *_token_specs(xs, im=im),
        pl.BlockSpec((6, 8, D_MODEL), im(lambda b, i: (0, 0, 0))),
        pl.BlockSpec((None, 1, D_MODEL), im(lambda b, i: (layer, 0, 0))),
        pl.BlockSpec((1, D_MODEL, IN_WIDTH), im(lambda b, i: (layer, 0, 0))),
        pl.BlockSpec((TOK_TILE, LANES), im(lambda b, i: (i, 0))),
        pl.BlockSpec((TOK_TILE, LANES), im(lambda b, i: (i, 0))),
    ]
    out_specs = [
        pl.BlockSpec((1, TOK_TILE, SSM_WIDTH), tok),
        pl.BlockSpec((1, TOK_TILE, 2 * KV_WIDTH), tok),
        pl.BlockSpec((1, TOK_TILE, ATTN_WIDTH), tok),
        pl.BlockSpec((1, TOK_TILE, POOL_WIDTH), tok),
    ]
    out_shape = [
        jax.ShapeDtypeStruct((BATCH, S_TOT, SSM_WIDTH), F32),
        jax.ShapeDtypeStruct((BATCH, S_TOT, 2 * KV_WIDTH), BF16),
        jax.ShapeDtypeStruct((BATCH, S_TOT, ATTN_WIDTH), BF16),
        jax.ShapeDtypeStruct((BATCH, S_TOT, POOL_WIDTH), F32),
    ]
    args = [*xs, mod, g1, w_in, cos_t, sin_t]
    if pending is not None:
        pmod, gathered, probs = pending
        in_specs += [
            pl.BlockSpec((6, 8, D_MODEL), im(lambda b, i: (0, 0, 0))),
            pl.BlockSpec((TOP_K, 1, TOK_TILE, PACKED), im(lambda b, i: (0, 0, i, 0))),
            pl.BlockSpec((1, TOK_TILE, TOP_K), tok),
        ]
        out_specs.append(pl.BlockSpec((1, TOK_TILE, D_MODEL), tok))
        out_shape.append(jax.ShapeDtypeStruct((BATCH, S_TOT, D_MODEL), F32))
        args += [pmod, gathered.reshape(TOP_K, 1, S_TOT, PACKED), probs]
    aliases = {}
    if outs is not None:
        aliases = {len(args) + k: k for k in range(len(outs))}
        in_specs += [pl.BlockSpec(memory_space=pl.ANY)] * len(outs)
        args += list(outs)
    return pl.pallas_call(
        functools.partial(_inproj_kernel, pending=pending is not None, batch=batch),
        grid=(BATCH, N_TILES) if batch is None else (N_TILES,),
        in_specs=in_specs,
        out_specs=out_specs,
        out_shape=out_shape,
        input_output_aliases=aliases,
        compiler_params=pltpu.CompilerParams(
            dimension_semantics=("parallel",) * (2 if batch is None else 1), vmem_limit_bytes=VMEM_LIMIT),
        name="inproj",
    )(*args)


SSM_HALF_IN = SSM_WIDTH // 2
SSM_HALF_ST = SSM_GROUPS // 2 * SSM_STATE
N_STATE_ROWS = 8
N_LANE_TILES = SSM_HALF_ST // LANES
SCAN_STRIDE = TOK_TILE + 8


def _ssm_kernel(uf_ref, ub_ref, wb_ref, wc_ref, ar_ref, ai_ref, yf_ref, yb_ref, buf_f, buf_b, hf_ref, hb_ref):
    i = pl.program_id(0)
    T = TOK_TILE

    @pl.when(i == 0)
    def _():
        hf_ref[...] = jnp.zeros_like(hf_ref)
        hb_ref[...] = jnp.zeros_like(hb_ref)

    for d, (u_ref, buf) in enumerate(((uf_ref, buf_f), (ub_ref, buf_b))):
        for b in range(BATCH):
            for half in range(2):
                ub16 = u_ref[b, :, half * SSM_HALF_IN:(half + 1) * SSM_HALF_IN].astype(BF16)
                for reim in range(2):
                    k = reim * 4 + half * 2 + b
                    bu = jnp.dot(ub16, wb_ref[d, reim, half], preferred_element_type=F32)
                    for l in range(N_LANE_TILES):
                        buf[l, k * SCAN_STRIDE:k * SCAN_STRIDE + T, :] = bu[:, l * LANES:(l + 1) * LANES]

    def step(t, carry):
        hf, hb = carry
        tb = T - 1 - t
        nf, nb = [], []
        for l in range(N_LANE_TILES):
            sl = pl.ds(l * LANES, LANES)
            rows_f = pl.ds(t, N_STATE_ROWS, stride=SCAN_STRIDE)
            h = ar_ref[0, :, sl] * hf[l] + ai_ref[0, :, sl] * pltpu.roll(hf[l], 4, 0) + buf_f[l, rows_f, :]
            buf_f[l, rows_f, :] = h
            nf.append(h)
            rows_b = pl.ds(tb, N_STATE_ROWS, stride=SCAN_STRIDE)
            h = ar_ref[1, :, sl] * hb[l] + ai_ref[1, :, sl] * pltpu.roll(hb[l], 4, 0) + buf_b[l, rows_b, :]
            buf_b[l, rows_b, :] = h
            nb.append(h)
        return tuple(nf), tuple(nb)

    hf0 = tuple(hf_ref[:, l * LANES:(l + 1) * LANES] for l in range(N_LANE_TILES))
    hb0 = tuple(hb_ref[:, l * LANES:(l + 1) * LANES] for l in range(N_LANE_TILES))
    hf1, hb1 = lax.fori_loop(0, T, step, (hf0, hb0), unroll=True)
    for l in range(N_LANE_TILES):
        hf_ref[:, l * LANES:(l + 1) * LANES] = hf1[l]
        hb_ref[:, l * LANES:(l + 1) * LANES] = hb1[l]

    for d, (buf, y_ref) in enumerate(((buf_f, yf_ref), (buf_b, yb_ref))):
        for b in range(BATCH):
            for half in range(2):
                kre = half * 2 + b
                kim = 4 + half * 2 + b
                state = lambda k: jnp.concatenate(
                    [buf[l, k * SCAN_STRIDE:k * SCAN_STRIDE + T, :] for l in range(N_LANE_TILES)],
                    axis=1).astype(BF16)
                y = jnp.dot(state(kre), wc_ref[d, 0, half], preferred_element_type=F32)
                y = y + jnp.dot(state(kim), wc_ref[d, 1, half], preferred_element_type=F32)
                y_ref[b, :, half * SSM_HALF_IN:(half + 1) * SSM_HALF_IN] = y


def _ssm(u, wb, wc, ar, ai, layer):
    fwd = lambda i: (0, i, 0)
    bwd = lambda i: (0, jnp.where(i == 0, 0, N_TILES - i), 0)
    return pl.pallas_call(
        _ssm_kernel,
        grid=(N_TILES,),
        in_specs=[
            pl.BlockSpec((BATCH, TOK_TILE, SSM_WIDTH), fwd),
            pl.BlockSpec((BATCH, TOK_TILE, SSM_WIDTH), bwd),
            pl.BlockSpec((None, 2, 2, 2, SSM_HALF_IN, SSM_HALF_ST), lambda i: (layer, 0, 0, 0, 0, 0)),
            pl.BlockSpec((None, 2, 2, 2, SSM_HALF_ST, SSM_HALF_IN), lambda i: (layer, 0, 0, 0, 0, 0)),
            pl.BlockSpec((None, 2, N_STATE_ROWS, SSM_HALF_ST), lambda i: (layer, 0, 0, 0)),
            pl.BlockSpec((None, 2, N_STATE_ROWS, SSM_HALF_ST), lambda i: (layer, 0, 0, 0)),
        ],
        out_specs=[
            pl.BlockSpec((BATCH, TOK_TILE, SSM_WIDTH), fwd),
            pl.BlockSpec((BATCH, TOK_TILE, SSM_WIDTH), bwd),
        ],
        out_shape=[jax.ShapeDtypeStruct((BATCH, S_TOT, SSM_WIDTH), F32)] * 2,
        scratch_shapes=[
            pltpu.VMEM((N_LANE_TILES, N_STATE_ROWS * SCAN_STRIDE, LANES), F32),
            pltpu.VMEM((N_LANE_TILES, N_STATE_ROWS * SCAN_STRIDE, LANES), F32),
            pltpu.VMEM((N_STATE_ROWS, SSM_HALF_ST), F32),
            pltpu.VMEM((N_STATE_ROWS, SSM_HALF_ST), F32),
        ],
        compiler_params=pltpu.CompilerParams(
            dimension_semantics=("arbitrary",), vmem_limit_bytes=VMEM_LIMIT),
        name="ssm",
    )(u, u, wb, wc, ar, ai)


def _ssm_params(lam_re, lam_im, log_dt, b_re, b_im, c_re, c_im):
    lr = lam_re.astype(F32)
    li = lam_im.astype(F32)
    dt = jnp.exp(log_dt.astype(F32))[..., None]
    mag = jnp.exp(lr * dt)
    ang = li * dt
    a_re = mag * jnp.cos(ang)
    a_im = mag * jnp.sin(ang)
    den = lr * lr + li * li
    k_re = ((a_re - 1.0) * lr + a_im * li) / den
    k_im = (a_im * lr - (a_re - 1.0) * li) / den
    bb_re = k_re[..., None] * b_re - k_im[..., None] * b_im
    bb_im = k_re[..., None] * b_im + k_im[..., None] * b_re
    n_g = SSM_GROUPS // 2
    lead = lam_re.shape[:2]
    drive_mask = np.equal.outer(np.arange(SSM_HALF_IN) // SSM_GROUP, np.arange(SSM_HALF_ST) // SSM_STATE)
    repeat_st = np.tile(np.eye(SSM_STATE, dtype=np.float32), (1, n_g))
    repeat_in = np.tile(np.eye(SSM_GROUP, dtype=np.float32), (1, n_g))

    def block_diag(parts, rows_per_half, repeat, mask):
        rows = jnp.stack([jnp.swapaxes(p, -1, -2) for p in parts], axis=2)
        rows = rows.reshape(*lead, 2, 2, rows_per_half, repeat.shape[0])
        full = jnp.einsum('...rp,pc->...rc', rows, repeat, precision=HIGHEST)
        return (full * mask.astype(np.float32)).astype(BF16)

    wb = block_diag((bb_re, bb_im), SSM_HALF_IN, repeat_st, drive_mask)
    wc = block_diag((c_re.astype(F32), -c_im.astype(F32)), SSM_HALF_ST, repeat_in, drive_mask.T)

    def rows(a, sign_re):
        halves = a.reshape(*lead, 1, 2, 1, SSM_HALF_ST)
        signs = jnp.asarray([sign_re, 1.0], F32).reshape(2, 1, 1, 1)
        return jnp.broadcast_to(halves * signs, (*lead, 2, 2, BATCH, SSM_HALF_ST)).reshape(
            *lead, N_STATE_ROWS, SSM_HALF_ST)

    return wb, wc, rows(a_re, 1.0), rows(a_im, -1.0)


def _attn_block(n, q, kv_band, kv_ctx, sink_ref, band_bias):
    kv_all = jnp.concatenate([kv_band, kv_ctx], axis=0)
    nb = 3 * ATTN_BLOCK
    last = SEQ // ATTN_BLOCK - 1
    krow = lax.broadcasted_iota(jnp.int32, (nb, 1), 0)
    lo_r = jnp.where(n < 0, nb, jnp.where(n == 0, ATTN_BLOCK, 0))
    hi_r = jnp.where(n == last, 2 * ATTN_BLOCK, nb)
    bias = band_bias + jnp.where((krow >= lo_r) & (krow < hi_r), 0.0, MASK_VALUE)
    bias = jnp.concatenate([bias] * GQA_GROUP, axis=1)
    nt = (((1,), (1,)), ((), ()))
    outs = []
    for hk in range(N_KV_HEADS):
        k = kv_all[:, hk * HEAD_DIM:(hk + 1) * HEAD_DIM]
        v_t = kv_all[:, KV_WIDTH + hk * HEAD_DIM:KV_WIDTH + (hk + 1) * HEAD_DIM].astype(F32).T.astype(BF16)
        heads = range(hk * GQA_GROUP, (hk + 1) * GQA_GROUP)
        qs = jnp.concatenate([q[:, hq * HEAD_DIM:(hq + 1) * HEAD_DIM] for hq in heads], axis=0)
        sink = jnp.concatenate([jnp.full((1, ATTN_BLOCK), sink_ref[hq], F32) for hq in heads], axis=1)
        s_t = lax.dot_general(k, qs, nt, preferred_element_type=F32)
        s_w = s_t[:nb] + bias
        s_c = s_t[nb:]
        m = jnp.maximum(jnp.maximum(jnp.max(s_w, axis=0, keepdims=True),
                                    jnp.max(s_c, axis=0, keepdims=True)), sink)
        e_w = jnp.exp(s_w - m)
        e_c = jnp.exp(s_c - m)
        den = jnp.sum(e_w, axis=0, keepdims=True) + jnp.sum(e_c, axis=0, keepdims=True) + jnp.exp(sink - m)
        e = jnp.concatenate([e_w, e_c], axis=0).astype(BF16)
        o_t = jnp.dot(v_t, e, preferred_element_type=F32) / den
        outs.extend(o_t[:, g * ATTN_BLOCK:(g + 1) * ATTN_BLOCK].T for g in range(GQA_GROUP))
    return jnp.concatenate(outs, axis=1).astype(BF16)


def _attn_kernel(sink_ref, band_ref, q_ref, kc_ref, *rest):
    kv_refs, o_ref = rest[:-1], rest[-1]
    j0 = pl.program_id(1) * ATTN_STEP
    for a in range(ATTN_STEP):
        n = j0 + a - CTX_LEN // ATTN_BLOCK
        band = jnp.concatenate([kv_refs[a + d][0] for d in range(3)], axis=0)
        rows = slice(a * ATTN_BLOCK, (a + 1) * ATTN_BLOCK)
        o_ref[0, rows, :] = _attn_block(n, q_ref[0, rows, :], band, kc_ref[0], sink_ref, band_ref[...])


def _attn(sink, q, kv):
    nq = S_TOT // ATTN_BLOCK
    first = CTX_LEN // ATTN_BLOCK
    r = jnp.arange(ATTN_BLOCK)[None, :]
    c = jnp.arange(3 * ATTN_BLOCK)[:, None] - ATTN_BLOCK
    band_bias = jnp.where(jnp.abs(c - r) <= WINDOW, 0.0, MASK_VALUE).astype(F32)

    def kv_spec(d):
        return pl.BlockSpec((1, ATTN_BLOCK, 2 * KV_WIDTH),
                            lambda b, j: (b, jnp.clip(j * ATTN_STEP + d - 1, first, nq - 1), 0))

    return pl.pallas_call(
        _attn_kernel,
        grid=(BATCH, nq // ATTN_STEP),
        in_specs=[
            pl.BlockSpec(memory_space=pltpu.SMEM),
            pl.BlockSpec((3 * ATTN_BLOCK, ATTN_BLOCK), lambda b, j: (0, 0)),
            pl.BlockSpec((1, ATTN_STEP * ATTN_BLOCK, ATTN_WIDTH), lambda b, j: (b, j, 0)),
            pl.BlockSpec((1, CTX_LEN, 2 * KV_WIDTH), lambda b, j: (b, 0, 0)),
            *[kv_spec(d) for d in range(ATTN_STEP + 2)],
        ],
        out_specs=pl.BlockSpec((1, ATTN_STEP * ATTN_BLOCK, ATTN_WIDTH), lambda b, j: (b, j, 0)),
        out_shape=jax.ShapeDtypeStruct((BATCH, S_TOT, ATTN_WIDTH), BF16),
        compiler_params=pltpu.CompilerParams(dimension_semantics=("parallel", "parallel")),
        name="attn",
    )(sink, band_bias, q, kv, *[kv] * (ATTN_STEP + 2))


def _gelu_tanh(x):
    return 0.5 * x * (1.0 + jnp.tanh(math.sqrt(2.0 / math.pi) * (x + 0.044715 * (x * x * x))))


def _pool_means(ext, i):
    T = TOK_TILE
    n_ext = T + 2 * POOL_HALO
    seq_len = jnp.where(i == 0, CTX_LEN, SEQ)
    start = jnp.where(i == 0, 0, (i - 1) * T)
    r = start - POOL_HALO + lax.broadcasted_iota(jnp.int32, (n_ext, 1), 0)
    ext = jnp.where((r >= 0) & (r < seq_len), ext, 0.0)
    tl = start + lax.broadcasted_iota(jnp.int32, (T, 1), 0)
    back = lambda a, s: pltpu.roll(a, s, 0)
    fwd = lambda a, s: pltpu.roll(a, n_ext - s, 0)
    diffs = []
    for gi, w in enumerate(POOL_WINDOWS):
        e = ext[:, gi * POOL_GROUP:(gi + 1) * POOL_GROUP]
        lo = w // 2
        hi = w - 1 - lo
        s = e + back(e, 1)
        span = 2
        while span < w:
            s = s + back(s, span)
            span *= 2
        if w > 2:
            s = fwd(s, hi)
        cnt = (jnp.clip(tl + hi + 1, 0, seq_len) - jnp.clip(tl - lo, 0, seq_len)).astype(F32)
        mean = s[POOL_HALO:POOL_HALO + T] / cnt
        diffs.append((mean - e[POOL_HALO:POOL_HALO + T]).astype(BF16))
    return diffs


def _merge_kernel(xc_ref, xl_ref, mod_ref, g1_ref, w_ref, yf_ref, yb_ref, u_ref, dsk_ref, wval_ref, wgate_ref,
                  o_ref, wup_ref, pc_ref, pp_ref, pn_ref, pw_ref, ps_ref, wout_ref, g2_ref, rw_ref, rb_ref,
                  xo_ref, f_ref, te_ref, tp_ref, cnt_ref, xprev_ref):
    b = pl.program_id(0)
    s = pl.program_id(1)

    @pl.when(s == 0)
    def _():
        xprev_ref[...] = jnp.zeros_like(xprev_ref)

    rowp = _mod_row(jnp.maximum(s - 1, 0), b)
    f = _rms(xprev_ref[...], g2_ref[...]) * (1.0 + mod_ref[4, pl.ds(rowp, 1), :]) + mod_ref[3, pl.ds(rowp, 1), :]
    f_ref[0] = _pack_rows(f)
    f_hi = f.astype(BF16)
    f_lo = (f - f_hi.astype(F32)).astype(BF16)
    r_hi = jnp.dot(f_hi, rw_ref[...], preferred_element_type=F32)
    r_lo = jnp.dot(f_lo, rw_ref[...], preferred_element_type=F32)
    logits = r_hi[:, :N_EXPERTS] + r_hi[:, N_EXPERTS:] + r_lo[:, :N_EXPERTS] + rb_ref[...]
    lane = lax.broadcasted_iota(jnp.int32, logits.shape, 1)
    vals, idxs = [], []
    for _ in range(TOP_K):
        mx = jnp.max(logits, axis=-1, keepdims=True)
        ix = jnp.min(jnp.where(logits == mx, lane, N_EXPERTS), axis=-1, keepdims=True)
        vals.append(mx)
        idxs.append(ix)
        logits = jnp.where(lane == ix, -jnp.inf, logits)
    ex = [jnp.exp(v - vals[0]) for v in vals]
    tot = ex[0] + ex[1] + ex[2] + ex[3]
    te_ref[0] = jnp.concatenate(idxs, axis=1)
    tp_ref[0] = jnp.concatenate(ex, axis=1) / tot
    chosen = sum((lane == ix).astype(F32) for ix in idxs)
    cnt_ref[0, 0] = jnp.sum(chosen, axis=0, keepdims=True)

    i = jnp.minimum(s, N_TILES - 1)
    row = _mod_row(i, b)
    shift, scale, gate = (mod_ref[j, pl.ds(row, 1), :] for j in range(3))
    ext = jnp.concatenate([pp_ref[0], pc_ref[0], pn_ref[0]], axis=0)
    diffs = _pool_means(ext, i)
    x = _token_tile(i, xc_ref, xl_ref)
    h = (_rms(x, g1_ref[...]) * (1.0 + scale) + shift).astype(BF16)
    t = jnp.tanh(jnp.dot(h, w_ref[0, :, OFF_GATE:], preferred_element_type=F32))

    z = yf_ref[0] + yb_ref[0] + dsk_ref[...] * u_ref[0]
    z = _gelu_tanh(z).astype(BF16)
    val = jnp.dot(z, wval_ref[...], preferred_element_type=F32)
    y_ssm = val * jnp.tanh(jnp.dot(z, wgate_ref[...], preferred_element_type=F32)) + val
    y_attn = jnp.dot(o_ref[0], wup_ref[...], preferred_element_type=F32)
    y_pool = jnp.concatenate(
        [jnp.dot(diffs[gi], pw_ref[gi], preferred_element_type=F32) for gi in range(len(POOL_WINDOWS))],
        axis=1) * ps_ref[...]

    m = (t[:, :D_MODEL] * y_ssm + t[:, D_MODEL:2 * D_MODEL] * y_attn + t[:, 2 * D_MODEL:] * y_pool
         + (y_ssm + y_attn + y_pool))
    xo = x + gate * jnp.dot(m.astype(BF16), wout_ref[...], preferred_element_type=F32)
    xo_ref[0] = xo
    xprev_ref[...] = xo


def _merge(xs, mod, g1, w_in, layer, yf, yb, u, dsk, wval, wgate, o, wup, pool_in, pw, ps, wout, g2, rw, rb):
    last = N_TILES - 1
    tok = lambda b, s: (b, jnp.minimum(s, last), 0)
    prev = lambda b, s: (b, jnp.maximum(s - 1, 0), 0)
    const3 = lambda b, s: (0, 0, 0)
    per_layer3 = lambda b, s: (layer, 0, 0)
    per_tile = TOK_TILE // POOL_HALO
    n_halo = S_TOT // POOL_HALO
    return pl.pallas_call(
        _merge_kernel,
        grid=(BATCH, N_TILES + 1),
        in_specs=[
            *_token_specs(xs, last),
            pl.BlockSpec((6, 8, D_MODEL), const3),
            pl.BlockSpec((None, 1, D_MODEL), per_layer3),
            pl.BlockSpec((1, D_MODEL, IN_WIDTH), per_layer3),
            pl.BlockSpec((1, TOK_TILE, SSM_WIDTH), tok),
            pl.BlockSpec((1, TOK_TILE, SSM_WIDTH), tok),
            pl.BlockSpec((1, TOK_TILE, SSM_WIDTH), tok),
            pl.BlockSpec((None, 1, SSM_WIDTH), per_layer3),
            pl.BlockSpec((None, SSM_WIDTH, D_MODEL), per_layer3),
            pl.BlockSpec((None, SSM_WIDTH, D_MODEL), per_layer3),
            pl.BlockSpec((1, TOK_TILE, ATTN_WIDTH), tok),
            pl.BlockSpec((None, ATTN_WIDTH, D_MODEL), per_layer3),
            pl.BlockSpec((1, TOK_TILE, POOL_WIDTH), tok),
            pl.BlockSpec((1, POOL_HALO, POOL_WIDTH),
                         lambda b, s: (b, jnp.maximum(jnp.minimum(s, last) * per_tile - 1, 0), 0)),
            pl.BlockSpec((1, POOL_HALO, POOL_WIDTH),
                         lambda b, s: (b, jnp.minimum((jnp.minimum(s, last) + 1) * per_tile, n_halo - 1), 0)),
            pl.BlockSpec((None, len(POOL_WINDOWS), POOL_GROUP, POOL_OUT_GROUP), lambda b, s: (layer, 0, 0, 0)),
            pl.BlockSpec((None, 1, D_MODEL), per_layer3),
            pl.BlockSpec((None, D_MODEL, D_MODEL), per_layer3),
            pl.BlockSpec((None, 1, D_MODEL), per_layer3),
            pl.BlockSpec((None, D_MODEL, 2 * N_EXPERTS), per_layer3),
            pl.BlockSpec((None, 1, N_EXPERTS), per_layer3),
        ],
        out_specs=[
            pl.BlockSpec((1, TOK_TILE, D_MODEL), tok),
            pl.BlockSpec((1, TOK_TILE, PACKED), prev),
            pl.BlockSpec((1, TOK_TILE, TOP_K), prev),
            pl.BlockSpec((1, TOK_TILE, TOP_K), prev),
            pl.BlockSpec((1, 1, 1, N_EXPERTS), lambda b, s: (b, jnp.maximum(s - 1, 0), 0, 0)),
        ],
        out_shape=[
            jax.ShapeDtypeStruct((BATCH, S_TOT, D_MODEL), F32),
            jax.ShapeDtypeStruct((BATCH, S_TOT, PACKED), jnp.uint32),
            jax.ShapeDtypeStruct((BATCH, S_TOT, TOP_K), jnp.int32),
            jax.ShapeDtypeStruct((BATCH, S_TOT, TOP_K), F32),
            jax.ShapeDtypeStruct((BATCH, N_TILES, 1, N_EXPERTS), F32),
        ],
        scratch_shapes=[pltpu.VMEM((TOK_TILE, D_MODEL), F32)],
        compiler_params=pltpu.CompilerParams(
            dimension_semantics=("parallel", "arbitrary"), vmem_limit_bytes=VMEM_LIMIT),
        name="merge",
    )(*xs, mod, g1, w_in, yf, yb, u, dsk, wval, wgate, o, wup, pool_in, pool_in, pool_in, pw, ps, wout, g2, rw, rb)


def _route_kernel(te_ref, below_ref, cnt_ref, dest_ref, be_ref, carry_ref):
    i = pl.program_id(0)
    te = te_ref[...]
    lane = lax.broadcasted_iota(jnp.int32, (ROUTE_TILE, LANES), 1)
    ohs = [(te[:, k:k + 1] == lane).astype(F32) for k in range(TOP_K)]
    oh_all = ohs[0] + ohs[1] + ohs[2] + ohs[3]

    @pl.when(i == 0)
    def _():
        cnt = jnp.broadcast_to(cnt_ref[...], (8, LANES))
        padded = jnp.floor((cnt + (ROW_BLOCK - 1)) / ROW_BLOCK) * ROW_BLOCK
        l8 = lax.broadcasted_iota(jnp.int32, (8, LANES), 1)
        end = padded
        s = 1
        while s < N_EXPERTS:
            end = end + jnp.where(l8 >= s, pltpu.roll(end, s, 1), 0.0)
            s *= 2
        carry_ref[...] = (end - padded)[0:1]
        blk_row = (lax.broadcasted_iota(jnp.int32, (N_BLOCKS_PAD, LANES), 0) * ROW_BLOCK).astype(F32)
        lb = lax.broadcasted_iota(jnp.int32, (N_BLOCKS_PAD, LANES), 1)
        hit = jnp.where((lb < N_EXPERTS) & (jnp.broadcast_to(end[0:1], blk_row.shape) <= blk_row), 1.0, 0.0)
        be = jnp.sum(hit, axis=-1, keepdims=True)
        be_ref[...] = jnp.broadcast_to(be, (N_BLOCKS_PAD, LANES)).astype(jnp.int32)

    rank = carry_ref[...] + jnp.dot(below_ref[...], oh_all.astype(BF16), preferred_element_type=F32)
    dests = [jnp.sum(ohs[k] * rank, axis=-1, keepdims=True) for k in range(TOP_K)]
    dest_ref[...] = jnp.concatenate(dests, axis=1).astype(jnp.int32)
    carry_ref[...] += jnp.sum(oh_all, axis=0, keepdims=True)


def _route(top_e, counts, group):
    n_steps = G_TOK // ROUTE_TILE
    return pl.pallas_call(
        _route_kernel,
        grid=(n_steps,),
        in_specs=[pl.BlockSpec((ROUTE_TILE, TOP_K), lambda i: (group * n_steps + i, 0)),
                  pl.BlockSpec((ROUTE_TILE, ROUTE_TILE), lambda i: (0, 0)),
                  pl.BlockSpec((1, LANES), lambda i: (0, 0))],
        out_specs=[
            pl.BlockSpec((ROUTE_TILE, TOP_K), lambda i: (i, 0)),
            pl.BlockSpec((N_BLOCKS_PAD, LANES), lambda i: (0, 0)),
        ],
        out_shape=[
            jax.ShapeDtypeStruct((G_TOK, TOP_K), jnp.int32),
            jax.ShapeDtypeStruct((N_BLOCKS_PAD, LANES), jnp.int32),
        ],
        scratch_shapes=[pltpu.VMEM((1, LANES), F32)],
        compiler_params=pltpu.CompilerParams(dimension_semantics=("arbitrary",)),
        name="route",
    )(top_e, jnp.asarray(np.tril(np.ones((ROUTE_TILE, ROUTE_TILE), np.float32), -1), BF16), counts)


def _sc_mesh():
    return plsc.VectorSubcoreMesh(core_axis_name="c", subcore_axis_name="s")


def _sc_worker():
    return lax.axis_index("s") * SC_CORES + lax.axis_index("c")


def _dispatch(f, dest_t, group):
    per_w = G_TOK // SC_WORKERS
    n_chunks = per_w // SC_SCATTER_ROWS

    @functools.partial(
        pl.kernel,
        out_type=jax.ShapeDtypeStruct((N_ROWS, PACKED), jnp.uint32),
        mesh=_sc_mesh(),
        scratch_types=[
            pltpu.VMEM((SC_SCATTER_ROWS,), jnp.int32),
            pltpu.VMEM((SC_SCATTER_ROWS, PACKED), jnp.uint32),
            pltpu.SemaphoreType.DMA,
        ],
        name="dispatch",
    )
    def k(f_hbm, i_hbm, o_hbm, idx_v, rows_v, sem):
        base = _sc_worker() * per_w

        @pl.loop(0, n_chunks)
        def _(j):
            t0 = pl.multiple_of(base + j * SC_SCATTER_ROWS, 8)
            pltpu.sync_copy(f_hbm.at[pl.ds(group * G_TOK + t0, SC_SCATTER_ROWS)], rows_v)
            for kk in range(TOP_K):
                pltpu.sync_copy(i_hbm.at[pl.ds(kk * G_TOK + t0, SC_SCATTER_ROWS)], idx_v)
                pltpu.async_copy(rows_v, o_hbm.at[idx_v], sem).wait()

    return k(f, dest_t.reshape(TOP_K * G_TOK))


def _combine_gather(yb, idx):
    n = idx.shape[0]
    assert n % (SC_WORKERS * SC_GATHER_ROWS) == 0
    per_w = n // SC_WORKERS
    n_chunks = per_w // SC_GATHER_ROWS

    @functools.partial(
        pl.kernel,
        out_type=jax.ShapeDtypeStruct((n, PACKED), jnp.uint32),
        mesh=_sc_mesh(),
        scratch_types=[
            pltpu.VMEM((SC_GATHER_ROWS,), jnp.int32),
            pltpu.VMEM((SC_GATHER_ROWS, PACKED), jnp.uint32),
            pltpu.SemaphoreType.DMA,
        ],
        name="combine_gather",
    )
    def k(y_hbm, i_hbm, o_hbm, idx_v, rows_v, sem):
        base = _sc_worker() * per_w

        @pl.loop(0, n_chunks)
        def _(j):
            r0 = pl.multiple_of(base + j * SC_GATHER_ROWS, 8)
            pltpu.sync_copy(i_hbm.at[pl.ds(r0, SC_GATHER_ROWS)], idx_v)
            pltpu.async_copy(y_hbm.at[idx_v], rows_v, sem).wait()
            pltpu.sync_copy(rows_v, o_hbm.at[pl.ds(r0, SC_GATHER_ROWS)])

    return k(yb, idx)


def _weight_copies(w1_hbm, w2_hbm, w1f, w2f, sem, idx):
    return (pltpu.make_async_copy(w1_hbm.at[idx], w1f, sem.at[0]),
            pltpu.make_async_copy(w2_hbm.at[idx], w2f, sem.at[1]))


def _expert_kernel(be_ref, x_ref, w1_hbm, b1_ref, w2_hbm, b2_ref, y_ref, w1f, w2f, w1b, w2b, sem, *, layer):
    copies = functools.partial(_weight_copies, w1_hbm, w2_hbm, w1f, w2f, sem)
    for a in range(EXPERT_STEP):
        j = pl.program_id(0) * EXPERT_STEP + a
        e = be_ref[j]
        prev = be_ref[jnp.maximum(j - 1, 0)]
        live = e < N_EXPERTS
        first = live & ((j == 0) | (e != prev))
        rows = slice(a * ROW_BLOCK, (a + 1) * ROW_BLOCK)

        if a == 0:
            @pl.when(first & (j == 0))
            def _():
                for c in copies(layer * N_EXPERTS + e):
                    c.start()

        @pl.when(first)
        def _():
            for c in copies(layer * N_EXPERTS + e):
                c.wait()
            w1b[...] = w1f[...].astype(BF16)
            w2b[...] = w2f[...].astype(BF16)
            nxt = lax.while_loop(
                lambda k: (k < N_BLOCKS) & (be_ref[jnp.minimum(k, N_BLOCKS - 1)] == e), lambda k: k + 1, j + 1)
            e_next = be_ref[jnp.minimum(nxt, N_BLOCKS - 1)]

            @pl.when((nxt < N_BLOCKS) & (e_next < N_EXPERTS))
            def _():
                for c in copies(layer * N_EXPERTS + e_next):
                    c.start()

        @pl.when(live)
        def _():
            idx = layer * N_EXPERTS + e
            x = jnp.concatenate(_unpack_rows(x_ref[rows, :]), axis=1).astype(BF16)
            gu = jnp.dot(x, w1b[...], preferred_element_type=F32) + b1_ref[idx]
            gate = jnp.minimum(gu[:, :D_EXPERT], SWIGLU_LIMIT)
            lin = jnp.clip(gu[:, D_EXPERT:], -SWIGLU_LIMIT, SWIGLU_LIMIT)
            act = gate * _sigmoid(SWIGLU_ALPHA * gate) * (lin + 1.0)
            y_ref[rows, :] = _pack_rows(
                jnp.dot(act.astype(BF16), w2b[...], preferred_element_type=F32) + b2_ref[idx])


def _experts(block_e, rows, w1, b1, w2, b2, layer):
    step_rows = EXPERT_STEP * ROW_BLOCK
    return pl.pallas_call(
        functools.partial(_expert_kernel, layer=layer),
        grid_spec=pltpu.PrefetchScalarGridSpec(
            num_scalar_prefetch=1,
            grid=(N_BLOCKS // EXPERT_STEP,),
            in_specs=[
                pl.BlockSpec((step_rows, PACKED), lambda j, be: (j, 0)),
                pl.BlockSpec(memory_space=pl.ANY),
                pl.BlockSpec((DEPTH * N_EXPERTS, 1, 2 * D_EXPERT), lambda j, be: (0, 0, 0)),
                pl.BlockSpec(memory_space=pl.ANY),
                pl.BlockSpec((DEPTH * N_EXPERTS, 1, D_MODEL), lambda j, be: (0, 0, 0)),
            ],
            out_specs=pl.BlockSpec((step_rows, PACKED), lambda j, be: (j, 0)),
            scratch_shapes=[
                pltpu.VMEM((D_MODEL, 2 * D_EXPERT), F32),
                pltpu.VMEM((D_EXPERT, D_MODEL), F32),
                pltpu.VMEM((D_MODEL, 2 * D_EXPERT), BF16),
                pltpu.VMEM((D_EXPERT, D_MODEL), BF16),
                pltpu.SemaphoreType.DMA((2,)),
            ],
        ),
        out_shape=jax.ShapeDtypeStruct((N_ROWS, PACKED), jnp.uint32),
        compiler_params=pltpu.CompilerParams(
            dimension_semantics=("arbitrary",), vmem_limit_bytes=VMEM_LIMIT),
        name="experts",
    )(block_e, rows, w1, b1, w2, b2)


def _final_kernel(x_ref, mod_ref, g_ref, p_ref, fg_ref, *rest, batch):
    o_ref = rest[-1]
    gate = mod_ref[5, pl.ds(batch, 1), :]
    o_ref[0] = _rms(x_ref[0] + gate * _routed_sum(g_ref, p_ref), fg_ref[...])


def _final(xa, mod, gathered, probs, final_g, batch, out=None):
    ctx_tiles = CTX_LEN // TOK_TILE
    tok = lambda i: (batch, i + ctx_tiles, 0)
    in_specs = [
        pl.BlockSpec((1, TOK_TILE, D_MODEL), tok),
        pl.BlockSpec((6, 8, D_MODEL), lambda i: (0, 0, 0)),
        pl.BlockSpec((TOP_K, 1, TOK_TILE, PACKED), lambda i: (0, 0, i + ctx_tiles, 0)),
        pl.BlockSpec((1, TOK_TILE, TOP_K), tok),
        pl.BlockSpec((1, D_MODEL), lambda i: (0, 0)),
    ]
    args = [xa, mod, gathered.reshape(TOP_K, 1, S_TOT, PACKED), probs, final_g]
    aliases = {}
    if out is not None:
        in_specs.append(pl.BlockSpec(memory_space=pl.ANY))
        args.append(out)
        aliases = {len(args) - 1: 0}
    return pl.pallas_call(
        functools.partial(_final_kernel, batch=batch),
        grid=(N_TILES - ctx_tiles,),
        in_specs=in_specs,
        out_specs=pl.BlockSpec((1, TOK_TILE, D_MODEL), lambda i: (batch, i, 0)),
        out_shape=jax.ShapeDtypeStruct((BATCH, SEQ, D_MODEL), F32),
        input_output_aliases=aliases,
        compiler_params=pltpu.CompilerParams(dimension_semantics=("parallel",)),
        name="final",
    )(*args)


def _split_bf16(w):
    bits = lax.bitcast_convert_type(w.astype(F32), jnp.uint32) & jnp.uint32(0xFFFF0000)
    hi = lax.bitcast_convert_type(bits, F32)
    return jnp.concatenate([hi.astype(BF16), (w - hi).astype(BF16)], axis=-1)


def _rope_tables():
    t = np.arange(SEQ)
    axis_dim = HEAD_DIM // 2
    inv_freq = (1.0 / (ROPE_BASE ** (np.arange(0, axis_dim, 2, dtype=np.float32) / axis_dim))).astype(np.float32)
    ang_r = (t // GRID_W).astype(np.float32)[:, None] * inv_freq[None]
    ang_c = (t % GRID_W).astype(np.float32)[:, None] * inv_freq[None]
    ang = np.concatenate([ang_r, ang_r, ang_c, ang_c] * (LANES // HEAD_DIM), axis=-1)
    cos = np.concatenate([np.ones((CTX_LEN, LANES), np.float32), np.cos(ang)], axis=0)
    sin = np.concatenate([np.zeros((CTX_LEN, LANES), np.float32), np.sin(ang)], axis=0)
    return jnp.asarray(cos, F32), jnp.asarray(sin, F32)


def kernel(x, c, ctx, c_ctx, w_ada, b_ada, norm1_g, norm2_g, w_in, ssm_lam_re, ssm_lam_im, ssm_log_dt, ssm_b_re, ssm_b_im, ssm_c_re, ssm_c_im, ssm_d, ssm_w_val, ssm_w_gate, attn_sink, attn_w_up, pool_w, pool_scale, w_out, router_w, router_b, exp_w1, exp_b1, exp_w2, exp_b2, final_g):
    assert x.shape == (BATCH, SEQ, D_MODEL) and ctx.shape == (BATCH, CTX_LEN, D_MODEL)
    cos_t, sin_t = _rope_tables()
    cv = jnp.concatenate([c, c_ctx[None], jnp.zeros((8 - BATCH - 1, D_MODEL), F32)], axis=0)
    mods = _ada(cv, w_ada, b_ada)
    xs = (ctx, x)
    ew1 = exp_w1.reshape(DEPTH * N_EXPERTS, D_MODEL, 2 * D_EXPERT)
    eb1 = exp_b1.reshape(DEPTH * N_EXPERTS, 1, 2 * D_EXPERT)
    ew2 = exp_w2.reshape(DEPTH * N_EXPERTS, D_EXPERT, D_MODEL)
    eb2 = exp_b2.reshape(DEPTH * N_EXPERTS, 1, D_MODEL)
    half_gate = jnp.where(jnp.arange(IN_WIDTH) >= OFF_GATE, 0.5, 1.0).astype(F32)
    w_in_b = (w_in * half_gate).astype(BF16)
    ssm_tables = _ssm_params(ssm_lam_re, ssm_lam_im, ssm_log_dt, ssm_b_re, ssm_b_im, ssm_c_re, ssm_c_im)
    merge_params = dict(
        g1=norm1_g[:, None], dsk=ssm_d[:, None], wval=(0.5 * ssm_w_val).astype(BF16),
        wgate=(0.5 * ssm_w_gate).astype(BF16), wup=attn_w_up.astype(BF16), pw=pool_w.astype(BF16),
        ps=pool_scale[:, None], wout=(0.5 * w_out).astype(BF16), g2=norm2_g[:, None],
        rw=_split_bf16(router_w), rb=router_b[:, None])
    proj = _inproj(xs, mods[0], norm1_g[:, None], w_in_b, 0, cos_t, sin_t)
    out = None
    for l in range(DEPTH):
        mod = mods[l]
        u, kv, q, pool_in = proj[:4]
        if len(proj) > 4:
            xs = (proj[4], proj[4])
        yf, yb = _ssm(u, *ssm_tables, l)
        o = _attn(attn_sink[l], q, kv)
        xa, f, top_e, top_p, tile_cnt = _merge(xs, mod, w_in=w_in_b, layer=l, yf=yf, yb=yb, u=u, o=o,
                                               pool_in=pool_in, **merge_params)
        proj = None
        for b in range(BATCH):
            counts = jnp.pad(tile_cnt[b, :, 0, :].sum(axis=0)[None], ((0, 0), (0, LANES - N_EXPERTS)))
            dest, block_e = _route(top_e.reshape(N_TOK, TOP_K), counts, b)
            dest_t = dest.T
            rows = _dispatch(f.reshape(N_TOK, PACKED), dest_t, b)
            yrows = _experts(block_e[:N_BLOCKS, 0], rows, ew1, eb1, ew2, eb2, l)
            gathered = _combine_gather(yrows, dest_t.reshape(-1))
            if l < DEPTH - 1:
                proj = _inproj((xa, xa), mods[l + 1], norm1_g[:, None], w_in_b, l + 1, cos_t, sin_t,
                               (mod, gathered, top_p), b, proj)
            else:
                out = _final(xa, mod, gathered, top_p, final_g[None], b, out)
    return out
```

```python
import functools
import math

import jax
import jax.numpy as jnp
import numpy as np
from jax import lax
from jax.experimental import pallas as pl
from jax.experimental.pallas import tpu as pltpu
from jax.experimental.pallas import tpu_sc as plsc

F32 = jnp.float32
BF16 = jnp.bfloat16
HIGHEST = lax.Precision.HIGHEST

D_MODEL = 1024
BATCH = 2
SEQ = 8192
DEPTH = 2
GRID_W = 64
CTX_LEN = 256
NORM_EPS = 1e-6
SSM_WIDTH = 512
SSM_GROUP = 16
SSM_GROUPS = 32
SSM_STATE = 64
HEAD_DIM = 64
N_Q_HEADS = 8
N_KV_HEADS = 2
GQA_GROUP = 4
ATTN_WIDTH = 512
KV_WIDTH = 128
WINDOW = 128
ROPE_BASE = 10000.0
ATTN_SCALE = HEAD_DIM ** -0.5
POOL_WINDOWS = (2, 4, 8, 16)
POOL_WIDTH = 512
POOL_GROUP = 128
POOL_OUT_GROUP = 256
OFF_K = 512
OFF_V = 640
OFF_Q = 768
OFF_POOL = 1280
OFF_GATE = 1792
IN_WIDTH = OFF_GATE + 3 * D_MODEL
N_EXPERTS = 32
TOP_K = 4
D_EXPERT = 1024
SWIGLU_LIMIT = 7.0
SWIGLU_ALPHA = 1.702

S_TOT = CTX_LEN + SEQ
N_TOK = BATCH * S_TOT
G_TOK = S_TOT
TOK_TILE = 256
N_TILES = S_TOT // TOK_TILE
assert CTX_LEN == TOK_TILE
ATTN_BLOCK = 128
ATTN_STEP = 6
POOL_HALO = 8
ROUTE_TILE = 1408
ROW_BLOCK = 256
EXPERT_STEP = 4
N_ROWS = -(-(G_TOK * TOP_K + N_EXPERTS * (ROW_BLOCK - 1)) // (ROW_BLOCK * EXPERT_STEP)) * ROW_BLOCK * EXPERT_STEP
N_BLOCKS = N_ROWS // ROW_BLOCK
N_BLOCKS_PAD = -(-N_BLOCKS // 8) * 8
SC_CORES = 2
SC_WORKERS = SC_CORES * 16
PACKED = D_MODEL // 2
SC_SCATTER_ROWS = 88
SC_GATHER_ROWS = 88
assert G_TOK % (SC_WORKERS * SC_SCATTER_ROWS) == 0 and G_TOK % ROUTE_TILE == 0
VMEM_LIMIT = 56 * 1024 * 1024
LANES = 128
MASK_VALUE = -1e30


def _rms(x, g):
    return x * lax.rsqrt(jnp.mean(x * x, axis=-1, keepdims=True) + NORM_EPS) * g


def _sigmoid(x):
    return 0.5 * jnp.tanh(0.5 * x) + 0.5


def _pack_rows(x):
    half = x.shape[1] // 2
    lo = lax.bitcast_convert_type(x[:, :half].astype(BF16).astype(F32), jnp.uint32)
    hi = lax.bitcast_convert_type(x[:, half:].astype(BF16).astype(F32), jnp.uint32)
    return (lo >> 16) | hi


def _unpack_rows(p):
    lo = lax.bitcast_convert_type(p << 16, F32)
    hi = lax.bitcast_convert_type(p & jnp.uint32(0xFFFF0000), F32)
    return lo, hi


def _token_specs(xs, last=N_TILES - 1, im=lambda f: f):
    ctx_tiles = CTX_LEN // TOK_TILE
    off = 0 if xs[1].shape[1] == S_TOT else ctx_tiles
    return [pl.BlockSpec((1, TOK_TILE, D_MODEL), im(lambda b, i: (b, 0, 0))),
            pl.BlockSpec((1, TOK_TILE, D_MODEL), im(lambda b, i: (b, jnp.clip(i, ctx_tiles, last) - off, 0)))]


def _token_tile(i, xc_ref, xl_ref):
    return jnp.where(i == 0, xc_ref[0], xl_ref[0])


def _mod_row(i, b):
    return jnp.where(i == 0, 2, b)


def _ada_kernel(cv_ref, w_ref, b_ref, o_ref):
    cv = cv_ref[...]
    s = cv * jax.nn.sigmoid(cv)
    o_ref[0, 0] = jnp.dot(s, w_ref[0], preferred_element_type=F32, precision=HIGHEST) + b_ref[0, 0]


def _ada(cv, w_ada, b_ada):
    return pl.pallas_call(
        _ada_kernel,
        grid=(DEPTH, 6),
        in_specs=[
            pl.BlockSpec((8, D_MODEL), lambda l, j: (0, 0)),
            pl.BlockSpec((1, D_MODEL, D_MODEL), lambda l, j: (l, 0, j)),
            pl.BlockSpec((1, 1, 1, D_MODEL), lambda l, j: (l, j, 0, 0)),
        ],
        out_specs=pl.BlockSpec((1, 1, 8, D_MODEL), lambda l, j: (l, j, 0, 0)),
        out_shape=jax.ShapeDtypeStruct((DEPTH, 6, 8, D_MODEL), F32),
        name="ada",
    )(cv, w_ada, b_ada.reshape(DEPTH, 6, 1, D_MODEL))


def _routed_sum(g_ref, p_ref):
    p = p_ref[0]
    lo, hi = _unpack_rows(g_ref[0, 0])
    y_lo = p[:, 0:1] * lo
    y_hi = p[:, 0:1] * hi
    for k in range(1, TOP_K):
        lo, hi = _unpack_rows(g_ref[k, 0])
        y_lo = y_lo + p[:, k:k + 1] * lo
        y_hi = y_hi + p[:, k:k + 1] * hi
    return jnp.concatenate([y_lo, y_hi], axis=1)


def _rope(t, cos, sin):
    lane = lax.broadcasted_iota(jnp.int32, t.shape, 1)
    rot = jnp.where((lane % 32) < 16, -pltpu.roll(t, LANES - 16, 1), pltpu.roll(t, 16, 1))
    return t * cos + rot * sin


def _inproj_kernel(*refs, pending, batch):
    xc_ref, xl_ref, mod_ref, g_ref, w_ref, cos_ref, sin_ref = refs[:7]
    if pending:
        pmod_ref, rows_ref, probs_ref = refs[7:10]
        u_ref, kv_ref, q_ref, pool_ref, xo_ref = refs[-5:]
    else:
        u_ref, kv_ref, q_ref, pool_ref = refs[-4:]
    b, i = (pl.program_id(0), pl.program_id(1)) if batch is None else (batch, pl.program_id(0))
    row = _mod_row(i, b)
    shift = mod_ref[0, pl.ds(row, 1), :]
    scale = mod_ref[1, pl.ds(row, 1), :]
    x = _token_tile(i, xc_ref, xl_ref)
    if pending:
        x = x + pmod_ref[5, pl.ds(row, 1), :] * _routed_sum(rows_ref, probs_ref)
        xo_ref[0] = x
    h = _rms(x, g_ref[...]) * (1.0 + scale) + shift
    p = jnp.dot(h.astype(BF16), w_ref[0, :, :OFF_GATE], preferred_element_type=F32)
    cos = cos_ref[...]
    sin = sin_ref[...]
    u_ref[0] = p[:, :OFF_K]
    kv_ref[0, :, :KV_WIDTH] = _rope(p[:, OFF_K:OFF_V], cos, sin).astype(BF16)
    kv_ref[0, :, KV_WIDTH:] = p[:, OFF_V:OFF_Q].astype(BF16)
    for c in range(ATTN_WIDTH // LANES):
        qc = p[:, OFF_Q + c * LANES:OFF_Q + (c + 1) * LANES]
        q_ref[0, :, c * LANES:(c + 1) * LANES] = (_rope(qc, cos, sin) * ATTN_SCALE).astype(BF16)
    pool_ref[0] = p[:, OFF_POOL:OFF_GATE]


def _inproj(xs, mod, g1, w_in, layer, cos_t, sin_t, pending=None, batch=None, outs=None):
    im = (lambda f: f) if batch is None else (lambda f: lambda i: f(batch, i))
    tok = im(lambda b, i: (b, i, 0))
    in_specs = [
        *_token_specs(xs, im=im),
        pl.BlockSpec((6, 8, D_MODEL), im(lambda b, i: (0, 0, 0))),
        pl.BlockSpec((None, 1, D_MODEL), im(lambda b, i: (layer, 0, 0))),
        pl.BlockSpec((1, D_MODEL, IN_WIDTH), im(lambda b, i: (layer, 0, 0))),
        pl.BlockSpec((TOK_TILE, LANES), im(lambda b, i: (i, 0))),
        pl.BlockSpec((TOK_TILE, LANES), im(lambda b, i: (i, 0))),
    ]
    out_specs = [
        pl.BlockSpec((1, TOK_TILE, SSM_WIDTH), tok),
        pl.BlockSpec((1, TOK_TILE, 2 * KV_WIDTH), tok),
        pl.BlockSpec((1, TOK_TILE, ATTN_WIDTH), tok),
        pl.BlockSpec((1, TOK_TILE, POOL_WIDTH), tok),
    ]
    out_shape = [
        jax.ShapeDtypeStruct((BATCH, S_TOT, SSM_WIDTH), F32),
        jax.ShapeDtypeStruct((BATCH, S_TOT, 2 * KV_WIDTH), BF16),
        jax.ShapeDtypeStruct((BATCH, S_TOT, ATTN_WIDTH), BF16),
        jax.ShapeDtypeStruct((BATCH, S_TOT, POOL_WIDTH), F32),
    ]
    args = [*xs, mod, g1, w_in, cos_t, sin_t]
    if pending is not None:
        pmod, gathered, probs = pending
        in_specs += [
            pl.BlockSpec((6, 8, D_MODEL), im(lambda b, i: (0, 0, 0))),
            pl.BlockSpec((TOP_K, 1, TOK_TILE, PACKED), im(lambda b, i: (0, 0, i, 0))),
            pl.BlockSpec((1, TOK_TILE, TOP_K), tok),
        ]
        out_specs.append(pl.BlockSpec((1, TOK_TILE, D_MODEL), tok))
        out_shape.append(jax.ShapeDtypeStruct((BATCH, S_TOT, D_MODEL), F32))
        args += [pmod, gathered.reshape(TOP_K, 1, S_TOT, PACKED), probs]
    aliases = {}
    if outs is not None:
        aliases = {len(args) + k: k for k in range(len(outs))}
        in_specs += [pl.BlockSpec(memory_space=pl.ANY)] * len(outs)
        args += list(outs)
    return pl.pallas_call(
        functools.partial(_inproj_kernel, pending=pending is not None, batch=batch),
        grid=(BATCH, N_TILES) if batch is None else (N_TILES,),
        in_specs=in_specs,
        out_specs=out_specs,
        out_shape=out_shape,
        input_output_aliases=aliases,
        compiler_params=pltpu.CompilerParams(
            dimension_semantics=("parallel",) * (2 if batch is None else 1), vmem_limit_bytes=VMEM_LIMIT),
        name="inproj",
    )(*args)


SSM_HALF_IN = SSM_WIDTH // 2
SSM_HALF_ST = SSM_GROUPS // 2 * SSM_STATE
N_STATE_ROWS = 8
N_LANE_TILES = SSM_HALF_ST // LANES
SCAN_STRIDE = TOK_TILE + 8


def _ssm_kernel(uf_ref, ub_ref, wb_ref, wc_ref, ar_ref, ai_ref, yf_ref, yb_ref, buf_f, buf_b, hf_ref, hb_ref):
    i = pl.program_id(0)
    T = TOK_TILE

    @pl.when(i == 0)
    def _():
        hf_ref[...] = jnp.zeros_like(hf_ref)
        hb_ref[...] = jnp.zeros_like(hb_ref)

    for d, (u_ref, buf) in enumerate(((uf_ref, buf_f), (ub_ref, buf_b))):
        for b in range(BATCH):
            for half in range(2):
                ub16 = u_ref[b, :, half * SSM_HALF_IN:(half + 1) * SSM_HALF_IN].astype(BF16)
                for reim in range(2):
                    k = reim * 4 + half * 2 + b
                    bu = jnp.dot(ub16, wb_ref[d, reim, half], preferred_element_type=F32)
                    for l in range(N_LANE_TILES):
                        buf[l, k * SCAN_STRIDE:k * SCAN_STRIDE + T, :] = bu[:, l * LANES:(l + 1) * LANES]

    def step(t, carry):
        hf, hb = carry
        tb = T - 1 - t
        nf, nb = [], []
        for l in range(N_LANE_TILES):
            sl = pl.ds(l * LANES, LANES)
            rows_f = pl.ds(t, N_STATE_ROWS, stride=SCAN_STRIDE)
            h = ar_ref[0, :, sl] * hf[l] + ai_ref[0, :, sl] * pltpu.roll(hf[l], 4, 0) + buf_f[l, rows_f, :]
            buf_f[l, rows_f, :] = h
            nf.append(h)
            rows_b = pl.ds(tb, N_STATE_ROWS, stride=SCAN_STRIDE)
            h = ar_ref[1, :, sl] * hb[l] + ai_ref[1, :, sl] * pltpu.roll(hb[l], 4, 0) + buf_b[l, rows_b, :]
            buf_b[l, rows_b, :] = h
            nb.append(h)
        return tuple(nf), tuple(nb)

    hf0 = tuple(hf_ref[:, l * LANES:(l + 1) * LANES] for l in range(N_LANE_TILES))
    hb0 = tuple(hb_ref[:, l * LANES:(l + 1) * LANES] for l in range(N_LANE_TILES))
    hf1, hb1 = lax.fori_loop(0, T, step, (hf0, hb0), unroll=True)
    for l in range(N_LANE_TILES):
        hf_ref[:, l * LANES:(l + 1) * LANES] = hf1[l]
        hb_ref[:, l * LANES:(l + 1) * LANES] = hb1[l]

    for d, (buf, y_ref) in enumerate(((buf_f, yf_ref), (buf_b, yb_ref))):
        for b in range(BATCH):
            for half in range(2):
                kre = half * 2 + b
                kim = 4 + half * 2 + b
                state = lambda k: jnp.concatenate(
                    [buf[l, k * SCAN_STRIDE:k * SCAN_STRIDE + T, :] for l in range(N_LANE_TILES)],
                    axis=1).astype(BF16)
                y = jnp.dot(state(kre), wc_ref[d, 0, half], preferred_element_type=F32)
                y = y + jnp.dot(state(kim), wc_ref[d, 1, half], preferred_element_type=F32)
                y_ref[b, :, half * SSM_HALF_IN:(half + 1) * SSM_HALF_IN] = y


def _ssm(u, wb, wc, ar, ai, layer):
    fwd = lambda i: (0, i, 0)
    bwd = lambda i: (0, jnp.where(i == 0, 0, N_TILES - i), 0)
    return pl.pallas_call(
        _ssm_kernel,
        grid=(N_TILES,),
        in_specs=[
            pl.BlockSpec((BATCH, TOK_TILE, SSM_WIDTH), fwd),
            pl.BlockSpec((BATCH, TOK_TILE, SSM_WIDTH), bwd),
            pl.BlockSpec((None, 2, 2, 2, SSM_HALF_IN, SSM_HALF_ST), lambda i: (layer, 0, 0, 0, 0, 0)),
            pl.BlockSpec((None, 2, 2, 2, SSM_HALF_ST, SSM_HALF_IN), lambda i: (layer, 0, 0, 0, 0, 0)),
            pl.BlockSpec((None, 2, N_STATE_ROWS, SSM_HALF_ST), lambda i: (layer, 0, 0, 0)),
            pl.BlockSpec((None, 2, N_STATE_ROWS, SSM_HALF_ST), lambda i: (layer, 0, 0, 0)),
        ],
        out_specs=[
            pl.BlockSpec((BATCH, TOK_TILE, SSM_WIDTH), fwd),
            pl.BlockSpec((BATCH, TOK_TILE, SSM_WIDTH), bwd),
        ],
        out_shape=[jax.ShapeDtypeStruct((BATCH, S_TOT, SSM_WIDTH), F32)] * 2,
        scratch_shapes=[
            pltpu.VMEM((N_LANE_TILES, N_STATE_ROWS * SCAN_STRIDE, LANES), F32),
            pltpu.VMEM((N_LANE_TILES, N_STATE_ROWS * SCAN_STRIDE, LANES), F32),
            pltpu.VMEM((N_STATE_ROWS, SSM_HALF_ST), F32),
            pltpu.VMEM((N_STATE_ROWS, SSM_HALF_ST), F32),
        ],
        compiler_params=pltpu.CompilerParams(
            dimension_semantics=("arbitrary",), vmem_limit_bytes=VMEM_LIMIT),
        name="ssm",
    )(u, u, wb, wc, ar, ai)


def _ssm_params(lam_re, lam_im, log_dt, b_re, b_im, c_re, c_im):
    lr = lam_re.astype(F32)
    li = lam_im.astype(F32)
    dt = jnp.exp(log_dt.astype(F32))[..., None]
    mag = jnp.exp(lr * dt)
    ang = li * dt
    a_re = mag * jnp.cos(ang)
    a_im = mag * jnp.sin(ang)
    den = lr * lr + li * li
    k_re = ((a_re - 1.0) * lr + a_im * li) / den
    k_im = (a_im * lr - (a_re - 1.0) * li) / den
    bb_re = k_re[..., None] * b_re - k_im[..., None] * b_im
    bb_im = k_re[..., None] * b_im + k_im[..., None] * b_re
    n_g = SSM_GROUPS // 2
    lead = lam_re.shape[:2]
    drive_mask = np.equal.outer(np.arange(SSM_HALF_IN) // SSM_GROUP, np.arange(SSM_HALF_ST) // SSM_STATE)
    repeat_st = np.tile(np.eye(SSM_STATE, dtype=np.float32), (1, n_g))
    repeat_in = np.tile(np.eye(SSM_GROUP, dtype=np.float32), (1, n_g))

    def block_diag(parts, rows_per_half, repeat, mask):
        rows = jnp.stack([jnp.swapaxes(p, -1, -2) for p in parts], axis=2)
        rows = rows.reshape(*lead, 2, 2, rows_per_half, repeat.shape[0])
        full = jnp.einsum('...rp,pc->...rc', rows, repeat, precision=HIGHEST)
        return (full * mask.astype(np.float32)).astype(BF16)

    wb = block_diag((bb_re, bb_im), SSM_HALF_IN, repeat_st, drive_mask)
    wc = block_diag((c_re.astype(F32), -c_im.astype(F32)), SSM_HALF_ST, repeat_in, drive_mask.T)

    def rows(a, sign_re):
        halves = a.reshape(*lead, 1, 2, 1, SSM_HALF_ST)
        signs = jnp.asarray([sign_re, 1.0], F32).reshape(2, 1, 1, 1)
        return jnp.broadcast_to(halves * signs, (*lead, 2, 2, BATCH, SSM_HALF_ST)).reshape(
            *lead, N_STATE_ROWS, SSM_HALF_ST)

    return wb, wc, rows(a_re, 1.0), rows(a_im, -1.0)


def _attn_block(n, q, kv_band, kv_ctx, sink_ref, band_bias):
    kv_all = jnp.concatenate([kv_band, kv_ctx], axis=0)
    nb = 3 * ATTN_BLOCK
    last = SEQ // ATTN_BLOCK - 1
    krow = lax.broadcasted_iota(jnp.int32, (nb, 1), 0)
    lo_r = jnp.where(n < 0, nb, jnp.where(n == 0, ATTN_BLOCK, 0))
    hi_r = jnp.where(n == last, 2 * ATTN_BLOCK, nb)
    bias = band_bias + jnp.where((krow >= lo_r) & (krow < hi_r), 0.0, MASK_VALUE)
    bias = jnp.concatenate([bias] * GQA_GROUP, axis=1)
    nt = (((1,), (1,)), ((), ()))
    outs = []
    for hk in range(N_KV_HEADS):
        k = kv_all[:, hk * HEAD_DIM:(hk + 1) * HEAD_DIM]
        v_t = kv_all[:, KV_WIDTH + hk * HEAD_DIM:KV_WIDTH + (hk + 1) * HEAD_DIM].astype(F32).T.astype(BF16)
        heads = range(hk * GQA_GROUP, (hk + 1) * GQA_GROUP)
        qs = jnp.concatenate([q[:, hq * HEAD_DIM:(hq + 1) * HEAD_DIM] for hq in heads], axis=0)
        sink = jnp.concatenate([jnp.full((1, ATTN_BLOCK), sink_ref[hq], F32) for hq in heads], axis=1)
        s_t = lax.dot_general(k, qs, nt, preferred_element_type=F32)
        s_w = s_t[:nb] + bias
        s_c = s_t[nb:]
        m = jnp.maximum(jnp.maximum(jnp.max(s_w, axis=0, keepdims=True),
                                    jnp.max(s_c, axis=0, keepdims=True)), sink)
        e_w = jnp.exp(s_w - m)
        e_c = jnp.exp(s_c - m)
        den = jnp.sum(e_w, axis=0, keepdims=True) + jnp.sum(e_c, axis=0, keepdims=True) + jnp.exp(sink - m)
        e = jnp.concatenate([e_w, e_c], axis=0).astype(BF16)
        o_t = jnp.dot(v_t, e, preferred_element_type=F32) / den
        outs.extend(o_t[:, g * ATTN_BLOCK:(g + 1) * ATTN_BLOCK].T for g in range(GQA_GROUP))
    return jnp.concatenate(outs, axis=1).astype(BF16)


def _attn_kernel(sink_ref, band_ref, q_ref, kc_ref, *rest):
    kv_refs, o_ref = rest[:-1], rest[-1]
    j0 = pl.program_id(1) * ATTN_STEP
    for a in range(ATTN_STEP):
        n = j0 + a - CTX_LEN // ATTN_BLOCK
        band = jnp.concatenate([kv_refs[a + d][0] for d in range(3)], axis=0)
        rows = slice(a * ATTN_BLOCK, (a + 1) * ATTN_BLOCK)
        o_ref[0, rows, :] = _attn_block(n, q_ref[0, rows, :], band, kc_ref[0], sink_ref, band_ref[...])


def _attn(sink, q, kv):
    nq = S_TOT // ATTN_BLOCK
    first = CTX_LEN // ATTN_BLOCK
    r = jnp.arange(ATTN_BLOCK)[None, :]
    c = jnp.arange(3 * ATTN_BLOCK)[:, None] - ATTN_BLOCK
    band_bias = jnp.where(jnp.abs(c - r) <= WINDOW, 0.0, MASK_VALUE).astype(F32)

    def kv_spec(d):
        return pl.BlockSpec((1, ATTN_BLOCK, 2 * KV_WIDTH),
                            lambda b, j: (b, jnp.clip(j * ATTN_STEP + d - 1, first, nq - 1), 0))

    return pl.pallas_call(
        _attn_kernel,
        grid=(BATCH, nq // ATTN_STEP),
        in_specs=[
            pl.BlockSpec(memory_space=pltpu.SMEM),
            pl.BlockSpec((3 * ATTN_BLOCK, ATTN_BLOCK), lambda b, j: (0, 0)),
            pl.BlockSpec((1, ATTN_STEP * ATTN_BLOCK, ATTN_WIDTH), lambda b, j: (b, j, 0)),
            pl.BlockSpec((1, CTX_LEN, 2 * KV_WIDTH), lambda b, j: (b, 0, 0)),
            *[kv_spec(d) for d in range(ATTN_STEP + 2)],
        ],
        out_specs=pl.BlockSpec((1, ATTN_STEP * ATTN_BLOCK, ATTN_WIDTH), lambda b, j: (b, j, 0)),
        out_shape=jax.ShapeDtypeStruct((BATCH, S_TOT, ATTN_WIDTH), BF16),
        compiler_params=pltpu.CompilerParams(dimension_semantics=("parallel", "parallel")),
        name="attn",
    )(sink, band_bias, q, kv, *[kv] * (ATTN_STEP + 2))


def _gelu_tanh(x):
    return 0.5 * x * (1.0 + jnp.tanh(math.sqrt(2.0 / math.pi) * (x + 0.044715 * (x * x * x))))


def _pool_means(ext, i):
    T = TOK_TILE
    n_ext = T + 2 * POOL_HALO
    seq_len = jnp.where(i == 0, CTX_LEN, SEQ)
    start = jnp.where(i == 0, 0, (i - 1) * T)
    r = start - POOL_HALO + lax.broadcasted_iota(jnp.int32, (n_ext, 1), 0)
    ext = jnp.where((r >= 0) & (r < seq_len), ext, 0.0)
    tl = start + lax.broadcasted_iota(jnp.int32, (T, 1), 0)
    back = lambda a, s: pltpu.roll(a, s, 0)
    fwd = lambda a, s: pltpu.roll(a, n_ext - s, 0)
    diffs = []
    for gi, w in enumerate(POOL_WINDOWS):
        e = ext[:, gi * POOL_GROUP:(gi + 1) * POOL_GROUP]
        lo = w // 2
        hi = w - 1 - lo
        s = e + back(e, 1)
        span = 2
        while span < w:
            s = s + back(s, span)
            span *= 2
        if w > 2:
            s = fwd(s, hi)
        cnt = (jnp.clip(tl + hi + 1, 0, seq_len) - jnp.clip(tl - lo, 0, seq_len)).astype(F32)
        mean = s[POOL_HALO:POOL_HALO + T] / cnt
        diffs.append((mean - e[POOL_HALO:POOL_HALO + T]).astype(BF16))
    return diffs


def _merge_kernel(xc_ref, xl_ref, mod_ref, g1_ref, w_ref, yf_ref, yb_ref, u_ref, dsk_ref, wval_ref, wgate_ref,
                  o_ref, wup_ref, pc_ref, pp_ref, pn_ref, pw_ref, ps_ref, wout_ref, g2_ref, rw_ref, rb_ref,
                  xo_ref, f_ref, te_ref, tp_ref, cnt_ref, xprev_ref):
    b = pl.program_id(0)
    s = pl.program_id(1)

    @pl.when(s == 0)
    def _():
        xprev_ref[...] = jnp.zeros_like(xprev_ref)

    rowp = _mod_row(jnp.maximum(s - 1, 0), b)
    f = _rms(xprev_ref[...], g2_ref[...]) * (1.0 + mod_ref[4, pl.ds(rowp, 1), :]) + mod_ref[3, pl.ds(rowp, 1), :]
    f_ref[0] = _pack_rows(f)
    f_hi = f.astype(BF16)
    f_lo = (f - f_hi.astype(F32)).astype(BF16)
    r_hi = jnp.dot(f_hi, rw_ref[...], preferred_element_type=F32)
    r_lo = jnp.dot(f_lo, rw_ref[...], preferred_element_type=F32)
    logits = r_hi[:, :N_EXPERTS] + r_hi[:, N_EXPERTS:] + r_lo[:, :N_EXPERTS] + rb_ref[...]
    lane = lax.broadcasted_iota(jnp.int32, logits.shape, 1)
    vals, idxs = [], []
    for _ in range(TOP_K):
        mx = jnp.max(logits, axis=-1, keepdims=True)
        ix = jnp.min(jnp.where(logits == mx, lane, N_EXPERTS), axis=-1, keepdims=True)
        vals.append(mx)
        idxs.append(ix)
        logits = jnp.where(lane == ix, -jnp.inf, logits)
    ex = [jnp.exp(v - vals[0]) for v in vals]
    tot = ex[0] + ex[1] + ex[2] + ex[3]
    te_ref[0] = jnp.concatenate(idxs, axis=1)
    tp_ref[0] = jnp.concatenate(ex, axis=1) / tot
    chosen = sum((lane == ix).astype(F32) for ix in idxs)
    cnt_ref[0, 0] = jnp.sum(chosen, axis=0, keepdims=True)

    i = jnp.minimum(s, N_TILES - 1)
    row = _mod_row(i, b)
    shift, scale, gate = (mod_ref[j, pl.ds(row, 1), :] for j in range(3))
    ext = jnp.concatenate([pp_ref[0], pc_ref[0], pn_ref[0]], axis=0)
    diffs = _pool_means(ext, i)
    x = _token_tile(i, xc_ref, xl_ref)
    h = (_rms(x, g1_ref[...]) * (1.0 + scale) + shift).astype(BF16)
    t = jnp.tanh(jnp.dot(h, w_ref[0, :, OFF_GATE:], preferred_element_type=F32))

    z = yf_ref[0] + yb_ref[0] + dsk_ref[...] * u_ref[0]
    z = _gelu_tanh(z).astype(BF16)
    val = jnp.dot(z, wval_ref[...], preferred_element_type=F32)
    y_ssm = val * jnp.tanh(jnp.dot(z, wgate_ref[...], preferred_element_type=F32)) + val
    y_attn = jnp.dot(o_ref[0], wup_ref[...], preferred_element_type=F32)
    y_pool = jnp.concatenate(
        [jnp.dot(diffs[gi], pw_ref[gi], preferred_element_type=F32) for gi in range(len(POOL_WINDOWS))],
        axis=1) * ps_ref[...]

    m = (t[:, :D_MODEL] * y_ssm + t[:, D_MODEL:2 * D_MODEL] * y_attn + t[:, 2 * D_MODEL:] * y_pool
         + (y_ssm + y_attn + y_pool))
    xo = x + gate * jnp.dot(m.astype(BF16), wout_ref[...], preferred_element_type=F32)
    xo_ref[0] = xo
    xprev_ref[...] = xo


def _merge(xs, mod, g1, w_in, layer, yf, yb, u, dsk, wval, wgate, o, wup, pool_in, pw, ps, wout, g2, rw, rb):
    last = N_TILES - 1
    tok = lambda b, s: (b, jnp.minimum(s, last), 0)
    prev = lambda b, s: (b, jnp.maximum(s - 1, 0), 0)
    const3 = lambda b, s: (0, 0, 0)
    per_layer3 = lambda b, s: (layer, 0, 0)
    per_tile = TOK_TILE // POOL_HALO
    n_halo = S_TOT // POOL_HALO
    return pl.pallas_call(
        _merge_kernel,
        grid=(BATCH, N_TILES + 1),
        in_specs=[
            *_token_specs(xs, last),
            pl.BlockSpec((6, 8, D_MODEL), const3),
            pl.BlockSpec((None, 1, D_MODEL), per_layer3),
            pl.BlockSpec((1, D_MODEL, IN_WIDTH), per_layer3),
            pl.BlockSpec((1, TOK_TILE, SSM_WIDTH), tok),
            pl.BlockSpec((1, TOK_TILE, SSM_WIDTH), tok),
            pl.BlockSpec((1, TOK_TILE, SSM_WIDTH), tok),
            pl.BlockSpec((None, 1, SSM_WIDTH), per_layer3),
            pl.BlockSpec((None, SSM_WIDTH, D_MODEL), per_layer3),
            pl.BlockSpec((None, SSM_WIDTH, D_MODEL), per_layer3),
            pl.BlockSpec((1, TOK_TILE, ATTN_WIDTH), tok),
            pl.BlockSpec((None, ATTN_WIDTH, D_MODEL), per_layer3),
            pl.BlockSpec((1, TOK_TILE, POOL_WIDTH), tok),
            pl.BlockSpec((1, POOL_HALO, POOL_WIDTH),
                         lambda b, s: (b, jnp.maximum(jnp.minimum(s, last) * per_tile - 1, 0), 0)),
            pl.BlockSpec((1, POOL_HALO, POOL_WIDTH),
                         lambda b, s: (b, jnp.minimum((jnp.minimum(s, last) + 1) * per_tile, n_halo - 1), 0)),
            pl.BlockSpec((None, len(POOL_WINDOWS), POOL_GROUP, POOL_OUT_GROUP), lambda b, s: (layer, 0, 0, 0)),
            pl.BlockSpec((None, 1, D_MODEL), per_layer3),
            pl.BlockSpec((None, D_MODEL, D_MODEL), per_layer3),
            pl.BlockSpec((None, 1, D_MODEL), per_layer3),
            pl.BlockSpec((None, D_MODEL, 2 * N_EXPERTS), per_layer3),
            pl.BlockSpec((None, 1, N_EXPERTS), per_layer3),
        ],
        out_specs=[
            pl.BlockSpec((1, TOK_TILE, D_MODEL), tok),
            pl.BlockSpec((1, TOK_TILE, PACKED), prev),
            pl.BlockSpec((1, TOK_TILE, TOP_K), prev),
            pl.BlockSpec((1, TOK_TILE, TOP_K), prev),
            pl.BlockSpec((1, 1, 1, N_EXPERTS), lambda b, s: (b, jnp.maximum(s - 1, 0), 0, 0)),
        ],
        out_shape=[
            jax.ShapeDtypeStruct((BATCH, S_TOT, D_MODEL), F32),
            jax.ShapeDtypeStruct((BATCH, S_TOT, PACKED), jnp.uint32),
            jax.ShapeDtypeStruct((BATCH, S_TOT, TOP_K), jnp.int32),
            jax.ShapeDtypeStruct((BATCH, S_TOT, TOP_K), F32),
            jax.ShapeDtypeStruct((BATCH, N_TILES, 1, N_EXPERTS), F32),
        ],
        scratch_shapes=[pltpu.VMEM((TOK_TILE, D_MODEL), F32)],
        compiler_params=pltpu.CompilerParams(
            dimension_semantics=("parallel", "arbitrary"), vmem_limit_bytes=VMEM_LIMIT),
        name="merge",
    )(*xs, mod, g1, w_in, yf, yb, u, dsk, wval, wgate, o, wup, pool_in, pool_in, pool_in, pw, ps, wout, g2, rw, rb)


def _route_kernel(te_ref, below_ref, cnt_ref, dest_ref, be_ref, carry_ref):
    i = pl.program_id(0)
    te = te_ref[...]
    lane = lax.broadcasted_iota(jnp.int32, (ROUTE_TILE, LANES), 1)
    ohs = [(te[:, k:k + 1] == lane).astype(F32) for k in range(TOP_K)]
    oh_all = ohs[0] + ohs[1] + ohs[2] + ohs[3]

    @pl.when(i == 0)
    def _():
        cnt = jnp.broadcast_to(cnt_ref[...], (8, LANES))
        padded = jnp.floor((cnt + (ROW_BLOCK - 1)) / ROW_BLOCK) * ROW_BLOCK
        l8 = lax.broadcasted_iota(jnp.int32, (8, LANES), 1)
        end = padded
        s = 1
        while s < N_EXPERTS:
            end = end + jnp.where(l8 >= s, pltpu.roll(end, s, 1), 0.0)
            s *= 2
        carry_ref[...] = (end - padded)[0:1]
        blk_row = (lax.broadcasted_iota(jnp.int32, (N_BLOCKS_PAD, LANES), 0) * ROW_BLOCK).astype(F32)
        lb = lax.broadcasted_iota(jnp.int32, (N_BLOCKS_PAD, LANES), 1)
        hit = jnp.where((lb < N_EXPERTS) & (jnp.broadcast_to(end[0:1], blk_row.shape) <= blk_row), 1.0, 0.0)
        be = jnp.sum(hit, axis=-1, keepdims=True)
        be_ref[...] = jnp.broadcast_to(be, (N_BLOCKS_PAD, LANES)).astype(jnp.int32)

    rank = carry_ref[...] + jnp.dot(below_ref[...], oh_all.astype(BF16), preferred_element_type=F32)
    dests = [jnp.sum(ohs[k] * rank, axis=-1, keepdims=True) for k in range(TOP_K)]
    wide = jnp.concatenate(dests + [jnp.zeros((ROUTE_TILE, LANES - TOP_K), F32)], axis=1)
    dest_ref[...] = wide.T[:8].astype(jnp.int32)
    carry_ref[...] += jnp.sum(oh_all, axis=0, keepdims=True)


def _route(top_e, counts, group):
    n_steps = G_TOK // ROUTE_TILE
    return pl.pallas_call(
        _route_kernel,
        grid=(n_steps,),
        in_specs=[pl.BlockSpec((ROUTE_TILE, TOP_K), lambda i: (group * n_steps + i, 0)),
                  pl.BlockSpec((ROUTE_TILE, ROUTE_TILE), lambda i: (0, 0)),
                  pl.BlockSpec((1, LANES), lambda i: (0, 0))],
        out_specs=[
            pl.BlockSpec((8, ROUTE_TILE), lambda i: (0, i)),
            pl.BlockSpec((N_BLOCKS_PAD, LANES), lambda i: (0, 0)),
        ],
        out_shape=[
            jax.ShapeDtypeStruct((8, G_TOK), jnp.int32),
            jax.ShapeDtypeStruct((N_BLOCKS_PAD, LANES), jnp.int32),
        ],
        scratch_shapes=[pltpu.VMEM((1, LANES), F32)],
        compiler_params=pltpu.CompilerParams(dimension_semantics=("arbitrary",)),
        name="route",
    )(top_e, jnp.asarray(np.tril(np.ones((ROUTE_TILE, ROUTE_TILE), np.float32), -1), BF16), counts)


def _sc_mesh():
    return plsc.VectorSubcoreMesh(core_axis_name="c", subcore_axis_name="s")


def _sc_worker():
    return lax.axis_index("s") * SC_CORES + lax.axis_index("c")


def _dispatch(f, dest_t, group):
    per_w = G_TOK // SC_WORKERS
    n_chunks = per_w // SC_SCATTER_ROWS

    @functools.partial(
        pl.kernel,
        out_type=jax.ShapeDtypeStruct((N_ROWS, PACKED), jnp.uint32),
        mesh=_sc_mesh(),
        scratch_types=[
            pltpu.VMEM((SC_SCATTER_ROWS,), jnp.int32),
            pltpu.VMEM((SC_SCATTER_ROWS, PACKED), jnp.uint32),
            pltpu.SemaphoreType.DMA,
        ],
        name="dispatch",
    )
    def k(f_hbm, i_hbm, o_hbm, idx_v, rows_v, sem):
        base = _sc_worker() * per_w

        @pl.loop(0, n_chunks)
        def _(j):
            t0 = pl.multiple_of(base + j * SC_SCATTER_ROWS, 8)
            pltpu.sync_copy(f_hbm.at[pl.ds(group * G_TOK + t0, SC_SCATTER_ROWS)], rows_v)
            for kk in range(TOP_K):
                pltpu.sync_copy(i_hbm.at[pl.ds(kk * G_TOK + t0, SC_SCATTER_ROWS)], idx_v)
                pltpu.async_copy(rows_v, o_hbm.at[idx_v], sem).wait()

    return k(f, dest_t.reshape(-1))


def _combine_gather(yb, idx, n):
    assert n % (SC_WORKERS * SC_GATHER_ROWS) == 0
    per_w = n // SC_WORKERS
    n_chunks = per_w // SC_GATHER_ROWS

    @functools.partial(
        pl.kernel,
        out_type=jax.ShapeDtypeStruct((n, PACKED), jnp.uint32),
        mesh=_sc_mesh(),
        scratch_types=[
            pltpu.VMEM((SC_GATHER_ROWS,), jnp.int32),
            pltpu.VMEM((SC_GATHER_ROWS, PACKED), jnp.uint32),
            pltpu.SemaphoreType.DMA,
        ],
        name="combine_gather",
    )
    def k(y_hbm, i_hbm, o_hbm, idx_v, rows_v, sem):
        base = _sc_worker() * per_w

        @pl.loop(0, n_chunks)
        def _(j):
            r0 = pl.multiple_of(base + j * SC_GATHER_ROWS, 8)
            pltpu.sync_copy(i_hbm.at[pl.ds(r0, SC_GATHER_ROWS)], idx_v)
            pltpu.async_copy(y_hbm.at[idx_v], rows_v, sem).wait()
            pltpu.sync_copy(rows_v, o_hbm.at[pl.ds(r0, SC_GATHER_ROWS)])

    return k(yb, idx)


def _weight_copies(w1_hbm, w2_hbm, w1f, w2f, sem, idx):
    return (pltpu.make_async_copy(w1_hbm.at[idx], w1f, sem.at[0]),
            pltpu.make_async_copy(w2_hbm.at[idx], w2f, sem.at[1]))


def _expert_kernel(be_ref, x_ref, w1_hbm, b1_ref, w2_hbm, b2_ref, y_ref, w1f, w2f, w1b, w2b, sem, *, layer):
    copies = functools.partial(_weight_copies, w1_hbm, w2_hbm, w1f, w2f, sem)
    for a in range(EXPERT_STEP):
        j = pl.program_id(0) * EXPERT_STEP + a
        e = be_ref[j]
        prev = be_ref[jnp.maximum(j - 1, 0)]
        live = e < N_EXPERTS
        first = live & ((j == 0) | (e != prev))
        rows = slice(a * ROW_BLOCK, (a + 1) * ROW_BLOCK)

        if a == 0:
            @pl.when(first & (j == 0))
            def _():
                for c in copies(layer * N_EXPERTS + e):
                    c.start()

        @pl.when(first)
        def _():
            for c in copies(layer * N_EXPERTS + e):
                c.wait()
            w1b[...] = w1f[...].astype(BF16)
            w2b[...] = w2f[...].astype(BF16)
            nxt = lax.while_loop(
                lambda k: (k < N_BLOCKS) & (be_ref[jnp.minimum(k, N_BLOCKS - 1)] == e), lambda k: k + 1, j + 1)
            e_next = be_ref[jnp.minimum(nxt, N_BLOCKS - 1)]

            @pl.when((nxt < N_BLOCKS) & (e_next < N_EXPERTS))
            def _():
                for c in copies(layer * N_EXPERTS + e_next):
                    c.start()

        @pl.when(live)
        def _():
            idx = layer * N_EXPERTS + e
            x = jnp.concatenate(_unpack_rows(x_ref[rows, :]), axis=1).astype(BF16)
            gu = jnp.dot(x, w1b[...], preferred_element_type=F32) + b1_ref[idx]
            gate = jnp.minimum(gu[:, :D_EXPERT], SWIGLU_LIMIT)
            lin = jnp.clip(gu[:, D_EXPERT:], -SWIGLU_LIMIT, SWIGLU_LIMIT)
            act = gate * _sigmoid(SWIGLU_ALPHA * gate) * (lin + 1.0)
            y_ref[rows, :] = _pack_rows(
                jnp.dot(act.astype(BF16), w2b[...], preferred_element_type=F32) + b2_ref[idx])


def _experts(block_e, rows, w1, b1, w2, b2, layer):
    step_rows = EXPERT_STEP * ROW_BLOCK
    return pl.pallas_call(
        functools.partial(_expert_kernel, layer=layer),
        grid_spec=pltpu.PrefetchScalarGridSpec(
            num_scalar_prefetch=1,
            grid=(N_BLOCKS // EXPERT_STEP,),
            in_specs=[
                pl.BlockSpec((step_rows, PACKED), lambda j, be: (j, 0)),
                pl.BlockSpec(memory_space=pl.ANY),
                pl.BlockSpec((DEPTH * N_EXPERTS, 1, 2 * D_EXPERT), lambda j, be: (0, 0, 0)),
                pl.BlockSpec(memory_space=pl.ANY),
                pl.BlockSpec((DEPTH * N_EXPERTS, 1, D_MODEL), lambda j, be: (0, 0, 0)),
            ],
            out_specs=pl.BlockSpec((step_rows, PACKED), lambda j, be: (j, 0)),
            scratch_shapes=[
                pltpu.VMEM((D_MODEL, 2 * D_EXPERT), F32),
                pltpu.VMEM((D_EXPERT, D_MODEL), F32),
                pltpu.VMEM((D_MODEL, 2 * D_EXPERT), BF16),
                pltpu.VMEM((D_EXPERT, D_MODEL), BF16),
                pltpu.SemaphoreType.DMA((2,)),
            ],
        ),
        out_shape=jax.ShapeDtypeStruct((N_ROWS, PACKED), jnp.uint32),
        compiler_params=pltpu.CompilerParams(
            dimension_semantics=("arbitrary",), vmem_limit_bytes=VMEM_LIMIT),
        name="experts",
    )(block_e, rows, w1, b1, w2, b2)


def _final_kernel(x_ref, mod_ref, g_ref, p_ref, fg_ref, *rest, batch):
    o_ref = rest[-1]
    gate = mod_ref[5, pl.ds(batch, 1), :]
    o_ref[0] = _rms(x_ref[0] + gate * _routed_sum(g_ref, p_ref), fg_ref[...])


def _final(xa, mod, gathered, probs, final_g, batch, out=None):
    ctx_tiles = CTX_LEN // TOK_TILE
    tok = lambda i: (batch, i + ctx_tiles, 0)
    in_specs = [
        pl.BlockSpec((1, TOK_TILE, D_MODEL), tok),
        pl.BlockSpec((6, 8, D_MODEL), lambda i: (0, 0, 0)),
        pl.BlockSpec((TOP_K, 1, TOK_TILE, PACKED), lambda i: (0, 0, i + ctx_tiles, 0)),
        pl.BlockSpec((1, TOK_TILE, TOP_K), tok),
        pl.BlockSpec((1, D_MODEL), lambda i: (0, 0)),
    ]
    args = [xa, mod, gathered.reshape(TOP_K, 1, S_TOT, PACKED), probs, final_g]
    aliases = {}
    if out is not None:
        in_specs.append(pl.BlockSpec(memory_space=pl.ANY))
        args.append(out)
        aliases = {len(args) - 1: 0}
    return pl.pallas_call(
        functools.partial(_final_kernel, batch=batch),
        grid=(N_TILES - ctx_tiles,),
        in_specs=in_specs,
        out_specs=pl.BlockSpec((1, TOK_TILE, D_MODEL), lambda i: (batch, i, 0)),
        out_shape=jax.ShapeDtypeStruct((BATCH, SEQ, D_MODEL), F32),
        input_output_aliases=aliases,
        compiler_params=pltpu.CompilerParams(dimension_semantics=("parallel",)),
        name="final",
    )(*args)


def _split_bf16(w):
    bits = lax.bitcast_convert_type(w.astype(F32), jnp.uint32) & jnp.uint32(0xFFFF0000)
    hi = lax.bitcast_convert_type(bits, F32)
    return jnp.concatenate([hi.astype(BF16), (w - hi).astype(BF16)], axis=-1)


def _rope_tables():
    t = np.arange(SEQ)
    axis_dim = HEAD_DIM // 2
    inv_freq = (1.0 / (ROPE_BASE ** (np.arange(0, axis_dim, 2, dtype=np.float32) / axis_dim))).astype(np.float32)
    ang_r = (t // GRID_W).astype(np.float32)[:, None] * inv_freq[None]
    ang_c = (t % GRID_W).astype(np.float32)[:, None] * inv_freq[None]
    ang = np.concatenate([ang_r, ang_r, ang_c, ang_c] * (LANES // HEAD_DIM), axis=-1)
    cos = np.concatenate([np.ones((CTX_LEN, LANES), np.float32), np.cos(ang)], axis=0)
    sin = np.concatenate([np.zeros((CTX_LEN, LANES), np.float32), np.sin(ang)], axis=0)
    return jnp.asarray(cos, F32), jnp.asarray(sin, F32)


def kernel(x, c, ctx, c_ctx, w_ada, b_ada, norm1_g, norm2_g, w_in, ssm_lam_re, ssm_lam_im, ssm_log_dt, ssm_b_re, ssm_b_im, ssm_c_re, ssm_c_im, ssm_d, ssm_w_val, ssm_w_gate, attn_sink, attn_w_up, pool_w, pool_scale, w_out, router_w, router_b, exp_w1, exp_b1, exp_w2, exp_b2, final_g):
    assert x.shape == (BATCH, SEQ, D_MODEL) and ctx.shape == (BATCH, CTX_LEN, D_MODEL)
    cos_t, sin_t = _rope_tables()
    cv = jnp.concatenate([c, c_ctx[None], jnp.zeros((8 - BATCH - 1, D_MODEL), F32)], axis=0)
    mods = _ada(cv, w_ada, b_ada)
    xs = (ctx, x)
    ew1 = exp_w1.reshape(DEPTH * N_EXPERTS, D_MODEL, 2 * D_EXPERT)
    eb1 = exp_b1.reshape(DEPTH * N_EXPERTS, 1, 2 * D_EXPERT)
    ew2 = exp_w2.reshape(DEPTH * N_EXPERTS, D_EXPERT, D_MODEL)
    eb2 = exp_b2.reshape(DEPTH * N_EXPERTS, 1, D_MODEL)
    half_gate = jnp.where(jnp.arange(IN_WIDTH) >= OFF_GATE, 0.5, 1.0).astype(F32)
    w_in_b = (w_in * half_gate).astype(BF16)
    ssm_tables = _ssm_params(ssm_lam_re, ssm_lam_im, ssm_log_dt, ssm_b_re, ssm_b_im, ssm_c_re, ssm_c_im)
    merge_params = dict(
        g1=norm1_g[:, None], dsk=ssm_d[:, None], wval=(0.5 * ssm_w_val).astype(BF16),
        wgate=(0.5 * ssm_w_gate).astype(BF16), wup=attn_w_up.astype(BF16), pw=pool_w.astype(BF16),
        ps=pool_scale[:, None], wout=(0.5 * w_out).astype(BF16), g2=norm2_g[:, None],
        rw=_split_bf16(router_w), rb=router_b[:, None])
    proj = _inproj(xs, mods[0], norm1_g[:, None], w_in_b, 0, cos_t, sin_t)
    out = None
    for l in range(DEPTH):
        mod = mods[l]
        u, kv, q, pool_in = proj[:4]
        if len(proj) > 4:
            xs = (proj[4], proj[4])
        yf, yb = _ssm(u, *ssm_tables, l)
        o = _attn(attn_sink[l], q, kv)
        xa, f, top_e, top_p, tile_cnt = _merge(xs, mod, w_in=w_in_b, layer=l, yf=yf, yb=yb, u=u, o=o,
                                               pool_in=pool_in, **merge_params)
        proj = None
        for b in range(BATCH):
            counts = jnp.pad(tile_cnt[b, :, 0, :].sum(axis=0)[None], ((0, 0), (0, LANES - N_EXPERTS)))
            dest_t, block_e = _route(top_e.reshape(N_TOK, TOP_K), counts, b)
            rows = _dispatch(f.reshape(N_TOK, PACKED), dest_t, b)
            yrows = _experts(block_e[:N_BLOCKS, 0], rows, ew1, eb1, ew2, eb2, l)
            gathered = _combine_gather(yrows, dest_t.reshape(-1), TOP_K * G_TOK)
            if l < DEPTH - 1:
                proj = _inproj((xa, xa), mods[l + 1], norm1_g[:, None], w_in_b, l + 1, cos_t, sin_t,
                               (mod, gathered, top_p), b, proj)
            else:
                out = _final(xa, mod, gathered, top_p, final_g[None], b, out)
    return out
```

```python
import functools
import math

import jax
import jax.numpy as jnp
import numpy as np
from jax import lax
from jax.experimental import pallas as pl
from jax.experimental.pallas import tpu as pltpu
from jax.experimental.pallas import tpu_sc as plsc

F32 = jnp.float32
BF16 = jnp.bfloat16
HIGHEST = lax.Precision.HIGHEST

D_MODEL = 1024
BATCH = 2
SEQ = 8192
DEPTH = 2
GRID_W = 64
CTX_LEN = 256
NORM_EPS = 1e-6
SSM_WIDTH = 512
SSM_GROUP = 16
SSM_GROUPS = 32
SSM_STATE = 64
HEAD_DIM = 64
N_Q_HEADS = 8
N_KV_HEADS = 2
GQA_GROUP = 4
ATTN_WIDTH = 512
KV_WIDTH = 128
WINDOW = 128
ROPE_BASE = 10000.0
ATTN_SCALE = HEAD_DIM ** -0.5
POOL_WINDOWS = (2, 4, 8, 16)
POOL_WIDTH = 512
POOL_GROUP = 128
POOL_OUT_GROUP = 256
OFF_K = 512
OFF_V = 640
OFF_Q = 768
OFF_POOL = 1280
OFF_GATE = 1792
IN_WIDTH = OFF_GATE + 3 * D_MODEL
N_EXPERTS = 32
TOP_K = 4
D_EXPERT = 1024
SWIGLU_LIMIT = 7.0
SWIGLU_ALPHA = 1.702

S_TOT = CTX_LEN + SEQ
N_TOK = BATCH * S_TOT
G_TOK = S_TOT
TOK_TILE = 256
N_TILES = S_TOT // TOK_TILE
assert CTX_LEN == TOK_TILE
ATTN_BLOCK = 128
ATTN_STEP = 6
POOL_HALO = 8
ROUTE_TILE = 1056
ROW_BLOCK = 256
EXPERT_STEP = 4
N_ROWS = -(-(G_TOK * TOP_K + N_EXPERTS * (ROW_BLOCK - 1)) // (ROW_BLOCK * EXPERT_STEP)) * ROW_BLOCK * EXPERT_STEP
N_BLOCKS = N_ROWS // ROW_BLOCK
N_BLOCKS_PAD = -(-N_BLOCKS // 8) * 8
SC_CORES = 2
SC_WORKERS = SC_CORES * 16
PACKED = D_MODEL // 2
SC_SCATTER_ROWS = 88
SC_GATHER_ROWS = 88
assert G_TOK % (SC_WORKERS * SC_SCATTER_ROWS) == 0 and G_TOK % ROUTE_TILE == 0
VMEM_LIMIT = 56 * 1024 * 1024
LANES = 128
MASK_VALUE = -1e30


def _rms(x, g):
    return x * lax.rsqrt(jnp.mean(x * x, axis=-1, keepdims=True) + NORM_EPS) * g


def _sigmoid(x):
    return 0.5 * jnp.tanh(0.5 * x) + 0.5


def _pack_rows(x):
    half = x.shape[1] // 2
    lo = lax.bitcast_convert_type(x[:, :half].astype(BF16).astype(F32), jnp.uint32)
    hi = lax.bitcast_convert_type(x[:, half:].astype(BF16).astype(F32), jnp.uint32)
    return (lo >> 16) | hi


def _unpack_rows(p):
    lo = lax.bitcast_convert_type(p << 16, F32)
    hi = lax.bitcast_convert_type(p & jnp.uint32(0xFFFF0000), F32)
    return lo, hi


def _token_specs(xs, last=N_TILES - 1, im=lambda f: f):
    ctx_tiles = CTX_LEN // TOK_TILE
    off = 0 if xs[1].shape[1] == S_TOT else ctx_tiles
    return [pl.BlockSpec((1, TOK_TILE, D_MODEL), im(lambda b, i: (b, 0, 0))),
            pl.BlockSpec((1, TOK_TILE, D_MODEL), im(lambda b, i: (b, jnp.clip(i, ctx_tiles, last) - off, 0)))]


def _token_tile(i, xc_ref, xl_ref):
    return jnp.where(i == 0, xc_ref[0], xl_ref[0])


def _mod_row(i, b):
    return jnp.where(i == 0, 2, b)


def _ada_kernel(cv_ref, w_ref, b_ref, o_ref):
    cv = cv_ref[...]
    s = cv * jax.nn.sigmoid(cv)
    o_ref[0, 0] = jnp.dot(s, w_ref[0], preferred_element_type=F32, precision=HIGHEST) + b_ref[0, 0]


def _ada(cv, w_ada, b_ada):
    return pl.pallas_call(
        _ada_kernel,
        grid=(DEPTH, 6),
        in_specs=[
            pl.BlockSpec((8, D_MODEL), lambda l, j: (0, 0)),
            pl.BlockSpec((1, D_MODEL, D_MODEL), lambda l, j: (l, 0, j)),
            pl.BlockSpec((1, 1, 1, D_MODEL), lambda l, j: (l, j, 0, 0)),
        ],
        out_specs=pl.BlockSpec((1, 1, 8, D_MODEL), lambda l, j: (l, j, 0, 0)),
        out_shape=jax.ShapeDtypeStruct((DEPTH, 6, 8, D_MODEL), F32),
        name="ada",
    )(cv, w_ada, b_ada.reshape(DEPTH, 6, 1, D_MODEL))


def _routed_sum(g_ref, p_ref):
    p = p_ref[0]
    lo, hi = _unpack_rows(g_ref[0, 0])
    y_lo = p[:, 0:1] * lo
    y_hi = p[:, 0:1] * hi
    for k in range(1, TOP_K):
        lo, hi = _unpack_rows(g_ref[k, 0])
        y_lo = y_lo + p[:, k:k + 1] * lo
        y_hi = y_hi + p[:, k:k + 1] * hi
    return jnp.concatenate([y_lo, y_hi], axis=1)


def _rope(t, cos, sin):
    lane = lax.broadcasted_iota(jnp.int32, t.shape, 1)
    rot = jnp.where((lane % 32) < 16, -pltpu.roll(t, LANES - 16, 1), pltpu.roll(t, 16, 1))
    return t * cos + rot * sin


def _inproj_kernel(*refs, pending, batch):
    xc_ref, xl_ref, mod_ref, g_ref, w_ref, cos_ref, sin_ref = refs[:7]
    if pending:
        pmod_ref, rows_ref, probs_ref = refs[7:10]
        u_ref, kv_ref, q_ref, pool_ref, xo_ref = refs[-5:]
    else:
        u_ref, kv_ref, q_ref, pool_ref = refs[-4:]
    b, i = (pl.program_id(0), pl.program_id(1)) if batch is None else (batch, pl.program_id(0))
    row = _mod_row(i, b)
    shift = mod_ref[0, pl.ds(row, 1), :]
    scale = mod_ref[1, pl.ds(row, 1), :]
    x = _token_tile(i, xc_ref, xl_ref)
    if pending:
        x = x + pmod_ref[5, pl.ds(row, 1), :] * _routed_sum(rows_ref, probs_ref)
        xo_ref[0] = x
    h = _rms(x, g_ref[...]) * (1.0 + scale) + shift
    p = jnp.dot(h.astype(BF16), w_ref[0, :, :OFF_GATE], preferred_element_type=F32)
    cos = cos_ref[...]
    sin = sin_ref[...]
    u_ref[0] = p[:, :OFF_K]
    kv_ref[0, :, :KV_WIDTH] = _rope(p[:, OFF_K:OFF_V], cos, sin).astype(BF16)
    kv_ref[0, :, KV_WIDTH:] = p[:, OFF_V:OFF_Q].astype(BF16)
    for c in range(ATTN_WIDTH // LANES):
        qc = p[:, OFF_Q + c * LANES:OFF_Q + (c + 1) * LANES]
        q_ref[0, :, c * LANES:(c + 1) * LANES] = (_rope(qc, cos, sin) * ATTN_SCALE).astype(BF16)
    pool_ref[0] = p[:, OFF_POOL:OFF_GATE]


def _inproj(xs, mod, g1, w_in, layer, cos_t, sin_t, pending=None, batch=None, outs=None):
    im = (lambda f: f) if batch is None else (lambda f: lambda i: f(batch, i))
    tok = im(lambda b, i: (b, i, 0))
    in_specs = [
        *_token_specs(xs, im=im),
        pl.BlockSpec((6, 8, D_MODEL), im(lambda b, i: (0, 0, 0))),
        pl.BlockSpec((None, 1, D_MODEL), im(lambda b, i: (layer, 0, 0))),
        pl.BlockSpec((1, D_MODEL, IN_WIDTH), im(lambda b, i: (layer, 0, 0))),
        pl.BlockSpec((TOK_TILE, LANES), im(lambda b, i: (i, 0))),
        pl.BlockSpec((TOK_TILE, LANES), im(lambda b, i: (i, 0))),
    ]
    out_specs = [
        pl.BlockSpec((1, TOK_TILE, SSM_WIDTH), tok),
        pl.BlockSpec((1, TOK_TILE, 2 * KV_WIDTH), tok),
        pl.BlockSpec((1, TOK_TILE, ATTN_WIDTH), tok),
        pl.BlockSpec((1, TOK_TILE, POOL_WIDTH), tok),
    ]
    out_shape = [
        jax.ShapeDtypeStruct((BATCH, S_TOT, SSM_WIDTH), F32),
        jax.ShapeDtypeStruct((BATCH, S_TOT, 2 * KV_WIDTH), BF16),
        jax.ShapeDtypeStruct((BATCH, S_TOT, ATTN_WIDTH), BF16),
        jax.ShapeDtypeStruct((BATCH, S_TOT, POOL_WIDTH), F32),
    ]
    args = [*xs, mod, g1, w_in, cos_t, sin_t]
    if pending is not None:
        pmod, gathered, probs = pending
        in_specs += [
            pl.BlockSpec((6, 8, D_MODEL), im(lambda b, i: (0, 0, 0))),
            pl.BlockSpec((TOP_K, 1, TOK_TILE, PACKED), im(lambda b, i: (0, 0, i, 0))),
            pl.BlockSpec((1, TOK_TILE, TOP_K), tok),
        ]
        out_specs.append(pl.BlockSpec((1, TOK_TILE, D_MODEL), tok))
        out_shape.append(jax.ShapeDtypeStruct((BATCH, S_TOT, D_MODEL), F32))
        args += [pmod, gathered.reshape(TOP_K, 1, S_TOT, PACKED), probs]
    aliases = {}
    if outs is not None:
        aliases = {len(args) + k: k for k in range(len(outs))}
        in_specs += [pl.BlockSpec(memory_space=pl.ANY)] * len(outs)
        args += list(outs)
    return pl.pallas_call(
        functools.partial(_inproj_kernel, pending=pending is not None, batch=batch),
        grid=(BATCH, N_TILES) if batch is None else (N_TILES,),
        in_specs=in_specs,
        out_specs=out_specs,
        out_shape=out_shape,
        input_output_aliases=aliases,
        compiler_params=pltpu.CompilerParams(
            dimension_semantics=("parallel",) * (2 if batch is None else 1), vmem_limit_bytes=VMEM_LIMIT),
        name="inproj",
    )(*args)


SSM_HALF_IN = SSM_WIDTH // 2
SSM_HALF_ST = SSM_GROUPS // 2 * SSM_STATE
N_STATE_ROWS = 8
N_LANE_TILES = SSM_HALF_ST // LANES
SCAN_STRIDE = TOK_TILE + 8


def _ssm_kernel(uf_ref, ub_ref, wbr_ref, wcr_ref, dmask_ref, rmask_ref, ar_ref, ai_ref, yf_ref, yb_ref,
                buf_f, buf_b, hf_ref, hb_ref, wb_ref, wc_ref):
    i = pl.program_id(0)
    T = TOK_TILE

    @pl.when(i == 0)
    def _():
        hf_ref[...] = jnp.zeros_like(hf_ref)
        hb_ref[...] = jnp.zeros_like(hb_ref)
        def widen(rows, width):
            while rows.shape[1] < min(width, LANES):
                rows = jnp.concatenate([rows, rows], axis=1)
            return jnp.concatenate([rows] * (width // rows.shape[1]), axis=1)

        for d in range(2):
            for reim in range(2):
                for half in range(2):
                    wb_ref[d, reim, half] = (widen(wbr_ref[d, reim, half], SSM_HALF_ST)
                                             * dmask_ref[...]).astype(BF16)
                    wc_ref[d, reim, half] = (widen(wcr_ref[d, reim, half], SSM_HALF_IN)
                                             * rmask_ref[...]).astype(BF16)

    for d, (u_ref, buf) in enumerate(((uf_ref, buf_f), (ub_ref, buf_b))):
        for b in range(BATCH):
            for half in range(2):
                ub16 = u_ref[b, :, half * SSM_HALF_IN:(half + 1) * SSM_HALF_IN].astype(BF16)
                for reim in range(2):
                    k = reim * 4 + half * 2 + b
                    bu = jnp.dot(ub16, wb_ref[d, reim, half], preferred_element_type=F32)
                    for l in range(N_LANE_TILES):
                        buf[l, k * SCAN_STRIDE:k * SCAN_STRIDE + T, :] = bu[:, l * LANES:(l + 1) * LANES]

    def step(t, carry):
        hf, hb = carry
        tb = T - 1 - t
        nf, nb = [], []
        for l in range(N_LANE_TILES):
            sl = pl.ds(l * LANES, LANES)
            rows_f = pl.ds(t, N_STATE_ROWS, stride=SCAN_STRIDE)
            h = ar_ref[0, :, sl] * hf[l] + ai_ref[0, :, sl] * pltpu.roll(hf[l], 4, 0) + buf_f[l, rows_f, :]
            buf_f[l, rows_f, :] = h
            nf.append(h)
            rows_b = pl.ds(tb, N_STATE_ROWS, stride=SCAN_STRIDE)
            h = ar_ref[1, :, sl] * hb[l] + ai_ref[1, :, sl] * pltpu.roll(hb[l], 4, 0) + buf_b[l, rows_b, :]
            buf_b[l, rows_b, :] = h
            nb.append(h)
        return tuple(nf), tuple(nb)

    hf0 = tuple(hf_ref[:, l * LANES:(l + 1) * LANES] for l in range(N_LANE_TILES))
    hb0 = tuple(hb_ref[:, l * LANES:(l + 1) * LANES] for l in range(N_LANE_TILES))
    hf1, hb1 = lax.fori_loop(0, T, step, (hf0, hb0), unroll=True)
    for l in range(N_LANE_TILES):
        hf_ref[:, l * LANES:(l + 1) * LANES] = hf1[l]
        hb_ref[:, l * LANES:(l + 1) * LANES] = hb1[l]

    for d, (buf, y_ref) in enumerate(((buf_f, yf_ref), (buf_b, yb_ref))):
        for b in range(BATCH):
            for half in range(2):
                kre = half * 2 + b
                kim = 4 + half * 2 + b
                state = lambda k: jnp.concatenate(
                    [buf[l, k * SCAN_STRIDE:k * SCAN_STRIDE + T, :] for l in range(N_LANE_TILES)],
                    axis=1).astype(BF16)
                y = jnp.dot(state(kre), wc_ref[d, 0, half], preferred_element_type=F32)
                y = y + jnp.dot(state(kim), wc_ref[d, 1, half], preferred_element_type=F32)
                y_ref[b, :, half * SSM_HALF_IN:(half + 1) * SSM_HALF_IN] = y


def _ssm(u, wb, wc, ar, ai, layer):
    fwd = lambda i: (0, i, 0)
    bwd = lambda i: (0, jnp.where(i == 0, 0, N_TILES - i), 0)
    drive_mask = np.equal.outer(np.arange(SSM_HALF_IN) // SSM_GROUP, np.arange(SSM_HALF_ST) // SSM_STATE)
    masks = jnp.asarray(drive_mask, F32), jnp.asarray(drive_mask.T, F32)
    return pl.pallas_call(
        _ssm_kernel,
        grid=(N_TILES,),
        in_specs=[
            pl.BlockSpec((BATCH, TOK_TILE, SSM_WIDTH), fwd),
            pl.BlockSpec((BATCH, TOK_TILE, SSM_WIDTH), bwd),
            pl.BlockSpec((None, 2, 2, 2, SSM_HALF_IN, SSM_STATE), lambda i: (layer, 0, 0, 0, 0, 0)),
            pl.BlockSpec((None, 2, 2, 2, SSM_HALF_ST, SSM_GROUP), lambda i: (layer, 0, 0, 0, 0, 0)),
            pl.BlockSpec((SSM_HALF_IN, SSM_HALF_ST), lambda i: (0, 0)),
            pl.BlockSpec((SSM_HALF_ST, SSM_HALF_IN), lambda i: (0, 0)),
            pl.BlockSpec((None, 2, N_STATE_ROWS, SSM_HALF_ST), lambda i: (layer, 0, 0, 0)),
            pl.BlockSpec((None, 2, N_STATE_ROWS, SSM_HALF_ST), lambda i: (layer, 0, 0, 0)),
        ],
        out_specs=[
            pl.BlockSpec((BATCH, TOK_TILE, SSM_WIDTH), fwd),
            pl.BlockSpec((BATCH, TOK_TILE, SSM_WIDTH), bwd),
        ],
        out_shape=[jax.ShapeDtypeStruct((BATCH, S_TOT, SSM_WIDTH), F32)] * 2,
        scratch_shapes=[
            pltpu.VMEM((N_LANE_TILES, N_STATE_ROWS * SCAN_STRIDE, LANES), F32),
            pltpu.VMEM((N_LANE_TILES, N_STATE_ROWS * SCAN_STRIDE, LANES), F32),
            pltpu.VMEM((N_STATE_ROWS, SSM_HALF_ST), F32),
            pltpu.VMEM((N_STATE_ROWS, SSM_HALF_ST), F32),
            pltpu.VMEM((2, 2, 2, SSM_HALF_IN, SSM_HALF_ST), BF16),
            pltpu.VMEM((2, 2, 2, SSM_HALF_ST, SSM_HALF_IN), BF16),
        ],
        compiler_params=pltpu.CompilerParams(
            dimension_semantics=("arbitrary",), vmem_limit_bytes=VMEM_LIMIT),
        name="ssm",
    )(u, u, wb, wc, *masks, ar, ai)


def _ssm_params(lam_re, lam_im, log_dt, b_re, b_im, c_re, c_im):
    lr = lam_re.astype(F32)
    li = lam_im.astype(F32)
    dt = jnp.exp(log_dt.astype(F32))[..., None]
    mag = jnp.exp(lr * dt)
    ang = li * dt
    a_re = mag * jnp.cos(ang)
    a_im = mag * jnp.sin(ang)
    den = lr * lr + li * li
    k_re = ((a_re - 1.0) * lr + a_im * li) / den
    k_im = (a_im * lr - (a_re - 1.0) * li) / den
    bb_re = k_re[..., None] * b_re - k_im[..., None] * b_im
    bb_im = k_re[..., None] * b_im + k_im[..., None] * b_re
    lead = lam_re.shape[:2]

    def block_rows(parts, rows_per_half):
        rows = jnp.stack([jnp.swapaxes(p, -1, -2) for p in parts], axis=2)
        return rows.reshape(*lead, 2, 2, rows_per_half, rows.shape[-1])

    wb = block_rows((bb_re, bb_im), SSM_HALF_IN)
    wc = block_rows((c_re.astype(F32), -c_im.astype(F32)), SSM_HALF_ST)

    def rows(a, sign_re):
        halves = a.reshape(*lead, 1, 2, 1, SSM_HALF_ST)
        signs = jnp.asarray([sign_re, 1.0], F32).reshape(2, 1, 1, 1)
        return jnp.broadcast_to(halves * signs, (*lead, 2, 2, BATCH, SSM_HALF_ST)).reshape(
            *lead, N_STATE_ROWS, SSM_HALF_ST)

    return wb, wc, rows(a_re, 1.0), rows(a_im, -1.0)


def _attn_block(n, q, kv_band, kv_ctx, sink_ref, band_bias):
    kv_all = jnp.concatenate([kv_band, kv_ctx], axis=0)
    nb = 3 * ATTN_BLOCK
    last = SEQ // ATTN_BLOCK - 1
    krow = lax.broadcasted_iota(jnp.int32, (nb, 1), 0)
    lo_r = jnp.where(n < 0, nb, jnp.where(n == 0, ATTN_BLOCK, 0))
    hi_r = jnp.where(n == last, 2 * ATTN_BLOCK, nb)
    bias = band_bias + jnp.where((krow >= lo_r) & (krow < hi_r), 0.0, MASK_VALUE)
    bias = jnp.concatenate([bias] * GQA_GROUP, axis=1)
    nt = (((1,), (1,)), ((), ()))
    outs = []
    for hk in range(N_KV_HEADS):
        k = kv_all[:, hk * HEAD_DIM:(hk + 1) * HEAD_DIM]
        v_t = kv_all[:, KV_WIDTH + hk * HEAD_DIM:KV_WIDTH + (hk + 1) * HEAD_DIM].astype(F32).T.astype(BF16)
        heads = range(hk * GQA_GROUP, (hk + 1) * GQA_GROUP)
        qs = jnp.concatenate([q[:, hq * HEAD_DIM:(hq + 1) * HEAD_DIM] for hq in heads], axis=0)
        sink = jnp.concatenate([jnp.full((1, ATTN_BLOCK), sink_ref[hq], F32) for hq in heads], axis=1)
        s_t = lax.dot_general(k, qs, nt, preferred_element_type=F32)
        s_w = s_t[:nb] + bias
        s_c = s_t[nb:]
        m = jnp.maximum(jnp.maximum(jnp.max(s_w, axis=0, keepdims=True),
                                    jnp.max(s_c, axis=0, keepdims=True)), sink)
        e_w = jnp.exp(s_w - m)
        e_c = jnp.exp(s_c - m)
        den = jnp.sum(e_w, axis=0, keepdims=True) + jnp.sum(e_c, axis=0, keepdims=True) + jnp.exp(sink - m)
        e = jnp.concatenate([e_w, e_c], axis=0).astype(BF16)
        o_t = jnp.dot(v_t, e, preferred_element_type=F32) / den
        outs.extend(o_t[:, g * ATTN_BLOCK:(g + 1) * ATTN_BLOCK].T for g in range(GQA_GROUP))
    return jnp.concatenate(outs, axis=1).astype(BF16)


def _attn_kernel(sink_ref, band_ref, q_ref, kc_ref, *rest):
    kv_refs, o_ref = rest[:-1], rest[-1]
    j0 = pl.program_id(1) * ATTN_STEP
    for a in range(ATTN_STEP):
        n = j0 + a - CTX_LEN // ATTN_BLOCK
        band = jnp.concatenate([kv_refs[a + d][0] for d in range(3)], axis=0)
        rows = slice(a * ATTN_BLOCK, (a + 1) * ATTN_BLOCK)
        o_ref[0, rows, :] = _attn_block(n, q_ref[0, rows, :], band, kc_ref[0], sink_ref, band_ref[...])


def _attn(sink, q, kv):
    nq = S_TOT // ATTN_BLOCK
    first = CTX_LEN // ATTN_BLOCK
    r = jnp.arange(ATTN_BLOCK)[None, :]
    c = jnp.arange(3 * ATTN_BLOCK)[:, None] - ATTN_BLOCK
    band_bias = jnp.where(jnp.abs(c - r) <= WINDOW, 0.0, MASK_VALUE).astype(F32)

    def kv_spec(d):
        return pl.BlockSpec((1, ATTN_BLOCK, 2 * KV_WIDTH),
                            lambda b, j: (b, jnp.clip(j * ATTN_STEP + d - 1, first, nq - 1), 0))

    return pl.pallas_call(
        _attn_kernel,
        grid=(BATCH, nq // ATTN_STEP),
        in_specs=[
            pl.BlockSpec(memory_space=pltpu.SMEM),
            pl.BlockSpec((3 * ATTN_BLOCK, ATTN_BLOCK), lambda b, j: (0, 0)),
            pl.BlockSpec((1, ATTN_STEP * ATTN_BLOCK, ATTN_WIDTH), lambda b, j: (b, j, 0)),
            pl.BlockSpec((1, CTX_LEN, 2 * KV_WIDTH), lambda b, j: (b, 0, 0)),
            *[kv_spec(d) for d in range(ATTN_STEP + 2)],
        ],
        out_specs=pl.BlockSpec((1, ATTN_STEP * ATTN_BLOCK, ATTN_WIDTH), lambda b, j: (b, j, 0)),
        out_shape=jax.ShapeDtypeStruct((BATCH, S_TOT, ATTN_WIDTH), BF16),
        compiler_params=pltpu.CompilerParams(dimension_semantics=("parallel", "parallel")),
        name="attn",
    )(sink, band_bias, q, kv, *[kv] * (ATTN_STEP + 2))


def _gelu_tanh(x):
    return 0.5 * x * (1.0 + jnp.tanh(math.sqrt(2.0 / math.pi) * (x + 0.044715 * (x * x * x))))


def _pool_means(ext, i):
    T = TOK_TILE
    n_ext = T + 2 * POOL_HALO
    seq_len = jnp.where(i == 0, CTX_LEN, SEQ)
    start = jnp.where(i == 0, 0, (i - 1) * T)
    r = start - POOL_HALO + lax.broadcasted_iota(jnp.int32, (n_ext, 1), 0)
    ext = jnp.where((r >= 0) & (r < seq_len), ext, 0.0)
    tl = start + lax.broadcasted_iota(jnp.int32, (T, 1), 0)
    back = lambda a, s: pltpu.roll(a, s, 0)
    fwd = lambda a, s: pltpu.roll(a, n_ext - s, 0)
    diffs = []
    for gi, w in enumerate(POOL_WINDOWS):
        e = ext[:, gi * POOL_GROUP:(gi + 1) * POOL_GROUP]
        lo = w // 2
        hi = w - 1 - lo
        s = e + back(e, 1)
        span = 2
        while span < w:
            s = s + back(s, span)
            span *= 2
        if w > 2:
            s = fwd(s, hi)
        cnt = (jnp.clip(tl + hi + 1, 0, seq_len) - jnp.clip(tl - lo, 0, seq_len)).astype(F32)
        mean = s[POOL_HALO:POOL_HALO + T] / cnt
        diffs.append((mean - e[POOL_HALO:POOL_HALO + T]).astype(BF16))
    return diffs


def _merge_kernel(xc_ref, xl_ref, mod_ref, g1_ref, w_ref, yf_ref, yb_ref, u_ref, dsk_ref, wval_ref, wgate_ref,
                  o_ref, wup_ref, pc_ref, pp_ref, pn_ref, pw_ref, ps_ref, wout_ref, g2_ref, rw_ref, rb_ref,
                  xo_ref, f_ref, te_ref, tp_ref, cnt_ref, xprev_ref):
    b = pl.program_id(0)
    s = pl.program_id(1)

    @pl.when(s == 0)
    def _():
        xprev_ref[...] = jnp.zeros_like(xprev_ref)

    rowp = _mod_row(jnp.maximum(s - 1, 0), b)
    f = _rms(xprev_ref[...], g2_ref[...]) * (1.0 + mod_ref[4, pl.ds(rowp, 1), :]) + mod_ref[3, pl.ds(rowp, 1), :]
    f_ref[0] = _pack_rows(f)
    f_hi = f.astype(BF16)
    f_lo = (f - f_hi.astype(F32)).astype(BF16)
    r_hi = jnp.dot(f_hi, rw_ref[...], preferred_element_type=F32)
    r_lo = jnp.dot(f_lo, rw_ref[...], preferred_element_type=F32)
    logits = r_hi[:, :N_EXPERTS] + r_hi[:, N_EXPERTS:] + r_lo[:, :N_EXPERTS] + rb_ref[...]
    lane = lax.broadcasted_iota(jnp.int32, logits.shape, 1)
    vals, idxs = [], []
    for _ in range(TOP_K):
        mx = jnp.max(logits, axis=-1, keepdims=True)
        ix = jnp.min(jnp.where(logits == mx, lane, N_EXPERTS), axis=-1, keepdims=True)
        vals.append(mx)
        idxs.append(ix)
        logits = jnp.where(lane == ix, -jnp.inf, logits)
    ex = [jnp.exp(v - vals[0]) for v in vals]
    tot = ex[0] + ex[1] + ex[2] + ex[3]
    te_ref[0] = jnp.concatenate(idxs, axis=1)
    tp_ref[0] = jnp.concatenate(ex, axis=1) / tot
    chosen = sum((lane == ix).astype(F32) for ix in idxs)
    cnt_ref[0, 0] = jnp.sum(chosen, axis=0, keepdims=True)

    i = jnp.minimum(s, N_TILES - 1)
    row = _mod_row(i, b)
    shift, scale, gate = (mod_ref[j, pl.ds(row, 1), :] for j in range(3))
    ext = jnp.concatenate([pp_ref[0], pc_ref[0], pn_ref[0]], axis=0)
    diffs = _pool_means(ext, i)
    x = _token_tile(i, xc_ref, xl_ref)
    h = (_rms(x, g1_ref[...]) * (1.0 + scale) + shift).astype(BF16)
    t = jnp.tanh(jnp.dot(h, w_ref[0, :, OFF_GATE:], preferred_element_type=F32))

    z = yf_ref[0] + yb_ref[0] + dsk_ref[...] * u_ref[0]
    z = _gelu_tanh(z).astype(BF16)
    val = jnp.dot(z, wval_ref[...], preferred_element_type=F32)
    y_ssm = val * jnp.tanh(jnp.dot(z, wgate_ref[...], preferred_element_type=F32)) + val
    y_attn = jnp.dot(o_ref[0], wup_ref[...], preferred_element_type=F32)
    y_pool = jnp.concatenate(
        [jnp.dot(diffs[gi], pw_ref[gi], preferred_element_type=F32) for gi in range(len(POOL_WINDOWS))],
        axis=1) * ps_ref[...]

    m = (t[:, :D_MODEL] * y_ssm + t[:, D_MODEL:2 * D_MODEL] * y_attn + t[:, 2 * D_MODEL:] * y_pool
         + (y_ssm + y_attn + y_pool))
    xo = x + gate * jnp.dot(m.astype(BF16), wout_ref[...], preferred_element_type=F32)
    xo_ref[0] = xo
    xprev_ref[...] = xo


def _merge(xs, mod, g1, w_in, layer, yf, yb, u, dsk, wval, wgate, o, wup, pool_in, pw, ps, wout, g2, rw, rb):
    last = N_TILES - 1
    tok = lambda b, s: (b, jnp.minimum(s, last), 0)
    prev = lambda b, s: (b, jnp.maximum(s - 1, 0), 0)
    const3 = lambda b, s: (0, 0, 0)
    per_layer3 = lambda b, s: (layer, 0, 0)
    per_tile = TOK_TILE // POOL_HALO
    n_halo = S_TOT // POOL_HALO
    return pl.pallas_call(
        _merge_kernel,
        grid=(BATCH, N_TILES + 1),
        in_specs=[
            *_token_specs(xs, last),
            pl.BlockSpec((6, 8, D_MODEL), const3),
            pl.BlockSpec((None, 1, D_MODEL), per_layer3),
            pl.BlockSpec((1, D_MODEL, IN_WIDTH), per_layer3),
            pl.BlockSpec((1, TOK_TILE, SSM_WIDTH), tok),
            pl.BlockSpec((1, TOK_TILE, SSM_WIDTH), tok),
            pl.BlockSpec((1, TOK_TILE, SSM_WIDTH), tok),
            pl.BlockSpec((None, 1, SSM_WIDTH), per_layer3),
            pl.BlockSpec((None, SSM_WIDTH, D_MODEL), per_layer3),
            pl.BlockSpec((None, SSM_WIDTH, D_MODEL), per_layer3),
            pl.BlockSpec((1, TOK_TILE, ATTN_WIDTH), tok),
            pl.BlockSpec((None, ATTN_WIDTH, D_MODEL), per_layer3),
            pl.BlockSpec((1, TOK_TILE, POOL_WIDTH), tok),
            pl.BlockSpec((1, POOL_HALO, POOL_WIDTH),
                         lambda b, s: (b, jnp.maximum(jnp.minimum(s, last) * per_tile - 1, 0), 0)),
            pl.BlockSpec((1, POOL_HALO, POOL_WIDTH),
                         lambda b, s: (b, jnp.minimum((jnp.minimum(s, last) + 1) * per_tile, n_halo - 1), 0)),
            pl.BlockSpec((None, len(POOL_WINDOWS), POOL_GROUP, POOL_OUT_GROUP), lambda b, s: (layer, 0, 0, 0)),
            pl.BlockSpec((None, 1, D_MODEL), per_layer3),
            pl.BlockSpec((None, D_MODEL, D_MODEL), per_layer3),
            pl.BlockSpec((None, 1, D_MODEL), per_layer3),
            pl.BlockSpec((None, D_MODEL, 2 * N_EXPERTS), per_layer3),
            pl.BlockSpec((None, 1, N_EXPERTS), per_layer3),
        ],
        out_specs=[
            pl.BlockSpec((1, TOK_TILE, D_MODEL), tok),
            pl.BlockSpec((1, TOK_TILE, PACKED), prev),
            pl.BlockSpec((1, TOK_TILE, TOP_K), prev),
            pl.BlockSpec((1, TOK_TILE, TOP_K), prev),
            pl.BlockSpec((1, 1, 1, N_EXPERTS), lambda b, s: (b, jnp.maximum(s - 1, 0), 0, 0)),
        ],
        out_shape=[
            jax.ShapeDtypeStruct((BATCH, S_TOT, D_MODEL), F32),
            jax.ShapeDtypeStruct((BATCH, S_TOT, PACKED), jnp.uint32),
            jax.ShapeDtypeStruct((BATCH, S_TOT, TOP_K), jnp.int32),
            jax.ShapeDtypeStruct((BATCH, S_TOT, TOP_K), F32),
            jax.ShapeDtypeStruct((BATCH, N_TILES, 1, N_EXPERTS), F32),
        ],
        scratch_shapes=[pltpu.VMEM((TOK_TILE, D_MODEL), F32)],
        compiler_params=pltpu.CompilerParams(
            dimension_semantics=("parallel", "arbitrary"), vmem_limit_bytes=VMEM_LIMIT),
        name="merge",
    )(*xs, mod, g1, w_in, yf, yb, u, dsk, wval, wgate, o, wup, pool_in, pool_in, pool_in, pw, ps, wout, g2, rw, rb)


def _route_kernel(te_ref, below_ref, cnt_ref, dest_ref, be_ref, carry_ref):
    i = pl.program_id(0)
    te = te_ref[...]
    lane = lax.broadcasted_iota(jnp.int32, (ROUTE_TILE, LANES), 1)
    ohs = [(te[:, k:k + 1] == lane).astype(F32) for k in range(TOP_K)]
    oh_all = ohs[0] + ohs[1] + ohs[2] + ohs[3]

    @pl.when(i == 0)
    def _():
        cnt = jnp.broadcast_to(cnt_ref[...], (8, LANES))
        padded = jnp.floor((cnt + (ROW_BLOCK - 1)) / ROW_BLOCK) * ROW_BLOCK
        l8 = lax.broadcasted_iota(jnp.int32, (8, LANES), 1)
        end = padded
        s = 1
        while s < N_EXPERTS:
            end = end + jnp.where(l8 >= s, pltpu.roll(end, s, 1), 0.0)
            s *= 2
        carry_ref[...] = (end - padded)[0:1]
        blk_row = (lax.broadcasted_iota(jnp.int32, (N_BLOCKS_PAD, LANES), 0) * ROW_BLOCK).astype(F32)
        lb = lax.broadcasted_iota(jnp.int32, (N_BLOCKS_PAD, LANES), 1)
        hit = jnp.where((lb < N_EXPERTS) & (jnp.broadcast_to(end[0:1], blk_row.shape) <= blk_row), 1.0, 0.0)
        be = jnp.sum(hit, axis=-1, keepdims=True)
        be_ref[...] = jnp.broadcast_to(be, (N_BLOCKS_PAD, LANES)).astype(jnp.int32)

    rank = carry_ref[...] + jnp.dot(below_ref[...], oh_all.astype(BF16), preferred_element_type=F32)
    dests = [jnp.sum(ohs[k] * rank, axis=-1, keepdims=True) for k in range(TOP_K)]
    dest_ref[...] = jnp.concatenate(dests, axis=1).astype(jnp.int32)
    carry_ref[...] += jnp.sum(oh_all, axis=0, keepdims=True)


def _route(top_e, counts, group):
    n_steps = G_TOK // ROUTE_TILE
    return pl.pallas_call(
        _route_kernel,
        grid=(n_steps,),
        in_specs=[pl.BlockSpec((ROUTE_TILE, TOP_K), lambda i: (group * n_steps + i, 0)),
                  pl.BlockSpec((ROUTE_TILE, ROUTE_TILE), lambda i: (0, 0)),
                  pl.BlockSpec((1, LANES), lambda i: (0, 0))],
        out_specs=[
            pl.BlockSpec((ROUTE_TILE, TOP_K), lambda i: (i, 0)),
            pl.BlockSpec((N_BLOCKS_PAD, LANES), lambda i: (0, 0)),
        ],
        out_shape=[
            jax.ShapeDtypeStruct((G_TOK, TOP_K), jnp.int32),
            jax.ShapeDtypeStruct((N_BLOCKS_PAD, LANES), jnp.int32),
        ],
        scratch_shapes=[pltpu.VMEM((1, LANES), F32)],
        compiler_params=pltpu.CompilerParams(dimension_semantics=("arbitrary",)),
        name="route",
    )(top_e, jnp.asarray(np.tril(np.ones((ROUTE_TILE, ROUTE_TILE), np.float32), -1), BF16), counts)


def _sc_mesh():
    return plsc.VectorSubcoreMesh(core_axis_name="c", subcore_axis_name="s")


def _sc_worker():
    return lax.axis_index("s") * SC_CORES + lax.axis_index("c")


def _dispatch(f, dest_t, group):
    per_w = G_TOK // SC_WORKERS
    n_chunks = per_w // SC_SCATTER_ROWS

    @functools.partial(
        pl.kernel,
        out_type=jax.ShapeDtypeStruct((N_ROWS, PACKED), jnp.uint32),
        mesh=_sc_mesh(),
        scratch_types=[
            pltpu.VMEM((SC_SCATTER_ROWS,), jnp.int32),
            pltpu.VMEM((SC_SCATTER_ROWS, PACKED), jnp.uint32),
            pltpu.SemaphoreType.DMA,
        ],
        name="dispatch",
    )
    def k(f_hbm, i_hbm, o_hbm, idx_v, rows_v, sem):
        base = _sc_worker() * per_w

        @pl.loop(0, n_chunks)
        def _(j):
            t0 = pl.multiple_of(base + j * SC_SCATTER_ROWS, 8)
            pltpu.sync_copy(f_hbm.at[pl.ds(group * G_TOK + t0, SC_SCATTER_ROWS)], rows_v)
            for kk in range(TOP_K):
                pltpu.sync_copy(i_hbm.at[pl.ds(kk * G_TOK + t0, SC_SCATTER_ROWS)], idx_v)
                pltpu.async_copy(rows_v, o_hbm.at[idx_v], sem).wait()

    return k(f, dest_t.reshape(TOP_K * G_TOK))


def _combine_gather(yb, idx):
    n = idx.shape[0]
    assert n % (SC_WORKERS * SC_GATHER_ROWS) == 0
    per_w = n // SC_WORKERS
    n_chunks = per_w // SC_GATHER_ROWS

    @functools.partial(
        pl.kernel,
        out_type=jax.ShapeDtypeStruct((n, PACKED), jnp.uint32),
        mesh=_sc_mesh(),
        scratch_types=[
            pltpu.VMEM((SC_GATHER_ROWS,), jnp.int32),
            pltpu.VMEM((SC_GATHER_ROWS, PACKED), jnp.uint32),
            pltpu.SemaphoreType.DMA,
        ],
        name="combine_gather",
    )
    def k(y_hbm, i_hbm, o_hbm, idx_v, rows_v, sem):
        base = _sc_worker() * per_w

        @pl.loop(0, n_chunks)
        def _(j):
            r0 = pl.multiple_of(base + j * SC_GATHER_ROWS, 8)
            pltpu.sync_copy(i_hbm.at[pl.ds(r0, SC_GATHER_ROWS)], idx_v)
            pltpu.async_copy(y_hbm.at[idx_v], rows_v, sem).wait()
            pltpu.sync_copy(rows_v, o_hbm.at[pl.ds(r0, SC_GATHER_ROWS)])

    return k(yb, idx)


def _weight_copies(w1_hbm, w2_hbm, w1f, w2f, sem, idx):
    return (pltpu.make_async_copy(w1_hbm.at[idx], w1f, sem.at[0]),
            pltpu.make_async_copy(w2_hbm.at[idx], w2f, sem.at[1]))


def _expert_kernel(be_ref, x_ref, w1_hbm, b1_ref, w2_hbm, b2_ref, y_ref, w1f, w2f, w1b, w2b, sem, *, layer):
    copies = functools.partial(_weight_copies, w1_hbm, w2_hbm, w1f, w2f, sem)
    for a in range(EXPERT_STEP):
        j = pl.program_id(0) * EXPERT_STEP + a
        e = be_ref[j]
        prev = be_ref[jnp.maximum(j - 1, 0)]
        live = e < N_EXPERTS
        first = live & ((j == 0) | (e != prev))
        rows = slice(a * ROW_BLOCK, (a + 1) * ROW_BLOCK)

        if a == 0:
            @pl.when(first & (j == 0))
            def _():
                for c in copies(layer * N_EXPERTS + e):
                    c.start()

        @pl.when(first)
        def _():
            for c in copies(layer * N_EXPERTS + e):
                c.wait()
            w1b[...] = w1f[...].astype(BF16)
            w2b[...] = w2f[...].astype(BF16)
            nxt = lax.while_loop(
                lambda k: (k < N_BLOCKS) & (be_ref[jnp.minimum(k, N_BLOCKS - 1)] == e), lambda k: k + 1, j + 1)
            e_next = be_ref[jnp.minimum(nxt, N_BLOCKS - 1)]

            @pl.when((nxt < N_BLOCKS) & (e_next < N_EXPERTS))
            def _():
                for c in copies(layer * N_EXPERTS + e_next):
                    c.start()

        @pl.when(live)
        def _():
            idx = layer * N_EXPERTS + e
            x = jnp.concatenate(_unpack_rows(x_ref[rows, :]), axis=1).astype(BF16)
            gu = jnp.dot(x, w1b[...], preferred_element_type=F32) + b1_ref[idx]
            gate = jnp.minimum(gu[:, :D_EXPERT], SWIGLU_LIMIT)
            lin = jnp.clip(gu[:, D_EXPERT:], -SWIGLU_LIMIT, SWIGLU_LIMIT)
            act = gate * _sigmoid(SWIGLU_ALPHA * gate) * (lin + 1.0)
            y_ref[rows, :] = _pack_rows(
                jnp.dot(act.astype(BF16), w2b[...], preferred_element_type=F32) + b2_ref[idx])


def _experts(block_e, rows, w1, b1, w2, b2, layer):
    step_rows = EXPERT_STEP * ROW_BLOCK
    return pl.pallas_call(
        functools.partial(_expert_kernel, layer=layer),
        grid_spec=pltpu.PrefetchScalarGridSpec(
            num_scalar_prefetch=1,
            grid=(N_BLOCKS // EXPERT_STEP,),
            in_specs=[
                pl.BlockSpec((step_rows, PACKED), lambda j, be: (j, 0)),
                pl.BlockSpec(memory_space=pl.ANY),
                pl.BlockSpec((DEPTH * N_EXPERTS, 1, 2 * D_EXPERT), lambda j, be: (0, 0, 0)),
                pl.BlockSpec(memory_space=pl.ANY),
                pl.BlockSpec((DEPTH * N_EXPERTS, 1, D_MODEL), lambda j, be: (0, 0, 0)),
            ],
            out_specs=pl.BlockSpec((step_rows, PACKED), lambda j, be: (j, 0)),
            scratch_shapes=[
                pltpu.VMEM((D_MODEL, 2 * D_EXPERT), F32),
                pltpu.VMEM((D_EXPERT, D_MODEL), F32),
                pltpu.VMEM((D_MODEL, 2 * D_EXPERT), BF16),
                pltpu.VMEM((D_EXPERT, D_MODEL), BF16),
                pltpu.SemaphoreType.DMA((2,)),
            ],
        ),
        out_shape=jax.ShapeDtypeStruct((N_ROWS, PACKED), jnp.uint32),
        compiler_params=pltpu.CompilerParams(
            dimension_semantics=("arbitrary",), vmem_limit_bytes=VMEM_LIMIT),
        name="experts",
    )(block_e, rows, w1, b1, w2, b2)


def _final_kernel(x_ref, mod_ref, g_ref, p_ref, fg_ref, *rest, batch):
    o_ref = rest[-1]
    gate = mod_ref[5, pl.ds(batch, 1), :]
    o_ref[0] = _rms(x_ref[0] + gate * _routed_sum(g_ref, p_ref), fg_ref[...])


def _final(xa, mod, gathered, probs, final_g, batch, out=None):
    ctx_tiles = CTX_LEN // TOK_TILE
    tok = lambda i: (batch, i + ctx_tiles, 0)
    in_specs = [
        pl.BlockSpec((1, TOK_TILE, D_MODEL), tok),
        pl.BlockSpec((6, 8, D_MODEL), lambda i: (0, 0, 0)),
        pl.BlockSpec((TOP_K, 1, TOK_TILE, PACKED), lambda i: (0, 0, i + ctx_tiles, 0)),
        pl.BlockSpec((1, TOK_TILE, TOP_K), tok),
        pl.BlockSpec((1, D_MODEL), lambda i: (0, 0)),
    ]
    args = [xa, mod, gathered.reshape(TOP_K, 1, S_TOT, PACKED), probs, final_g]
    aliases = {}
    if out is not None:
        in_specs.append(pl.BlockSpec(memory_space=pl.ANY))
        args.append(out)
        aliases = {len(args) - 1: 0}
    return pl.pallas_call(
        functools.partial(_final_kernel, batch=batch),
        grid=(N_TILES - ctx_tiles,),
        in_specs=in_specs,
        out_specs=pl.BlockSpec((1, TOK_TILE, D_MODEL), lambda i: (batch, i, 0)),
        out_shape=jax.ShapeDtypeStruct((BATCH, SEQ, D_MODEL), F32),
        input_output_aliases=aliases,
        compiler_params=pltpu.CompilerParams(dimension_semantics=("parallel",)),
        name="final",
    )(*args)


def _split_bf16(w):
    bits = lax.bitcast_convert_type(w.astype(F32), jnp.uint32) & jnp.uint32(0xFFFF0000)
    hi = lax.bitcast_convert_type(bits, F32)
    return jnp.concatenate([hi.astype(BF16), (w - hi).astype(BF16)], axis=-1)


def _rope_tables():
    t = np.arange(SEQ)
    axis_dim = HEAD_DIM // 2
    inv_freq = (1.0 / (ROPE_BASE ** (np.arange(0, axis_dim, 2, dtype=np.float32) / axis_dim))).astype(np.float32)
    ang_r = (t // GRID_W).astype(np.float32)[:, None] * inv_freq[None]
    ang_c = (t % GRID_W).astype(np.float32)[:, None] * inv_freq[None]
    ang = np.concatenate([ang_r, ang_r, ang_c, ang_c] * (LANES // HEAD_DIM), axis=-1)
    cos = np.concatenate([np.ones((CTX_LEN, LANES), np.float32), np.cos(ang)], axis=0)
    sin = np.concatenate([np.zeros((CTX_LEN, LANES), np.float32), np.sin(ang)], axis=0)
    return jnp.asarray(cos, F32), jnp.asarray(sin, F32)


def kernel(x, c, ctx, c_ctx, w_ada, b_ada, norm1_g, norm2_g, w_in, ssm_lam_re, ssm_lam_im, ssm_log_dt, ssm_b_re, ssm_b_im, ssm_c_re, ssm_c_im, ssm_d, ssm_w_val, ssm_w_gate, attn_sink, attn_w_up, pool_w, pool_scale, w_out, router_w, router_b, exp_w1, exp_b1, exp_w2, exp_b2, final_g):
    assert x.shape == (BATCH, SEQ, D_MODEL) and ctx.shape == (BATCH, CTX_LEN, D_MODEL)
    cos_t, sin_t = _rope_tables()
    cv = jnp.concatenate([c, c_ctx[None], jnp.zeros((8 - BATCH - 1, D_MODEL), F32)], axis=0)
    mods = _ada(cv, w_ada, b_ada)
    xs = (ctx, x)
    ew1 = exp_w1.reshape(DEPTH * N_EXPERTS, D_MODEL, 2 * D_EXPERT)
    eb1 = exp_b1.reshape(DEPTH * N_EXPERTS, 1, 2 * D_EXPERT)
    ew2 = exp_w2.reshape(DEPTH * N_EXPERTS, D_EXPERT, D_MODEL)
    eb2 = exp_b2.reshape(DEPTH * N_EXPERTS, 1, D_MODEL)
    half_gate = jnp.where(jnp.arange(IN_WIDTH) >= OFF_GATE, 0.5, 1.0).astype(F32)
    w_in_b = (w_in * half_gate).astype(BF16)
    ssm_tables = _ssm_params(ssm_lam_re, ssm_lam_im, ssm_log_dt, ssm_b_re, ssm_b_im, ssm_c_re, ssm_c_im)
    merge_params = dict(
        g1=norm1_g[:, None], dsk=ssm_d[:, None], wval=(0.5 * ssm_w_val).astype(BF16),
        wgate=(0.5 * ssm_w_gate).astype(BF16), wup=attn_w_up.astype(BF16), pw=pool_w.astype(BF16),
        ps=pool_scale[:, None], wout=(0.5 * w_out).astype(BF16), g2=norm2_g[:, None],
        rw=_split_bf16(router_w), rb=router_b[:, None])
    proj = _inproj(xs, mods[0], norm1_g[:, None], w_in_b, 0, cos_t, sin_t)
    out = None
    for l in range(DEPTH):
        mod = mods[l]
        u, kv, q, pool_in = proj[:4]
        if len(proj) > 4:
            xs = (proj[4], proj[4])
        yf, yb = _ssm(u, *ssm_tables, l)
        o = _attn(attn_sink[l], q, kv)
        xa, f, top_e, top_p, tile_cnt = _merge(xs, mod, w_in=w_in_b, layer=l, yf=yf, yb=yb, u=u, o=o,
                                               pool_in=pool_in, **merge_params)
        proj = None
        for b in range(BATCH):
            counts = jnp.pad(tile_cnt[b, :, 0, :].sum(axis=0)[None], ((0, 0), (0, LANES - N_EXPERTS)))
            dest, block_e = _route(top_e.reshape(N_TOK, TOP_K), counts, b)
            dest_t = dest.T
            rows = _dispatch(f.reshape(N_TOK, PACKED), dest_t, b)
            yrows = _experts(block_e[:N_BLOCKS, 0], rows, ew1, eb1, ew2, eb2, l)
            gathered = _combine_gather(yrows, dest_t.reshape(-1))
            if l < DEPTH - 1:
                proj = _inproj((xa, xa), mods[l + 1], norm1_g[:, None], w_in_b, l + 1, cos_t, sin_t,
                               (mod, gathered, top_p), b, proj)
            else:
                out = _final(xa, mod, gathered, top_p, final_g[None], b, out)
    return out
```
